```python
import math
import jax, jax.numpy as jnp
from jax import lax
import numpy as np

D_MODEL = 1024
BATCH = 16
SEQ = 4096
DEPTH = 4

GRID_W = 64
CTX_LEN = 256
POOL_WINDOWS = (2, 4, 8, 16)
N_POOL = len(POOL_WINDOWS)
MIX_POOL = D_MODEL // 4
POOL_DIM = MIX_POOL // N_POOL
N_FOUR = 4
MIX_FOUR = D_MODEL // 4
FOUR_DIM = MIX_FOUR // N_FOUR
N_MLSTM = 4
MIX_MLSTM = D_MODEL // 2
MLSTM_DH = MIX_MLSTM // N_MLSTM
MLSTM_CHUNK = 128
QK_CONV = 3
N_GATES = 2 * 2 * N_MLSTM
D_FF = int(math.ceil(8 * D_MODEL / 3 / 128)) * 128
FFN_CONV = 3
OFF_FOUR = MIX_POOL
OFF_O = OFF_FOUR + MIX_FOUR
OFF_Q = OFF_O + MIX_MLSTM
OFF_K = OFF_Q + MIX_MLSTM
D_IN = OFF_K + 2 * MIX_MLSTM + N_GATES
MIX_WIDTH = MIX_POOL + MIX_FOUR + MIX_MLSTM
ALPHA = (2 * DEPTH) ** 0.25
BETA = (8 * DEPTH) ** -0.25
LN_EPS = 1e-6

kernel_name = 'hybrid_pool_fourier_mlstm_dit_block'


def layer_norm(x, g=None, b=None):
    xf = x.astype(jnp.float32)
    mu = xf.mean(-1, keepdims=True)
    var = jnp.square(xf - mu).mean(-1, keepdims=True)
    y = ((xf - mu) * lax.rsqrt(var + LN_EPS)).astype(x.dtype)
    if g is not None:
        y = y * g + b
    return y


def modulate(x, shift, scale):
    return layer_norm(x) * (1 + scale) + shift


def dwconv_centred(x, w):
    K = w.shape[0]
    T = x.shape[1]
    p = K // 2
    xp = jnp.pad(x, ((0, 0), (p, p), (0, 0)))
    return sum(xp[:, i:i + T] * w[i] for i in range(K))


def box_sum(x, w, axis):
    n = x.shape[axis]
    pos = np.arange(n)
    lo = np.maximum(pos - w // 2, 0)
    hi = np.minimum(pos + w // 2 - 1, n - 1)
    pad = [(0, 0)] * x.ndim
    pad[axis] = (1, 0)
    P = jnp.pad(jnp.cumsum(x, axis=axis), pad)
    s = jnp.take(P, hi + 1, axis=axis) - jnp.take(P, lo, axis=axis)
    return s, jnp.asarray(hi - lo + 1, jnp.float32)


def pool_branch(z, w_pool, s_pool, rows):
    B, T, _ = z.shape
    zf = z.astype(jnp.float32).reshape(B, T, N_POOL, POOL_DIM)
    outs = []
    for g, w in enumerate(POOL_WINDOWS):
        zg = zf[:, :, g]
        if rows is None:
            s, cnt = box_sum(zg, w, 1)
            mean = s / cnt[None, :, None]
        else:
            zg2 = zg.reshape(B, rows, GRID_W, POOL_DIM)
            s, cr = box_sum(zg2, w, 1)
            s, cc = box_sum(s, w, 2)
            mean = (s / (cr[:, None] * cc[None, :])[None, :, :, None]).reshape(B, T, POOL_DIM)
        outs.append(mean - zg)
    p = jnp.stack(outs, axis=2)
    y = jnp.einsum('btgc,gce->btge', p, w_pool.astype(jnp.float32)).reshape(B, T, MIX_POOL)
    return (y * s_pool.astype(jnp.float32)).astype(z.dtype)


def fourier_branch(z, w_four):
    B, T, _ = z.shape
    f = z.astype(jnp.float32).reshape(B, T, N_FOUR, FOUR_DIM).transpose(0, 2, 1, 3)
    fr = jnp.fft.fft2(f, norm='ortho').real
    y = jnp.einsum('bgtc,gce->btge', fr, w_four.astype(jnp.float32))
    return y.reshape(B, T, MIX_FOUR).astype(z.dtype)


def mlstm_scan(q, k, v, ig, lf, state):
    B, H, T = ig.shape
    nc = T // MLSTM_CHUNK

    def chunks(a):
        return jnp.moveaxis(a.reshape(B, H, nc, MLSTM_CHUNK, *a.shape[3:]), 2, 0)

    lower = jnp.tril(jnp.ones((MLSTM_CHUNK, MLSTM_CHUNK), bool))

    def step(carry, xs):
        C, n, m = carry
        kc, vc, ic, fc, qc = xs
        b = jnp.cumsum(fc, axis=-1)
        g_end = b[..., -1:] - b + ic
        m_new = jnp.maximum(b[..., -1] + m, g_end.max(-1))
        w_prev = jnp.exp(b[..., -1] + m - m_new)
        w_s = jnp.exp(g_end - m_new[..., None])
        C_new = w_prev[..., None, None] * C + jnp.einsum('bhs,bhsk,bhsv->bhkv', w_s, kc, vc)
        n_new = w_prev[..., None] * n + jnp.einsum('bhs,bhsk->bhk', w_s, kc)
        if qc is None:
            return (C_new, n_new, m_new), None
        dmat = jnp.where(lower, b[..., :, None] - b[..., None, :] + ic[..., None, :], -jnp.inf)
        inter = b + m[..., None]
        m_j = jnp.maximum(inter, dmat.max(-1))
        w_in = jnp.exp(inter - m_j)
        s = jnp.einsum('bhjk,bhsk->bhjs', qc, kc) * jnp.exp(dmat - m_j[..., None])
        num = w_in[..., None] * jnp.einsum('bhjk,bhkv->bhjv', qc, C) + jnp.einsum('bhjs,bhsv->bhjv', s, vc)
        den = w_in * jnp.einsum('bhjk,bhk->bhj', qc, n) + s.sum(-1)
        h = num / jnp.maximum(jnp.abs(den), jnp.exp(-m_j))[..., None]
        return (C_new, n_new, m_new), h

    xs = (chunks(k), chunks(v), chunks(ig), chunks(lf), None if q is None else chunks(q))
    state, hs = lax.scan(step, state, xs)
    if q is None:
        return None, state
    return jnp.moveaxis(hs, 0, 2).reshape(B, H, T, -1), state


def mlstm_branch(z_q, z_kvg, w_conv, b_gate, init_f, init_b):
    B, T, _ = z_kvg.shape

    def heads(a):
        return a.astype(jnp.float32).reshape(B, T, N_MLSTM, MLSTM_DH).transpose(0, 2, 1, 3)

    k = heads(jax.nn.silu(dwconv_centred(z_kvg[..., :MIX_MLSTM], w_conv[:, MIX_MLSTM:]))) * MLSTM_DH ** -0.5
    v = heads(z_kvg[..., MIX_MLSTM:2 * MIX_MLSTM])
    q = None if z_q is None else heads(jax.nn.silu(dwconv_centred(z_q, w_conv[:, :MIX_MLSTM])))
    g = (z_kvg[..., 2 * MIX_MLSTM:].astype(jnp.float32) + b_gate.astype(jnp.float32))
    g = g.reshape(B, T, 2, 2, N_MLSTM).transpose(2, 3, 0, 4, 1)
    ig, lf = g[:, 0], jax.nn.log_sigmoid(g[:, 1])

    def flip(a):
        return None if a is None else jnp.flip(a, axis=2)

    h_f, st_f = mlstm_scan(q, k, v, ig[0], lf[0], init_f)
    h_b, st_b = mlstm_scan(flip(q), flip(k), flip(v), flip(ig[1]), flip(lf[1]), init_b)
    if z_q is None:
        return None, st_f, st_b
    h = (h_f + flip(h_b)).transpose(0, 2, 1, 3).reshape(B, T, MIX_MLSTM)
    return h, st_f, st_b


def mix_stream(z, w_pool, s_pool, w_four, w_conv, b_gate, init_f, init_b, rows):
    h, st_f, st_b = mlstm_branch(z[..., OFF_Q:OFF_K], z[..., OFF_K:], w_conv, b_gate, init_f, init_b)
    y_mlstm = (jax.nn.sigmoid(z[..., OFF_O:OFF_Q].astype(jnp.float32)) * h).astype(z.dtype)
    y = jnp.concatenate([pool_branch(z[..., :OFF_FOUR], w_pool, s_pool, rows),
                         fourier_branch(z[..., OFF_FOUR:OFF_O], w_four),
                         y_mlstm], axis=-1)
    return y, st_f, st_b


def conv_ffn(u, w_up, w_conv, b_conv, w_down):
    h = dwconv_centred(u @ w_up, w_conv) + b_conv
    a, g = jnp.split(h, 2, axis=-1)
    return (jax.nn.gelu(g) * a) @ w_down


def zero_state(B):
    return (jnp.zeros((B, N_MLSTM, MLSTM_DH, MLSTM_DH), jnp.float32),
            jnp.zeros((B, N_MLSTM, MLSTM_DH), jnp.float32),
            jnp.zeros((B, N_MLSTM), jnp.float32))


def setup_inputs(seed: int = 0) -> dict:
    key = jax.random.key(seed)
    ks = jax.random.split(key, 24)
    L = DEPTH

    def nrm(k, shape, s):
        return jax.random.normal(k, shape, jnp.float32) * s

    gate_base = jnp.zeros((2, 2, N_MLSTM), jnp.float32).at[:, 1].set(jnp.linspace(3.0, 6.0, N_MLSTM))
    b_gates = (gate_base[None] + nrm(ks[8], (L, 2, 2, N_MLSTM), 0.1)).reshape(L, N_GATES)
    return {
        'x': nrm(ks[0], (BATCH, SEQ, D_MODEL), 1.0),
        'c': nrm(ks[1], (BATCH, D_MODEL), 1.0),
        'ctx': nrm(ks[2], (BATCH, CTX_LEN, D_MODEL), 1.0),
        'c_ctx': nrm(ks[3], (D_MODEL,), 1.0),
        'w_mod': nrm(ks[4], (L, D_MODEL, 6 * D_MODEL), 0.5 * D_MODEL ** -0.5),
        'b_mod': nrm(ks[5], (L, 6 * D_MODEL), 0.02),
        'w_in': nrm(ks[6], (L, D_MODEL, D_IN), D_MODEL ** -0.5),
        'conv_qk': nrm(ks[7], (L, QK_CONV, 2 * MIX_MLSTM), QK_CONV ** -0.5),
        'b_gates': b_gates,
        'pool_w': nrm(ks[9], (L, N_POOL, POOL_DIM, POOL_DIM), POOL_DIM ** -0.5),
        'pool_scale': 1.0 + nrm(ks[10], (L, MIX_POOL), 0.1),
        'fourier_w': nrm(ks[11], (L, N_FOUR, FOUR_DIM, FOUR_DIM), FOUR_DIM ** -0.5),
        'w_out': nrm(ks[12], (L, MIX_WIDTH, D_MODEL), BETA * MIX_WIDTH ** -0.5),
        'ln1_g': 1.0 + nrm(ks[13], (L, D_MODEL), 0.02),
        'ln1_b': nrm(ks[14], (L, D_MODEL), 0.02),
        'w_up': nrm(ks[15], (L, D_MODEL, 2 * D_FF), D_MODEL ** -0.5),
        'conv_ffn_w': nrm(ks[16], (L, FFN_CONV, 2 * D_FF), FFN_CONV ** -0.5),
        'conv_ffn_b': nrm(ks[17], (L, 2 * D_FF), 0.02),
        'w_down': nrm(ks[18], (L, D_FF, D_MODEL), BETA * D_FF ** -0.5),
        'ln2_g': 1.0 + nrm(ks[19], (L, D_MODEL), 0.02),
        'ln2_b': nrm(ks[20], (L, D_MODEL), 0.02),
    }


def reference(x, c, ctx, c_ctx, w_mod, b_mod, w_in, conv_qk, b_gates, pool_w, pool_scale, fourier_w,
              w_out, ln1_g, ln1_b, w_up, conv_ffn_w, conv_ffn_b, w_down, ln2_g, ln2_b):
    B = x.shape[0]
    rows = x.shape[1] // GRID_W
    xc = ctx
    s_c = jax.nn.silu(c)
    s_ctx = jax.nn.silu(c_ctx)
    for l in range(DEPTH):
        last = l == DEPTH - 1
        mx = jnp.split((s_c @ w_mod[l] + b_mod[l])[:, None, :], 6, axis=-1)
        mc = jnp.split(s_ctx @ w_mod[l] + b_mod[l], 6, axis=-1)
        uc = modulate(xc, mc[0], mc[1])
        zero = zero_state(B)
        if last:
            _, st_f, st_b = mlstm_branch(None, uc @ w_in[l][:, OFF_K:], conv_qk[l], b_gates[l], zero, zero)
        else:
            yc, st_f, st_b = mix_stream(uc @ w_in[l], pool_w[l], pool_scale[l], fourier_w[l], conv_qk[l],
                                        b_gates[l], zero, zero, None)
        ux = modulate(x, mx[0], mx[1])
        yx, _, _ = mix_stream(ux @ w_in[l], pool_w[l], pool_scale[l], fourier_w[l], conv_qk[l],
                              b_gates[l], st_f, st_b, rows)
        x = layer_norm(ALPHA * x + mx[2] * (yx @ w_out[l]), ln1_g[l], ln1_b[l])
        fx = conv_ffn(modulate(x, mx[3], mx[4]), w_up[l], conv_ffn_w[l], conv_ffn_b[l], w_down[l])
        x = layer_norm(ALPHA * x + mx[5] * fx, ln2_g[l], ln2_b[l])
        if not last:
            xc = layer_norm(ALPHA * xc + mc[2] * (yc @ w_out[l]), ln1_g[l], ln1_b[l])
            fc = conv_ffn(modulate(xc, mc[3], mc[4]), w_up[l], conv_ffn_w[l], conv_ffn_b[l], w_down[l])
            xc = layer_norm(ALPHA * xc + mc[5] * fc, ln2_g[l], ln2_b[l])
    return x
```

```python
import functools
import math

import jax
import jax.numpy as jnp
import numpy as np
from jax import lax
from jax.experimental import pallas as pl
from jax.experimental.pallas import tpu as pltpu

F32 = jnp.float32
BF16 = jnp.bfloat16

D_MODEL = 1024
DEPTH = 4
GRID_W = 64
POOL_WINDOWS = (2, 4, 8, 16)
N_POOL = len(POOL_WINDOWS)
MIX_POOL = D_MODEL // 4
POOL_DIM = MIX_POOL // N_POOL
N_FOUR = 4
MIX_FOUR = D_MODEL // 4
FOUR_DIM = MIX_FOUR // N_FOUR
N_MLSTM = 4
MIX_MLSTM = D_MODEL // 2
MLSTM_DH = MIX_MLSTM // N_MLSTM
MLSTM_CHUNK = 128
N_GATES = 2 * 2 * N_MLSTM
D_FF = int(math.ceil(8 * D_MODEL / 3 / 128)) * 128
OFF_FOUR = MIX_POOL
OFF_O = OFF_FOUR + MIX_FOUR
OFF_Q = OFF_O + MIX_MLSTM
OFF_K = OFF_Q + MIX_MLSTM
OFF_G = OFF_K + 2 * MIX_MLSTM
MIX_WIDTH = MIX_POOL + MIX_FOUR + MIX_MLSTM
ALPHA = (2 * DEPTH) ** 0.25
LN_EPS = 1e-6

LANE = 128
SUBLANE = 8
VMEM_LIMIT = 56 * 1024 * 1024
FF_CHUNK = 256
PROJ_CHUNK = 512


def _ln_rows(x):
    mu = jnp.mean(x, axis=-1, keepdims=True)
    xc = x - mu
    var = jnp.mean(xc * xc, axis=-1, keepdims=True)
    return xc * lax.rsqrt(var + LN_EPS)


def _proj_in_kernel(x_ref, sh_ref, sc_ref, w_ref, wg_ref, z_ref, g_ref):
    u = _ln_rows(x_ref[0]) * (1.0 + sc_ref[0]) + sh_ref[0]
    ub = u.astype(BF16)
    n = z_ref.shape[-1]
    for c0 in range(0, n, PROJ_CHUNK):
        c1 = min(c0 + PROJ_CHUNK, n)
        z_ref[0, :, c0:c1] = jnp.dot(ub, w_ref[:, c0:c1], preferred_element_type=F32).astype(z_ref.dtype)
    g_ref[0] = jnp.dot(ub, wg_ref[...], preferred_element_type=F32)


def _proj_in(x, shift, scale, w_main, w_gate, tm):
    B, T, D = x.shape
    n = w_main.shape[1]
    return pl.pallas_call(
        _proj_in_kernel,
        grid=(B, T // tm),
        in_specs=[
            pl.BlockSpec((1, tm, D), lambda b, i: (b, i, 0)),
            pl.BlockSpec((1, 1, D), lambda b, i: (b, 0, 0)),
            pl.BlockSpec((1, 1, D), lambda b, i: (b, 0, 0)),
            pl.BlockSpec((D, n), lambda b, i: (0, 0)),
            pl.BlockSpec((D, LANE), lambda b, i: (0, 0)),
        ],
        out_specs=[
            pl.BlockSpec((1, tm, n), lambda b, i: (b, i, 0)),
            pl.BlockSpec((1, tm, LANE), lambda b, i: (b, i, 0)),
        ],
        out_shape=[
            jax.ShapeDtypeStruct((B, T, n), BF16),
            jax.ShapeDtypeStruct((B, T, LANE), F32),
        ],
        compiler_params=pltpu.CompilerParams(
            dimension_semantics=("arbitrary", "arbitrary"), vmem_limit_bytes=VMEM_LIMIT),
        name="proj_in",
    )(x, shift, scale, w_main, w_gate)


def _proj_out_kernel(y_ref, x_ref, gate_ref, w_ref, lg_ref, lb_ref, o_ref):
    f = jnp.dot(y_ref[0], w_ref[...], preferred_element_type=F32)
    r = ALPHA * x_ref[0] + gate_ref[0] * f
    o_ref[0] = _ln_rows(r) * lg_ref[...] + lb_ref[...]


def _proj_out(y, x, gate, w, ln_g, ln_b, tm):
    B, T, D = x.shape
    K = y.shape[-1]
    return pl.pallas_call(
        _proj_out_kernel,
        grid=(B, T // tm),
        in_specs=[
            pl.BlockSpec((1, tm, K), lambda b, i: (b, i, 0)),
            pl.BlockSpec((1, tm, D), lambda b, i: (b, i, 0)),
            pl.BlockSpec((1, 1, D), lambda b, i: (b, 0, 0)),
            pl.BlockSpec((K, D), lambda b, i: (0, 0)),
            pl.BlockSpec((1, D), lambda b, i: (0, 0)),
            pl.BlockSpec((1, D), lambda b, i: (0, 0)),
        ],
        out_specs=pl.BlockSpec((1, tm, D), lambda b, i: (b, i, 0)),
        out_shape=jax.ShapeDtypeStruct((B, T, D), F32),
        compiler_params=pltpu.CompilerParams(
            dimension_semantics=("arbitrary", "arbitrary"), vmem_limit_bytes=VMEM_LIMIT),
        name="proj_out",
    )(y, x, gate, w, ln_g, ln_b)


def _gelu_tanh(x):
    c = math.sqrt(2.0 / math.pi)
    return 0.5 * x * (1.0 + jnp.tanh(c * (x + 0.044715 * (x * x * x))))


def _ffn_kernel(x_ref, xp_ref, xn_ref, sh_ref, sc_ref, gate_ref, wup_ref, wc_ref, bc_ref, wdn_ref,
                lg_ref, lb_ref, o_ref, ha_ref, hg_ref, p_ref):
    i = pl.program_id(1)
    tm = x_ref.shape[1]
    x = x_ref[0]
    mod_scale = 1.0 + sc_ref[0]
    mod_shift = sh_ref[0]
    has_prev = (i > 0).astype(F32)
    has_next = (i < pl.num_programs(1) - 1).astype(F32)
    u_prev = (_ln_rows(xp_ref[0]) * mod_scale + mod_shift) * has_prev
    u_next = (_ln_rows(xn_ref[0]) * mod_scale + mod_shift) * has_next
    u_main = _ln_rows(x) * mod_scale + mod_shift
    u_ext = jnp.concatenate([u_prev, u_main, u_next], axis=0).astype(BF16)

    def conv(h_ref, cols):
        w = wc_ref[:, cols]
        return (h_ref[SUBLANE - 1:SUBLANE - 1 + tm, :] * w[0:1]
                + h_ref[SUBLANE:SUBLANE + tm, :] * w[1:2]
                + h_ref[SUBLANE + 1:SUBLANE + 1 + tm, :] * w[2:3]
                + bc_ref[:, cols])

    for c in range(D_FF // FF_CHUNK):
        ca = slice(c * FF_CHUNK, (c + 1) * FF_CHUNK)
        cg = slice(D_FF + c * FF_CHUNK, D_FF + (c + 1) * FF_CHUNK)
        ha_ref[...] = jnp.dot(u_ext, wup_ref[:, ca], preferred_element_type=F32)
        hg_ref[...] = jnp.dot(u_ext, wup_ref[:, cg], preferred_element_type=F32)
        p_ref[:, ca] = (_gelu_tanh(conv(hg_ref, cg)) * conv(ha_ref, ca)).astype(BF16)

    f = jnp.dot(p_ref[...], wdn_ref[...], preferred_element_type=F32)
    r = ALPHA * x + gate_ref[0] * f
    o_ref[0] = _ln_rows(r) * lg_ref[...] + lb_ref[...]


def _ffn(x, shift, scale, gate, w_up, w_conv, b_conv, w_down, ln_g, ln_b, tm):
    B, T, D = x.shape
    nb = tm // SUBLANE
    last = T // SUBLANE - 1
    const = dict(pipeline_mode=pl.Buffered(1))
    return pl.pallas_call(
        _ffn_kernel,
        grid=(B, T // tm),
        in_specs=[
            pl.BlockSpec((1, tm, D), lambda b, i: (b, i, 0)),
            pl.BlockSpec((1, SUBLANE, D), lambda b, i: (b, jnp.maximum(i * nb - 1, 0), 0)),
            pl.BlockSpec((1, SUBLANE, D), lambda b, i: (b, jnp.minimum((i + 1) * nb, last), 0)),
            pl.BlockSpec((1, 1, D), lambda b, i: (b, 0, 0)),
            pl.BlockSpec((1, 1, D), lambda b, i: (b, 0, 0)),
            pl.BlockSpec((1, 1, D), lambda b, i: (b, 0, 0)),
            pl.BlockSpec((D, 2 * D_FF), lambda b, i: (0, 0), **const),
            pl.BlockSpec((3, 2 * D_FF), lambda b, i: (0, 0), **const),
            pl.BlockSpec((1, 2 * D_FF), lambda b, i: (0, 0), **const),
            pl.BlockSpec((D_FF, D), lambda b, i: (0, 0), **const),
            pl.BlockSpec((1, D), lambda b, i: (0, 0)),
            pl.BlockSpec((1, D), lambda b, i: (0, 0)),
        ],
        out_specs=pl.BlockSpec((1, tm, D), lambda b, i: (b, i, 0)),
        out_shape=jax.ShapeDtypeStruct((B, T, D), F32),
        scratch_shapes=[
            pltpu.VMEM((tm + 2 * SUBLANE, FF_CHUNK), F32),
            pltpu.VMEM((tm + 2 * SUBLANE, FF_CHUNK), F32),
            pltpu.VMEM((tm, D_FF), BF16),
        ],
        compiler_params=pltpu.CompilerParams(
            dimension_semantics=("arbitrary", "arbitrary"), vmem_limit_bytes=VMEM_LIMIT),
        name="conv_ffn",
    )(x, x, x, shift, scale, gate, w_up, w_conv, b_conv, w_down, ln_g, ln_b)


def _dwconv_centred(x, w):
    K = w.shape[0]
    T = x.shape[1]
    p = K // 2
    xp = jnp.pad(x, ((0, 0), (p, p), (0, 0)))
    return sum(xp[:, i:i + T] * w[i] for i in range(K))


def _box_sum(x, w, axis):
    n = x.shape[axis]
    pos = np.arange(n)
    lo = np.maximum(pos - w // 2, 0)
    hi = np.minimum(pos + w // 2 - 1, n - 1)
    pad = [(0, 0)] * x.ndim
    pad[axis] = (1, 0)
    P = jnp.pad(jnp.cumsum(x, axis=axis), pad)
    s = jnp.take(P, hi + 1, axis=axis) - jnp.take(P, lo, axis=axis)
    return s, jnp.asarray(hi - lo + 1, jnp.float32)


def _pool_branch(z, w_pool, s_pool, rows):
    B, T, _ = z.shape
    zf = z.astype(F32).reshape(B, T, N_POOL, POOL_DIM)
    outs = []
    for g, w in enumerate(POOL_WINDOWS):
        zg = zf[:, :, g]
        if rows is None:
            s, cnt = _box_sum(zg, w, 1)
            mean = s / cnt[None, :, None]
        else:
            zg2 = zg.reshape(B, rows, GRID_W, POOL_DIM)
            s, cr = _box_sum(zg2, w, 1)
            s, cc = _box_sum(s, w, 2)
            mean = (s / (cr[:, None] * cc[None, :])[None, :, :, None]).reshape(B, T, POOL_DIM)
        outs.append(mean - zg)
    p = jnp.stack(outs, axis=2)
    y = jnp.einsum('btgc,gce->btge', p, w_pool.astype(F32)).reshape(B, T, MIX_POOL)
    return y * s_pool.astype(F32)


def _fourier_branch(z, w_four):
    B, T, _ = z.shape
    f = z.astype(F32).reshape(B, T, N_FOUR, FOUR_DIM).transpose(0, 2, 1, 3)
    fr = jnp.fft.fft2(f, norm='ortho').real
    y = jnp.einsum('bgtc,gce->btge', fr, w_four.astype(F32))
    return y.reshape(B, T, MIX_FOUR)


def _mlstm_scan(q, k, v, ig, lf, state):
    B, H, T = ig.shape
    nc = T // MLSTM_CHUNK

    def chunks(a):
        return jnp.moveaxis(a.reshape(B, H, nc, MLSTM_CHUNK, *a.shape[3:]), 2, 0)

    lower = jnp.tril(jnp.ones((MLSTM_CHUNK, MLSTM_CHUNK), bool))

    def step(carry, xs):
        C, n, m = carry
        kc, vc, ic, fc, qc = xs
        b = jnp.cumsum(fc, axis=-1)
        g_end = b[..., -1:] - b + ic
        m_new = jnp.maximum(b[..., -1] + m, g_end.max(-1))
        w_prev = jnp.exp(b[..., -1] + m - m_new)
        w_s = jnp.exp(g_end - m_new[..., None])
        C_new = w_prev[..., None, None] * C + jnp.einsum('bhs,bhsk,bhsv->bhkv', w_s, kc, vc)
        n_new = w_prev[..., None] * n + jnp.einsum('bhs,bhsk->bhk', w_s, kc)
        if qc is None:
            return (C_new, n_new, m_new), None
        dmat = jnp.where(lower, b[..., :, None] - b[..., None, :] + ic[..., None, :], -jnp.inf)
        inter = b + m[..., None]
        m_j = jnp.maximum(inter, dmat.max(-1))
        w_in = jnp.exp(inter - m_j)
        s = jnp.einsum('bhjk,bhsk->bhjs', qc, kc) * jnp.exp(dmat - m_j[..., None])
        num = w_in[..., None] * jnp.einsum('bhjk,bhkv->bhjv', qc, C) + jnp.einsum('bhjs,bhsv->bhjv', s, vc)
        den = w_in * jnp.einsum('bhjk,bhk->bhj', qc, n) + s.sum(-1)
        h = num / jnp.maximum(jnp.abs(den), jnp.exp(-m_j))[..., None]
        return (C_new, n_new, m_new), h

    xs = (chunks(k), chunks(v), chunks(ig), chunks(lf), None if q is None else chunks(q))
    state, hs = lax.scan(step, state, xs)
    if q is None:
        return None, state
    return jnp.moveaxis(hs, 0, 2).reshape(B, H, T, -1), state


def _mlstm_branch(z_q, z_k, z_v, z_g, w_conv, b_gate, init_f, init_b):
    B, T, _ = z_k.shape

    def heads(a):
        return a.astype(F32).reshape(B, T, N_MLSTM, MLSTM_DH).transpose(0, 2, 1, 3)

    k = heads(jax.nn.silu(_dwconv_centred(z_k.astype(F32), w_conv[:, MIX_MLSTM:]))) * MLSTM_DH ** -0.5
    v = heads(z_v)
    q = None if z_q is None else heads(jax.nn.silu(_dwconv_centred(z_q.astype(F32), w_conv[:, :MIX_MLSTM])))
    g = z_g.astype(F32) + b_gate.astype(F32)
    g = g.reshape(B, T, 2, 2, N_MLSTM).transpose(2, 3, 0, 4, 1)
    ig, lf = g[:, 0], jax.nn.log_sigmoid(g[:, 1])

    def flip(a):
        return None if a is None else jnp.flip(a, axis=2)

    h_f, st_f = _mlstm_scan(q, k, v, ig[0], lf[0], init_f)
    h_b, st_b = _mlstm_scan(flip(q), flip(k), flip(v), flip(ig[1]), flip(lf[1]), init_b)
    if z_q is None:
        return None, st_f, st_b
    h = (h_f + flip(h_b)).transpose(0, 2, 1, 3).reshape(B, T, MIX_MLSTM)
    return h, st_f, st_b


def _mix_stream(z, zg, w_pool, s_pool, w_four, w_conv, b_gate, init_f, init_b, rows):
    h, st_f, st_b = _mlstm_branch(z[..., OFF_Q:OFF_K], z[..., OFF_K:OFF_K + MIX_MLSTM],
                                  z[..., OFF_K + MIX_MLSTM:OFF_G], zg, w_conv, b_gate, init_f, init_b)
    y_mlstm = jax.nn.sigmoid(z[..., OFF_O:OFF_Q].astype(F32)) * h
    y = jnp.concatenate([_pool_branch(z[..., :OFF_FOUR], w_pool, s_pool, rows),
                         _fourier_branch(z[..., OFF_FOUR:OFF_O], w_four),
                         y_mlstm], axis=-1)
    return y.astype(BF16), st_f, st_b


def _zero_state(B):
    return (jnp.zeros((B, N_MLSTM, MLSTM_DH, MLSTM_DH), F32),
            jnp.zeros((B, N_MLSTM, MLSTM_DH), F32),
            jnp.zeros((B, N_MLSTM), F32))


def kernel(x, c, ctx, c_ctx, w_mod, b_mod, w_in, conv_qk, b_gates, pool_w, pool_scale, fourier_w,
           w_out, ln1_g, ln1_b, w_up, conv_ffn_w, conv_ffn_b, w_down, ln2_g, ln2_b):
    B, T, D = x.shape
    Tc = ctx.shape[1]
    rows = T // GRID_W
    tm_x = 512
    tm_c = Tc
    xc = ctx
    s_c = jax.nn.silu(c)
    s_ctx = jax.nn.silu(c_ctx)
    for l in range(DEPTH):
        last = l == DEPTH - 1
        mx = [m[:, None, :] for m in jnp.split(s_c @ w_mod[l] + b_mod[l], 6, axis=-1)]
        mc = [jnp.broadcast_to(m[None, None, :], (B, 1, D))
              for m in jnp.split(s_ctx @ w_mod[l] + b_mod[l], 6, axis=-1)]
        w_main = w_in[l][:, :OFF_G].astype(BF16)
        w_gate = jnp.pad(w_in[l][:, OFF_G:], ((0, 0), (0, LANE - N_GATES))).astype(BF16)
        w_out_b = w_out[l].astype(BF16)
        w_up_b = w_up[l].astype(BF16)
        w_down_b = w_down[l].astype(BF16)
        g1, b1 = ln1_g[l][None], ln1_b[l][None]
        g2, b2 = ln2_g[l][None], ln2_b[l][None]
        bcv = conv_ffn_b[l][None]

        zc, zgc = _proj_in(xc, mc[0], mc[1], w_main, w_gate, tm_c)
        zero = _zero_state(B)
        if last:
            _, st_f, st_b = _mlstm_branch(None, zc[..., OFF_K:OFF_K + MIX_MLSTM], zc[..., OFF_K + MIX_MLSTM:OFF_G],
                                          zgc[..., :N_GATES], conv_qk[l], b_gates[l], zero, zero)
        else:
            yc, st_f, st_b = _mix_stream(zc, zgc[..., :N_GATES], pool_w[l], pool_scale[l], fourier_w[l],
                                         conv_qk[l], b_gates[l], zero, zero, None)
        zx, zgx = _proj_in(x, mx[0], mx[1], w_main, w_gate, tm_x)
        yx, _, _ = _mix_stream(zx, zgx[..., :N_GATES], pool_w[l], pool_scale[l], fourier_w[l],
                               conv_qk[l], b_gates[l], st_f, st_b, rows)
        x = _proj_out(yx, x, mx[2], w_out_b, g1, b1, tm_x)
        x = _ffn(x, mx[3], mx[4], mx[5], w_up_b, conv_ffn_w[l], bcv, w_down_b, g2, b2, tm_x)
        if not last:
            xc = _proj_out(yc, xc, mc[2], w_out_b, g1, b1, tm_c)
            xc = _ffn(xc, mc[3], mc[4], mc[5], w_up_b, conv_ffn_w[l], bcv, w_down_b, g2, b2, tm_c)
    return x
```

```python
import functools
import math

import jax
import jax.numpy as jnp
import numpy as np
from jax import lax
from jax.experimental import pallas as pl
from jax.experimental.pallas import tpu as pltpu

F32 = jnp.float32
BF16 = jnp.bfloat16

D_MODEL = 1024
DEPTH = 4
GRID_W = 64
GRID_SHIFT = GRID_W.bit_length() - 1
POOL_WINDOWS = (2, 4, 8, 16)
N_POOL = len(POOL_WINDOWS)
MIX_POOL = D_MODEL // 4
POOL_DIM = MIX_POOL // N_POOL
N_FOUR = 4
MIX_FOUR = D_MODEL // 4
FOUR_DIM = MIX_FOUR // N_FOUR
N_MLSTM = 4
MIX_MLSTM = D_MODEL // 2
MLSTM_DH = MIX_MLSTM // N_MLSTM
MLSTM_CHUNK = 128
N_GATES = 2 * 2 * N_MLSTM
D_FF = int(math.ceil(8 * D_MODEL / 3 / 128)) * 128
OFF_FOUR = MIX_POOL
OFF_O = OFF_FOUR + MIX_FOUR
OFF_Q = OFF_O + MIX_MLSTM
OFF_K = OFF_Q + MIX_MLSTM
OFF_V = OFF_K + MIX_MLSTM
OFF_G = OFF_V + MIX_MLSTM
ALPHA = (2 * DEPTH) ** 0.25
LN_EPS = 1e-6

LANE = 128
SUBLANE = 8
VMEM_LIMIT = 56 * 1024 * 1024
FF_CHUNK = 256
PROJ_CHUNK = 512
POOL_BLOCK = 256
POOL_PAD = (max(POOL_WINDOWS) // 2) * GRID_W
NEG_BIG = -1e30


def _params(n_axes):
    return pltpu.CompilerParams(dimension_semantics=("arbitrary",) * n_axes, vmem_limit_bytes=VMEM_LIMIT)


def _ln_rows(x):
    mu = jnp.mean(x, axis=-1, keepdims=True)
    xc = x - mu
    var = jnp.mean(xc * xc, axis=-1, keepdims=True)
    return xc * lax.rsqrt(var + LN_EPS)


def _proj_in_kernel(x_ref, sh_ref, sc_ref, w_ref, wg_ref, z_ref, g_ref):
    u = _ln_rows(x_ref[0]) * (1.0 + sc_ref[0]) + sh_ref[0]
    ub = u.astype(BF16)
    n = z_ref.shape[-1]
    for c0 in range(0, n, PROJ_CHUNK):
        c1 = min(c0 + PROJ_CHUNK, n)
        z_ref[0, :, c0:c1] = jnp.dot(ub, w_ref[:, c0:c1], preferred_element_type=F32).astype(z_ref.dtype)
    g_ref[0] = jnp.dot(ub, wg_ref[...], preferred_element_type=F32)


def _proj_in(x, shift, scale, w_main, w_gate, tm):
    B, T, D = x.shape
    n = w_main.shape[1]
    return pl.pallas_call(
        _proj_in_kernel,
        grid=(B, T // tm),
        in_specs=[
            pl.BlockSpec((1, tm, D), lambda b, i: (b, i, 0)),
            pl.BlockSpec((1, 1, D), lambda b, i: (b, 0, 0)),
            pl.BlockSpec((1, 1, D), lambda b, i: (b, 0, 0)),
            pl.BlockSpec((D, n), lambda b, i: (0, 0)),
            pl.BlockSpec((D, LANE), lambda b, i: (0, 0)),
        ],
        out_specs=[
            pl.BlockSpec((1, tm, n), lambda b, i: (b, i, 0)),
            pl.BlockSpec((1, tm, LANE), lambda b, i: (b, i, 0)),
        ],
        out_shape=[
            jax.ShapeDtypeStruct((B, T, n), BF16),
            jax.ShapeDtypeStruct((B, T, LANE), F32),
        ],
        compiler_params=_params(2),
        name="proj_in",
    )(x, shift, scale, w_main, w_gate)


def _proj_out_kernel(yp_ref, yf_ref, ym_ref, x_ref, gate_ref, wp_ref, wf_ref, wm_ref, lg_ref, lb_ref, o_ref):
    f = (jnp.dot(yp_ref[0], wp_ref[...], preferred_element_type=F32)
         + jnp.dot(yf_ref[0], wf_ref[...], preferred_element_type=F32)
         + jnp.dot(ym_ref[0], wm_ref[...], preferred_element_type=F32))
    r = ALPHA * x_ref[0] + gate_ref[0] * f
    o_ref[0] = _ln_rows(r) * lg_ref[...] + lb_ref[...]


def _proj_out(yp, yf, ym, x, gate, w, ln_g, ln_b, tm):
    B, T, D = x.shape
    tok = lambda b, i: (b, i, 0)
    return pl.pallas_call(
        _proj_out_kernel,
        grid=(B, T // tm),
        in_specs=[
            pl.BlockSpec((1, tm, MIX_POOL), tok),
            pl.BlockSpec((1, tm, MIX_FOUR), tok),
            pl.BlockSpec((1, tm, MIX_MLSTM), tok),
            pl.BlockSpec((1, tm, D), tok),
            pl.BlockSpec((1, 1, D), lambda b, i: (b, 0, 0)),
            pl.BlockSpec((MIX_POOL, D), lambda b, i: (0, 0)),
            pl.BlockSpec((MIX_FOUR, D), lambda b, i: (OFF_FOUR // MIX_FOUR, 0)),
            pl.BlockSpec((MIX_MLSTM, D), lambda b, i: (OFF_O // MIX_MLSTM, 0)),
            pl.BlockSpec((1, D), lambda b, i: (0, 0)),
            pl.BlockSpec((1, D), lambda b, i: (0, 0)),
        ],
        out_specs=pl.BlockSpec((1, tm, D), tok),
        out_shape=jax.ShapeDtypeStruct((B, T, D), F32),
        compiler_params=_params(2),
        name="proj_out",
    )(yp, yf, ym, x, gate, w, w, w, ln_g, ln_b)


def _gelu_tanh(x):
    c = math.sqrt(2.0 / math.pi)
    return 0.5 * x * (1.0 + jnp.tanh(c * (x + 0.044715 * (x * x * x))))


def _ffn_kernel(x_ref, xp_ref, xn_ref, sh_ref, sc_ref, gate_ref, wup_ref, wc_ref, bc_ref, wdn_ref,
                lg_ref, lb_ref, o_ref, ha_ref, hg_ref, p_ref):
    i = pl.program_id(1)
    tm = x_ref.shape[1]
    x = x_ref[0]
    mod_scale = 1.0 + sc_ref[0]
    mod_shift = sh_ref[0]
    has_prev = (i > 0).astype(F32)
    has_next = (i < pl.num_programs(1) - 1).astype(F32)
    u_prev = (_ln_rows(xp_ref[0]) * mod_scale + mod_shift) * has_prev
    u_next = (_ln_rows(xn_ref[0]) * mod_scale + mod_shift) * has_next
    u_main = _ln_rows(x) * mod_scale + mod_shift
    u_ext = jnp.concatenate([u_prev, u_main, u_next], axis=0).astype(BF16)

    def conv(h_ref, cols):
        w = wc_ref[:, cols]
        return (h_ref[SUBLANE - 1:SUBLANE - 1 + tm, :] * w[0:1]
                + h_ref[SUBLANE:SUBLANE + tm, :] * w[1:2]
                + h_ref[SUBLANE + 1:SUBLANE + 1 + tm, :] * w[2:3]
                + bc_ref[:, cols])

    for c in range(D_FF // FF_CHUNK):
        ca = slice(c * FF_CHUNK, (c + 1) * FF_CHUNK)
        cg = slice(D_FF + c * FF_CHUNK, D_FF + (c + 1) * FF_CHUNK)
        ha_ref[...] = jnp.dot(u_ext, wup_ref[:, ca], preferred_element_type=F32)
        hg_ref[...] = jnp.dot(u_ext, wup_ref[:, cg], preferred_element_type=F32)
        p_ref[:, ca] = (_gelu_tanh(conv(hg_ref, cg)) * conv(ha_ref, ca)).astype(BF16)

    f = jnp.dot(p_ref[...], wdn_ref[...], preferred_element_type=F32)
    r = ALPHA * x + gate_ref[0] * f
    o_ref[0] = _ln_rows(r) * lg_ref[...] + lb_ref[...]


def _ffn(x, shift, scale, gate, w_up, w_conv, b_conv, w_down, ln_g, ln_b, tm):
    B, T, D = x.shape
    nb = tm // SUBLANE
    last = T // SUBLANE - 1
    const = dict(pipeline_mode=pl.Buffered(1))
    return pl.pallas_call(
        _ffn_kernel,
        grid=(B, T // tm),
        in_specs=[
            pl.BlockSpec((1, tm, D), lambda b, i: (b, i, 0)),
            pl.BlockSpec((1, SUBLANE, D), lambda b, i: (b, jnp.maximum(i * nb - 1, 0), 0)),
            pl.BlockSpec((1, SUBLANE, D), lambda b, i: (b, jnp.minimum((i + 1) * nb, last), 0)),
            pl.BlockSpec((1, 1, D), lambda b, i: (b, 0, 0)),
            pl.BlockSpec((1, 1, D), lambda b, i: (b, 0, 0)),
            pl.BlockSpec((1, 1, D), lambda b, i: (b, 0, 0)),
            pl.BlockSpec((D, 2 * D_FF), lambda b, i: (0, 0), **const),
            pl.BlockSpec((3, 2 * D_FF), lambda b, i: (0, 0), **const),
            pl.BlockSpec((1, 2 * D_FF), lambda b, i: (0, 0), **const),
            pl.BlockSpec((D_FF, D), lambda b, i: (0, 0), **const),
            pl.BlockSpec((1, D), lambda b, i: (0, 0)),
            pl.BlockSpec((1, D), lambda b, i: (0, 0)),
        ],
        out_specs=pl.BlockSpec((1, tm, D), lambda b, i: (b, i, 0)),
        out_shape=jax.ShapeDtypeStruct((B, T, D), F32),
        scratch_shapes=[
            pltpu.VMEM((tm + 2 * SUBLANE, FF_CHUNK), F32),
            pltpu.VMEM((tm + 2 * SUBLANE, FF_CHUNK), F32),
            pltpu.VMEM((tm, D_FF), BF16),
        ],
        compiler_params=_params(2),
        name="conv_ffn",
    )(x, x, x, shift, scale, gate, w_up, w_conv, b_conv, w_down, ln_g, ln_b)


def _band_matrix(n, w):
    pos = np.arange(n)
    lo = np.maximum(pos - w // 2, 0)
    hi = np.minimum(pos + w // 2 - 1, n - 1)
    return ((pos[None, :] >= lo[:, None]) & (pos[None, :] <= hi[:, None])).astype(np.float32)


def _pool_filters(grid2d):
    mats = []
    for w in POOL_WINDOWS:
        if grid2d:
            mats.append(np.kron(np.eye(POOL_BLOCK // GRID_W, dtype=np.float32), _band_matrix(GRID_W, w)))
        else:
            mats.append(_band_matrix(POOL_BLOCK, w))
    return jnp.asarray(np.stack(mats), BF16)


def _window_count(idx, half, n):
    return jnp.minimum(idx + (half - 1), n - 1) - jnp.maximum(idx - half, 0) + 1


def _pool_kernel(z_ref, a_ref, w_ref, s_ref, y_ref, acc_ref, *, grid2d):
    T = z_ref.shape[1]
    nblk = T // POOL_BLOCK
    lane = lax.broadcasted_iota(jnp.int32, (POOL_BLOCK, MIX_POOL), 1)
    group = jnp.right_shift(lane, GRID_SHIFT)
    half = jnp.left_shift(1, group)

    def col_filter(blk):
        r0 = pl.multiple_of(blk * POOL_BLOCK, POOL_BLOCK)
        xb = z_ref[0, pl.ds(r0, POOL_BLOCK), :]
        out = jnp.dot(a_ref[0], xb, preferred_element_type=F32)
        for g in range(1, N_POOL):
            out = jnp.where(group == g, jnp.dot(a_ref[g], xb, preferred_element_type=F32), out)
        return r0, xb, out

    def finish(r0, xb, box, count):
        p = box / count.astype(F32) - xb.astype(F32)
        y = jnp.dot(p.astype(BF16), w_ref[...], preferred_element_type=F32) * s_ref[...]
        y_ref[0, pl.ds(r0, POOL_BLOCK), :] = y.astype(y_ref.dtype)

    if not grid2d:
        assert T == POOL_BLOCK
        r0, xb, box = col_filter(0)
        pos = lax.broadcasted_iota(jnp.int32, (POOL_BLOCK, MIX_POOL), 0)
        finish(r0, xb, box, _window_count(pos, half, T))
        return

    rows = T // GRID_W
    zeros = jnp.zeros((POOL_PAD, MIX_POOL), F32)
    acc_ref[0:POOL_PAD, :] = zeros
    acc_ref[POOL_PAD + T:POOL_PAD + T + POOL_PAD, :] = zeros

    def phase1(blk, carry):
        r0, _, out = col_filter(blk)
        acc_ref[pl.ds(POOL_PAD + r0, POOL_BLOCK), :] = out
        return carry

    lax.fori_loop(0, nblk, phase1, 0)

    lane_t = lax.broadcasted_iota(jnp.int32, (POOL_BLOCK, LANE), 1)

    def phase2(blk, carry):
        r0 = pl.multiple_of(blk * POOL_BLOCK, POOL_BLOCK)
        base = POOL_PAD + r0

        def slab(k, tile):
            return acc_ref[pl.ds(base + k * GRID_W, POOL_BLOCK), tile * LANE:(tile + 1) * LANE]

        s2 = slab(-1, 0) + slab(0, 0)
        s4 = s2 + slab(-2, 0) + slab(1, 0)
        s8 = slab(-4, 1)
        for k in range(-3, 4):
            s8 = s8 + slab(k, 1)
        s16 = s8
        for k in list(range(-8, -4)) + list(range(4, 8)):
            s16 = s16 + slab(k, 1)
        box = jnp.concatenate([jnp.where(lane_t < POOL_DIM, s2, s4), jnp.where(lane_t < POOL_DIM, s8, s16)], axis=1)
        pos = r0 + lax.broadcasted_iota(jnp.int32, (POOL_BLOCK, MIX_POOL), 0)
        count = (_window_count(jnp.right_shift(pos, GRID_SHIFT), half, rows)
                 * _window_count(jnp.bitwise_and(pos, GRID_W - 1), half, GRID_W))
        xb = z_ref[0, pl.ds(r0, POOL_BLOCK), :]
        finish(r0, xb, box, count)
        return carry

    lax.fori_loop(0, nblk, phase2, 0)


def _pool(z, filters, w_bd, s_row, grid2d):
    B, T, _ = z.shape
    return pl.pallas_call(
        functools.partial(_pool_kernel, grid2d=grid2d),
        grid=(B,),
        in_specs=[
            pl.BlockSpec((1, T, MIX_POOL), lambda b: (b, 0, 0)),
            pl.BlockSpec((N_POOL, POOL_BLOCK, POOL_BLOCK), lambda b: (0, 0, 0)),
            pl.BlockSpec((MIX_POOL, MIX_POOL), lambda b: (0, 0)),
            pl.BlockSpec((1, MIX_POOL), lambda b: (0, 0)),
        ],
        out_specs=pl.BlockSpec((1, T, MIX_POOL), lambda b: (b, 0, 0)),
        out_shape=jax.ShapeDtypeStruct((B, T, MIX_POOL), BF16),
        scratch_shapes=[pltpu.VMEM((T + 2 * POOL_PAD, MIX_POOL), F32)],
        compiler_params=_params(1),
        name="pool2d" if grid2d else "pool1d",
    )(z, filters, w_bd, s_row)


def _block_diag(w):
    G, n, m = w.shape
    eye = jnp.eye(G, dtype=w.dtype)
    return (eye[:, None, :, None] * w[:, :, None, :]).reshape(G * n, G * m)


def _four_mix_kernel(z_ref, g_ref, y_ref):
    y = jnp.dot(z_ref[0], g_ref[...], preferred_element_type=F32).astype(y_ref.dtype)
    y_ref[0, 0] = y[:, :MIX_FOUR]
    y_ref[0, 1] = y[:, MIX_FOUR:]


def _four_dft_kernel(d_ref, y_ref, o_ref):
    rows = 256
    for r0 in range(0, o_ref.shape[1], rows):
        o_ref[0, r0:r0 + rows, :] = jnp.dot(d_ref[r0:r0 + rows, :], y_ref[0],
                                            preferred_element_type=F32).astype(o_ref.dtype)


def _fourier(z, g_mix, dft):
    B, T, _ = z.shape
    tm = min(T, 512)
    y = pl.pallas_call(
        _four_mix_kernel,
        grid=(B, T // tm),
        in_specs=[
            pl.BlockSpec((1, tm, MIX_FOUR), lambda b, i: (b, i, OFF_FOUR // MIX_FOUR)),
            pl.BlockSpec((MIX_FOUR, 2 * MIX_FOUR), lambda b, i: (0, 0)),
        ],
        out_specs=pl.BlockSpec((1, 2, tm, MIX_FOUR), lambda b, i: (b, 0, i, 0)),
        out_shape=jax.ShapeDtypeStruct((B, 2, T, MIX_FOUR), BF16),
        compiler_params=_params(2),
        name="fourier_mix",
    )(z, g_mix)
    y = y.reshape(B, 2 * T, MIX_FOUR)
    tf = min(T, 1024)
    return pl.pallas_call(
        _four_dft_kernel,
        grid=(T // tf, B),
        in_specs=[
            pl.BlockSpec((tf, 2 * T), lambda m, b: (m, 0)),
            pl.BlockSpec((1, 2 * T, MIX_FOUR), lambda m, b: (b, 0, 0)),
        ],
        out_specs=pl.BlockSpec((1, tf, MIX_FOUR), lambda m, b: (b, m, 0)),
        out_shape=jax.ShapeDtypeStruct((B, T, MIX_FOUR), BF16),
        compiler_params=_params(2),
        name="fourier_dft",
    )(dft, y)


def _dft_matrix(n):
    f = lax.broadcasted_iota(jnp.int32, (n, n), 0)
    t = lax.broadcasted_iota(jnp.int32, (n, n), 1)
    ang = ((f * t) % n).astype(F32) * (2.0 * math.pi / n)
    return jnp.concatenate([jnp.cos(ang), jnp.sin(ang)], axis=1).astype(BF16)


def _fourier_mix_weights(w_four, T):
    c = np.arange(FOUR_DIM)
    ang = 2.0 * np.pi * np.outer(c, c) / FOUR_DIM
    norm = 1.0 / math.sqrt(T * FOUR_DIM)
    gr = jnp.einsum('cf,gfe->gce', jnp.asarray(np.cos(ang) * norm, F32), w_four, precision='highest')
    gi = jnp.einsum('cf,gfe->gce', jnp.asarray(-np.sin(ang) * norm, F32), w_four, precision='highest')
    return jnp.concatenate([_block_diag(gr), _block_diag(gi)], axis=1).astype(BF16)


def _log_sigmoid(x):
    return jnp.minimum(x, 0.0) - jnp.log(1.0 + jnp.exp(-jnp.abs(x)))


def _mlstm_kernel(q_ref, k_ref, v_ref, o_ref, g_ref, wq_ref, wk_ref, bg_ref, cf0_ref, mf0_ref, cb0_ref, mb0_ref,
                  y_ref, cf1_ref, mf1_ref, cb1_ref, mb1_ref,
                  pad_ref, qc_ref, kc_ref, hf_ref, hb_ref, cf_ref, cb_ref):
    T = q_ref.shape[1]
    L = MLSTM_CHUNK
    nc = T // L
    blk = min(T, 512)
    head = pl.program_id(1)

    def conv_silu(src_ref, w_ref, dst_ref, scale):
        zeros = jnp.zeros((SUBLANE, MLSTM_DH), F32)
        pad_ref[0:SUBLANE, :] = zeros
        pad_ref[SUBLANE + T:2 * SUBLANE + T, :] = zeros
        for r0 in range(0, T, blk):
            pad_ref[SUBLANE + r0:SUBLANE + r0 + blk, :] = src_ref[0, r0:r0 + blk, :].astype(F32)
        w = w_ref[...]
        for r0 in range(0, T, blk):
            y = (pad_ref[SUBLANE - 1 + r0:SUBLANE - 1 + r0 + blk, :] * w[0:1]
                 + pad_ref[SUBLANE + r0:SUBLANE + r0 + blk, :] * w[1:2]
                 + pad_ref[SUBLANE + 1 + r0:SUBLANE + 1 + r0 + blk, :] * w[2:3])
            dst_ref[r0:r0 + blk, :] = (y * jax.nn.sigmoid(y) * scale).astype(BF16)

    conv_silu(q_ref, wq_ref, qc_ref, 1.0)
    conv_silu(k_ref, wk_ref, kc_ref, MLSTM_DH ** -0.5)

    cf_ref[...] = cf0_ref[0, 0]
    cb_ref[...] = cb0_ref[0, 0]

    row = lax.broadcasted_iota(jnp.int32, (L, L), 0)
    col = lax.broadcasted_iota(jnp.int32, (L, L), 1)
    masks = (col <= row, col >= row)
    tris = tuple(jnp.where(m, 1.0, 0.0).astype(BF16) for m in masks)
    ones = jnp.ones((L, MLSTM_DH), BF16)
    shift = (LANE - head) % LANE

    def step(d, r, c_ref, m, h_ref):
        q = qc_ref[pl.ds(r, L), :]
        k = kc_ref[pl.ds(r, L), :]
        v_aug = jnp.concatenate([v_ref[0, pl.ds(r, L), :], ones], axis=1)
        g = pltpu.roll(g_ref[0, pl.ds(r, L), :] + bg_ref[...], shift, 1)
        lf = _log_sigmoid(g)
        hi = lf.astype(BF16)
        lo = (lf - hi.astype(F32)).astype(BF16)
        cum2 = jnp.dot(tris[d], jnp.concatenate([hi, lo], axis=1), preferred_element_type=F32)
        cum = cum2[:, :L] + cum2[:, L:]
        il, fl = 8 * d, 8 * d + 4
        b_col = cum[:, fl:fl + 1]
        i_col = g[:, il:il + 1]
        b_row = cum.T[fl:fl + 1, :]
        i_row = g.T[il:il + 1, :]
        total = b_row[:, L - 1:L] if d == 0 else b_row[:, 0:1]

        dm = jnp.where(masks[d], b_col + (i_row - b_row), NEG_BIG)
        inter = b_col + m
        m_j = jnp.maximum(inter, jnp.max(dm, axis=1, keepdims=True))
        w_in = jnp.exp(inter - m_j)
        s = lax.dot_general(q, k, (((1,), (1,)), ((), ())), preferred_element_type=F32) * jnp.exp(dm - m_j)
        c_old = c_ref[...]
        lhs = jnp.concatenate([(q.astype(F32) * w_in).astype(BF16), s.astype(BF16)], axis=1)
        rhs = jnp.concatenate([c_old.astype(BF16), v_aug], axis=0)
        nd = jnp.dot(lhs, rhs, preferred_element_type=F32)
        h_ref[pl.ds(r, L), :] = nd[:, :MLSTM_DH] / jnp.maximum(jnp.abs(nd[:, MLSTM_DH:]), jnp.exp(-m_j))

        g_end_col = total - b_col + i_col
        g_end_row = total - b_row + i_row
        m_new = jnp.maximum(total + m, jnp.max(g_end_row, axis=1, keepdims=True))
        w_prev = jnp.exp(total + m - m_new)
        kw = (k.astype(F32) * jnp.exp(g_end_col - m_new)).T.astype(BF16)
        c_ref[...] = w_prev * c_old + jnp.dot(kw, v_aug, preferred_element_type=F32)
        return m_new

    def body(i, carry):
        mf, mb = carry
        mf = step(0, pl.multiple_of(i * L, L), cf_ref, mf, hf_ref)
        mb = step(1, pl.multiple_of((nc - 1 - i) * L, L), cb_ref, mb, hb_ref)
        return mf, mb

    mf, mb = lax.fori_loop(0, nc, body, (mf0_ref[0, 0][0:1, 0:1], mb0_ref[0, 0][0:1, 0:1]))

    cf1_ref[0, 0] = cf_ref[...]
    cb1_ref[0, 0] = cb_ref[...]
    mf1_ref[0, 0] = jnp.broadcast_to(mf, (SUBLANE, LANE))
    mb1_ref[0, 0] = jnp.broadcast_to(mb, (SUBLANE, LANE))
    for r0 in range(0, T, blk):
        o = o_ref[0, r0:r0 + blk, :].astype(F32)
        y_ref[0, r0:r0 + blk, :] = (jax.nn.sigmoid(o) * (hf_ref[r0:r0 + blk, :] + hb_ref[r0:r0 + blk, :])
                                    ).astype(y_ref.dtype)


def _mlstm(z, zg, w_conv, b_gate, cf0, mf0, cb0, mb0):
    B, T, _ = z.shape
    H, dh = N_MLSTM, MLSTM_DH
    zcol = lambda off: (lambda b, h: (b, 0, off // dh + h))
    st = lambda b, h: (b, h, 0, 0)
    c_spec = pl.BlockSpec((1, 1, dh, 2 * dh), st)
    m_spec = pl.BlockSpec((1, 1, SUBLANE, LANE), st)
    c_shape = jax.ShapeDtypeStruct((B, H, dh, 2 * dh), F32)
    m_shape = jax.ShapeDtypeStruct((B, H, SUBLANE, LANE), F32)
    return pl.pallas_call(
        _mlstm_kernel,
        grid=(B, H),
        in_specs=[
            pl.BlockSpec((1, T, dh), zcol(OFF_Q)),
            pl.BlockSpec((1, T, dh), zcol(OFF_K)),
            pl.BlockSpec((1, T, dh), zcol(OFF_V)),
            pl.BlockSpec((1, T, dh), zcol(OFF_O)),
            pl.BlockSpec((1, T, LANE), lambda b, h: (b, 0, 0)),
            pl.BlockSpec((3, dh), lambda b, h: (0, h)),
            pl.BlockSpec((3, dh), lambda b, h: (0, H + h)),
            pl.BlockSpec((1, LANE), lambda b, h: (0, 0)),
            c_spec, m_spec, c_spec, m_spec,
        ],
        out_specs=[pl.BlockSpec((1, T, dh), lambda b, h: (b, 0, h)), c_spec, m_spec, c_spec, m_spec],
        out_shape=[jax.ShapeDtypeStruct((B, T, MIX_MLSTM), BF16), c_shape, m_shape, c_shape, m_shape],
        scratch_shapes=[
            pltpu.VMEM((T + 2 * SUBLANE, dh), F32),
            pltpu.VMEM((T, dh), BF16),
            pltpu.VMEM((T, dh), BF16),
            pltpu.VMEM((T, dh), F32),
            pltpu.VMEM((T, dh), F32),
            pltpu.VMEM((dh, 2 * dh), F32),
            pltpu.VMEM((dh, 2 * dh), F32),
        ],
        compiler_params=_params(2),
        name="mlstm",
    )(z, z, z, z, zg, w_conv, w_conv, b_gate, cf0, mf0, cb0, mb0)


def kernel(x, c, ctx, c_ctx, w_mod, b_mod, w_in, conv_qk, b_gates, pool_w, pool_scale, fourier_w,
           w_out, ln1_g, ln1_b, w_up, conv_ffn_w, conv_ffn_b, w_down, ln2_g, ln2_b):
    B, T, D = x.shape
    Tc = ctx.shape[1]
    tm_x = 512
    tm_c = Tc
    xc = ctx
    s_c = jax.nn.silu(c)
    s_ctx = jax.nn.silu(c_ctx)
    filt_x, filt_c = _pool_filters(True), _pool_filters(False)
    dft_x, dft_c = _dft_matrix(T), _dft_matrix(Tc)
    zero_c = jnp.zeros((B, N_MLSTM, MLSTM_DH, 2 * MLSTM_DH), F32)
    zero_m = jnp.zeros((B, N_MLSTM, SUBLANE, LANE), F32)
    for l in range(DEPTH):
        last = l == DEPTH - 1
        mx = [m[:, None, :] for m in jnp.split(s_c @ w_mod[l] + b_mod[l], 6, axis=-1)]
        mc = [jnp.broadcast_to(m[None, None, :], (B, 1, D))
              for m in jnp.split(s_ctx @ w_mod[l] + b_mod[l], 6, axis=-1)]
        w_main = w_in[l][:, :OFF_G].astype(BF16)
        w_gate = jnp.pad(w_in[l][:, OFF_G:], ((0, 0), (0, LANE - N_GATES))).astype(BF16)
        b_gate = jnp.pad(b_gates[l], (0, LANE - N_GATES))[None]
        w_pool_bd = _block_diag(pool_w[l]).astype(BF16)
        s_pool = pool_scale[l][None]
        w_out_b = w_out[l].astype(BF16)
        w_up_b = w_up[l].astype(BF16)
        w_down_b = w_down[l].astype(BF16)
        g1, b1 = ln1_g[l][None], ln1_b[l][None]
        g2, b2 = ln2_g[l][None], ln2_b[l][None]
        bcv = conv_ffn_b[l][None]

        zc, zgc = _proj_in(xc, mc[0], mc[1], w_main, w_gate, tm_c)
        ymc, cf, mf, cb, mb = _mlstm(zc, zgc, conv_qk[l], b_gate, zero_c, zero_m, zero_c, zero_m)
        zx, zgx = _proj_in(x, mx[0], mx[1], w_main, w_gate, tm_x)
        ymx = _mlstm(zx, zgx, conv_qk[l], b_gate, cf, mf, cb, mb)[0]
        ypx = _pool(zx, filt_x, w_pool_bd, s_pool, True)
        yfx = _fourier(zx, _fourier_mix_weights(fourier_w[l], T), dft_x)
        x = _proj_out(ypx, yfx, ymx, x, mx[2], w_out_b, g1, b1, tm_x)
        x = _ffn(x, mx[3], mx[4], mx[5], w_up_b, conv_ffn_w[l], bcv, w_down_b, g2, b2, tm_x)
        if not last:
            ypc = _pool(zc, filt_c, w_pool_bd, s_pool, False)
            yfc = _fourier(zc, _fourier_mix_weights(fourier_w[l], Tc), dft_c)
            xc = _proj_out(ypc, yfc, ymc, xc, mc[2], w_out_b, g1, b1, tm_c)
            xc = _ffn(xc, mc[3], mc[4], mc[5], w_up_b, conv_ffn_w[l], bcv, w_down_b, g2, b2, tm_c)
    return x
```

```python
import functools
import math

import jax
import jax.numpy as jnp
import numpy as np
from jax import lax
from jax.experimental import pallas as pl
from jax.experimental.pallas import tpu as pltpu

F32 = jnp.float32
BF16 = jnp.bfloat16

D_MODEL = 1024
DEPTH = 4
GRID_W = 64
GRID_SHIFT = GRID_W.bit_length() - 1
POOL_WINDOWS = (2, 4, 8, 16)
N_POOL = len(POOL_WINDOWS)
MIX_POOL = D_MODEL // 4
POOL_DIM = MIX_POOL // N_POOL
N_FOUR = 4
MIX_FOUR = D_MODEL // 4
FOUR_DIM = MIX_FOUR // N_FOUR
N_MLSTM = 4
MIX_MLSTM = D_MODEL // 2
MLSTM_DH = MIX_MLSTM // N_MLSTM
MLSTM_CHUNK = 128
N_GATES = 2 * 2 * N_MLSTM
D_FF = int(math.ceil(8 * D_MODEL / 3 / 128)) * 128
OFF_FOUR = MIX_POOL
OFF_O = OFF_FOUR + MIX_FOUR
OFF_Q = OFF_O + MIX_MLSTM
OFF_K = OFF_Q + MIX_MLSTM
OFF_V = OFF_K + MIX_MLSTM
OFF_G = OFF_V + MIX_MLSTM
ALPHA = (2 * DEPTH) ** 0.25
LN_EPS = 1e-6

LANE = 128
SUBLANE = 8
VMEM_LIMIT = 56 * 1024 * 1024
FF_CHUNK = 256
PROJ_CHUNK = 512
POOL_BLOCK = 256
POOL_PAD = (max(POOL_WINDOWS) // 2) * GRID_W
NEG_BIG = -1e30


def _params(n_axes):
    return pltpu.CompilerParams(dimension_semantics=("arbitrary",) * n_axes, vmem_limit_bytes=VMEM_LIMIT)


def _ln_rows(x):
    mu = jnp.mean(x, axis=-1, keepdims=True)
    xc = x - mu
    var = jnp.mean(xc * xc, axis=-1, keepdims=True)
    return xc * lax.rsqrt(var + LN_EPS)


def _proj_in_kernel(x_ref, sh_ref, sc_ref, w_ref, wg_ref, z_ref, g_ref):
    u = _ln_rows(x_ref[0]) * (1.0 + sc_ref[0]) + sh_ref[0]
    ub = u.astype(BF16)
    n = z_ref.shape[-1]
    for c0 in range(0, n, PROJ_CHUNK):
        c1 = min(c0 + PROJ_CHUNK, n)
        z_ref[0, :, c0:c1] = jnp.dot(ub, w_ref[:, c0:c1], preferred_element_type=F32).astype(z_ref.dtype)
    g_ref[0] = jnp.dot(ub, wg_ref[...], preferred_element_type=F32)


def _proj_in(x, shift, scale, w_main, w_gate, tm):
    B, T, D = x.shape
    n = w_main.shape[1]
    return pl.pallas_call(
        _proj_in_kernel,
        grid=(B, T // tm),
        in_specs=[
            pl.BlockSpec((1, tm, D), lambda b, i: (b, i, 0)),
            pl.BlockSpec((1, 1, D), lambda b, i: (b, 0, 0)),
            pl.BlockSpec((1, 1, D), lambda b, i: (b, 0, 0)),
            pl.BlockSpec((D, n), lambda b, i: (0, 0)),
            pl.BlockSpec((D, LANE), lambda b, i: (0, 0)),
        ],
        out_specs=[
            pl.BlockSpec((1, tm, n), lambda b, i: (b, i, 0)),
            pl.BlockSpec((1, tm, LANE), lambda b, i: (b, i, 0)),
        ],
        out_shape=[
            jax.ShapeDtypeStruct((B, T, n), BF16),
            jax.ShapeDtypeStruct((B, T, LANE), F32),
        ],
        compiler_params=_params(2),
        name="proj_in",
    )(x, shift, scale, w_main, w_gate)


def _proj_out_kernel(yp_ref, yf_ref, ym_ref, x_ref, gate_ref, wp_ref, wf_ref, wm_ref, lg_ref, lb_ref, o_ref):
    f = (jnp.dot(yp_ref[0], wp_ref[...], preferred_element_type=F32)
         + jnp.dot(yf_ref[0], wf_ref[...], preferred_element_type=F32)
         + jnp.dot(ym_ref[0], wm_ref[...], preferred_element_type=F32))
    r = ALPHA * x_ref[0] + gate_ref[0] * f
    o_ref[0] = _ln_rows(r) * lg_ref[...] + lb_ref[...]


def _proj_out(yp, yf, ym, x, gate, w, ln_g, ln_b, tm):
    B, T, D = x.shape
    tok = lambda b, i: (b, i, 0)
    return pl.pallas_call(
        _proj_out_kernel,
        grid=(B, T // tm),
        in_specs=[
            pl.BlockSpec((1, tm, MIX_POOL), tok),
            pl.BlockSpec((1, tm, MIX_FOUR), tok),
            pl.BlockSpec((1, tm, MIX_MLSTM), tok),
            pl.BlockSpec((1, tm, D), tok),
            pl.BlockSpec((1, 1, D), lambda b, i: (b, 0, 0)),
            pl.BlockSpec((MIX_POOL, D), lambda b, i: (0, 0)),
            pl.BlockSpec((MIX_FOUR, D), lambda b, i: (OFF_FOUR // MIX_FOUR, 0)),
            pl.BlockSpec((MIX_MLSTM, D), lambda b, i: (OFF_O // MIX_MLSTM, 0)),
            pl.BlockSpec((1, D), lambda b, i: (0, 0)),
            pl.BlockSpec((1, D), lambda b, i: (0, 0)),
        ],
        out_specs=pl.BlockSpec((1, tm, D), tok),
        out_shape=jax.ShapeDtypeStruct((B, T, D), F32),
        compiler_params=_params(2),
        name="proj_out",
    )(yp, yf, ym, x, gate, w, w, w, ln_g, ln_b)


def _gelu_tanh(x):
    c = math.sqrt(2.0 / math.pi)
    return 0.5 * x * (1.0 + jnp.tanh(c * (x + 0.044715 * (x * x * x))))


def _ffn_kernel(x_ref, xp_ref, xn_ref, sh_ref, sc_ref, gate_ref, wup_ref, wc_ref, bc_ref, wdn_ref,
                lg_ref, lb_ref, o_ref, ha_ref, hg_ref, p_ref):
    i = pl.program_id(1)
    tm = x_ref.shape[1]
    x = x_ref[0]
    mod_scale = 1.0 + sc_ref[0]
    mod_shift = sh_ref[0]
    has_prev = (i > 0).astype(F32)
    has_next = (i < pl.num_programs(1) - 1).astype(F32)
    u_prev = (_ln_rows(xp_ref[0]) * mod_scale + mod_shift) * has_prev
    u_next = (_ln_rows(xn_ref[0]) * mod_scale + mod_shift) * has_next
    u_main = _ln_rows(x) * mod_scale + mod_shift
    u_ext = jnp.concatenate([u_prev, u_main, u_next], axis=0).astype(BF16)

    def conv(h_ref, cols):
        w = wc_ref[:, cols]
        return (h_ref[SUBLANE - 1:SUBLANE - 1 + tm, :] * w[0:1]
                + h_ref[SUBLANE:SUBLANE + tm, :] * w[1:2]
                + h_ref[SUBLANE + 1:SUBLANE + 1 + tm, :] * w[2:3]
                + bc_ref[:, cols])

    for c in range(D_FF // FF_CHUNK):
        ca = slice(c * FF_CHUNK, (c + 1) * FF_CHUNK)
        cg = slice(D_FF + c * FF_CHUNK, D_FF + (c + 1) * FF_CHUNK)
        ha_ref[...] = jnp.dot(u_ext, wup_ref[:, ca], preferred_element_type=F32)
        hg_ref[...] = jnp.dot(u_ext, wup_ref[:, cg], preferred_element_type=F32)
        p_ref[:, ca] = (_gelu_tanh(conv(hg_ref, cg)) * conv(ha_ref, ca)).astype(BF16)

    f = jnp.dot(p_ref[...], wdn_ref[...], preferred_element_type=F32)
    r = ALPHA * x + gate_ref[0] * f
    o_ref[0] = _ln_rows(r) * lg_ref[...] + lb_ref[...]


def _ffn(x, shift, scale, gate, w_up, w_conv, b_conv, w_down, ln_g, ln_b, tm):
    B, T, D = x.shape
    nb = tm // SUBLANE
    last = T // SUBLANE - 1
    const = dict(pipeline_mode=pl.Buffered(1))
    return pl.pallas_call(
        _ffn_kernel,
        grid=(B, T // tm),
        in_specs=[
            pl.BlockSpec((1, tm, D), lambda b, i: (b, i, 0)),
            pl.BlockSpec((1, SUBLANE, D), lambda b, i: (b, jnp.maximum(i * nb - 1, 0), 0)),
            pl.BlockSpec((1, SUBLANE, D), lambda b, i: (b, jnp.minimum((i + 1) * nb, last), 0)),
            pl.BlockSpec((1, 1, D), lambda b, i: (b, 0, 0)),
            pl.BlockSpec((1, 1, D), lambda b, i: (b, 0, 0)),
            pl.BlockSpec((1, 1, D), lambda b, i: (b, 0, 0)),
            pl.BlockSpec((D, 2 * D_FF), lambda b, i: (0, 0), **const),
            pl.BlockSpec((3, 2 * D_FF), lambda b, i: (0, 0), **const),
            pl.BlockSpec((1, 2 * D_FF), lambda b, i: (0, 0), **const),
            pl.BlockSpec((D_FF, D), lambda b, i: (0, 0), **const),
            pl.BlockSpec((1, D), lambda b, i: (0, 0)),
            pl.BlockSpec((1, D), lambda b, i: (0, 0)),
        ],
        out_specs=pl.BlockSpec((1, tm, D), lambda b, i: (b, i, 0)),
        out_shape=jax.ShapeDtypeStruct((B, T, D), F32),
        scratch_shapes=[
            pltpu.VMEM((tm + 2 * SUBLANE, FF_CHUNK), F32),
            pltpu.VMEM((tm + 2 * SUBLANE, FF_CHUNK), F32),
            pltpu.VMEM((tm, D_FF), BF16),
        ],
        compiler_params=_params(2),
        name="conv_ffn",
    )(x, x, x, shift, scale, gate, w_up, w_conv, b_conv, w_down, ln_g, ln_b)


def _band_matrix(n, w):
    pos = np.arange(n)
    lo = np.maximum(pos - w // 2, 0)
    hi = np.minimum(pos + w // 2 - 1, n - 1)
    return ((pos[None, :] >= lo[:, None]) & (pos[None, :] <= hi[:, None])).astype(np.float32)


def _pool_filters(grid2d):
    mats = []
    for w in POOL_WINDOWS:
        if grid2d:
            mats.append(np.kron(np.eye(POOL_BLOCK // GRID_W, dtype=np.float32), _band_matrix(GRID_W, w)))
        else:
            mats.append(_band_matrix(POOL_BLOCK, w))
    return jnp.asarray(np.stack(mats), BF16)


def _window_count(idx, half, n):
    return jnp.minimum(idx + (half - 1), n - 1) - jnp.maximum(idx - half, 0) + 1


def _pool_kernel(z_ref, a_ref, w_ref, s_ref, y_ref, acc_ref, *, grid2d):
    T = z_ref.shape[1]
    nblk = T // POOL_BLOCK
    lane = lax.broadcasted_iota(jnp.int32, (POOL_BLOCK, MIX_POOL), 1)
    group = jnp.right_shift(lane, GRID_SHIFT)
    half = jnp.left_shift(1, group)

    def col_filter(blk):
        r0 = pl.multiple_of(blk * POOL_BLOCK, POOL_BLOCK)
        xb = z_ref[0, pl.ds(r0, POOL_BLOCK), :]
        out = jnp.dot(a_ref[0], xb, preferred_element_type=F32)
        for g in range(1, N_POOL):
            out = jnp.where(group == g, jnp.dot(a_ref[g], xb, preferred_element_type=F32), out)
        return r0, xb, out

    def finish(r0, xb, box, count):
        p = box / count.astype(F32) - xb.astype(F32)
        y = jnp.dot(p.astype(BF16), w_ref[...], preferred_element_type=F32) * s_ref[...]
        y_ref[0, pl.ds(r0, POOL_BLOCK), :] = y.astype(y_ref.dtype)

    if not grid2d:
        assert T == POOL_BLOCK
        r0, xb, box = col_filter(0)
        pos = lax.broadcasted_iota(jnp.int32, (POOL_BLOCK, MIX_POOL), 0)
        finish(r0, xb, box, _window_count(pos, half, T))
        return

    rows = T // GRID_W
    zeros = jnp.zeros((POOL_PAD, MIX_POOL), F32)
    acc_ref[0:POOL_PAD, :] = zeros
    acc_ref[POOL_PAD + T:POOL_PAD + T + POOL_PAD, :] = zeros

    def phase1(blk, carry):
        r0, _, out = col_filter(blk)
        acc_ref[pl.ds(POOL_PAD + r0, POOL_BLOCK), :] = out
        return carry

    lax.fori_loop(0, nblk, phase1, 0)

    lane_t = lax.broadcasted_iota(jnp.int32, (POOL_BLOCK, LANE), 1)

    def phase2(blk, carry):
        r0 = pl.multiple_of(blk * POOL_BLOCK, POOL_BLOCK)
        base = POOL_PAD + r0

        def slab(k, tile):
            return acc_ref[pl.ds(base + k * GRID_W, POOL_BLOCK), tile * LANE:(tile + 1) * LANE]

        s2 = slab(-1, 0) + slab(0, 0)
        s4 = s2 + slab(-2, 0) + slab(1, 0)
        s8 = slab(-4, 1)
        for k in range(-3, 4):
            s8 = s8 + slab(k, 1)
        s16 = s8
        for k in list(range(-8, -4)) + list(range(4, 8)):
            s16 = s16 + slab(k, 1)
        box = jnp.concatenate([jnp.where(lane_t < POOL_DIM, s2, s4), jnp.where(lane_t < POOL_DIM, s8, s16)], axis=1)
        pos = r0 + lax.broadcasted_iota(jnp.int32, (POOL_BLOCK, MIX_POOL), 0)
        count = (_window_count(jnp.right_shift(pos, GRID_SHIFT), half, rows)
                 * _window_count(jnp.bitwise_and(pos, GRID_W - 1), half, GRID_W))
        xb = z_ref[0, pl.ds(r0, POOL_BLOCK), :]
        finish(r0, xb, box, count)
        return carry

    lax.fori_loop(0, nblk, phase2, 0)


def _pool(z, filters, w_bd, s_row, grid2d):
    B, T, _ = z.shape
    return pl.pallas_call(
        functools.partial(_pool_kernel, grid2d=grid2d),
        grid=(B,),
        in_specs=[
            pl.BlockSpec((1, T, MIX_POOL), lambda b: (b, 0, 0)),
            pl.BlockSpec((N_POOL, POOL_BLOCK, POOL_BLOCK), lambda b: (0, 0, 0)),
            pl.BlockSpec((MIX_POOL, MIX_POOL), lambda b: (0, 0)),
            pl.BlockSpec((1, MIX_POOL), lambda b: (0, 0)),
        ],
        out_specs=pl.BlockSpec((1, T, MIX_POOL), lambda b: (b, 0, 0)),
        out_shape=jax.ShapeDtypeStruct((B, T, MIX_POOL), BF16),
        scratch_shapes=[pltpu.VMEM((T + 2 * POOL_PAD, MIX_POOL), F32)],
        compiler_params=_params(1),
        name="pool2d" if grid2d else "pool1d",
    )(z, filters, w_bd, s_row)


def _block_diag(w):
    G, n, m = w.shape
    eye = jnp.eye(G, dtype=w.dtype)
    return (eye[:, None, :, None] * w[:, :, None, :]).reshape(G * n, G * m)


def _four_mix_kernel(z_ref, g_ref, y_ref):
    y = jnp.dot(z_ref[0], g_ref[...], preferred_element_type=F32).astype(y_ref.dtype)
    y_ref[0, 0] = y[:, :MIX_FOUR]
    y_ref[0, 1] = y[:, MIX_FOUR:]


def _four_dft_kernel(d_ref, y_ref, o_ref):
    rows = 256
    for r0 in range(0, o_ref.shape[1], rows):
        o_ref[0, r0:r0 + rows, :] = jnp.dot(d_ref[r0:r0 + rows, :], y_ref[0],
                                            preferred_element_type=F32).astype(o_ref.dtype)


def _fourier(z, g_mix, dft):
    B, T, _ = z.shape
    tm = min(T, 512)
    y = pl.pallas_call(
        _four_mix_kernel,
        grid=(B, T // tm),
        in_specs=[
            pl.BlockSpec((1, tm, MIX_FOUR), lambda b, i: (b, i, OFF_FOUR // MIX_FOUR)),
            pl.BlockSpec((MIX_FOUR, 2 * MIX_FOUR), lambda b, i: (0, 0)),
        ],
        out_specs=pl.BlockSpec((1, 2, tm, MIX_FOUR), lambda b, i: (b, 0, i, 0)),
        out_shape=jax.ShapeDtypeStruct((B, 2, T, MIX_FOUR), BF16),
        compiler_params=_params(2),
        name="fourier_mix",
    )(z, g_mix)
    y = y.reshape(B, 2 * T, MIX_FOUR)
    tf = min(T, 1024)
    return pl.pallas_call(
        _four_dft_kernel,
        grid=(T // tf, B),
        in_specs=[
            pl.BlockSpec((tf, 2 * T), lambda m, b: (m, 0)),
            pl.BlockSpec((1, 2 * T, MIX_FOUR), lambda m, b: (b, 0, 0)),
        ],
        out_specs=pl.BlockSpec((1, tf, MIX_FOUR), lambda m, b: (b, m, 0)),
        out_shape=jax.ShapeDtypeStruct((B, T, MIX_FOUR), BF16),
        compiler_params=_params(2),
        name="fourier_dft",
    )(dft, y)


def _dft_matrix(n):
    f = lax.broadcasted_iota(jnp.int32, (n, n), 0)
    t = lax.broadcasted_iota(jnp.int32, (n, n), 1)
    ang = ((f * t) % n).astype(F32) * (2.0 * math.pi / n)
    return jnp.concatenate([jnp.cos(ang), jnp.sin(ang)], axis=1).astype(BF16)


def _fourier_mix_weights(w_four, T):
    c = np.arange(FOUR_DIM)
    ang = 2.0 * np.pi * np.outer(c, c) / FOUR_DIM
    norm = 1.0 / math.sqrt(T * FOUR_DIM)
    gr = jnp.einsum('cf,gfe->gce', jnp.asarray(np.cos(ang) * norm, F32), w_four, precision='highest')
    gi = jnp.einsum('cf,gfe->gce', jnp.asarray(-np.sin(ang) * norm, F32), w_four, precision='highest')
    return jnp.concatenate([_block_diag(gr), _block_diag(gi)], axis=1).astype(BF16)


def _log_sigmoid(x):
    return jnp.minimum(x, 0.0) - jnp.log(1.0 + jnp.exp(-jnp.abs(x)))


def _sigmoid(x):
    return 0.5 * jnp.tanh(0.5 * x) + 0.5


def _mlstm_kernel(q_ref, k_ref, v_ref, o_ref, g_ref, wq_ref, wk_ref, bg_ref, cf0_ref, mf0_ref, cb0_ref, mb0_ref,
                  y_ref, cf1_ref, mf1_ref, cb1_ref, mb1_ref,
                  pad_ref, qc_ref, kc_ref, hf_ref, hb_ref, pf_ref, pb_ref, uf_ref, ub_ref, vf_ref, vb_ref,
                  sf_ref, sb_ref):
    T = q_ref.shape[1]
    L = MLSTM_CHUNK
    dh = MLSTM_DH
    nc = T // L
    blk = min(T, 512)
    head = pl.program_id(1)

    def conv_silu(src_ref, w_ref, dst_ref, scale):
        zeros = jnp.zeros((SUBLANE, dh), F32)
        pad_ref[0:SUBLANE, :] = zeros
        pad_ref[SUBLANE + T:2 * SUBLANE + T, :] = zeros
        for r0 in range(0, T, blk):
            pad_ref[SUBLANE + r0:SUBLANE + r0 + blk, :] = src_ref[0, r0:r0 + blk, :].astype(F32)
        w = w_ref[...]
        for r0 in range(0, T, blk):
            y = (pad_ref[SUBLANE - 1 + r0:SUBLANE - 1 + r0 + blk, :] * w[0:1]
                 + pad_ref[SUBLANE + r0:SUBLANE + r0 + blk, :] * w[1:2]
                 + pad_ref[SUBLANE + 1 + r0:SUBLANE + 1 + r0 + blk, :] * w[2:3])
            dst_ref[r0:r0 + blk, :] = (y * _sigmoid(y) * scale).astype(BF16)

    conv_silu(q_ref, wq_ref, qc_ref, 1.0)
    conv_silu(k_ref, wk_ref, kc_ref, dh ** -0.5)

    row = lax.broadcasted_iota(jnp.int32, (L, L), 0)
    col = lax.broadcasted_iota(jnp.int32, (L, L), 1)
    masks = (col <= row, col >= row)
    tris = tuple(jnp.where(m, 1.0, 0.0).astype(BF16) for m in masks)
    ones = jnp.ones((L, dh), BF16)
    shift = (LANE - head) % LANE
    p_refs, u_refs, vec_refs, sc_refs, h_refs = ((pf_ref, pb_ref), (uf_ref, ub_ref), (vf_ref, vb_ref),
                                                 (sf_ref, sb_ref), (hf_ref, hb_ref))

    def prep(d, c):
        r = pl.multiple_of(c * L, L)
        q = qc_ref[pl.ds(r, L), :]
        k = kc_ref[pl.ds(r, L), :]
        v_aug = jnp.concatenate([v_ref[0, pl.ds(r, L), :], ones], axis=1)
        g = pltpu.roll(g_ref[0, pl.ds(r, L), :] + bg_ref[...], shift, 1)
        lf = _log_sigmoid(g)
        hi = lf.astype(BF16)
        lo = (lf - hi.astype(F32)).astype(BF16)
        cum2 = jnp.dot(tris[d], jnp.concatenate([hi, lo], axis=1), preferred_element_type=F32)
        cum = cum2[:, :L] + cum2[:, L:]
        il, fl = 8 * d, 8 * d + 4
        b_col = cum[:, fl:fl + 1]
        i_col = g[:, il:il + 1]
        b_row = cum.T[fl:fl + 1, :]
        i_row = g.T[il:il + 1, :]
        total = b_row[:, L - 1:L] if d == 0 else b_row[:, 0:1]

        dm = jnp.where(masks[d], b_col + (i_row - b_row), NEG_BIG)
        a_j = jnp.max(dm, axis=1, keepdims=True)
        s = lax.dot_general(q, k, (((1,), (1,)), ((), ())), preferred_element_type=F32) * jnp.exp(dm - a_j)
        p_refs[d][pl.ds(r, L), :] = jnp.dot(s.astype(BF16), v_aug, preferred_element_type=F32)
        g_end_col = total - b_col + i_col
        g_end_row = total - b_row + i_row
        g_loc = jnp.max(g_end_row, axis=1, keepdims=True)
        kw = (k.astype(F32) * jnp.exp(g_end_col - g_loc)).T.astype(BF16)
        u_refs[d][pl.ds(r, L), :] = jnp.dot(kw, v_aug, preferred_element_type=F32)
        vec_refs[d][pl.ds(r, L), 0:1] = b_col
        vec_refs[d][pl.ds(r, L), 1:2] = a_j
        rs = pl.multiple_of(c * SUBLANE, SUBLANE)
        sc_refs[d][pl.ds(rs, SUBLANE), :] = jnp.concatenate(
            [jnp.broadcast_to(total, (SUBLANE, LANE // 2)), jnp.broadcast_to(g_loc, (SUBLANE, LANE // 2))], axis=1)

    ga = min(nc, 4)

    def body_a(i, carry):
        for j in range(ga):
            prep(0, i * ga + j)
            prep(1, i * ga + j)
        return carry

    lax.fori_loop(0, nc // ga, body_a, 0)

    def step(d, c, c_st, m):
        r = pl.multiple_of(c * L, L)
        rs = pl.multiple_of(c * SUBLANE, SUBLANE)
        b_col = vec_refs[d][pl.ds(r, L), 0:1]
        a_j = vec_refs[d][pl.ds(r, L), 1:2]
        sc = sc_refs[d][pl.ds(rs, 1), :]
        total, g_loc = sc[:, 0:1], sc[:, LANE // 2:LANE // 2 + 1]
        inter = b_col + m
        m_j = jnp.maximum(inter, a_j)
        qc = jnp.dot(qc_ref[pl.ds(r, L), :], c_st.astype(BF16), preferred_element_type=F32)
        nd = jnp.exp(inter - m_j) * qc + jnp.exp(a_j - m_j) * p_refs[d][pl.ds(r, L), :]
        h_refs[d][pl.ds(r, L), :] = nd[:, :dh] / jnp.maximum(jnp.abs(nd[:, dh:]), jnp.exp(-m_j))
        m_new = jnp.maximum(total + m, g_loc)
        c_new = jnp.exp(total + m - m_new) * c_st + jnp.exp(g_loc - m_new) * u_refs[d][pl.ds(r, L), :]
        return c_new, m_new

    gb = min(nc, 2)

    def body_b(i, carry):
        cf, mf, cb, mb = carry
        for j in range(gb):
            cf, mf = step(0, i * gb + j, cf, mf)
            cb, mb = step(1, nc - 1 - (i * gb + j), cb, mb)
        return cf, mf, cb, mb

    cf, mf, cb, mb = lax.fori_loop(
        0, nc // gb, body_b,
        (cf0_ref[0, 0], mf0_ref[0, 0][0:1, 0:1], cb0_ref[0, 0], mb0_ref[0, 0][0:1, 0:1]))

    cf1_ref[0, 0] = cf
    cb1_ref[0, 0] = cb
    mf1_ref[0, 0] = jnp.broadcast_to(mf, (SUBLANE, LANE))
    mb1_ref[0, 0] = jnp.broadcast_to(mb, (SUBLANE, LANE))
    for r0 in range(0, T, blk):
        o = o_ref[0, r0:r0 + blk, :].astype(F32)
        y_ref[0, r0:r0 + blk, :] = (_sigmoid(o) * (hf_ref[r0:r0 + blk, :] + hb_ref[r0:r0 + blk, :])
                                    ).astype(y_ref.dtype)


def _mlstm(z, zg, w_conv, b_gate, cf0, mf0, cb0, mb0):
    B, T, _ = z.shape
    H, dh = N_MLSTM, MLSTM_DH
    zcol = lambda off: (lambda b, h: (b, 0, off // dh + h))
    st = lambda b, h: (b, h, 0, 0)
    c_spec = pl.BlockSpec((1, 1, dh, 2 * dh), st)
    m_spec = pl.BlockSpec((1, 1, SUBLANE, LANE), st)
    c_shape = jax.ShapeDtypeStruct((B, H, dh, 2 * dh), F32)
    m_shape = jax.ShapeDtypeStruct((B, H, SUBLANE, LANE), F32)
    return pl.pallas_call(
        _mlstm_kernel,
        grid=(B, H),
        in_specs=[
            pl.BlockSpec((1, T, dh), zcol(OFF_Q)),
            pl.BlockSpec((1, T, dh), zcol(OFF_K)),
            pl.BlockSpec((1, T, dh), zcol(OFF_V)),
            pl.BlockSpec((1, T, dh), zcol(OFF_O)),
            pl.BlockSpec((1, T, LANE), lambda b, h: (b, 0, 0)),
            pl.BlockSpec((3, dh), lambda b, h: (0, h)),
            pl.BlockSpec((3, dh), lambda b, h: (0, H + h)),
            pl.BlockSpec((1, LANE), lambda b, h: (0, 0)),
            c_spec, m_spec, c_spec, m_spec,
        ],
        out_specs=[pl.BlockSpec((1, T, dh), lambda b, h: (b, 0, h)), c_spec, m_spec, c_spec, m_spec],
        out_shape=[jax.ShapeDtypeStruct((B, T, MIX_MLSTM), BF16), c_shape, m_shape, c_shape, m_shape],
        scratch_shapes=[
            pltpu.VMEM((T + 2 * SUBLANE, dh), F32),
            pltpu.VMEM((T, dh), BF16),
            pltpu.VMEM((T, dh), BF16),
            pltpu.VMEM((T, dh), F32),
            pltpu.VMEM((T, dh), F32),
            pltpu.VMEM((T, 2 * dh), F32),
            pltpu.VMEM((T, 2 * dh), F32),
            pltpu.VMEM((T, 2 * dh), F32),
            pltpu.VMEM((T, 2 * dh), F32),
            pltpu.VMEM((T, LANE), F32),
            pltpu.VMEM((T, LANE), F32),
            pltpu.VMEM((T // MLSTM_CHUNK * SUBLANE, LANE), F32),
            pltpu.VMEM((T // MLSTM_CHUNK * SUBLANE, LANE), F32),
        ],
        compiler_params=_params(2),
        name="mlstm",
    )(z, z, z, z, zg, w_conv, w_conv, b_gate, cf0, mf0, cb0, mb0)


MLSTM_GROUP = 4


def _split_hi_lo(x):
    hi = x.astype(BF16)
    return hi, (x - hi.astype(F32)).astype(BF16)


def _mlstm_t_kernel(q_ref, k_ref, v_ref, o_ref, g_ref, wq_ref, wk_ref, bg_ref, cf0_ref, mf0_ref, cb0_ref, mb0_ref,
                    y_ref, cf1_ref, mf1_ref, cb1_ref, mb1_ref,
                    pad_ref, qc_ref, kc_ref, hf_ref, hb_ref, pf_ref, pb_ref, uf_ref, ub_ref, rf_ref, rb_ref):
    T = q_ref.shape[1]
    L = MLSTM_CHUNK
    dh = MLSTM_DH
    nc = T // L
    blk = min(T, 512)
    head = pl.program_id(1)

    def conv_silu(src_ref, w_ref, dst_ref, scale):
        zeros = jnp.zeros((SUBLANE, dh), F32)
        pad_ref[0:SUBLANE, :] = zeros
        pad_ref[SUBLANE + T:2 * SUBLANE + T, :] = zeros
        for r0 in range(0, T, blk):
            pad_ref[SUBLANE + r0:SUBLANE + r0 + blk, :] = src_ref[0, r0:r0 + blk, :].astype(F32)
        w = w_ref[...]
        for r0 in range(0, T, blk):
            y = (pad_ref[SUBLANE - 1 + r0:SUBLANE - 1 + r0 + blk, :] * w[0:1]
                 + pad_ref[SUBLANE + r0:SUBLANE + r0 + blk, :] * w[1:2]
                 + pad_ref[SUBLANE + 1 + r0:SUBLANE + 1 + r0 + blk, :] * w[2:3])
            dst_ref[r0:r0 + blk, :] = (y * _sigmoid(y) * scale).astype(BF16)

    conv_silu(q_ref, wq_ref, qc_ref, 1.0)
    conv_silu(k_ref, wk_ref, kc_ref, dh ** -0.5)

    row = lax.broadcasted_iota(jnp.int32, (L, L), 0)
    col = lax.broadcasted_iota(jnp.int32, (L, L), 1)
    masks = (row <= col, row >= col)
    tris = tuple(jnp.where(m, 1.0, 0.0).astype(BF16) for m in (col <= row, col >= row))
    diag = row == col
    sub8 = lax.broadcasted_iota(jnp.int32, (SUBLANE, L), 0)
    ones_t = jnp.ones((dh, L), BF16)
    sel_row = jnp.bitwise_and(lax.broadcasted_iota(jnp.int32, (2 * LANE, 4 * LANE), 0), LANE - 1)
    sel_col = jnp.right_shift(lax.broadcasted_iota(jnp.int32, (2 * LANE, 4 * LANE), 1), LANE.bit_length() - 1)
    sel = jnp.where(sel_row == sel_col * N_MLSTM + head, 1.0, 0.0).astype(BF16)
    p_refs, u_refs, r_refs, h_refs = (pf_ref, pb_ref), (uf_ref, ub_ref), (rf_ref, rb_ref), (hf_ref, hb_ref)
    bias = bg_ref[...]

    def prep_group(c0):
        cs = [c0 + j for j in range(min(MLSTM_GROUP, nc))]
        rs = [pl.multiple_of(c * L, L) for c in cs]
        q = [qc_ref[pl.ds(r, L), :] for r in rs]
        k = [kc_ref[pl.ds(r, L), :] for r in rs]
        vt = [jnp.concatenate([v_ref[0, pl.ds(r, L), :].astype(F32).T.astype(BF16), ones_t], axis=0) for r in rs]
        gsp = [jnp.concatenate(_split_hi_lo(g_ref[0, pl.ds(r, L), :] + bias), axis=1) for r in rs]
        rep = [jnp.dot(g, sel, preferred_element_type=F32) for g in gsp]
        st = [lax.dot_general(kk, qq, (((1,), (1,)), ((), ())), preferred_element_type=F32)
              for kk, qq in zip(k, q)]
        items = [(j, d) for j in range(len(cs)) for d in range(2)]
        i_rep = {(j, d): rep[j][:, (2 * d) * L:(2 * d + 1) * L] for j, d in items}
        lf = {(j, d): _log_sigmoid(rep[j][:, (2 * d + 1) * L:(2 * d + 2) * L]) for j, d in items}
        b2 = {(j, d): jnp.dot(tris[d], jnp.concatenate(_split_hi_lo(lf[j, d]), axis=1),
                              preferred_element_type=F32) for j, d in items}
        for j, d in items:
            b_rep = b2[j, d][:, :L] + b2[j, d][:, L:]
            total = b_rep[L - 1:L, :] if d == 0 else b_rep[0:1, :]
            r_rep = i_rep[j, d] - b_rep
            dmt = jnp.where(masks[d], r_rep, NEG_BIG)
            pm = jnp.max(dmt, axis=0, keepdims=True)
            sg = (st[j] * jnp.exp(dmt - pm)).astype(BF16)
            b_row = jnp.sum(jnp.where(diag, b_rep, 0.0), axis=0, keepdims=True)
            g_end = total + r_rep
            g_loc = jnp.max(g_end, axis=0, keepdims=True)
            kw = (k[j].astype(F32) * jnp.exp(g_end - g_loc)).astype(BF16)
            base = pl.multiple_of(cs[j] * 2 * dh, 2 * dh)
            p_refs[d][pl.ds(base, 2 * dh), :] = jnp.dot(vt[j], sg, preferred_element_type=F32)
            u_refs[d][pl.ds(base, 2 * dh), :] = jnp.dot(vt[j], kw, preferred_element_type=F32)
            rbase = pl.multiple_of(cs[j] * SUBLANE, SUBLANE)
            r_refs[d][pl.ds(rbase, SUBLANE), :] = jnp.where(
                sub8 == 0, b_row, jnp.where(sub8 == 1, b_row + pm, jnp.where(sub8 == 2, total, g_loc)))

    ga = min(MLSTM_GROUP, nc)

    def body_a(i, carry):
        prep_group(i * ga)
        return carry

    lax.fori_loop(0, nc // ga, body_a, 0)

    def step(d, c, ct, m):
        base = pl.multiple_of(c * 2 * dh, 2 * dh)
        rows = r_refs[d][pl.ds(pl.multiple_of(c * SUBLANE, SUBLANE), SUBLANE), :]
        b_row, a_row, total, g_loc = rows[0:1], rows[1:2], rows[2:3], rows[3:4]
        inter = b_row + m
        m_j = jnp.maximum(inter, a_row)
        qct = lax.dot_general(ct.astype(BF16), qc_ref[pl.ds(pl.multiple_of(c * L, L), L), :],
                              (((1,), (1,)), ((), ())), preferred_element_type=F32)
        nd = jnp.exp(inter - m_j) * qct + jnp.exp(a_row - m_j) * p_refs[d][pl.ds(base, 2 * dh), :]
        h_refs[d][pl.ds(pl.multiple_of(c * dh, dh), dh), :] = (
            nd[:dh] / jnp.maximum(jnp.abs(nd[dh:]), jnp.exp(-m_j)))
        m_new = jnp.maximum(total + m, g_loc)
        ct_new = jnp.exp(total + m - m_new) * ct + jnp.exp(g_loc - m_new) * u_refs[d][pl.ds(base, 2 * dh), :]
        return ct_new, m_new

    gb = min(nc, 2)

    def body_b(i, carry):
        cf, mf, cb, mb = carry
        for j in range(gb):
            cf, mf = step(0, i * gb + j, cf, mf)
            cb, mb = step(1, nc - 1 - (i * gb + j), cb, mb)
        return cf, mf, cb, mb

    cf, mf, cb, mb = lax.fori_loop(
        0, nc // gb, body_b, (cf0_ref[0, 0], mf0_ref[0, 0][0:1, :], cb0_ref[0, 0], mb0_ref[0, 0][0:1, :]))

    cf1_ref[0, 0] = cf
    cb1_ref[0, 0] = cb
    mf1_ref[0, 0] = jnp.broadcast_to(mf, (SUBLANE, LANE))
    mb1_ref[0, 0] = jnp.broadcast_to(mb, (SUBLANE, LANE))

    def body_out(c, carry):
        r = pl.multiple_of(c * L, L)
        rh = pl.multiple_of(c * dh, dh)
        h = (hf_ref[pl.ds(rh, dh), :] + hb_ref[pl.ds(rh, dh), :]).T
        y_ref[0, pl.ds(r, L), :] = (_sigmoid(o_ref[0, pl.ds(r, L), :].astype(F32)) * h).astype(y_ref.dtype)
        return carry

    lax.fori_loop(0, nc, body_out, 0)


def _mlstm_t(z, zg, w_conv, b_gate, cf0, mf0, cb0, mb0):
    B, T, _ = z.shape
    H, dh, L = N_MLSTM, MLSTM_DH, MLSTM_CHUNK
    nc = T // L
    zcol = lambda off: (lambda b, h: (b, 0, off // dh + h))
    st = lambda b, h: (b, h, 0, 0)
    c_spec = pl.BlockSpec((1, 1, 2 * dh, dh), st)
    m_spec = pl.BlockSpec((1, 1, SUBLANE, LANE), st)
    c_shape = jax.ShapeDtypeStruct((B, H, 2 * dh, dh), F32)
    m_shape = jax.ShapeDtypeStruct((B, H, SUBLANE, LANE), F32)
    return pl.pallas_call(
        _mlstm_t_kernel,
        grid=(B, H),
        in_specs=[
            pl.BlockSpec((1, T, dh), zcol(OFF_Q)),
            pl.BlockSpec((1, T, dh), zcol(OFF_K)),
            pl.BlockSpec((1, T, dh), zcol(OFF_V)),
            pl.BlockSpec((1, T, dh), zcol(OFF_O)),
            pl.BlockSpec((1, T, LANE), lambda b, h: (b, 0, 0)),
            pl.BlockSpec((3, dh), lambda b, h: (0, h)),
            pl.BlockSpec((3, dh), lambda b, h: (0, H + h)),
            pl.BlockSpec((1, LANE), lambda b, h: (0, 0)),
            c_spec, m_spec, c_spec, m_spec,
        ],
        out_specs=[pl.BlockSpec((1, T, dh), lambda b, h: (b, 0, h)), c_spec, m_spec, c_spec, m_spec],
        out_shape=[jax.ShapeDtypeStruct((B, T, MIX_MLSTM), BF16), c_shape, m_shape, c_shape, m_shape],
        scratch_shapes=[
            pltpu.VMEM((T + 2 * SUBLANE, dh), F32),
            pltpu.VMEM((T, dh), BF16),
            pltpu.VMEM((T, dh), BF16),
            pltpu.VMEM((nc * dh, L), F32),
            pltpu.VMEM((nc * dh, L), F32),
            pltpu.VMEM((nc * 2 * dh, L), F32),
            pltpu.VMEM((nc * 2 * dh, L), F32),
            pltpu.VMEM((nc * 2 * dh, dh), F32),
            pltpu.VMEM((nc * 2 * dh, dh), F32),
            pltpu.VMEM((nc * SUBLANE, LANE), F32),
            pltpu.VMEM((nc * SUBLANE, LANE), F32),
        ],
        compiler_params=_params(2),
        name="mlstm",
    )(z, z, z, z, zg, w_conv, w_conv, b_gate, cf0, mf0, cb0, mb0)


def kernel(x, c, ctx, c_ctx, w_mod, b_mod, w_in, conv_qk, b_gates, pool_w, pool_scale, fourier_w,
           w_out, ln1_g, ln1_b, w_up, conv_ffn_w, conv_ffn_b, w_down, ln2_g, ln2_b):
    B, T, D = x.shape
    Tc = ctx.shape[1]
    tm_x = 512
    tm_c = Tc
    xc = ctx
    s_c = jax.nn.silu(c)
    s_ctx = jax.nn.silu(c_ctx)
    filt_x, filt_c = _pool_filters(True), _pool_filters(False)
    dft_x, dft_c = _dft_matrix(T), _dft_matrix(Tc)
    zero_c = jnp.zeros((B, N_MLSTM, 2 * MLSTM_DH, MLSTM_DH), F32)
    zero_m = jnp.zeros((B, N_MLSTM, SUBLANE, LANE), F32)
    for l in range(DEPTH):
        last = l == DEPTH - 1
        mx = [m[:, None, :] for m in jnp.split(s_c @ w_mod[l] + b_mod[l], 6, axis=-1)]
        mc = [jnp.broadcast_to(m[None, None, :], (B, 1, D))
              for m in jnp.split(s_ctx @ w_mod[l] + b_mod[l], 6, axis=-1)]
        w_main = w_in[l][:, :OFF_G].astype(BF16)
        w_gate = jnp.pad(w_in[l][:, OFF_G:], ((0, 0), (0, LANE - N_GATES))).astype(BF16)
        b_gate = jnp.pad(b_gates[l], (0, LANE - N_GATES))[None]
        w_pool_bd = _block_diag(pool_w[l]).astype(BF16)
        s_pool = pool_scale[l][None]
        w_out_b = w_out[l].astype(BF16)
        w_up_b = w_up[l].astype(BF16)
        w_down_b = w_down[l].astype(BF16)
        g1, b1 = ln1_g[l][None], ln1_b[l][None]
        g2, b2 = ln2_g[l][None], ln2_b[l][None]
        bcv = conv_ffn_b[l][None]

        zc, zgc = _proj_in(xc, mc[0], mc[1], w_main, w_gate, tm_c)
        ymc, cf, mf, cb, mb = _mlstm_t(zc, zgc, conv_qk[l], b_gate, zero_c, zero_m, zero_c, zero_m)
        zx, zgx = _proj_in(x, mx[0], mx[1], w_main, w_gate, tm_x)
        ymx = _mlstm_t(zx, zgx, conv_qk[l], b_gate, cf, mf, cb, mb)[0]
        ypx = _pool(zx, filt_x, w_pool_bd, s_pool, True)
        yfx = _fourier(zx, _fourier_mix_weights(fourier_w[l], T), dft_x)
        x = _proj_out(ypx, yfx, ymx, x, mx[2], w_out_b, g1, b1, tm_x)
        x = _ffn(x, mx[3], mx[4], mx[5], w_up_b, conv_ffn_w[l], bcv, w_down_b, g2, b2, tm_x)
        if not last:
            ypc = _pool(zc, filt_c, w_pool_bd, s_pool, False)
            yfc = _fourier(zc, _fourier_mix_weights(fourier_w[l], Tc), dft_c)
            xc = _proj_out(ypc, yfc, ymc, xc, mc[2], w_out_b, g1, b1, tm_c)
            xc = _ffn(xc, mc[3], mc[4], mc[5], w_up_b, conv_ffn_w[l], bcv, w_down_b, g2, b2, tm_c)
    return x
```

```python
import functools
import math

import jax
import jax.numpy as jnp
import numpy as np
from jax import lax
from jax.experimental import pallas as pl
from jax.experimental.pallas import tpu as pltpu

F32 = jnp.float32
BF16 = jnp.bfloat16

D_MODEL = 1024
DEPTH = 4
GRID_W = 64
GRID_SHIFT = GRID_W.bit_length() - 1
POOL_WINDOWS = (2, 4, 8, 16)
N_POOL = len(POOL_WINDOWS)
MIX_POOL = D_MODEL // 4
POOL_DIM = MIX_POOL // N_POOL
N_FOUR = 4
MIX_FOUR = D_MODEL // 4
FOUR_DIM = MIX_FOUR // N_FOUR
N_MLSTM = 4
MIX_MLSTM = D_MODEL // 2
MLSTM_DH = MIX_MLSTM // N_MLSTM
MLSTM_CHUNK = 128
N_GATES = 2 * 2 * N_MLSTM
D_FF = int(math.ceil(8 * D_MODEL / 3 / 128)) * 128
OFF_FOUR = MIX_POOL
OFF_O = OFF_FOUR + MIX_FOUR
OFF_Q = OFF_O + MIX_MLSTM
OFF_K = OFF_Q + MIX_MLSTM
OFF_V = OFF_K + MIX_MLSTM
OFF_G = OFF_V + MIX_MLSTM
ALPHA = (2 * DEPTH) ** 0.25
LN_EPS = 1e-6

LANE = 128
LANE_SHIFT = LANE.bit_length() - 1
SUBLANE = 8
BF16_ROWS = 16
VMEM_LIMIT = 56 * 1024 * 1024
FF_CHUNK = 256
PROJ_CHUNK = 512
POOL_BLOCK = 256
POOL_PAD = (max(POOL_WINDOWS) // 2) * GRID_W
NEG_BIG = -1e30
MLSTM_GROUP = 4
MLSTM_AUG = MLSTM_DH + BF16_ROWS


def _params(n_axes):
    return pltpu.CompilerParams(dimension_semantics=("arbitrary",) * n_axes, vmem_limit_bytes=VMEM_LIMIT)


def _ln_rows(x):
    mu = jnp.mean(x, axis=-1, keepdims=True)
    xc = x - mu
    var = jnp.mean(xc * xc, axis=-1, keepdims=True)
    return xc * lax.rsqrt(var + LN_EPS)


def _sigmoid(x):
    return 0.5 * jnp.tanh(0.5 * x) + 0.5


def _log_sigmoid(x):
    return jnp.minimum(x, 0.0) - jnp.log(1.0 + jnp.exp(-jnp.abs(x)))


def _split_hi_lo(x):
    hi = x.astype(BF16)
    return hi, (x - hi.astype(F32)).astype(BF16)


def _proj_in_kernel(x_ref, xp_ref, xn_ref, sh_ref, sc_ref, w_ref, wg_ref, wc_ref, z_ref, g_ref, h_ref):
    i = pl.program_id(1)
    tm = x_ref.shape[1]
    mod_scale = 1.0 + sc_ref[0]
    mod_shift = sh_ref[0]
    u = _ln_rows(x_ref[0]) * mod_scale + mod_shift
    u_prev = (_ln_rows(xp_ref[0]) * mod_scale + mod_shift) * (i > 0).astype(F32)
    u_next = (_ln_rows(xn_ref[0]) * mod_scale + mod_shift) * (i < pl.num_programs(1) - 1).astype(F32)
    ub = u.astype(BF16)
    u_ext = jnp.concatenate([u_prev, u, u_next], axis=0).astype(BF16)
    n = z_ref.shape[-1]
    for c0 in range(0, n, PROJ_CHUNK):
        c1 = min(c0 + PROJ_CHUNK, n)
        if OFF_Q <= c0 and c1 <= OFF_V:
            h_ref[...] = jnp.dot(u_ext, w_ref[:, c0:c1], preferred_element_type=F32)
            w = wc_ref[:, c0 - OFF_Q:c1 - OFF_Q]
            y = (h_ref[SUBLANE - 1:SUBLANE - 1 + tm, :] * w[0:1] + h_ref[SUBLANE:SUBLANE + tm, :] * w[1:2]
                 + h_ref[SUBLANE + 1:SUBLANE + 1 + tm, :] * w[2:3])
            scale = 1.0 if c0 < OFF_K else MLSTM_DH ** -0.5
            z_ref[0, :, c0:c1] = (y * _sigmoid(y) * scale).astype(z_ref.dtype)
        else:
            z_ref[0, :, c0:c1] = jnp.dot(ub, w_ref[:, c0:c1], preferred_element_type=F32).astype(z_ref.dtype)
    g_ref[0] = jnp.dot(ub, wg_ref[...], preferred_element_type=F32)


def _proj_in(x, shift, scale, w_main, w_gate, w_conv, tm):
    B, T, D = x.shape
    n = w_main.shape[1]
    assert OFF_Q % PROJ_CHUNK == 0 and OFF_K % PROJ_CHUNK == 0 and OFF_V % PROJ_CHUNK == 0
    nb = tm // SUBLANE
    last = T // SUBLANE - 1
    return pl.pallas_call(
        _proj_in_kernel,
        grid=(B, T // tm),
        in_specs=[
            pl.BlockSpec((1, tm, D), lambda b, i: (b, i, 0)),
            pl.BlockSpec((1, SUBLANE, D), lambda b, i: (b, jnp.maximum(i * nb - 1, 0), 0)),
            pl.BlockSpec((1, SUBLANE, D), lambda b, i: (b, jnp.minimum((i + 1) * nb, last), 0)),
            pl.BlockSpec((1, 1, D), lambda b, i: (b, 0, 0)),
            pl.BlockSpec((1, 1, D), lambda b, i: (b, 0, 0)),
            pl.BlockSpec((D, n), lambda b, i: (0, 0)),
            pl.BlockSpec((D, LANE), lambda b, i: (0, 0)),
            pl.BlockSpec((3, 2 * MIX_MLSTM), lambda b, i: (0, 0)),
        ],
        out_specs=[
            pl.BlockSpec((1, tm, n), lambda b, i: (b, i, 0)),
            pl.BlockSpec((1, tm, LANE), lambda b, i: (b, i, 0)),
        ],
        out_shape=[
            jax.ShapeDtypeStruct((B, T, n), BF16),
            jax.ShapeDtypeStruct((B, T, LANE), F32),
        ],
        scratch_shapes=[pltpu.VMEM((tm + 2 * SUBLANE, PROJ_CHUNK), F32)],
        compiler_params=_params(2),
        name="proj_in",
    )(x, x, x, shift, scale, w_main, w_gate, w_conv)


def _proj_out_kernel(yp_ref, yf_ref, ym_ref, x_ref, gate_ref, wp_ref, wf_ref, wm_ref, lg_ref, lb_ref, o_ref):
    f = (jnp.dot(yp_ref[0], wp_ref[...], preferred_element_type=F32)
         + jnp.dot(yf_ref[0], wf_ref[...], preferred_element_type=F32)
         + jnp.dot(ym_ref[0], wm_ref[...], preferred_element_type=F32))
    r = ALPHA * x_ref[0] + gate_ref[0] * f
    o_ref[0] = _ln_rows(r) * lg_ref[...] + lb_ref[...]


def _proj_out(yp, yf, ym, x, gate, w, ln_g, ln_b, tm):
    B, T, D = x.shape
    tok = lambda b, i: (b, i, 0)
    return pl.pallas_call(
        _proj_out_kernel,
        grid=(B, T // tm),
        in_specs=[
            pl.BlockSpec((1, tm, MIX_POOL), tok),
            pl.BlockSpec((1, tm, MIX_FOUR), tok),
            pl.BlockSpec((1, tm, MIX_MLSTM), tok),
            pl.BlockSpec((1, tm, D), tok),
            pl.BlockSpec((1, 1, D), lambda b, i: (b, 0, 0)),
            pl.BlockSpec((MIX_POOL, D), lambda b, i: (0, 0)),
            pl.BlockSpec((MIX_FOUR, D), lambda b, i: (OFF_FOUR // MIX_FOUR, 0)),
            pl.BlockSpec((MIX_MLSTM, D), lambda b, i: (OFF_O // MIX_MLSTM, 0)),
            pl.BlockSpec((1, D), lambda b, i: (0, 0)),
            pl.BlockSpec((1, D), lambda b, i: (0, 0)),
        ],
        out_specs=pl.BlockSpec((1, tm, D), tok),
        out_shape=jax.ShapeDtypeStruct((B, T, D), F32),
        compiler_params=_params(2),
        name="proj_out",
    )(yp, yf, ym, x, gate, w, w, w, ln_g, ln_b)


def _gelu_tanh(x):
    c = math.sqrt(2.0 / math.pi)
    return 0.5 * x * (1.0 + jnp.tanh(c * (x + 0.044715 * (x * x * x))))


def _ffn_kernel(x_ref, xp_ref, xn_ref, sh_ref, sc_ref, gate_ref, wup_ref, wc_ref, bc_ref, wdn_ref,
                lg_ref, lb_ref, o_ref, ha_ref, hg_ref, p_ref):
    i = pl.program_id(1)
    tm = x_ref.shape[1]
    x = x_ref[0]
    mod_scale = 1.0 + sc_ref[0]
    mod_shift = sh_ref[0]
    has_prev = (i > 0).astype(F32)
    has_next = (i < pl.num_programs(1) - 1).astype(F32)
    u_prev = (_ln_rows(xp_ref[0]) * mod_scale + mod_shift) * has_prev
    u_next = (_ln_rows(xn_ref[0]) * mod_scale + mod_shift) * has_next
    u_main = _ln_rows(x) * mod_scale + mod_shift
    u_ext = jnp.concatenate([u_prev, u_main, u_next], axis=0).astype(BF16)

    def conv(h_ref, cols):
        w = wc_ref[:, cols]
        return (h_ref[SUBLANE - 1:SUBLANE - 1 + tm, :] * w[0:1]
                + h_ref[SUBLANE:SUBLANE + tm, :] * w[1:2]
                + h_ref[SUBLANE + 1:SUBLANE + 1 + tm, :] * w[2:3]
                + bc_ref[:, cols])

    for c in range(D_FF // FF_CHUNK):
        ca = slice(c * FF_CHUNK, (c + 1) * FF_CHUNK)
        cg = slice(D_FF + c * FF_CHUNK, D_FF + (c + 1) * FF_CHUNK)
        ha_ref[...] = jnp.dot(u_ext, wup_ref[:, ca], preferred_element_type=F32)
        hg_ref[...] = jnp.dot(u_ext, wup_ref[:, cg], preferred_element_type=F32)
        p_ref[:, ca] = (_gelu_tanh(conv(hg_ref, cg)) * conv(ha_ref, ca)).astype(BF16)

    f = jnp.dot(p_ref[...], wdn_ref[...], preferred_element_type=F32)
    r = ALPHA * x + gate_ref[0] * f
    o_ref[0] = _ln_rows(r) * lg_ref[...] + lb_ref[...]


def _ffn(x, shift, scale, gate, w_up, w_conv, b_conv, w_down, ln_g, ln_b, tm):
    B, T, D = x.shape
    nb = tm // SUBLANE
    last = T // SUBLANE - 1
    const = dict(pipeline_mode=pl.Buffered(1))
    return pl.pallas_call(
        _ffn_kernel,
        grid=(B, T // tm),
        in_specs=[
            pl.BlockSpec((1, tm, D), lambda b, i: (b, i, 0)),
            pl.BlockSpec((1, SUBLANE, D), lambda b, i: (b, jnp.maximum(i * nb - 1, 0), 0)),
            pl.BlockSpec((1, SUBLANE, D), lambda b, i: (b, jnp.minimum((i + 1) * nb, last), 0)),
            pl.BlockSpec((1, 1, D), lambda b, i: (b, 0, 0)),
            pl.BlockSpec((1, 1, D), lambda b, i: (b, 0, 0)),
            pl.BlockSpec((1, 1, D), lambda b, i: (b, 0, 0)),
            pl.BlockSpec((D, 2 * D_FF), lambda b, i: (0, 0), **const),
            pl.BlockSpec((3, 2 * D_FF), lambda b, i: (0, 0), **const),
            pl.BlockSpec((1, 2 * D_FF), lambda b, i: (0, 0), **const),
            pl.BlockSpec((D_FF, D), lambda b, i: (0, 0), **const),
            pl.BlockSpec((1, D), lambda b, i: (0, 0)),
            pl.BlockSpec((1, D), lambda b, i: (0, 0)),
        ],
        out_specs=pl.BlockSpec((1, tm, D), lambda b, i: (b, i, 0)),
        out_shape=jax.ShapeDtypeStruct((B, T, D), F32),
        scratch_shapes=[
            pltpu.VMEM((tm + 2 * SUBLANE, FF_CHUNK), F32),
            pltpu.VMEM((tm + 2 * SUBLANE, FF_CHUNK), F32),
            pltpu.VMEM((tm, D_FF), BF16),
        ],
        compiler_params=_params(2),
        name="conv_ffn",
    )(x, x, x, shift, scale, gate, w_up, w_conv, b_conv, w_down, ln_g, ln_b)


def _band_matrix(n, w):
    pos = np.arange(n)
    lo = np.maximum(pos - w // 2, 0)
    hi = np.minimum(pos + w // 2 - 1, n - 1)
    return ((pos[None, :] >= lo[:, None]) & (pos[None, :] <= hi[:, None])).astype(np.float32)


def _pool_filters(grid2d):
    mats = []
    for w in POOL_WINDOWS:
        if grid2d:
            mats.append(np.kron(np.eye(POOL_BLOCK // GRID_W, dtype=np.float32), _band_matrix(GRID_W, w)))
        else:
            mats.append(_band_matrix(POOL_BLOCK, w))
    return jnp.asarray(np.stack(mats), BF16)


def _window_count(idx, half, n):
    return jnp.minimum(idx + (half - 1), n - 1) - jnp.maximum(idx - half, 0) + 1


def _pool_kernel(z_ref, a_ref, w_ref, s_ref, y_ref, acc_ref, *, grid2d):
    T = z_ref.shape[1]
    nblk = T // POOL_BLOCK
    lane = lax.broadcasted_iota(jnp.int32, (POOL_BLOCK, MIX_POOL), 1)
    group = jnp.right_shift(lane, GRID_SHIFT)
    half = jnp.left_shift(1, group)

    def col_filter(blk):
        r0 = pl.multiple_of(blk * POOL_BLOCK, POOL_BLOCK)
        xb = z_ref[0, pl.ds(r0, POOL_BLOCK), :]
        out = jnp.dot(a_ref[0], xb, preferred_element_type=F32)
        for g in range(1, N_POOL):
            out = jnp.where(group == g, jnp.dot(a_ref[g], xb, preferred_element_type=F32), out)
        return r0, xb, out

    def finish(r0, xb, box, count):
        p = box / count.astype(F32) - xb.astype(F32)
        y = jnp.dot(p.astype(BF16), w_ref[...], preferred_element_type=F32) * s_ref[...]
        y_ref[0, pl.ds(r0, POOL_BLOCK), :] = y.astype(y_ref.dtype)

    if not grid2d:
        assert T == POOL_BLOCK
        r0, xb, box = col_filter(0)
        pos = lax.broadcasted_iota(jnp.int32, (POOL_BLOCK, MIX_POOL), 0)
        finish(r0, xb, box, _window_count(pos, half, T))
        return

    rows = T // GRID_W
    zeros = jnp.zeros((POOL_PAD, MIX_POOL), F32)
    acc_ref[0:POOL_PAD, :] = zeros
    acc_ref[POOL_PAD + T:POOL_PAD + T + POOL_PAD, :] = zeros

    def phase1(blk, carry):
        r0, _, out = col_filter(blk)
        acc_ref[pl.ds(POOL_PAD + r0, POOL_BLOCK), :] = out
        return carry

    lax.fori_loop(0, nblk, phase1, 0)

    lane_t = lax.broadcasted_iota(jnp.int32, (POOL_BLOCK, LANE), 1)

    def phase2(blk, carry):
        r0 = pl.multiple_of(blk * POOL_BLOCK, POOL_BLOCK)
        base = POOL_PAD + r0

        def slab(k, tile):
            return acc_ref[pl.ds(base + k * GRID_W, POOL_BLOCK), tile * LANE:(tile + 1) * LANE]

        s2 = slab(-1, 0) + slab(0, 0)
        s4 = s2 + slab(-2, 0) + slab(1, 0)
        s8 = slab(-4, 1)
        for k in range(-3, 4):
            s8 = s8 + slab(k, 1)
        s16 = s8
        for k in list(range(-8, -4)) + list(range(4, 8)):
            s16 = s16 + slab(k, 1)
        box = jnp.concatenate([jnp.where(lane_t < POOL_DIM, s2, s4), jnp.where(lane_t < POOL_DIM, s8, s16)], axis=1)
        pos = r0 + lax.broadcasted_iota(jnp.int32, (POOL_BLOCK, MIX_POOL), 0)
        count = (_window_count(jnp.right_shift(pos, GRID_SHIFT), half, rows)
                 * _window_count(jnp.bitwise_and(pos, GRID_W - 1), half, GRID_W))
        xb = z_ref[0, pl.ds(r0, POOL_BLOCK), :]
        finish(r0, xb, box, count)
        return carry

    lax.fori_loop(0, nblk, phase2, 0)


def _pool(z, filters, w_bd, s_row, grid2d):
    B, T, _ = z.shape
    return pl.pallas_call(
        functools.partial(_pool_kernel, grid2d=grid2d),
        grid=(B,),
        in_specs=[
            pl.BlockSpec((1, T, MIX_POOL), lambda b: (b, 0, 0)),
            pl.BlockSpec((N_POOL, POOL_BLOCK, POOL_BLOCK), lambda b: (0, 0, 0)),
            pl.BlockSpec((MIX_POOL, MIX_POOL), lambda b: (0, 0)),
            pl.BlockSpec((1, MIX_POOL), lambda b: (0, 0)),
        ],
        out_specs=pl.BlockSpec((1, T, MIX_POOL), lambda b: (b, 0, 0)),
        out_shape=jax.ShapeDtypeStruct((B, T, MIX_POOL), BF16),
        scratch_shapes=[pltpu.VMEM((T + 2 * POOL_PAD, MIX_POOL), F32)],
        compiler_params=_params(1),
        name="pool2d" if grid2d else "pool1d",
    )(z, filters, w_bd, s_row)


def _block_diag(w):
    G, n, m = w.shape
    eye = jnp.eye(G, dtype=w.dtype)
    return (eye[:, None, :, None] * w[:, :, None, :]).reshape(G * n, G * m)


def _four_mix_kernel(z_ref, g_ref, y_ref):
    y = jnp.dot(z_ref[0], g_ref[...], preferred_element_type=F32).astype(y_ref.dtype)
    y_ref[0, 0] = y[:, :MIX_FOUR]
    y_ref[0, 1] = y[:, MIX_FOUR:]


def _four_dft_kernel(d_ref, y_ref, o_ref):
    rows = 256
    for r0 in range(0, o_ref.shape[1], rows):
        o_ref[0, r0:r0 + rows, :] = jnp.dot(d_ref[r0:r0 + rows, :], y_ref[0],
                                            preferred_element_type=F32).astype(o_ref.dtype)


def _fourier(z, g_mix, dft):
    B, T, _ = z.shape
    tm = min(T, 512)
    y = pl.pallas_call(
        _four_mix_kernel,
        grid=(B, T // tm),
        in_specs=[
            pl.BlockSpec((1, tm, MIX_FOUR), lambda b, i: (b, i, OFF_FOUR // MIX_FOUR)),
            pl.BlockSpec((MIX_FOUR, 2 * MIX_FOUR), lambda b, i: (0, 0)),
        ],
        out_specs=pl.BlockSpec((1, 2, tm, MIX_FOUR), lambda b, i: (b, 0, i, 0)),
        out_shape=jax.ShapeDtypeStruct((B, 2, T, MIX_FOUR), BF16),
        compiler_params=_params(2),
        name="fourier_mix",
    )(z, g_mix)
    y = y.reshape(B, 2 * T, MIX_FOUR)
    tf = min(T, 1024)
    return pl.pallas_call(
        _four_dft_kernel,
        grid=(T // tf, B),
        in_specs=[
            pl.BlockSpec((tf, 2 * T), lambda m, b: (m, 0)),
            pl.BlockSpec((1, 2 * T, MIX_FOUR), lambda m, b: (b, 0, 0)),
        ],
        out_specs=pl.BlockSpec((1, tf, MIX_FOUR), lambda m, b: (b, m, 0)),
        out_shape=jax.ShapeDtypeStruct((B, T, MIX_FOUR), BF16),
        compiler_params=_params(2),
        name="fourier_dft",
    )(dft, y)


def _dft_matrix(n):
    a = n // GRID_W
    f = lax.broadcasted_iota(jnp.int32, (n, 1), 0)
    ang1 = ((f * GRID_W * lax.broadcasted_iota(jnp.int32, (1, a), 1)) % n).astype(F32) * (2.0 * math.pi / n)
    ang2 = ((f * lax.broadcasted_iota(jnp.int32, (1, GRID_W), 1)) % n).astype(F32) * (2.0 * math.pi / n)
    c1, s1 = jnp.cos(ang1)[:, :, None], jnp.sin(ang1)[:, :, None]
    c2, s2 = jnp.cos(ang2)[:, None, :], jnp.sin(ang2)[:, None, :]
    cos = (c1 * c2 - s1 * s2).reshape(n, n)
    sin = (s1 * c2 + c1 * s2).reshape(n, n)
    return jnp.concatenate([cos, sin], axis=1).astype(BF16)


def _fourier_mix_weights(w_four, T):
    c = np.arange(FOUR_DIM)
    ang = 2.0 * np.pi * np.outer(c, c) / FOUR_DIM
    norm = 1.0 / math.sqrt(T * FOUR_DIM)
    gr = jnp.einsum('cf,gfe->gce', jnp.asarray(np.cos(ang) * norm, F32), w_four, precision='highest')
    gi = jnp.einsum('cf,gfe->gce', jnp.asarray(-np.sin(ang) * norm, F32), w_four, precision='highest')
    return jnp.concatenate([_block_diag(gr), _block_diag(gi)], axis=1).astype(BF16)


def _mlstm_kernel(q_ref, k_ref, v_ref, o_ref, g_ref, bg_ref, cf0_ref, mf0_ref, cb0_ref, mb0_ref,
                  y_ref, cf1_ref, mf1_ref, cb1_ref, mb1_ref,
                  hf_ref, hb_ref, pf_ref, pb_ref, uf_ref, ub_ref, rf_ref, rb_ref):
    T = q_ref.shape[1]
    L = MLSTM_CHUNK
    dh = MLSTM_DH
    aug = MLSTM_AUG
    nc = T // L
    head = pl.program_id(1)

    row = lax.broadcasted_iota(jnp.int32, (L, L), 0)
    col = lax.broadcasted_iota(jnp.int32, (L, L), 1)
    masks = (row <= col, row >= col)
    row2 = lax.broadcasted_iota(jnp.int32, (L, 2 * L), 0)
    col2 = jnp.bitwise_and(lax.broadcasted_iota(jnp.int32, (L, 2 * L), 1), L - 1)
    tri2 = jnp.where(col2 <= row2, 1.0, 0.0).astype(BF16)
    diag = row == col
    sub8 = lax.broadcasted_iota(jnp.int32, (SUBLANE, L), 0)
    ones_t = jnp.ones((aug - dh, L), F32)
    sel_row = jnp.bitwise_and(lax.broadcasted_iota(jnp.int32, (2 * LANE, 2 * LANE), 0), LANE - 1)
    sel_dir = jnp.right_shift(lax.broadcasted_iota(jnp.int32, (2 * LANE, 2 * LANE), 1), LANE_SHIFT)
    sel_i = jnp.where(sel_row == sel_dir * (2 * N_MLSTM) + head, 1.0, 0.0).astype(BF16)
    sel_f = jnp.where(sel_row == sel_dir * (2 * N_MLSTM) + N_MLSTM + head, 1.0, 0.0).astype(BF16)
    p_refs, u_refs, r_refs, h_refs = (pf_ref, pb_ref), (uf_ref, ub_ref), (rf_ref, rb_ref), (hf_ref, hb_ref)
    bias = bg_ref[...]

    def prep_group(c0):
        cs = [c0 + j for j in range(min(MLSTM_GROUP, nc))]
        rs = [pl.multiple_of(c * L, L) for c in cs]
        q = [q_ref[0, pl.ds(r, L), :] for r in rs]
        k = [k_ref[0, pl.ds(r, L), :] for r in rs]
        vt = [jnp.concatenate([v_ref[0, pl.ds(r, L), :].astype(F32).T, ones_t], axis=0).astype(BF16) for r in rs]
        g = [g_ref[0, pl.ds(r, L), :] + bias for r in rs]
        i_rep = [jnp.dot(jnp.concatenate(_split_hi_lo(gg), axis=1), sel_i, preferred_element_type=F32)
                 for gg in g]
        lf_rep = [jnp.dot(jnp.concatenate(_split_hi_lo(_log_sigmoid(gg)), axis=1), sel_f,
                          preferred_element_type=F32) for gg in g]
        st = [lax.dot_general(kk, qq, (((1,), (1,)), ((), ())), preferred_element_type=F32)
              for kk, qq in zip(k, q)]
        cum = [jnp.dot(tri2, jnp.concatenate(_split_hi_lo(f), axis=0), preferred_element_type=F32)
               for f in lf_rep]
        items = [(j, d) for j in range(len(cs)) for d in range(2)]
        for j, d in items:
            if d == 0:
                b = cum[j][:, :L]
                total = b[L - 1:L, :]
            else:
                total = cum[j][L - 1:L, L:]
                b = total - cum[j][:, L:] + lf_rep[j][:, L:]
            r_rep = i_rep[j][:, d * L:(d + 1) * L] - b
            dmt = jnp.where(masks[d], r_rep, NEG_BIG)
            pm = jnp.max(dmt, axis=0, keepdims=True)
            sg = (st[j] * jnp.exp(dmt - pm)).astype(BF16)
            b_row = jnp.sum(jnp.where(diag, b, 0.0), axis=0, keepdims=True)
            g_end = total + r_rep
            g_loc = jnp.max(g_end, axis=0, keepdims=True)
            kw = (k[j].astype(F32) * jnp.exp(g_end - g_loc)).astype(BF16)
            base = pl.multiple_of(cs[j] * aug, BF16_ROWS)
            pu = jnp.dot(vt[j], jnp.concatenate([sg, kw], axis=1), preferred_element_type=F32)
            p_refs[d][pl.ds(base, aug), :] = pu[:, :L]
            u_refs[d][pl.ds(base, aug), :] = pu[:, L:]
            rbase = pl.multiple_of(cs[j] * SUBLANE, SUBLANE)
            r_refs[d][pl.ds(rbase, SUBLANE), :] = jnp.where(
                sub8 == 0, b_row, jnp.where(sub8 == 1, b_row + pm, jnp.where(sub8 == 2, total, g_loc)))

    ga = min(MLSTM_GROUP, nc)

    def body_a(i, carry):
        prep_group(i * ga)
        return carry

    lax.fori_loop(0, nc // ga, body_a, 0)

    def step(d, c, ct, m):
        base = pl.multiple_of(c * aug, BF16_ROWS)
        rows = r_refs[d][pl.ds(pl.multiple_of(c * SUBLANE, SUBLANE), SUBLANE), :]
        b_row, a_row, total, g_loc = rows[0:1], rows[1:2], rows[2:3], rows[3:4]
        inter = b_row + m
        m_j = jnp.maximum(inter, a_row)
        qct = lax.dot_general(ct.astype(BF16), q_ref[0, pl.ds(pl.multiple_of(c * L, L), L), :],
                              (((1,), (1,)), ((), ())), preferred_element_type=F32)
        nd = jnp.exp(inter - m_j) * qct + jnp.exp(a_row - m_j) * p_refs[d][pl.ds(base, aug), :]
        den = jnp.maximum(jnp.abs(nd[dh:dh + 1]), jnp.exp(-m_j))
        h_refs[d][pl.ds(pl.multiple_of(c * dh, dh), dh), :] = nd[:dh] / den
        m_new = jnp.maximum(total + m, g_loc)
        ct_new = jnp.exp(total + m - m_new) * ct + jnp.exp(g_loc - m_new) * u_refs[d][pl.ds(base, aug), :]
        return ct_new, m_new

    gb = min(nc, 4)

    def body_b(i, carry):
        cf, mf, cb, mb = carry
        for j in range(gb):
            cf, mf = step(0, i * gb + j, cf, mf)
            cb, mb = step(1, nc - 1 - (i * gb + j), cb, mb)
        return cf, mf, cb, mb

    cf, mf, cb, mb = lax.fori_loop(
        0, nc // gb, body_b, (cf0_ref[0, 0], mf0_ref[0, 0][0:1, :], cb0_ref[0, 0], mb0_ref[0, 0][0:1, :]))

    cf1_ref[0, 0] = cf
    cb1_ref[0, 0] = cb
    mf1_ref[0, 0] = jnp.broadcast_to(mf, (SUBLANE, LANE))
    mb1_ref[0, 0] = jnp.broadcast_to(mb, (SUBLANE, LANE))

    go = min(nc, 4)

    def body_out(i, carry):
        for j in range(go):
            c = i * go + j
            r = pl.multiple_of(c * L, L)
            rh = pl.multiple_of(c * dh, dh)
            h = (hf_ref[pl.ds(rh, dh), :] + hb_ref[pl.ds(rh, dh), :]).T
            y_ref[0, pl.ds(r, L), :] = (_sigmoid(o_ref[0, pl.ds(r, L), :].astype(F32)) * h).astype(y_ref.dtype)
        return carry

    lax.fori_loop(0, nc // go, body_out, 0)


def _mlstm(z, zg, b_gate, cf0, mf0, cb0, mb0):
    B, T, _ = z.shape
    H, dh, L, aug = N_MLSTM, MLSTM_DH, MLSTM_CHUNK, MLSTM_AUG
    nc = T // L
    zcol = lambda off: (lambda b, h: (b, 0, off // dh + h))
    st = lambda b, h: (b, h, 0, 0)
    c_spec = pl.BlockSpec((1, 1, aug, dh), st)
    m_spec = pl.BlockSpec((1, 1, SUBLANE, LANE), st)
    c_shape = jax.ShapeDtypeStruct((B, H, aug, dh), F32)
    m_shape = jax.ShapeDtypeStruct((B, H, SUBLANE, LANE), F32)
    return pl.pallas_call(
        _mlstm_kernel,
        grid=(B, H),
        in_specs=[
            pl.BlockSpec((1, T, dh), zcol(OFF_Q)),
            pl.BlockSpec((1, T, dh), zcol(OFF_K)),
            pl.BlockSpec((1, T, dh), zcol(OFF_V)),
            pl.BlockSpec((1, T, dh), zcol(OFF_O)),
            pl.BlockSpec((1, T, LANE), lambda b, h: (b, 0, 0)),
            pl.BlockSpec((1, LANE), lambda b, h: (0, 0)),
            c_spec, m_spec, c_spec, m_spec,
        ],
        out_specs=[pl.BlockSpec((1, T, dh), lambda b, h: (b, 0, h)), c_spec, m_spec, c_spec, m_spec],
        out_shape=[jax.ShapeDtypeStruct((B, T, MIX_MLSTM), BF16), c_shape, m_shape, c_shape, m_shape],
        scratch_shapes=[
            pltpu.VMEM((nc * dh, L), F32),
            pltpu.VMEM((nc * dh, L), F32),
            pltpu.VMEM((nc * aug, L), F32),
            pltpu.VMEM((nc * aug, L), F32),
            pltpu.VMEM((nc * aug, dh), F32),
            pltpu.VMEM((nc * aug, dh), F32),
            pltpu.VMEM((nc * SUBLANE, LANE), F32),
            pltpu.VMEM((nc * SUBLANE, LANE), F32),
        ],
        compiler_params=_params(2),
        name="mlstm",
    )(z, z, z, z, zg, b_gate, cf0, mf0, cb0, mb0)


def kernel(x, c, ctx, c_ctx, w_mod, b_mod, w_in, conv_qk, b_gates, pool_w, pool_scale, fourier_w,
           w_out, ln1_g, ln1_b, w_up, conv_ffn_w, conv_ffn_b, w_down, ln2_g, ln2_b):
    B, T, D = x.shape
    Tc = ctx.shape[1]
    tm_x = 512
    tm_c = Tc
    xc = ctx
    s_c = jax.nn.silu(c)
    s_ctx = jax.nn.silu(c_ctx)
    filt_x, filt_c = _pool_filters(True), _pool_filters(False)
    dft_x, dft_c = _dft_matrix(T), _dft_matrix(Tc)
    zero_c = jnp.zeros((B, N_MLSTM, MLSTM_AUG, MLSTM_DH), F32)
    zero_m = jnp.zeros((B, N_MLSTM, SUBLANE, LANE), F32)
    for l in range(DEPTH):
        last = l == DEPTH - 1
        mx = [m[:, None, :] for m in jnp.split(s_c @ w_mod[l] + b_mod[l], 6, axis=-1)]
        mc = [jnp.broadcast_to(m[None, None, :], (B, 1, D))
              for m in jnp.split(s_ctx @ w_mod[l] + b_mod[l], 6, axis=-1)]
        w_main = w_in[l][:, :OFF_G].astype(BF16)
        w_gate = jnp.pad(w_in[l][:, OFF_G:], ((0, 0), (0, LANE - N_GATES))).astype(BF16)
        b_gate = jnp.pad(b_gates[l], (0, LANE - N_GATES))[None]
        w_pool_bd = _block_diag(pool_w[l]).astype(BF16)
        s_pool = pool_scale[l][None]
        w_out_b = w_out[l].astype(BF16)
        w_up_b = w_up[l].astype(BF16)
        w_down_b = w_down[l].astype(BF16)
        g1, b1 = ln1_g[l][None], ln1_b[l][None]
        g2, b2 = ln2_g[l][None], ln2_b[l][None]
        bcv = conv_ffn_b[l][None]

        zc, zgc = _proj_in(xc, mc[0], mc[1], w_main, w_gate, conv_qk[l], tm_c)
        ymc, cf, mf, cb, mb = _mlstm(zc, zgc, b_gate, zero_c, zero_m, zero_c, zero_m)
        zx, zgx = _proj_in(x, mx[0], mx[1], w_main, w_gate, conv_qk[l], tm_x)
        ymx = _mlstm(zx, zgx, b_gate, cf, mf, cb, mb)[0]
        ypx = _pool(zx, filt_x, w_pool_bd, s_pool, True)
        yfx = _fourier(zx, _fourier_mix_weights(fourier_w[l], T), dft_x)
        x = _proj_out(ypx, yfx, ymx, x, mx[2], w_out_b, g1, b1, tm_x)
        x = _ffn(x, mx[3], mx[4], mx[5], w_up_b, conv_ffn_w[l], bcv, w_down_b, g2, b2, tm_x)
        if not last:
            ypc = _pool(zc, filt_c, w_pool_bd, s_pool, False)
            yfc = _fourier(zc, _fourier_mix_weights(fourier_w[l], Tc), dft_c)
            xc = _proj_out(ypc, yfc, ymc, xc, mc[2], w_out_b, g1, b1, tm_c)
            xc = _ffn(xc, mc[3], mc[4], mc[5], w_up_b, conv_ffn_w[l], bcv, w_down_b, g2, b2, tm_c)
    return x
```

```python
import functools
import math

import jax
import jax.numpy as jnp
import numpy as np
from jax import lax
from jax.experimental import pallas as pl
from jax.experimental.pallas import tpu as pltpu

F32 = jnp.float32
BF16 = jnp.bfloat16

D_MODEL = 1024
DEPTH = 4
GRID_W = 64
GRID_SHIFT = GRID_W.bit_length() - 1
POOL_WINDOWS = (2, 4, 8, 16)
N_POOL = len(POOL_WINDOWS)
MIX_POOL = D_MODEL // 4
POOL_DIM = MIX_POOL // N_POOL
N_FOUR = 4
MIX_FOUR = D_MODEL // 4
FOUR_DIM = MIX_FOUR // N_FOUR
N_MLSTM = 4
MIX_MLSTM = D_MODEL // 2
MLSTM_DH = MIX_MLSTM // N_MLSTM
MLSTM_CHUNK = 128
N_GATES = 2 * 2 * N_MLSTM
D_FF = int(math.ceil(8 * D_MODEL / 3 / 128)) * 128
OFF_FOUR = MIX_POOL
OFF_O = OFF_FOUR + MIX_FOUR
OFF_Q = OFF_O + MIX_MLSTM
OFF_K = OFF_Q + MIX_MLSTM
OFF_V = OFF_K + MIX_MLSTM
OFF_G = OFF_V + MIX_MLSTM
ALPHA = (2 * DEPTH) ** 0.25
LN_EPS = 1e-6
LOG2_E = math.log2(math.e)

LANE = 128
LANE_SHIFT = LANE.bit_length() - 1
SUBLANE = 8
BF16_ROWS = 16
VMEM_LIMIT = 56 * 1024 * 1024
FF_CHUNK = 256
PROJ_CHUNK = 512
POOL_BLOCK = 256
POOL_PAD = (max(POOL_WINDOWS) // 2) * GRID_W
NEG_BIG = -1e30
MLSTM_GROUP = 8
MLSTM_AUG = MLSTM_DH + BF16_ROWS


def _params(n_axes):
    return pltpu.CompilerParams(dimension_semantics=("arbitrary",) * n_axes, vmem_limit_bytes=VMEM_LIMIT)


def _ln_rows(x):
    mu = jnp.mean(x, axis=-1, keepdims=True)
    xc = x - mu
    var = jnp.mean(xc * xc, axis=-1, keepdims=True)
    return xc * lax.rsqrt(var + LN_EPS)


def _sigmoid(x):
    return 0.5 * jnp.tanh(0.5 * x) + 0.5


def _log2_sigmoid(x2):
    return jnp.minimum(x2, 0.0) - jnp.log2(1.0 + jnp.exp2(-jnp.abs(x2)))


def _split_hi_lo(x):
    hi = x.astype(BF16)
    return hi, (x - hi.astype(F32)).astype(BF16)


def _proj_in_kernel(x_ref, xp_ref, xn_ref, sh_ref, sc_ref, w_ref, wg_ref, wc_ref, z_ref, g_ref, h_ref):
    i = pl.program_id(1)
    tm = x_ref.shape[1]
    mod_scale = 1.0 + sc_ref[0]
    mod_shift = sh_ref[0]
    u = _ln_rows(x_ref[0]) * mod_scale + mod_shift
    u_prev = (_ln_rows(xp_ref[0]) * mod_scale + mod_shift) * (i > 0).astype(F32)
    u_next = (_ln_rows(xn_ref[0]) * mod_scale + mod_shift) * (i < pl.num_programs(1) - 1).astype(F32)
    ub = u.astype(BF16)
    u_ext = jnp.concatenate([u_prev, u, u_next], axis=0).astype(BF16)
    n = z_ref.shape[-1]
    for c0 in range(0, n, PROJ_CHUNK):
        c1 = min(c0 + PROJ_CHUNK, n)
        if OFF_Q <= c0 and c1 <= OFF_V:
            h_ref[...] = jnp.dot(u_ext, w_ref[:, c0:c1], preferred_element_type=F32)
            w = wc_ref[:, c0 - OFF_Q:c1 - OFF_Q]
            y = (h_ref[SUBLANE - 1:SUBLANE - 1 + tm, :] * w[0:1] + h_ref[SUBLANE:SUBLANE + tm, :] * w[1:2]
                 + h_ref[SUBLANE + 1:SUBLANE + 1 + tm, :] * w[2:3])
            scale = 1.0 if c0 < OFF_K else MLSTM_DH ** -0.5
            z_ref[0, :, c0:c1] = (y * _sigmoid(y) * scale).astype(z_ref.dtype)
        else:
            z_ref[0, :, c0:c1] = jnp.dot(ub, w_ref[:, c0:c1], preferred_element_type=F32).astype(z_ref.dtype)
    g_ref[0] = jnp.dot(ub, wg_ref[...], preferred_element_type=F32)


def _proj_in(x, shift, scale, w_main, w_gate, w_conv, tm):
    B, T, D = x.shape
    n = w_main.shape[1]
    assert OFF_Q % PROJ_CHUNK == 0 and OFF_K % PROJ_CHUNK == 0 and OFF_V % PROJ_CHUNK == 0
    nb = tm // SUBLANE
    last = T // SUBLANE - 1
    return pl.pallas_call(
        _proj_in_kernel,
        grid=(B, T // tm),
        in_specs=[
            pl.BlockSpec((1, tm, D), lambda b, i: (b, i, 0)),
            pl.BlockSpec((1, SUBLANE, D), lambda b, i: (b, jnp.maximum(i * nb - 1, 0), 0)),
            pl.BlockSpec((1, SUBLANE, D), lambda b, i: (b, jnp.minimum((i + 1) * nb, last), 0)),
            pl.BlockSpec((1, 1, D), lambda b, i: (b, 0, 0)),
            pl.BlockSpec((1, 1, D), lambda b, i: (b, 0, 0)),
            pl.BlockSpec((D, n), lambda b, i: (0, 0)),
            pl.BlockSpec((D, LANE), lambda b, i: (0, 0)),
            pl.BlockSpec((3, 2 * MIX_MLSTM), lambda b, i: (0, 0)),
        ],
        out_specs=[
            pl.BlockSpec((1, tm, n), lambda b, i: (b, i, 0)),
            pl.BlockSpec((1, tm, LANE), lambda b, i: (b, i, 0)),
        ],
        out_shape=[
            jax.ShapeDtypeStruct((B, T, n), BF16),
            jax.ShapeDtypeStruct((B, T, LANE), F32),
        ],
        scratch_shapes=[pltpu.VMEM((tm + 2 * SUBLANE, PROJ_CHUNK), F32)],
        compiler_params=_params(2),
        name="proj_in",
    )(x, x, x, shift, scale, w_main, w_gate, w_conv)


def _proj_out_kernel(yp_ref, yf_ref, ym_ref, x_ref, gate_ref, wp_ref, wf_ref, wm_ref, lg_ref, lb_ref, o_ref):
    f = (jnp.dot(yp_ref[0], wp_ref[...], preferred_element_type=F32)
         + jnp.dot(yf_ref[0], wf_ref[...], preferred_element_type=F32)
         + jnp.dot(ym_ref[0], wm_ref[...], preferred_element_type=F32))
    r = ALPHA * x_ref[0] + gate_ref[0] * f
    o_ref[0] = _ln_rows(r) * lg_ref[...] + lb_ref[...]


def _proj_out(yp, yf, ym, x, gate, w, ln_g, ln_b, tm):
    B, T, D = x.shape
    tok = lambda b, i: (b, i, 0)
    return pl.pallas_call(
        _proj_out_kernel,
        grid=(B, T // tm),
        in_specs=[
            pl.BlockSpec((1, tm, MIX_POOL), tok),
            pl.BlockSpec((1, tm, MIX_FOUR), tok),
            pl.BlockSpec((1, tm, MIX_MLSTM), tok),
            pl.BlockSpec((1, tm, D), tok),
            pl.BlockSpec((1, 1, D), lambda b, i: (b, 0, 0)),
            pl.BlockSpec((MIX_POOL, D), lambda b, i: (0, 0)),
            pl.BlockSpec((MIX_FOUR, D), lambda b, i: (OFF_FOUR // MIX_FOUR, 0)),
            pl.BlockSpec((MIX_MLSTM, D), lambda b, i: (OFF_O // MIX_MLSTM, 0)),
            pl.BlockSpec((1, D), lambda b, i: (0, 0)),
            pl.BlockSpec((1, D), lambda b, i: (0, 0)),
        ],
        out_specs=pl.BlockSpec((1, tm, D), tok),
        out_shape=jax.ShapeDtypeStruct((B, T, D), F32),
        compiler_params=_params(2),
        name="proj_out",
    )(yp, yf, ym, x, gate, w, w, w, ln_g, ln_b)


def _gelu_tanh(x):
    c = math.sqrt(2.0 / math.pi)
    return 0.5 * x * (1.0 + jnp.tanh(c * (x + 0.044715 * (x * x * x))))


def _ffn_kernel(x_ref, xp_ref, xn_ref, sh_ref, sc_ref, gate_ref, wup_ref, wc_ref, bc_ref, wdn_ref,
                lg_ref, lb_ref, o_ref, ha_ref, hg_ref, p_ref):
    i = pl.program_id(1)
    tm = x_ref.shape[1]
    x = x_ref[0]
    mod_scale = 1.0 + sc_ref[0]
    mod_shift = sh_ref[0]
    has_prev = (i > 0).astype(F32)
    has_next = (i < pl.num_programs(1) - 1).astype(F32)
    u_prev = (_ln_rows(xp_ref[0]) * mod_scale + mod_shift) * has_prev
    u_next = (_ln_rows(xn_ref[0]) * mod_scale + mod_shift) * has_next
    u_main = _ln_rows(x) * mod_scale + mod_shift
    u_ext = jnp.concatenate([u_prev, u_main, u_next], axis=0).astype(BF16)

    def conv(h_ref, cols):
        w = wc_ref[:, cols]
        return (h_ref[SUBLANE - 1:SUBLANE - 1 + tm, :] * w[0:1]
                + h_ref[SUBLANE:SUBLANE + tm, :] * w[1:2]
                + h_ref[SUBLANE + 1:SUBLANE + 1 + tm, :] * w[2:3]
                + bc_ref[:, cols])

    for c in range(D_FF // FF_CHUNK):
        ca = slice(c * FF_CHUNK, (c + 1) * FF_CHUNK)
        cg = slice(D_FF + c * FF_CHUNK, D_FF + (c + 1) * FF_CHUNK)
        ha_ref[...] = jnp.dot(u_ext, wup_ref[:, ca], preferred_element_type=F32)
        hg_ref[...] = jnp.dot(u_ext, wup_ref[:, cg], preferred_element_type=F32)
        p_ref[:, ca] = (_gelu_tanh(conv(hg_ref, cg)) * conv(ha_ref, ca)).astype(BF16)

    f = jnp.dot(p_ref[...], wdn_ref[...], preferred_element_type=F32)
    r = ALPHA * x + gate_ref[0] * f
    o_ref[0] = _ln_rows(r) * lg_ref[...] + lb_ref[...]


def _ffn(x, shift, scale, gate, w_up, w_conv, b_conv, w_down, ln_g, ln_b, tm):
    B, T, D = x.shape
    nb = tm // SUBLANE
    last = T // SUBLANE - 1
    const = dict(pipeline_mode=pl.Buffered(1))
    return pl.pallas_call(
        _ffn_kernel,
        grid=(B, T // tm),
        in_specs=[
            pl.BlockSpec((1, tm, D), lambda b, i: (b, i, 0)),
            pl.BlockSpec((1, SUBLANE, D), lambda b, i: (b, jnp.maximum(i * nb - 1, 0), 0)),
            pl.BlockSpec((1, SUBLANE, D), lambda b, i: (b, jnp.minimum((i + 1) * nb, last), 0)),
            pl.BlockSpec((1, 1, D), lambda b, i: (b, 0, 0)),
            pl.BlockSpec((1, 1, D), lambda b, i: (b, 0, 0)),
            pl.BlockSpec((1, 1, D), lambda b, i: (b, 0, 0)),
            pl.BlockSpec((D, 2 * D_FF), lambda b, i: (0, 0), **const),
            pl.BlockSpec((3, 2 * D_FF), lambda b, i: (0, 0), **const),
            pl.BlockSpec((1, 2 * D_FF), lambda b, i: (0, 0), **const),
            pl.BlockSpec((D_FF, D), lambda b, i: (0, 0), **const),
            pl.BlockSpec((1, D), lambda b, i: (0, 0)),
            pl.BlockSpec((1, D), lambda b, i: (0, 0)),
        ],
        out_specs=pl.BlockSpec((1, tm, D), lambda b, i: (b, i, 0)),
        out_shape=jax.ShapeDtypeStruct((B, T, D), F32),
        scratch_shapes=[
            pltpu.VMEM((tm + 2 * SUBLANE, FF_CHUNK), F32),
            pltpu.VMEM((tm + 2 * SUBLANE, FF_CHUNK), F32),
            pltpu.VMEM((tm, D_FF), BF16),
        ],
        compiler_params=_params(2),
        name="conv_ffn",
    )(x, x, x, shift, scale, gate, w_up, w_conv, b_conv, w_down, ln_g, ln_b)


def _band_matrix(n, w):
    pos = np.arange(n)
    lo = np.maximum(pos - w // 2, 0)
    hi = np.minimum(pos + w // 2 - 1, n - 1)
    return ((pos[None, :] >= lo[:, None]) & (pos[None, :] <= hi[:, None])).astype(np.float32)


def _pool_filters(grid2d):
    mats = []
    for w in POOL_WINDOWS:
        if grid2d:
            mats.append(np.kron(np.eye(POOL_BLOCK // GRID_W, dtype=np.float32), _band_matrix(GRID_W, w)))
        else:
            mats.append(_band_matrix(POOL_BLOCK, w))
    return jnp.asarray(np.stack(mats), BF16)


def _window_count(idx, half, n):
    return jnp.minimum(idx + (half - 1), n - 1) - jnp.maximum(idx - half, 0) + 1


def _pool_inv_count(T, grid2d):
    pos = lax.broadcasted_iota(jnp.int32, (T, MIX_POOL), 0)
    group = jnp.right_shift(lax.broadcasted_iota(jnp.int32, (T, MIX_POOL), 1), GRID_SHIFT)
    half = jnp.left_shift(1, group)
    if grid2d:
        count = (_window_count(jnp.right_shift(pos, GRID_SHIFT), half, T // GRID_W)
                 * _window_count(jnp.bitwise_and(pos, GRID_W - 1), half, GRID_W))
    else:
        count = _window_count(pos, half, T)
    return 1.0 / count.astype(F32)


def _pool_kernel(z_ref, a_ref, n_ref, w_ref, s_ref, y_ref, acc_ref, *, grid2d):
    T = z_ref.shape[1]
    nblk = T // POOL_BLOCK
    lane = lax.broadcasted_iota(jnp.int32, (POOL_BLOCK, MIX_POOL), 1)
    group = jnp.right_shift(lane, GRID_SHIFT)

    def col_filter(blk):
        r0 = pl.multiple_of(blk * POOL_BLOCK, POOL_BLOCK)
        xb = z_ref[0, pl.ds(r0, POOL_BLOCK), :]
        out = jnp.dot(a_ref[0], xb, preferred_element_type=F32)
        for g in range(1, N_POOL):
            out = jnp.where(group == g, jnp.dot(a_ref[g], xb, preferred_element_type=F32), out)
        return r0, xb, out

    def finish(r0, xb, box):
        p = box * n_ref[pl.ds(r0, POOL_BLOCK), :] - xb.astype(F32)
        y = jnp.dot(p.astype(BF16), w_ref[...], preferred_element_type=F32) * s_ref[...]
        y_ref[0, pl.ds(r0, POOL_BLOCK), :] = y.astype(y_ref.dtype)

    if not grid2d:
        assert T == POOL_BLOCK
        finish(*col_filter(0))
        return

    zeros = jnp.zeros((POOL_PAD, MIX_POOL), F32)
    acc_ref[0:POOL_PAD, :] = zeros
    acc_ref[POOL_PAD + T:POOL_PAD + T + POOL_PAD, :] = zeros

    def phase1(blk, carry):
        r0, _, out = col_filter(blk)
        acc_ref[pl.ds(POOL_PAD + r0, POOL_BLOCK), :] = out
        return carry

    lax.fori_loop(0, nblk, phase1, 0)

    lane_t = lax.broadcasted_iota(jnp.int32, (POOL_BLOCK, LANE), 1)

    def phase2(blk, carry):
        r0 = pl.multiple_of(blk * POOL_BLOCK, POOL_BLOCK)
        base = POOL_PAD + r0

        def slab(k, tile):
            return acc_ref[pl.ds(base + k * GRID_W, POOL_BLOCK), tile * LANE:(tile + 1) * LANE]

        s2 = slab(-1, 0) + slab(0, 0)
        s4 = s2 + slab(-2, 0) + slab(1, 0)
        s8 = slab(-4, 1)
        for k in range(-3, 4):
            s8 = s8 + slab(k, 1)
        s16 = s8
        for k in list(range(-8, -4)) + list(range(4, 8)):
            s16 = s16 + slab(k, 1)
        box = jnp.concatenate([jnp.where(lane_t < POOL_DIM, s2, s4), jnp.where(lane_t < POOL_DIM, s8, s16)], axis=1)
        finish(r0, z_ref[0, pl.ds(r0, POOL_BLOCK), :], box)
        return carry

    lax.fori_loop(0, nblk, phase2, 0)


def _pool(z, filters, inv_count, w_bd, s_row, grid2d):
    B, T, _ = z.shape
    return pl.pallas_call(
        functools.partial(_pool_kernel, grid2d=grid2d),
        grid=(B,),
        in_specs=[
            pl.BlockSpec((1, T, MIX_POOL), lambda b: (b, 0, 0)),
            pl.BlockSpec((N_POOL, POOL_BLOCK, POOL_BLOCK), lambda b: (0, 0, 0)),
            pl.BlockSpec((T, MIX_POOL), lambda b: (0, 0)),
            pl.BlockSpec((MIX_POOL, MIX_POOL), lambda b: (0, 0)),
            pl.BlockSpec((1, MIX_POOL), lambda b: (0, 0)),
        ],
        out_specs=pl.BlockSpec((1, T, MIX_POOL), lambda b: (b, 0, 0)),
        out_shape=jax.ShapeDtypeStruct((B, T, MIX_POOL), BF16),
        scratch_shapes=[pltpu.VMEM((T + 2 * POOL_PAD, MIX_POOL), F32)],
        compiler_params=_params(1),
        name="pool2d" if grid2d else "pool1d",
    )(z, filters, inv_count, w_bd, s_row)


def _block_diag(w):
    G, n, m = w.shape
    eye = jnp.eye(G, dtype=w.dtype)
    return (eye[:, None, :, None] * w[:, :, None, :]).reshape(G * n, G * m)


def _four_mix_kernel(z_ref, g_ref, y_ref):
    y = jnp.dot(z_ref[0], g_ref[...], preferred_element_type=F32).astype(y_ref.dtype)
    y_ref[0, 0] = y[:, :MIX_FOUR]
    y_ref[0, 1] = y[:, MIX_FOUR:]


def _four_dft_kernel(d_ref, y_ref, o_ref):
    rows = 256
    for r0 in range(0, o_ref.shape[1], rows):
        o_ref[0, r0:r0 + rows, :] = jnp.dot(d_ref[r0:r0 + rows, :], y_ref[0],
                                            preferred_element_type=F32).astype(o_ref.dtype)


def _fourier(z, g_mix, dft):
    B, T, _ = z.shape
    tm = min(T, 2048)
    y = pl.pallas_call(
        _four_mix_kernel,
        grid=(B, T // tm),
        in_specs=[
            pl.BlockSpec((1, tm, MIX_FOUR), lambda b, i: (b, i, OFF_FOUR // MIX_FOUR)),
            pl.BlockSpec((MIX_FOUR, 2 * MIX_FOUR), lambda b, i: (0, 0)),
        ],
        out_specs=pl.BlockSpec((1, 2, tm, MIX_FOUR), lambda b, i: (b, 0, i, 0)),
        out_shape=jax.ShapeDtypeStruct((B, 2, T, MIX_FOUR), BF16),
        compiler_params=_params(2),
        name="fourier_mix",
    )(z, g_mix)
    y = y.reshape(B, 2 * T, MIX_FOUR)
    tf = min(T, 1024)
    return pl.pallas_call(
        _four_dft_kernel,
        grid=(T // tf, B),
        in_specs=[
            pl.BlockSpec((tf, 2 * T), lambda m, b: (m, 0)),
            pl.BlockSpec((1, 2 * T, MIX_FOUR), lambda m, b: (b, 0, 0)),
        ],
        out_specs=pl.BlockSpec((1, tf, MIX_FOUR), lambda m, b: (b, m, 0)),
        out_shape=jax.ShapeDtypeStruct((B, T, MIX_FOUR), BF16),
        compiler_params=_params(2),
        name="fourier_dft",
    )(dft, y)


def _dft_matrix(n):
    a = n // GRID_W
    f = lax.broadcasted_iota(jnp.int32, (n, 1), 0)
    ang1 = ((f * GRID_W * lax.broadcasted_iota(jnp.int32, (1, a), 1)) % n).astype(F32) * (2.0 * math.pi / n)
    ang2 = ((f * lax.broadcasted_iota(jnp.int32, (1, GRID_W), 1)) % n).astype(F32) * (2.0 * math.pi / n)
    c1, s1 = jnp.cos(ang1)[:, :, None], jnp.sin(ang1)[:, :, None]
    c2, s2 = jnp.cos(ang2)[:, None, :], jnp.sin(ang2)[:, None, :]
    cos = (c1 * c2 - s1 * s2).reshape(n, n)
    sin = (s1 * c2 + c1 * s2).reshape(n, n)
    return jnp.concatenate([cos, sin], axis=1).astype(BF16)


def _fourier_mix_weights(w_four, T):
    c = np.arange(FOUR_DIM)
    ang = 2.0 * np.pi * np.outer(c, c) / FOUR_DIM
    norm = 1.0 / math.sqrt(T * FOUR_DIM)
    gr = jnp.einsum('cf,gfe->gce', jnp.asarray(np.cos(ang) * norm, F32), w_four, precision='highest')
    gi = jnp.einsum('cf,gfe->gce', jnp.asarray(-np.sin(ang) * norm, F32), w_four, precision='highest')
    return jnp.concatenate([_block_diag(gr), _block_diag(gi)], axis=1).astype(BF16)


def _mlstm_kernel(q_ref, k_ref, v_ref, o_ref, g_ref, bg_ref, cf0_ref, mf0_ref, cb0_ref, mb0_ref,
                  y_ref, cf1_ref, mf1_ref, cb1_ref, mb1_ref,
                  hf_ref, hb_ref, pf_ref, pb_ref, uf_ref, ub_ref, rf_ref, rb_ref):
    T = q_ref.shape[1]
    L = MLSTM_CHUNK
    dh = MLSTM_DH
    aug = MLSTM_AUG
    nc = T // L
    head = pl.program_id(1)

    row = lax.broadcasted_iota(jnp.int32, (L, L), 0)
    col = lax.broadcasted_iota(jnp.int32, (L, L), 1)
    masks = (row <= col, row >= col)
    row2 = lax.broadcasted_iota(jnp.int32, (L, 2 * L), 0)
    col2 = jnp.bitwise_and(lax.broadcasted_iota(jnp.int32, (L, 2 * L), 1), L - 1)
    tri2 = jnp.where(col2 <= row2, 1.0, 0.0).astype(BF16)
    diag = row == col
    sub8 = lax.broadcasted_iota(jnp.int32, (SUBLANE, L), 0)
    ones_t = jnp.ones((aug - dh, L), F32)
    sel_row = jnp.bitwise_and(lax.broadcasted_iota(jnp.int32, (2 * LANE, 2 * LANE), 0), LANE - 1)
    sel_dir = jnp.right_shift(lax.broadcasted_iota(jnp.int32, (2 * LANE, 2 * LANE), 1), LANE_SHIFT)
    sel_i = jnp.where(sel_row == sel_dir * (2 * N_MLSTM) + head, 1.0, 0.0).astype(BF16)
    sel_f = jnp.where(sel_row == sel_dir * (2 * N_MLSTM) + N_MLSTM + head, 1.0, 0.0).astype(BF16)
    p_refs, u_refs, r_refs, h_refs = (pf_ref, pb_ref), (uf_ref, ub_ref), (rf_ref, rb_ref), (hf_ref, hb_ref)
    bias = bg_ref[...]

    def prep_group(c0):
        cs = [c0 + j for j in range(min(MLSTM_GROUP, nc))]
        rs = [pl.multiple_of(c * L, L) for c in cs]
        q = [q_ref[0, pl.ds(r, L), :] for r in rs]
        k = [k_ref[0, pl.ds(r, L), :] for r in rs]
        vt = [jnp.concatenate([v_ref[0, pl.ds(r, L), :].astype(F32).T, ones_t], axis=0).astype(BF16) for r in rs]
        g = [(g_ref[0, pl.ds(r, L), :] + bias) * LOG2_E for r in rs]
        i_rep = [jnp.dot(jnp.concatenate(_split_hi_lo(gg), axis=1), sel_i, preferred_element_type=F32)
                 for gg in g]
        lf_rep = [jnp.dot(jnp.concatenate(_split_hi_lo(_log2_sigmoid(gg)), axis=1), sel_f,
                          preferred_element_type=F32) for gg in g]
        st = [lax.dot_general(kk, qq, (((1,), (1,)), ((), ())), preferred_element_type=F32)
              for kk, qq in zip(k, q)]
        cum = [jnp.dot(tri2, jnp.concatenate(_split_hi_lo(f), axis=0), preferred_element_type=F32)
               for f in lf_rep]
        items = [(j, d) for j in range(len(cs)) for d in range(2)]
        for j, d in items:
            if d == 0:
                b = cum[j][:, :L]
                total = b[L - 1:L, :]
            else:
                total = cum[j][L - 1:L, L:]
                b = total - cum[j][:, L:] + lf_rep[j][:, L:]
            r_rep = i_rep[j][:, d * L:(d + 1) * L] - b
            dmt = jnp.where(masks[d], r_rep, NEG_BIG)
            pm = jnp.max(dmt, axis=0, keepdims=True)
            sg = (st[j] * jnp.exp2(dmt - pm)).astype(BF16)
            b_row = jnp.sum(jnp.where(diag, b, 0.0), axis=0, keepdims=True)
            g_end = total + r_rep
            g_loc = jnp.max(g_end, axis=0, keepdims=True)
            kw = (k[j].astype(F32) * jnp.exp2(g_end - g_loc)).astype(BF16)
            base = pl.multiple_of(cs[j] * aug, BF16_ROWS)
            pu = jnp.dot(vt[j], jnp.concatenate([sg, kw], axis=1), preferred_element_type=F32)
            p_refs[d][pl.ds(base, aug), :] = pu[:, :L]
            u_refs[d][pl.ds(base, aug), :] = pu[:, L:]
            rbase = pl.multiple_of(cs[j] * SUBLANE, SUBLANE)
            r_refs[d][pl.ds(rbase, SUBLANE), :] = jnp.where(
                sub8 == 0, b_row, jnp.where(sub8 == 1, b_row + pm, jnp.where(sub8 == 2, total, g_loc)))

    ga = min(MLSTM_GROUP, nc)

    def body_a(i, carry):
        prep_group(i * ga)
        return carry

    lax.fori_loop(0, nc // ga, body_a, 0)

    def step(d, c, ct, m):
        base = pl.multiple_of(c * aug, BF16_ROWS)
        rows = r_refs[d][pl.ds(pl.multiple_of(c * SUBLANE, SUBLANE), SUBLANE), :]
        b_row, a_row, total, g_loc = rows[0:1], rows[1:2], rows[2:3], rows[3:4]
        inter = b_row + m
        m_j = jnp.maximum(inter, a_row)
        qct = lax.dot_general(ct.astype(BF16), q_ref[0, pl.ds(pl.multiple_of(c * L, L), L), :],
                              (((1,), (1,)), ((), ())), preferred_element_type=F32)
        nd = jnp.exp2(inter - m_j) * qct + jnp.exp2(a_row - m_j) * p_refs[d][pl.ds(base, aug), :]
        den = jnp.maximum(jnp.abs(nd[dh:dh + 1]), jnp.exp2(-m_j))
        h_refs[d][pl.ds(pl.multiple_of(c * dh, dh), dh), :] = nd[:dh] / den
        m_new = jnp.maximum(total + m, g_loc)
        ct_new = jnp.exp2(total + m - m_new) * ct + jnp.exp2(g_loc - m_new) * u_refs[d][pl.ds(base, aug), :]
        return ct_new, m_new

    gb = min(nc, MLSTM_GROUP)

    def body_b(i, carry):
        cf, mf, cb, mb = carry
        for j in range(gb):
            cf, mf = step(0, i * gb + j, cf, mf)
            cb, mb = step(1, nc - 1 - (i * gb + j), cb, mb)
        return cf, mf, cb, mb

    cf, mf, cb, mb = lax.fori_loop(
        0, nc // gb, body_b, (cf0_ref[0, 0], mf0_ref[0, 0][0:1, :], cb0_ref[0, 0], mb0_ref[0, 0][0:1, :]))

    cf1_ref[0, 0] = cf
    cb1_ref[0, 0] = cb
    mf1_ref[0, 0] = jnp.broadcast_to(mf, (SUBLANE, LANE))
    mb1_ref[0, 0] = jnp.broadcast_to(mb, (SUBLANE, LANE))

    go = min(nc, MLSTM_GROUP)

    def body_out(i, carry):
        for j in range(go):
            c = i * go + j
            r = pl.multiple_of(c * L, L)
            rh = pl.multiple_of(c * dh, dh)
            h = (hf_ref[pl.ds(rh, dh), :] + hb_ref[pl.ds(rh, dh), :]).T
            y_ref[0, pl.ds(r, L), :] = (_sigmoid(o_ref[0, pl.ds(r, L), :].astype(F32)) * h).astype(y_ref.dtype)
        return carry

    lax.fori_loop(0, nc // go, body_out, 0)


def _mlstm(z, zg, b_gate, cf0, mf0, cb0, mb0):
    B, T, _ = z.shape
    H, dh, L, aug = N_MLSTM, MLSTM_DH, MLSTM_CHUNK, MLSTM_AUG
    nc = T // L
    zcol = lambda off: (lambda b, h: (b, 0, off // dh + h))
    st = lambda b, h: (b, h, 0, 0)
    c_spec = pl.BlockSpec((1, 1, aug, dh), st)
    m_spec = pl.BlockSpec((1, 1, SUBLANE, LANE), st)
    c_shape = jax.ShapeDtypeStruct((B, H, aug, dh), F32)
    m_shape = jax.ShapeDtypeStruct((B, H, SUBLANE, LANE), F32)
    return pl.pallas_call(
        _mlstm_kernel,
        grid=(B, H),
        in_specs=[
            pl.BlockSpec((1, T, dh), zcol(OFF_Q)),
            pl.BlockSpec((1, T, dh), zcol(OFF_K)),
            pl.BlockSpec((1, T, dh), zcol(OFF_V)),
            pl.BlockSpec((1, T, dh), zcol(OFF_O)),
            pl.BlockSpec((1, T, LANE), lambda b, h: (b, 0, 0)),
            pl.BlockSpec((1, LANE), lambda b, h: (0, 0)),
            c_spec, m_spec, c_spec, m_spec,
        ],
        out_specs=[pl.BlockSpec((1, T, dh), lambda b, h: (b, 0, h)), c_spec, m_spec, c_spec, m_spec],
        out_shape=[jax.ShapeDtypeStruct((B, T, MIX_MLSTM), BF16), c_shape, m_shape, c_shape, m_shape],
        scratch_shapes=[
            pltpu.VMEM((nc * dh, L), F32),
            pltpu.VMEM((nc * dh, L), F32),
            pltpu.VMEM((nc * aug, L), F32),
            pltpu.VMEM((nc * aug, L), F32),
            pltpu.VMEM((nc * aug, dh), F32),
            pltpu.VMEM((nc * aug, dh), F32),
            pltpu.VMEM((nc * SUBLANE, LANE), F32),
            pltpu.VMEM((nc * SUBLANE, LANE), F32),
        ],
        compiler_params=_params(2),
        name="mlstm",
    )(z, z, z, z, zg, b_gate, cf0, mf0, cb0, mb0)


def kernel(x, c, ctx, c_ctx, w_mod, b_mod, w_in, conv_qk, b_gates, pool_w, pool_scale, fourier_w,
           w_out, ln1_g, ln1_b, w_up, conv_ffn_w, conv_ffn_b, w_down, ln2_g, ln2_b):
    B, T, D = x.shape
    Tc = ctx.shape[1]
    tm_x = 512
    tm_c = Tc
    xc = ctx
    s_c = jax.nn.silu(c)
    s_ctx = jax.nn.silu(c_ctx)
    filt_x, filt_c = _pool_filters(True), _pool_filters(False)
    cnt_x, cnt_c = _pool_inv_count(T, True), _pool_inv_count(Tc, False)
    dft_x, dft_c = _dft_matrix(T), _dft_matrix(Tc)
    zero_c = jnp.zeros((B, N_MLSTM, MLSTM_AUG, MLSTM_DH), F32)
    zero_m = jnp.zeros((B, N_MLSTM, SUBLANE, LANE), F32)
    for l in range(DEPTH):
        last = l == DEPTH - 1
        mx = [m[:, None, :] for m in jnp.split(s_c @ w_mod[l] + b_mod[l], 6, axis=-1)]
        mc = [jnp.broadcast_to(m[None, None, :], (B, 1, D))
              for m in jnp.split(s_ctx @ w_mod[l] + b_mod[l], 6, axis=-1)]
        w_main = w_in[l][:, :OFF_G].astype(BF16)
        w_gate = jnp.pad(w_in[l][:, OFF_G:], ((0, 0), (0, LANE - N_GATES))).astype(BF16)
        b_gate = jnp.pad(b_gates[l], (0, LANE - N_GATES))[None]
        w_pool_bd = _block_diag(pool_w[l]).astype(BF16)
        s_pool = pool_scale[l][None]
        w_out_b = w_out[l].astype(BF16)
        w_up_b = w_up[l].astype(BF16)
        w_down_b = w_down[l].astype(BF16)
        g1, b1 = ln1_g[l][None], ln1_b[l][None]
        g2, b2 = ln2_g[l][None], ln2_b[l][None]
        bcv = conv_ffn_b[l][None]

        zc, zgc = _proj_in(xc, mc[0], mc[1], w_main, w_gate, conv_qk[l], tm_c)
        ymc, cf, mf, cb, mb = _mlstm(zc, zgc, b_gate, zero_c, zero_m, zero_c, zero_m)
        zx, zgx = _proj_in(x, mx[0], mx[1], w_main, w_gate, conv_qk[l], tm_x)
        ymx = _mlstm(zx, zgx, b_gate, cf, mf, cb, mb)[0]
        ypx = _pool(zx, filt_x, cnt_x, w_pool_bd, s_pool, True)
        yfx = _fourier(zx, _fourier_mix_weights(fourier_w[l], T), dft_x)
        x = _proj_out(ypx, yfx, ymx, x, mx[2], w_out_b, g1, b1, tm_x)
        x = _ffn(x, mx[3], mx[4], mx[5], w_up_b, conv_ffn_w[l], bcv, w_down_b, g2, b2, tm_x)
        if not last:
            ypc = _pool(zc, filt_c, cnt_c, w_pool_bd, s_pool, False)
            yfc = _fourier(zc, _fourier_mix_weights(fourier_w[l], Tc), dft_c)
            xc = _proj_out(ypc, yfc, ymc, xc, mc[2], w_out_b, g1, b1, tm_c)
            xc = _ffn(xc, mc[3], mc[4], mc[5], w_up_b, conv_ffn_w[l], bcv, w_down_b, g2, b2, tm_c)
    return x
```

```python
import functools
import math

import jax
import jax.numpy as jnp
import numpy as np
from jax import lax
from jax.experimental import pallas as pl
from jax.experimental.pallas import tpu as pltpu

F32 = jnp.float32
BF16 = jnp.bfloat16

D_MODEL = 1024
DEPTH = 4
GRID_W = 64
GRID_SHIFT = GRID_W.bit_length() - 1
POOL_WINDOWS = (2, 4, 8, 16)
N_POOL = len(POOL_WINDOWS)
MIX_POOL = D_MODEL // 4
POOL_DIM = MIX_POOL // N_POOL
N_FOUR = 4
MIX_FOUR = D_MODEL // 4
FOUR_DIM = MIX_FOUR // N_FOUR
N_MLSTM = 4
MIX_MLSTM = D_MODEL // 2
MLSTM_DH = MIX_MLSTM // N_MLSTM
MLSTM_CHUNK = 128
N_GATES = 2 * 2 * N_MLSTM
D_FF = int(math.ceil(8 * D_MODEL / 3 / 128)) * 128
OFF_FOUR = MIX_POOL
OFF_O = OFF_FOUR + MIX_FOUR
OFF_Q = OFF_O + MIX_MLSTM
OFF_K = OFF_Q + MIX_MLSTM
OFF_V = OFF_K + MIX_MLSTM
OFF_G = OFF_V + MIX_MLSTM
ALPHA = (2 * DEPTH) ** 0.25
LN_EPS = 1e-6
LOG2_E = math.log2(math.e)

LANE = 128
LANE_SHIFT = LANE.bit_length() - 1
SUBLANE = 8
BF16_ROWS = 16
VMEM_LIMIT = 56 * 1024 * 1024
FF_CHUNK = 256
PROJ_CHUNK = 512
POOL_BLOCK = 256
POOL_PAD = (max(POOL_WINDOWS) // 2) * GRID_W
NEG_BIG = -1e30
MLSTM_GROUP = 8
MLSTM_AUG = MLSTM_DH + BF16_ROWS


def _params(n_axes):
    return pltpu.CompilerParams(dimension_semantics=("arbitrary",) * n_axes, vmem_limit_bytes=VMEM_LIMIT)


def _ln_rows(x):
    mu = jnp.mean(x, axis=-1, keepdims=True)
    xc = x - mu
    var = jnp.mean(xc * xc, axis=-1, keepdims=True)
    return xc * lax.rsqrt(var + LN_EPS)


def _sigmoid(x):
    return 0.5 * jnp.tanh(0.5 * x) + 0.5


def _log2_sigmoid(x2):
    return jnp.minimum(x2, 0.0) - jnp.log2(1.0 + jnp.exp2(-jnp.abs(x2)))


def _split_hi_lo(x):
    hi = x.astype(BF16)
    return hi, (x - hi.astype(F32)).astype(BF16)


def _proj_in_kernel(x_ref, xp_ref, xn_ref, sh_ref, sc_ref, w_ref, wg_ref, wc_ref, z_ref, g_ref, h_ref):
    i = pl.program_id(1)
    tm = x_ref.shape[1]
    mod_scale = 1.0 + sc_ref[0]
    mod_shift = sh_ref[0]
    u = _ln_rows(x_ref[0]) * mod_scale + mod_shift
    u_prev = (_ln_rows(xp_ref[0]) * mod_scale + mod_shift) * (i > 0).astype(F32)
    u_next = (_ln_rows(xn_ref[0]) * mod_scale + mod_shift) * (i < pl.num_programs(1) - 1).astype(F32)
    ub = u.astype(BF16)
    u_ext = jnp.concatenate([u_prev, u, u_next], axis=0).astype(BF16)
    n = z_ref.shape[-1]
    chunks = [(c0, min(c0 + PROJ_CHUNK, n)) for c0 in range(0, n, PROJ_CHUNK)]
    conv_chunks = [c for c in chunks if OFF_Q <= c[0] and c[1] <= OFF_V]
    for slot, (c0, c1) in enumerate(conv_chunks):
        h = h_ref.at[slot]
        h[...] = jnp.dot(u_ext, w_ref[:, c0:c1], preferred_element_type=F32)
    for c0, c1 in chunks:
        if (c0, c1) not in conv_chunks:
            z_ref[0, :, c0:c1] = jnp.dot(ub, w_ref[:, c0:c1], preferred_element_type=F32).astype(z_ref.dtype)
    g_ref[0] = jnp.dot(ub, wg_ref[...], preferred_element_type=F32)
    for slot, (c0, c1) in enumerate(conv_chunks):
        h = h_ref.at[slot]
        w = wc_ref[:, c0 - OFF_Q:c1 - OFF_Q]
        y = (h[SUBLANE - 1:SUBLANE - 1 + tm, :] * w[0:1] + h[SUBLANE:SUBLANE + tm, :] * w[1:2]
             + h[SUBLANE + 1:SUBLANE + 1 + tm, :] * w[2:3])
        scale = 1.0 if c0 < OFF_K else MLSTM_DH ** -0.5
        z_ref[0, :, c0:c1] = (y * _sigmoid(y) * scale).astype(z_ref.dtype)


def _proj_in(x, shift, scale, w_main, w_gate, w_conv, tm):
    B, T, D = x.shape
    n = w_main.shape[1]
    assert OFF_Q % PROJ_CHUNK == 0 and OFF_K % PROJ_CHUNK == 0 and OFF_V % PROJ_CHUNK == 0
    nb = tm // SUBLANE
    last = T // SUBLANE - 1
    return pl.pallas_call(
        _proj_in_kernel,
        grid=(B, T // tm),
        in_specs=[
            pl.BlockSpec((1, tm, D), lambda b, i: (b, i, 0)),
            pl.BlockSpec((1, SUBLANE, D), lambda b, i: (b, jnp.maximum(i * nb - 1, 0), 0)),
            pl.BlockSpec((1, SUBLANE, D), lambda b, i: (b, jnp.minimum((i + 1) * nb, last), 0)),
            pl.BlockSpec((1, 1, D), lambda b, i: (b, 0, 0)),
            pl.BlockSpec((1, 1, D), lambda b, i: (b, 0, 0)),
            pl.BlockSpec((D, n), lambda b, i: (0, 0)),
            pl.BlockSpec((D, LANE), lambda b, i: (0, 0)),
            pl.BlockSpec((3, 2 * MIX_MLSTM), lambda b, i: (0, 0)),
        ],
        out_specs=[
            pl.BlockSpec((1, tm, n), lambda b, i: (b, i, 0)),
            pl.BlockSpec((1, tm, LANE), lambda b, i: (b, i, 0)),
        ],
        out_shape=[
            jax.ShapeDtypeStruct((B, T, n), BF16),
            jax.ShapeDtypeStruct((B, T, LANE), F32),
        ],
        scratch_shapes=[pltpu.VMEM((2 * MIX_MLSTM // PROJ_CHUNK, tm + 2 * SUBLANE, PROJ_CHUNK), F32)],
        compiler_params=_params(2),
        name="proj_in",
    )(x, x, x, shift, scale, w_main, w_gate, w_conv)


def _proj_out_kernel(yp_ref, yf_ref, ym_ref, x_ref, gate_ref, wp_ref, wf_ref, wm_ref, lg_ref, lb_ref, o_ref):
    f = (jnp.dot(yp_ref[0], wp_ref[...], preferred_element_type=F32)
         + jnp.dot(yf_ref[0], wf_ref[...], preferred_element_type=F32)
         + jnp.dot(ym_ref[0], wm_ref[...], preferred_element_type=F32))
    r = ALPHA * x_ref[0] + gate_ref[0] * f
    o_ref[0] = _ln_rows(r) * lg_ref[...] + lb_ref[...]


def _proj_out(yp, yf, ym, x, gate, w, ln_g, ln_b, tm):
    B, T, D = x.shape
    tok = lambda b, i: (b, i, 0)
    return pl.pallas_call(
        _proj_out_kernel,
        grid=(B, T // tm),
        in_specs=[
            pl.BlockSpec((1, tm, MIX_POOL), tok),
            pl.BlockSpec((1, tm, MIX_FOUR), tok),
            pl.BlockSpec((1, tm, MIX_MLSTM), tok),
            pl.BlockSpec((1, tm, D), tok),
            pl.BlockSpec((1, 1, D), lambda b, i: (b, 0, 0)),
            pl.BlockSpec((MIX_POOL, D), lambda b, i: (0, 0)),
            pl.BlockSpec((MIX_FOUR, D), lambda b, i: (OFF_FOUR // MIX_FOUR, 0)),
            pl.BlockSpec((MIX_MLSTM, D), lambda b, i: (OFF_O // MIX_MLSTM, 0)),
            pl.BlockSpec((1, D), lambda b, i: (0, 0)),
            pl.BlockSpec((1, D), lambda b, i: (0, 0)),
        ],
        out_specs=pl.BlockSpec((1, tm, D), tok),
        out_shape=jax.ShapeDtypeStruct((B, T, D), F32),
        compiler_params=_params(2),
        name="proj_out",
    )(yp, yf, ym, x, gate, w, w, w, ln_g, ln_b)


def _gelu_tanh(x):
    c = math.sqrt(2.0 / math.pi)
    return 0.5 * x * (1.0 + jnp.tanh(c * (x + 0.044715 * (x * x * x))))


def _ffn_kernel(x_ref, xp_ref, xn_ref, sh_ref, sc_ref, gate_ref, wup_ref, wc_ref, bc_ref, wdn_ref,
                lg_ref, lb_ref, o_ref, ha_ref, hg_ref, p_ref):
    i = pl.program_id(1)
    tm = x_ref.shape[1]
    x = x_ref[0]
    mod_scale = 1.0 + sc_ref[0]
    mod_shift = sh_ref[0]
    has_prev = (i > 0).astype(F32)
    has_next = (i < pl.num_programs(1) - 1).astype(F32)
    u_prev = (_ln_rows(xp_ref[0]) * mod_scale + mod_shift) * has_prev
    u_next = (_ln_rows(xn_ref[0]) * mod_scale + mod_shift) * has_next
    u_main = _ln_rows(x) * mod_scale + mod_shift
    u_ext = jnp.concatenate([u_prev, u_main, u_next], axis=0).astype(BF16)

    def conv(h_ref, cols):
        w = wc_ref[:, cols]
        return (h_ref[SUBLANE - 1:SUBLANE - 1 + tm, :] * w[0:1]
                + h_ref[SUBLANE:SUBLANE + tm, :] * w[1:2]
                + h_ref[SUBLANE + 1:SUBLANE + 1 + tm, :] * w[2:3]
                + bc_ref[:, cols])

    for c in range(D_FF // FF_CHUNK):
        ca = slice(c * FF_CHUNK, (c + 1) * FF_CHUNK)
        cg = slice(D_FF + c * FF_CHUNK, D_FF + (c + 1) * FF_CHUNK)
        ha_ref[...] = jnp.dot(u_ext, wup_ref[:, ca], preferred_element_type=F32)
        hg_ref[...] = jnp.dot(u_ext, wup_ref[:, cg], preferred_element_type=F32)
        p_ref[:, ca] = (_gelu_tanh(conv(hg_ref, cg)) * conv(ha_ref, ca)).astype(BF16)

    f = jnp.dot(p_ref[...], wdn_ref[...], preferred_element_type=F32)
    r = ALPHA * x + gate_ref[0] * f
    o_ref[0] = _ln_rows(r) * lg_ref[...] + lb_ref[...]


def _ffn(x, shift, scale, gate, w_up, w_conv, b_conv, w_down, ln_g, ln_b, tm):
    B, T, D = x.shape
    nb = tm // SUBLANE
    last = T // SUBLANE - 1
    const = dict(pipeline_mode=pl.Buffered(1))
    return pl.pallas_call(
        _ffn_kernel,
        grid=(B, T // tm),
        in_specs=[
            pl.BlockSpec((1, tm, D), lambda b, i: (b, i, 0)),
            pl.BlockSpec((1, SUBLANE, D), lambda b, i: (b, jnp.maximum(i * nb - 1, 0), 0)),
            pl.BlockSpec((1, SUBLANE, D), lambda b, i: (b, jnp.minimum((i + 1) * nb, last), 0)),
            pl.BlockSpec((1, 1, D), lambda b, i: (b, 0, 0)),
            pl.BlockSpec((1, 1, D), lambda b, i: (b, 0, 0)),
            pl.BlockSpec((1, 1, D), lambda b, i: (b, 0, 0)),
            pl.BlockSpec((D, 2 * D_FF), lambda b, i: (0, 0), **const),
            pl.BlockSpec((3, 2 * D_FF), lambda b, i: (0, 0), **const),
            pl.BlockSpec((1, 2 * D_FF), lambda b, i: (0, 0), **const),
            pl.BlockSpec((D_FF, D), lambda b, i: (0, 0), **const),
            pl.BlockSpec((1, D), lambda b, i: (0, 0)),
            pl.BlockSpec((1, D), lambda b, i: (0, 0)),
        ],
        out_specs=pl.BlockSpec((1, tm, D), lambda b, i: (b, i, 0)),
        out_shape=jax.ShapeDtypeStruct((B, T, D), F32),
        scratch_shapes=[
            pltpu.VMEM((tm + 2 * SUBLANE, FF_CHUNK), F32),
            pltpu.VMEM((tm + 2 * SUBLANE, FF_CHUNK), F32),
            pltpu.VMEM((tm, D_FF), BF16),
        ],
        compiler_params=_params(2),
        name="conv_ffn",
    )(x, x, x, shift, scale, gate, w_up, w_conv, b_conv, w_down, ln_g, ln_b)


def _band_matrix(n, w):
    pos = np.arange(n)
    lo = np.maximum(pos - w // 2, 0)
    hi = np.minimum(pos + w // 2 - 1, n - 1)
    return ((pos[None, :] >= lo[:, None]) & (pos[None, :] <= hi[:, None])).astype(np.float32)


def _pool_filters(grid2d):
    mats = []
    for w in POOL_WINDOWS:
        if grid2d:
            mats.append(np.kron(np.eye(POOL_BLOCK // GRID_W, dtype=np.float32), _band_matrix(GRID_W, w)))
        else:
            mats.append(_band_matrix(POOL_BLOCK, w))
    return jnp.asarray(np.stack(mats), BF16)


def _window_count(idx, half, n):
    return jnp.minimum(idx + (half - 1), n - 1) - jnp.maximum(idx - half, 0) + 1


def _pool_inv_count(T, grid2d):
    pos = lax.broadcasted_iota(jnp.int32, (T, MIX_POOL), 0)
    group = jnp.right_shift(lax.broadcasted_iota(jnp.int32, (T, MIX_POOL), 1), GRID_SHIFT)
    half = jnp.left_shift(1, group)
    if grid2d:
        count = (_window_count(jnp.right_shift(pos, GRID_SHIFT), half, T // GRID_W)
                 * _window_count(jnp.bitwise_and(pos, GRID_W - 1), half, GRID_W))
    else:
        count = _window_count(pos, half, T)
    return 1.0 / count.astype(F32)


def _pool_kernel(z_ref, a_ref, n_ref, w_ref, s_ref, y_ref, acc_ref, *, grid2d):
    T = z_ref.shape[1]
    nblk = T // POOL_BLOCK
    lane = lax.broadcasted_iota(jnp.int32, (POOL_BLOCK, MIX_POOL), 1)
    group = jnp.right_shift(lane, GRID_SHIFT)

    def col_filter(blk):
        r0 = pl.multiple_of(blk * POOL_BLOCK, POOL_BLOCK)
        xb = z_ref[0, pl.ds(r0, POOL_BLOCK), :]
        out = jnp.dot(a_ref[0], xb, preferred_element_type=F32)
        for g in range(1, N_POOL):
            out = jnp.where(group == g, jnp.dot(a_ref[g], xb, preferred_element_type=F32), out)
        return r0, xb, out

    def finish(r0, xb, box):
        p = box * n_ref[pl.ds(r0, POOL_BLOCK), :] - xb.astype(F32)
        y = jnp.dot(p.astype(BF16), w_ref[...], preferred_element_type=F32) * s_ref[...]
        y_ref[0, pl.ds(r0, POOL_BLOCK), :] = y.astype(y_ref.dtype)

    if not grid2d:
        assert T == POOL_BLOCK
        finish(*col_filter(0))
        return

    zeros = jnp.zeros((POOL_PAD, MIX_POOL), F32)
    acc_ref[0:POOL_PAD, :] = zeros
    acc_ref[POOL_PAD + T:POOL_PAD + T + POOL_PAD, :] = zeros

    def phase1(blk, carry):
        r0, _, out = col_filter(blk)
        acc_ref[pl.ds(POOL_PAD + r0, POOL_BLOCK), :] = out
        return carry

    lax.fori_loop(0, nblk, phase1, 0)

    lane_t = lax.broadcasted_iota(jnp.int32, (POOL_BLOCK, LANE), 1)

    def phase2(blk, carry):
        r0 = pl.multiple_of(blk * POOL_BLOCK, POOL_BLOCK)
        base = POOL_PAD + r0

        def slab(k, tile):
            return acc_ref[pl.ds(base + k * GRID_W, POOL_BLOCK), tile * LANE:(tile + 1) * LANE]

        s2 = slab(-1, 0) + slab(0, 0)
        s4 = s2 + slab(-2, 0) + slab(1, 0)
        s8 = slab(-4, 1)
        for k in range(-3, 4):
            s8 = s8 + slab(k, 1)
        s16 = s8
        for k in list(range(-8, -4)) + list(range(4, 8)):
            s16 = s16 + slab(k, 1)
        box = jnp.concatenate([jnp.where(lane_t < POOL_DIM, s2, s4), jnp.where(lane_t < POOL_DIM, s8, s16)], axis=1)
        finish(r0, z_ref[0, pl.ds(r0, POOL_BLOCK), :], box)
        return carry

    lax.fori_loop(0, nblk, phase2, 0)


def _pool(z, filters, inv_count, w_bd, s_row, grid2d):
    B, T, _ = z.shape
    return pl.pallas_call(
        functools.partial(_pool_kernel, grid2d=grid2d),
        grid=(B,),
        in_specs=[
            pl.BlockSpec((1, T, MIX_POOL), lambda b: (b, 0, 0)),
            pl.BlockSpec((N_POOL, POOL_BLOCK, POOL_BLOCK), lambda b: (0, 0, 0)),
            pl.BlockSpec((T, MIX_POOL), lambda b: (0, 0)),
            pl.BlockSpec((MIX_POOL, MIX_POOL), lambda b: (0, 0)),
            pl.BlockSpec((1, MIX_POOL), lambda b: (0, 0)),
        ],
        out_specs=pl.BlockSpec((1, T, MIX_POOL), lambda b: (b, 0, 0)),
        out_shape=jax.ShapeDtypeStruct((B, T, MIX_POOL), BF16),
        scratch_shapes=[pltpu.VMEM((T + 2 * POOL_PAD, MIX_POOL), F32)],
        compiler_params=_params(1),
        name="pool2d" if grid2d else "pool1d",
    )(z, filters, inv_count, w_bd, s_row)


def _block_diag(w):
    G, n, m = w.shape
    eye = jnp.eye(G, dtype=w.dtype)
    return (eye[:, None, :, None] * w[:, :, None, :]).reshape(G * n, G * m)


def _four_mix_kernel(z_ref, zr_ref, zm_ref, g_ref, y_ref):
    x = z_ref[0].astype(F32)
    xr = zr_ref[0].astype(F32)
    gr, gi = g_ref[:, :MIX_FOUR], g_ref[:, MIX_FOUR:]
    even = jnp.dot((x + xr).astype(BF16), gr, preferred_element_type=F32)
    odd = jnp.dot((x - xr).astype(BF16), gi, preferred_element_type=F32)
    mid = jnp.dot(zm_ref[0], gr, preferred_element_type=F32)[0:1]
    first = (lax.broadcasted_iota(jnp.int32, odd.shape, 0) == 0) & (pl.program_id(1) == 0)
    y_ref[0, 0] = even.astype(y_ref.dtype)
    y_ref[0, 1] = jnp.where(first, mid, odd).astype(y_ref.dtype)


def _four_dft_kernel(d_ref, y_ref, o_ref):
    rows = 256
    for r0 in range(0, o_ref.shape[1], rows):
        o_ref[0, r0:r0 + rows, :] = jnp.dot(d_ref[r0:r0 + rows, :], y_ref[0],
                                            preferred_element_type=F32).astype(o_ref.dtype)


def _fourier(z, g_mix, dft):
    B, T, _ = z.shape
    M = T // 2
    tm = min(M, 2048)
    fcol = OFF_FOUR // MIX_FOUR
    zr = jnp.roll(jnp.flip(z[:, :, OFF_FOUR:OFF_O], axis=1), 1, axis=1)
    y = pl.pallas_call(
        _four_mix_kernel,
        grid=(B, M // tm),
        in_specs=[
            pl.BlockSpec((1, tm, MIX_FOUR), lambda b, i: (b, i, fcol)),
            pl.BlockSpec((1, tm, MIX_FOUR), lambda b, i: (b, i, 0)),
            pl.BlockSpec((1, BF16_ROWS, MIX_FOUR), lambda b, i: (b, M // BF16_ROWS, fcol)),
            pl.BlockSpec((MIX_FOUR, 2 * MIX_FOUR), lambda b, i: (0, 0)),
        ],
        out_specs=pl.BlockSpec((1, 2, tm, MIX_FOUR), lambda b, i: (b, 0, i, 0)),
        out_shape=jax.ShapeDtypeStruct((B, 2, M, MIX_FOUR), BF16),
        compiler_params=_params(2),
        name="fourier_mix",
    )(z, zr, z, g_mix)
    y = y.reshape(B, T, MIX_FOUR)
    tf = min(T, 1024)
    return pl.pallas_call(
        _four_dft_kernel,
        grid=(T // tf, B),
        in_specs=[
            pl.BlockSpec((tf, T), lambda m, b: (m, 0)),
            pl.BlockSpec((1, T, MIX_FOUR), lambda m, b: (b, 0, 0)),
        ],
        out_specs=pl.BlockSpec((1, tf, MIX_FOUR), lambda m, b: (b, m, 0)),
        out_shape=jax.ShapeDtypeStruct((B, T, MIX_FOUR), BF16),
        compiler_params=_params(2),
        name="fourier_dft",
    )(dft, y)


def _dft_matrix(n):
    m = n // 2
    a = m // GRID_W
    f = lax.broadcasted_iota(jnp.int32, (n, 1), 0)
    ang1 = ((f * GRID_W * lax.broadcasted_iota(jnp.int32, (1, a), 1)) % n).astype(F32) * (2.0 * math.pi / n)
    ang2 = ((f * lax.broadcasted_iota(jnp.int32, (1, GRID_W), 1)) % n).astype(F32) * (2.0 * math.pi / n)
    c1, s1 = jnp.cos(ang1)[:, :, None], jnp.sin(ang1)[:, :, None]
    c2, s2 = jnp.cos(ang2)[:, None, :], jnp.sin(ang2)[:, None, :]
    cos = (c1 * c2 - s1 * s2).reshape(n, m)
    sin = (s1 * c2 + c1 * s2).reshape(n, m)
    t0 = lax.broadcasted_iota(jnp.int32, (n, m), 1) == 0
    sign = (1 - 2 * (f % 2)).astype(F32)
    return jnp.concatenate([jnp.where(t0, 0.5, cos), jnp.where(t0, sign, sin)], axis=1).astype(BF16)


def _fourier_mix_weights(w_four, T):
    c = np.arange(FOUR_DIM)
    ang = 2.0 * np.pi * np.outer(c, c) / FOUR_DIM
    norm = 1.0 / math.sqrt(T * FOUR_DIM)
    gr = jnp.einsum('cf,gfe->gce', jnp.asarray(np.cos(ang) * norm, F32), w_four, precision='highest')
    gi = jnp.einsum('cf,gfe->gce', jnp.asarray(-np.sin(ang) * norm, F32), w_four, precision='highest')
    return jnp.concatenate([_block_diag(gr), _block_diag(gi)], axis=1).astype(BF16)


def _mlstm_kernel(q_ref, k_ref, v_ref, o_ref, g_ref, bg_ref, cf0_ref, mf0_ref, cb0_ref, mb0_ref,
                  y_ref, cf1_ref, mf1_ref, cb1_ref, mb1_ref,
                  hf_ref, hb_ref, pf_ref, pb_ref, uf_ref, ub_ref, rf_ref, rb_ref):
    T = q_ref.shape[1]
    L = MLSTM_CHUNK
    dh = MLSTM_DH
    aug = MLSTM_AUG
    nc = T // L
    head = pl.program_id(1)

    row = lax.broadcasted_iota(jnp.int32, (L, L), 0)
    col = lax.broadcasted_iota(jnp.int32, (L, L), 1)
    masks = (row <= col, row >= col)
    row2 = lax.broadcasted_iota(jnp.int32, (L, 2 * L), 0)
    col2 = jnp.bitwise_and(lax.broadcasted_iota(jnp.int32, (L, 2 * L), 1), L - 1)
    tri2 = jnp.where(col2 <= row2, 1.0, 0.0).astype(BF16)
    diag = row == col
    sub8 = lax.broadcasted_iota(jnp.int32, (SUBLANE, L), 0)
    ones_t = jnp.ones((aug - dh, L), F32)
    sel_row = jnp.bitwise_and(lax.broadcasted_iota(jnp.int32, (2 * LANE, 2 * LANE), 0), LANE - 1)
    sel_dir = jnp.right_shift(lax.broadcasted_iota(jnp.int32, (2 * LANE, 2 * LANE), 1), LANE_SHIFT)
    sel_i = jnp.where(sel_row == sel_dir * (2 * N_MLSTM) + head, 1.0, 0.0).astype(BF16)
    sel_f = jnp.where(sel_row == sel_dir * (2 * N_MLSTM) + N_MLSTM + head, 1.0, 0.0).astype(BF16)
    p_refs, u_refs, r_refs, h_refs = (pf_ref, pb_ref), (uf_ref, ub_ref), (rf_ref, rb_ref), (hf_ref, hb_ref)
    bias = bg_ref[...]

    def prep_group(c0):
        cs = [c0 + j for j in range(min(MLSTM_GROUP, nc))]
        rs = [pl.multiple_of(c * L, L) for c in cs]
        q = [q_ref[0, pl.ds(r, L), :] for r in rs]
        k = [k_ref[0, pl.ds(r, L), :] for r in rs]
        vt = [jnp.concatenate([v_ref[0, pl.ds(r, L), :].astype(F32).T, ones_t], axis=0).astype(BF16) for r in rs]
        g = [(g_ref[0, pl.ds(r, L), :] + bias) * LOG2_E for r in rs]
        i_rep = [jnp.dot(jnp.concatenate(_split_hi_lo(gg), axis=1), sel_i, preferred_element_type=F32)
                 for gg in g]
        lf_rep = [jnp.dot(jnp.concatenate(_split_hi_lo(_log2_sigmoid(gg)), axis=1), sel_f,
                          preferred_element_type=F32) for gg in g]
        st = [lax.dot_general(kk, qq, (((1,), (1,)), ((), ())), preferred_element_type=F32)
              for kk, qq in zip(k, q)]
        cum = [jnp.dot(tri2, jnp.concatenate(_split_hi_lo(f), axis=0), preferred_element_type=F32)
               for f in lf_rep]
        items = [(j, d) for j in range(len(cs)) for d in range(2)]
        for j, d in items:
            if d == 0:
                b = cum[j][:, :L]
                total = b[L - 1:L, :]
            else:
                total = cum[j][L - 1:L, L:]
                b = total - cum[j][:, L:] + lf_rep[j][:, L:]
            r_rep = i_rep[j][:, d * L:(d + 1) * L] - b
            dmt = jnp.where(masks[d], r_rep, NEG_BIG)
            pm = jnp.max(dmt, axis=0, keepdims=True)
            sg = (st[j] * jnp.exp2(dmt - pm)).astype(BF16)
            b_row = jnp.sum(jnp.where(diag, b, 0.0), axis=0, keepdims=True)
            g_end = total + r_rep
            g_loc = jnp.max(g_end, axis=0, keepdims=True)
            kw = (k[j].astype(F32) * jnp.exp2(g_end - g_loc)).astype(BF16)
            base = pl.multiple_of(cs[j] * aug, BF16_ROWS)
            pu = jnp.dot(vt[j], jnp.concatenate([sg, kw], axis=1), preferred_element_type=F32)
            p_refs[d][pl.ds(base, aug), :] = pu[:, :L]
            u_refs[d][pl.ds(base, aug), :] = pu[:, L:]
            rbase = pl.multiple_of(cs[j] * SUBLANE, SUBLANE)
            r_refs[d][pl.ds(rbase, SUBLANE), :] = jnp.where(
                sub8 == 0, b_row, jnp.where(sub8 == 1, b_row + pm, jnp.where(sub8 == 2, total, g_loc)))

    ga = min(MLSTM_GROUP, nc)

    def body_a(i, carry):
        prep_group(i * ga)
        return carry

    lax.fori_loop(0, nc // ga, body_a, 0)

    def step(d, c, ct, m):
        base = pl.multiple_of(c * aug, BF16_ROWS)
        rows = r_refs[d][pl.ds(pl.multiple_of(c * SUBLANE, SUBLANE), SUBLANE), :]
        b_row, a_row, total, g_loc = rows[0:1], rows[1:2], rows[2:3], rows[3:4]
        inter = b_row + m
        m_j = jnp.maximum(inter, a_row)
        qct = lax.dot_general(ct.astype(BF16), q_ref[0, pl.ds(pl.multiple_of(c * L, L), L), :],
                              (((1,), (1,)), ((), ())), preferred_element_type=F32)
        nd = jnp.exp2(inter - m_j) * qct + jnp.exp2(a_row - m_j) * p_refs[d][pl.ds(base, aug), :]
        den = jnp.maximum(jnp.abs(nd[dh:dh + 1]), jnp.exp2(-m_j))
        h_refs[d][pl.ds(pl.multiple_of(c * dh, dh), dh), :] = nd[:dh] / den
        m_new = jnp.maximum(total + m, g_loc)
        ct_new = jnp.exp2(total + m - m_new) * ct + jnp.exp2(g_loc - m_new) * u_refs[d][pl.ds(base, aug), :]
        return ct_new, m_new

    gb = min(nc, MLSTM_GROUP)

    def body_b(i, carry):
        cf, mf, cb, mb = carry
        for j in range(gb):
            cf, mf = step(0, i * gb + j, cf, mf)
            cb, mb = step(1, nc - 1 - (i * gb + j), cb, mb)
        return cf, mf, cb, mb

    cf, mf, cb, mb = lax.fori_loop(
        0, nc // gb, body_b, (cf0_ref[0, 0], mf0_ref[0, 0][0:1, :], cb0_ref[0, 0], mb0_ref[0, 0][0:1, :]))

    cf1_ref[0, 0] = cf
    cb1_ref[0, 0] = cb
    mf1_ref[0, 0] = jnp.broadcast_to(mf, (SUBLANE, LANE))
    mb1_ref[0, 0] = jnp.broadcast_to(mb, (SUBLANE, LANE))

    go = min(nc, MLSTM_GROUP)

    def body_out(i, carry):
        for j in range(go):
            c = i * go + j
            r = pl.multiple_of(c * L, L)
            rh = pl.multiple_of(c * dh, dh)
            h = (hf_ref[pl.ds(rh, dh), :] + hb_ref[pl.ds(rh, dh), :]).T
            y_ref[0, pl.ds(r, L), :] = (_sigmoid(o_ref[0, pl.ds(r, L), :].astype(F32)) * h).astype(y_ref.dtype)
        return carry

    lax.fori_loop(0, nc // go, body_out, 0)


def _mlstm(z, zg, b_gate, cf0, mf0, cb0, mb0):
    B, T, _ = z.shape
    H, dh, L, aug = N_MLSTM, MLSTM_DH, MLSTM_CHUNK, MLSTM_AUG
    nc = T // L
    zcol = lambda off: (lambda b, h: (b, 0, off // dh + h))
    st = lambda b, h: (b, h, 0, 0)
    c_spec = pl.BlockSpec((1, 1, aug, dh), st)
    m_spec = pl.BlockSpec((1, 1, SUBLANE, LANE), st)
    c_shape = jax.ShapeDtypeStruct((B, H, aug, dh), F32)
    m_shape = jax.ShapeDtypeStruct((B, H, SUBLANE, LANE), F32)
    return pl.pallas_call(
        _mlstm_kernel,
        grid=(B, H),
        in_specs=[
            pl.BlockSpec((1, T, dh), zcol(OFF_Q)),
            pl.BlockSpec((1, T, dh), zcol(OFF_K)),
            pl.BlockSpec((1, T, dh), zcol(OFF_V)),
            pl.BlockSpec((1, T, dh), zcol(OFF_O)),
            pl.BlockSpec((1, T, LANE), lambda b, h: (b, 0, 0)),
            pl.BlockSpec((1, LANE), lambda b, h: (0, 0)),
            c_spec, m_spec, c_spec, m_spec,
        ],
        out_specs=[pl.BlockSpec((1, T, dh), lambda b, h: (b, 0, h)), c_spec, m_spec, c_spec, m_spec],
        out_shape=[jax.ShapeDtypeStruct((B, T, MIX_MLSTM), BF16), c_shape, m_shape, c_shape, m_shape],
        scratch_shapes=[
            pltpu.VMEM((nc * dh, L), F32),
            pltpu.VMEM((nc * dh, L), F32),
            pltpu.VMEM((nc * aug, L), F32),
            pltpu.VMEM((nc * aug, L), F32),
            pltpu.VMEM((nc * aug, dh), F32),
            pltpu.VMEM((nc * aug, dh), F32),
            pltpu.VMEM((nc * SUBLANE, LANE), F32),
            pltpu.VMEM((nc * SUBLANE, LANE), F32),
        ],
        compiler_params=_params(2),
        name="mlstm",
    )(z, z, z, z, zg, b_gate, cf0, mf0, cb0, mb0)


def kernel(x, c, ctx, c_ctx, w_mod, b_mod, w_in, conv_qk, b_gates, pool_w, pool_scale, fourier_w,
           w_out, ln1_g, ln1_b, w_up, conv_ffn_w, conv_ffn_b, w_down, ln2_g, ln2_b):
    B, T, D = x.shape
    Tc = ctx.shape[1]
    tm_x = 512
    tm_c = Tc
    xc = ctx
    s_c = jax.nn.silu(c)
    s_ctx = jax.nn.silu(c_ctx)
    filt_x, filt_c = _pool_filters(True), _pool_filters(False)
    cnt_x, cnt_c = _pool_inv_count(T, True), _pool_inv_count(Tc, False)
    dft_x, dft_c = _dft_matrix(T), _dft_matrix(Tc)
    zero_c = jnp.zeros((B, N_MLSTM, MLSTM_AUG, MLSTM_DH), F32)
    zero_m = jnp.zeros((B, N_MLSTM, SUBLANE, LANE), F32)
    for l in range(DEPTH):
        last = l == DEPTH - 1
        mx = [m[:, None, :] for m in jnp.split(s_c @ w_mod[l] + b_mod[l], 6, axis=-1)]
        mc = [jnp.broadcast_to(m[None, None, :], (B, 1, D))
              for m in jnp.split(s_ctx @ w_mod[l] + b_mod[l], 6, axis=-1)]
        w_main = w_in[l][:, :OFF_G].astype(BF16)
        w_gate = jnp.pad(w_in[l][:, OFF_G:], ((0, 0), (0, LANE - N_GATES))).astype(BF16)
        b_gate = jnp.pad(b_gates[l], (0, LANE - N_GATES))[None]
        w_pool_bd = _block_diag(pool_w[l]).astype(BF16)
        s_pool = pool_scale[l][None]
        w_out_b = w_out[l].astype(BF16)
        w_up_b = w_up[l].astype(BF16)
        w_down_b = w_down[l].astype(BF16)
        g1, b1 = ln1_g[l][None], ln1_b[l][None]
        g2, b2 = ln2_g[l][None], ln2_b[l][None]
        bcv = conv_ffn_b[l][None]

        zc, zgc = _proj_in(xc, mc[0], mc[1], w_main, w_gate, conv_qk[l], tm_c)
        ymc, cf, mf, cb, mb = _mlstm(zc, zgc, b_gate, zero_c, zero_m, zero_c, zero_m)
        zx, zgx = _proj_in(x, mx[0], mx[1], w_main, w_gate, conv_qk[l], tm_x)
        ymx = _mlstm(zx, zgx, b_gate, cf, mf, cb, mb)[0]
        ypx = _pool(zx, filt_x, cnt_x, w_pool_bd, s_pool, True)
        yfx = _fourier(zx, _fourier_mix_weights(fourier_w[l], T), dft_x)
        x = _proj_out(ypx, yfx, ymx, x, mx[2], w_out_b, g1, b1, 2 * tm_x)
        x = _ffn(x, mx[3], mx[4], mx[5], w_up_b, conv_ffn_w[l], bcv, w_down_b, g2, b2, tm_x)
        if not last:
            ypc = _pool(zc, filt_c, cnt_c, w_pool_bd, s_pool, False)
            yfc = _fourier(zc, _fourier_mix_weights(fourier_w[l], Tc), dft_c)
            xc = _proj_out(ypc, yfc, ymc, xc, mc[2], w_out_b, g1, b1, tm_c)
            xc = _ffn(xc, mc[3], mc[4], mc[5], w_up_b, conv_ffn_w[l], bcv, w_down_b, g2, b2, tm_c)
    return x
```

```python
import functools
import math

import jax
import jax.numpy as jnp
import numpy as np
from jax import lax
from jax.experimental import pallas as pl
from jax.experimental.pallas import tpu as pltpu

F32 = jnp.float32
BF16 = jnp.bfloat16

D_MODEL = 1024
DEPTH = 4
GRID_W = 64
GRID_SHIFT = GRID_W.bit_length() - 1
POOL_WINDOWS = (2, 4, 8, 16)
N_POOL = len(POOL_WINDOWS)
MIX_POOL = D_MODEL // 4
POOL_DIM = MIX_POOL // N_POOL
N_FOUR = 4
MIX_FOUR = D_MODEL // 4
FOUR_DIM = MIX_FOUR // N_FOUR
N_MLSTM = 4
MIX_MLSTM = D_MODEL // 2
MLSTM_DH = MIX_MLSTM // N_MLSTM
MLSTM_CHUNK = 128
N_GATES = 2 * 2 * N_MLSTM
D_FF = int(math.ceil(8 * D_MODEL / 3 / 128)) * 128
OFF_FOUR = MIX_POOL
OFF_O = OFF_FOUR + MIX_FOUR
OFF_Q = OFF_O + MIX_MLSTM
OFF_K = OFF_Q + MIX_MLSTM
OFF_V = OFF_K + MIX_MLSTM
OFF_G = OFF_V + MIX_MLSTM
ALPHA = (2 * DEPTH) ** 0.25
LN_EPS = 1e-6
LOG2_E = math.log2(math.e)

LANE = 128
LANE_SHIFT = LANE.bit_length() - 1
SUBLANE = 8
BF16_ROWS = 16
VMEM_LIMIT = 56 * 1024 * 1024
FF_CHUNK = 256
PROJ_CHUNK = 512
POOL_BLOCK = 256
FOUR_MIRROR_BLOCK = 256
POOL_PAD = (max(POOL_WINDOWS) // 2) * GRID_W
NEG_BIG = -1e30
MLSTM_GROUP = 8
MLSTM_AUG = MLSTM_DH + BF16_ROWS


def _params(n_axes):
    return pltpu.CompilerParams(dimension_semantics=("arbitrary",) * n_axes, vmem_limit_bytes=VMEM_LIMIT)


def _ln_rows(x):
    mu = jnp.mean(x, axis=-1, keepdims=True)
    xc = x - mu
    var = jnp.mean(xc * xc, axis=-1, keepdims=True)
    return xc * lax.rsqrt(var + LN_EPS)


def _sigmoid(x):
    return 0.5 * jnp.tanh(0.5 * x) + 0.5


def _log2_sigmoid(x2):
    return jnp.minimum(x2, 0.0) - jnp.log2(1.0 + jnp.exp2(-jnp.abs(x2)))


def _split_hi_lo(x):
    hi = x.astype(BF16)
    return hi, (x - hi.astype(F32)).astype(BF16)


def _proj_in_kernel(x_ref, xp_ref, xn_ref, sh_ref, sc_ref, w_ref, wg_ref, wc_ref, z_ref, g_ref, h_ref):
    i = pl.program_id(1)
    tm = x_ref.shape[1]
    mod_scale = 1.0 + sc_ref[0]
    mod_shift = sh_ref[0]
    u = _ln_rows(x_ref[0]) * mod_scale + mod_shift
    u_prev = (_ln_rows(xp_ref[0]) * mod_scale + mod_shift) * (i > 0).astype(F32)
    u_next = (_ln_rows(xn_ref[0]) * mod_scale + mod_shift) * (i < pl.num_programs(1) - 1).astype(F32)
    ub = u.astype(BF16)
    u_ext = jnp.concatenate([u_prev, u, u_next], axis=0).astype(BF16)
    n = z_ref.shape[-1]
    for c0 in range(0, n, PROJ_CHUNK):
        c1 = min(c0 + PROJ_CHUNK, n)
        if OFF_Q <= c0 and c1 <= OFF_V:
            h_ref[...] = jnp.dot(u_ext, w_ref[:, c0:c1], preferred_element_type=F32)
            w = wc_ref[:, c0 - OFF_Q:c1 - OFF_Q]
            y = (h_ref[SUBLANE - 1:SUBLANE - 1 + tm, :] * w[0:1] + h_ref[SUBLANE:SUBLANE + tm, :] * w[1:2]
                 + h_ref[SUBLANE + 1:SUBLANE + 1 + tm, :] * w[2:3])
            scale = 1.0 if c0 < OFF_K else MLSTM_DH ** -0.5
            z_ref[0, :, c0:c1] = (y * _sigmoid(y) * scale).astype(z_ref.dtype)
        else:
            z_ref[0, :, c0:c1] = jnp.dot(ub, w_ref[:, c0:c1], preferred_element_type=F32).astype(z_ref.dtype)
    g_ref[0] = jnp.dot(ub, wg_ref[...], preferred_element_type=F32)


def _proj_in(x, shift, scale, w_main, w_gate, w_conv, tm):
    B, T, D = x.shape
    n = w_main.shape[1]
    assert OFF_Q % PROJ_CHUNK == 0 and OFF_K % PROJ_CHUNK == 0 and OFF_V % PROJ_CHUNK == 0
    nb = tm // SUBLANE
    last = T // SUBLANE - 1
    return pl.pallas_call(
        _proj_in_kernel,
        grid=(B, T // tm),
        in_specs=[
            pl.BlockSpec((1, tm, D), lambda b, i: (b, i, 0)),
            pl.BlockSpec((1, SUBLANE, D), lambda b, i: (b, jnp.maximum(i * nb - 1, 0), 0)),
            pl.BlockSpec((1, SUBLANE, D), lambda b, i: (b, jnp.minimum((i + 1) * nb, last), 0)),
            pl.BlockSpec((1, 1, D), lambda b, i: (b, 0, 0)),
            pl.BlockSpec((1, 1, D), lambda b, i: (b, 0, 0)),
            pl.BlockSpec((D, n), lambda b, i: (0, 0)),
            pl.BlockSpec((D, LANE), lambda b, i: (0, 0)),
            pl.BlockSpec((3, 2 * MIX_MLSTM), lambda b, i: (0, 0)),
        ],
        out_specs=[
            pl.BlockSpec((1, tm, n), lambda b, i: (b, i, 0)),
            pl.BlockSpec((1, tm, LANE), lambda b, i: (b, i, 0)),
        ],
        out_shape=[
            jax.ShapeDtypeStruct((B, T, n), BF16),
            jax.ShapeDtypeStruct((B, T, LANE), F32),
        ],
        scratch_shapes=[pltpu.VMEM((tm + 2 * SUBLANE, PROJ_CHUNK), F32)],
        compiler_params=_params(2),
        name="proj_in",
    )(x, x, x, shift, scale, w_main, w_gate, w_conv)


def _proj_out_kernel(yp_ref, yf_ref, ym_ref, x_ref, gate_ref, wp_ref, wf_ref, wm_ref, lg_ref, lb_ref, o_ref):
    f = (jnp.dot(yp_ref[0], wp_ref[...], preferred_element_type=F32)
         + jnp.dot(yf_ref[0], wf_ref[...], preferred_element_type=F32)
         + jnp.dot(ym_ref[0], wm_ref[...], preferred_element_type=F32))
    r = ALPHA * x_ref[0] + gate_ref[0] * f
    o_ref[0] = _ln_rows(r) * lg_ref[...] + lb_ref[...]


def _proj_out(yp, yf, ym, x, gate, w, ln_g, ln_b, tm):
    B, T, D = x.shape
    tok = lambda b, i: (b, i, 0)
    return pl.pallas_call(
        _proj_out_kernel,
        grid=(B, T // tm),
        in_specs=[
            pl.BlockSpec((1, tm, MIX_POOL), tok),
            pl.BlockSpec((1, tm, MIX_FOUR), tok),
            pl.BlockSpec((1, tm, MIX_MLSTM), tok),
            pl.BlockSpec((1, tm, D), tok),
            pl.BlockSpec((1, 1, D), lambda b, i: (b, 0, 0)),
            pl.BlockSpec((MIX_POOL, D), lambda b, i: (0, 0)),
            pl.BlockSpec((MIX_FOUR, D), lambda b, i: (OFF_FOUR // MIX_FOUR, 0)),
            pl.BlockSpec((MIX_MLSTM, D), lambda b, i: (OFF_O // MIX_MLSTM, 0)),
            pl.BlockSpec((1, D), lambda b, i: (0, 0)),
            pl.BlockSpec((1, D), lambda b, i: (0, 0)),
        ],
        out_specs=pl.BlockSpec((1, tm, D), tok),
        out_shape=jax.ShapeDtypeStruct((B, T, D), F32),
        compiler_params=_params(2),
        name="proj_out",
    )(yp, yf, ym, x, gate, w, w, w, ln_g, ln_b)


def _gelu_tanh(x):
    c = math.sqrt(2.0 / math.pi)
    return 0.5 * x * (1.0 + jnp.tanh(c * (x + 0.044715 * (x * x * x))))


def _ffn_kernel(x_ref, xp_ref, xn_ref, sh_ref, sc_ref, gate_ref, wup_ref, wc_ref, bc_ref, wdn_ref,
                lg_ref, lb_ref, o_ref, ha_ref, hg_ref, p_ref):
    i = pl.program_id(1)
    tm = x_ref.shape[1]
    x = x_ref[0]
    mod_scale = 1.0 + sc_ref[0]
    mod_shift = sh_ref[0]
    has_prev = (i > 0).astype(F32)
    has_next = (i < pl.num_programs(1) - 1).astype(F32)
    u_prev = (_ln_rows(xp_ref[0]) * mod_scale + mod_shift) * has_prev
    u_next = (_ln_rows(xn_ref[0]) * mod_scale + mod_shift) * has_next
    u_main = _ln_rows(x) * mod_scale + mod_shift
    u_ext = jnp.concatenate([u_prev, u_main, u_next], axis=0).astype(BF16)

    def conv(h_ref, cols):
        w = wc_ref[:, cols]
        return (h_ref[SUBLANE - 1:SUBLANE - 1 + tm, :] * w[0:1]
                + h_ref[SUBLANE:SUBLANE + tm, :] * w[1:2]
                + h_ref[SUBLANE + 1:SUBLANE + 1 + tm, :] * w[2:3]
                + bc_ref[:, cols])

    for c in range(D_FF // FF_CHUNK):
        ca = slice(c * FF_CHUNK, (c + 1) * FF_CHUNK)
        cg = slice(D_FF + c * FF_CHUNK, D_FF + (c + 1) * FF_CHUNK)
        ha_ref[...] = jnp.dot(u_ext, wup_ref[:, ca], preferred_element_type=F32)
        hg_ref[...] = jnp.dot(u_ext, wup_ref[:, cg], preferred_element_type=F32)
        p_ref[:, ca] = (_gelu_tanh(conv(hg_ref, cg)) * conv(ha_ref, ca)).astype(BF16)

    f = jnp.dot(p_ref[...], wdn_ref[...], preferred_element_type=F32)
    r = ALPHA * x + gate_ref[0] * f
    o_ref[0] = _ln_rows(r) * lg_ref[...] + lb_ref[...]


def _ffn(x, shift, scale, gate, w_up, w_conv, b_conv, w_down, ln_g, ln_b, tm):
    B, T, D = x.shape
    nb = tm // SUBLANE
    last = T // SUBLANE - 1
    const = dict(pipeline_mode=pl.Buffered(1))
    return pl.pallas_call(
        _ffn_kernel,
        grid=(B, T // tm),
        in_specs=[
            pl.BlockSpec((1, tm, D), lambda b, i: (b, i, 0)),
            pl.BlockSpec((1, SUBLANE, D), lambda b, i: (b, jnp.maximum(i * nb - 1, 0), 0)),
            pl.BlockSpec((1, SUBLANE, D), lambda b, i: (b, jnp.minimum((i + 1) * nb, last), 0)),
            pl.BlockSpec((1, 1, D), lambda b, i: (b, 0, 0)),
            pl.BlockSpec((1, 1, D), lambda b, i: (b, 0, 0)),
            pl.BlockSpec((1, 1, D), lambda b, i: (b, 0, 0)),
            pl.BlockSpec((D, 2 * D_FF), lambda b, i: (0, 0), **const),
            pl.BlockSpec((3, 2 * D_FF), lambda b, i: (0, 0), **const),
            pl.BlockSpec((1, 2 * D_FF), lambda b, i: (0, 0), **const),
            pl.BlockSpec((D_FF, D), lambda b, i: (0, 0), **const),
            pl.BlockSpec((1, D), lambda b, i: (0, 0)),
            pl.BlockSpec((1, D), lambda b, i: (0, 0)),
        ],
        out_specs=pl.BlockSpec((1, tm, D), lambda b, i: (b, i, 0)),
        out_shape=jax.ShapeDtypeStruct((B, T, D), F32),
        scratch_shapes=[
            pltpu.VMEM((tm + 2 * SUBLANE, FF_CHUNK), F32),
            pltpu.VMEM((tm + 2 * SUBLANE, FF_CHUNK), F32),
            pltpu.VMEM((tm, D_FF), BF16),
        ],
        compiler_params=_params(2),
        name="conv_ffn",
    )(x, x, x, shift, scale, gate, w_up, w_conv, b_conv, w_down, ln_g, ln_b)


def _band_matrix(n, w):
    pos = np.arange(n)
    lo = np.maximum(pos - w // 2, 0)
    hi = np.minimum(pos + w // 2 - 1, n - 1)
    return ((pos[None, :] >= lo[:, None]) & (pos[None, :] <= hi[:, None])).astype(np.float32)


def _pool_filters(grid2d):
    mats = []
    for w in POOL_WINDOWS:
        if grid2d:
            mats.append(np.kron(np.eye(POOL_BLOCK // GRID_W, dtype=np.float32), _band_matrix(GRID_W, w)))
        else:
            mats.append(_band_matrix(POOL_BLOCK, w))
    return jnp.asarray(np.stack(mats), BF16)


def _window_count(idx, half, n):
    return jnp.minimum(idx + (half - 1), n - 1) - jnp.maximum(idx - half, 0) + 1


def _pool_inv_count(T, grid2d):
    pos = lax.broadcasted_iota(jnp.int32, (T, MIX_POOL), 0)
    group = jnp.right_shift(lax.broadcasted_iota(jnp.int32, (T, MIX_POOL), 1), GRID_SHIFT)
    half = jnp.left_shift(1, group)
    if grid2d:
        count = (_window_count(jnp.right_shift(pos, GRID_SHIFT), half, T // GRID_W)
                 * _window_count(jnp.bitwise_and(pos, GRID_W - 1), half, GRID_W))
    else:
        count = _window_count(pos, half, T)
    return 1.0 / count.astype(F32)


def _pool_kernel(z_ref, a_ref, n_ref, w_ref, s_ref, y_ref, acc_ref, *, grid2d):
    T = z_ref.shape[1]
    nblk = T // POOL_BLOCK
    lane = lax.broadcasted_iota(jnp.int32, (POOL_BLOCK, MIX_POOL), 1)
    group = jnp.right_shift(lane, GRID_SHIFT)

    def col_filter(blk):
        r0 = pl.multiple_of(blk * POOL_BLOCK, POOL_BLOCK)
        xb = z_ref[0, pl.ds(r0, POOL_BLOCK), :]
        out = jnp.dot(a_ref[0], xb, preferred_element_type=F32)
        for g in range(1, N_POOL):
            out = jnp.where(group == g, jnp.dot(a_ref[g], xb, preferred_element_type=F32), out)
        return r0, xb, out

    def finish(r0, xb, box):
        p = box * n_ref[pl.ds(r0, POOL_BLOCK), :] - xb.astype(F32)
        y = jnp.dot(p.astype(BF16), w_ref[...], preferred_element_type=F32) * s_ref[...]
        y_ref[0, pl.ds(r0, POOL_BLOCK), :] = y.astype(y_ref.dtype)

    if not grid2d:
        assert T == POOL_BLOCK
        finish(*col_filter(0))
        return

    zeros = jnp.zeros((POOL_PAD, MIX_POOL), F32)
    acc_ref[0:POOL_PAD, :] = zeros
    acc_ref[POOL_PAD + T:POOL_PAD + T + POOL_PAD, :] = zeros

    def phase1(blk, carry):
        r0, _, out = col_filter(blk)
        acc_ref[pl.ds(POOL_PAD + r0, POOL_BLOCK), :] = out
        return carry

    lax.fori_loop(0, nblk, phase1, 0)

    lane_t = lax.broadcasted_iota(jnp.int32, (POOL_BLOCK, LANE), 1)

    def phase2(blk, carry):
        r0 = pl.multiple_of(blk * POOL_BLOCK, POOL_BLOCK)
        base = POOL_PAD + r0

        def slab(k, tile):
            return acc_ref[pl.ds(base + k * GRID_W, POOL_BLOCK), tile * LANE:(tile + 1) * LANE]

        s2 = slab(-1, 0) + slab(0, 0)
        s4 = s2 + slab(-2, 0) + slab(1, 0)
        s8 = slab(-4, 1)
        for k in range(-3, 4):
            s8 = s8 + slab(k, 1)
        s16 = s8
        for k in list(range(-8, -4)) + list(range(4, 8)):
            s16 = s16 + slab(k, 1)
        box = jnp.concatenate([jnp.where(lane_t < POOL_DIM, s2, s4), jnp.where(lane_t < POOL_DIM, s8, s16)], axis=1)
        finish(r0, z_ref[0, pl.ds(r0, POOL_BLOCK), :], box)
        return carry

    lax.fori_loop(0, nblk, phase2, 0)


def _pool(z, filters, inv_count, w_bd, s_row, grid2d):
    B, T, _ = z.shape
    return pl.pallas_call(
        functools.partial(_pool_kernel, grid2d=grid2d),
        grid=(B,),
        in_specs=[
            pl.BlockSpec((1, T, MIX_POOL), lambda b: (b, 0, 0)),
            pl.BlockSpec((N_POOL, POOL_BLOCK, POOL_BLOCK), lambda b: (0, 0, 0)),
            pl.BlockSpec((T, MIX_POOL), lambda b: (0, 0)),
            pl.BlockSpec((MIX_POOL, MIX_POOL), lambda b: (0, 0)),
            pl.BlockSpec((1, MIX_POOL), lambda b: (0, 0)),
        ],
        out_specs=pl.BlockSpec((1, T, MIX_POOL), lambda b: (b, 0, 0)),
        out_shape=jax.ShapeDtypeStruct((B, T, MIX_POOL), BF16),
        scratch_shapes=[pltpu.VMEM((T + 2 * POOL_PAD, MIX_POOL), F32)],
        compiler_params=_params(1),
        name="pool2d" if grid2d else "pool1d",
    )(z, filters, inv_count, w_bd, s_row)


def _block_diag(w):
    G, n, m = w.shape
    eye = jnp.eye(G, dtype=w.dtype)
    return (eye[:, None, :, None] * w[:, :, None, :]).reshape(G * n, G * m)


def _four_mix_kernel(z_ref, za_ref, zb_ref, zm_ref, g_ref, y_ref):
    tm = z_ref.shape[1]
    sb = zb_ref.shape[1]
    nsub = tm // sb
    gr, gi = g_ref[:, :MIX_FOUR], g_ref[:, MIX_FOUR:]
    r = lax.broadcasted_iota(jnp.int32, (sb, 2 * sb), 0)
    c = lax.broadcasted_iota(jnp.int32, (sb, 2 * sb), 1)
    perm = jnp.where(((r >= 1) & (c == sb - r)) | ((r == 0) & (c == sb)), 1.0, 0.0).astype(BF16)
    mid = jnp.dot(zm_ref[0], gr, preferred_element_type=F32)[0:1]
    row0 = lax.broadcasted_iota(jnp.int32, (sb, MIX_FOUR), 0) == 0
    for j in range(nsub):
        lo = (nsub - 1 - j) * sb
        if j == 0:
            src = jnp.concatenate([za_ref[0, lo:lo + sb, :], zb_ref[0]], axis=0)
        else:
            src = za_ref[0, lo:lo + 2 * sb, :]
        xr = jnp.dot(perm, src, preferred_element_type=F32)
        x = z_ref[0, j * sb:(j + 1) * sb, :].astype(F32)
        even = jnp.dot((x + xr).astype(BF16), gr, preferred_element_type=F32)
        odd = jnp.dot((x - xr).astype(BF16), gi, preferred_element_type=F32)
        if j == 0:
            odd = jnp.where(row0 & (pl.program_id(1) == 0), mid, odd)
        y_ref[0, 0, j * sb:(j + 1) * sb, :] = even.astype(y_ref.dtype)
        y_ref[0, 1, j * sb:(j + 1) * sb, :] = odd.astype(y_ref.dtype)


def _four_dft_kernel(d_ref, y_ref, o_ref):
    rows = 256
    for r0 in range(0, o_ref.shape[1], rows):
        o_ref[0, r0:r0 + rows, :] = jnp.dot(d_ref[r0:r0 + rows, :], y_ref[0],
                                            preferred_element_type=F32).astype(o_ref.dtype)


def _fourier(z, g_mix, dft):
    B, T, _ = z.shape
    M = T // 2
    tm = min(M, 2048)
    fcol = OFF_FOUR // MIX_FOUR
    sb = min(tm, FOUR_MIRROR_BLOCK)
    nt, nsb, per = T // tm, T // sb, tm // sb
    y = pl.pallas_call(
        _four_mix_kernel,
        grid=(B, M // tm),
        in_specs=[
            pl.BlockSpec((1, tm, MIX_FOUR), lambda b, i: (b, i, fcol)),
            pl.BlockSpec((1, tm, MIX_FOUR), lambda b, i: (b, nt - 1 - i, fcol)),
            pl.BlockSpec((1, sb, MIX_FOUR), lambda b, i: (b, (nsb - i * per) % nsb, fcol)),
            pl.BlockSpec((1, BF16_ROWS, MIX_FOUR), lambda b, i: (b, M // BF16_ROWS, fcol)),
            pl.BlockSpec((MIX_FOUR, 2 * MIX_FOUR), lambda b, i: (0, 0)),
        ],
        out_specs=pl.BlockSpec((1, 2, tm, MIX_FOUR), lambda b, i: (b, 0, i, 0)),
        out_shape=jax.ShapeDtypeStruct((B, 2, M, MIX_FOUR), BF16),
        compiler_params=_params(2),
        name="fourier_mix",
    )(z, z, z, z, g_mix)
    y = y.reshape(B, T, MIX_FOUR)
    tf = min(T, 1024)
    return pl.pallas_call(
        _four_dft_kernel,
        grid=(T // tf, B),
        in_specs=[
            pl.BlockSpec((tf, T), lambda m, b: (m, 0)),
            pl.BlockSpec((1, T, MIX_FOUR), lambda m, b: (b, 0, 0)),
        ],
        out_specs=pl.BlockSpec((1, tf, MIX_FOUR), lambda m, b: (b, m, 0)),
        out_shape=jax.ShapeDtypeStruct((B, T, MIX_FOUR), BF16),
        compiler_params=_params(2),
        name="fourier_dft",
    )(dft, y)


def _dft_matrix(n):
    m = n // 2
    a = m // GRID_W
    f = lax.broadcasted_iota(jnp.int32, (n, 1), 0)
    ang1 = ((f * GRID_W * lax.broadcasted_iota(jnp.int32, (1, a), 1)) % n).astype(F32) * (2.0 * math.pi / n)
    ang2 = ((f * lax.broadcasted_iota(jnp.int32, (1, GRID_W), 1)) % n).astype(F32) * (2.0 * math.pi / n)
    c1, s1 = jnp.cos(ang1)[:, :, None], jnp.sin(ang1)[:, :, None]
    c2, s2 = jnp.cos(ang2)[:, None, :], jnp.sin(ang2)[:, None, :]
    cos = (c1 * c2 - s1 * s2).reshape(n, m)
    sin = (s1 * c2 + c1 * s2).reshape(n, m)
    t0 = lax.broadcasted_iota(jnp.int32, (n, m), 1) == 0
    sign = (1 - 2 * (f % 2)).astype(F32)
    return jnp.concatenate([jnp.where(t0, 0.5, cos), jnp.where(t0, sign, sin)], axis=1).astype(BF16)


def _fourier_mix_weights(w_four, T):
    c = np.arange(FOUR_DIM)
    ang = 2.0 * np.pi * np.outer(c, c) / FOUR_DIM
    norm = 1.0 / math.sqrt(T * FOUR_DIM)
    gr = jnp.einsum('cf,gfe->gce', jnp.asarray(np.cos(ang) * norm, F32), w_four, precision='highest')
    gi = jnp.einsum('cf,gfe->gce', jnp.asarray(-np.sin(ang) * norm, F32), w_four, precision='highest')
    return jnp.concatenate([_block_diag(gr), _block_diag(gi)], axis=1).astype(BF16)


def _mlstm_kernel(q_ref, k_ref, v_ref, o_ref, g_ref, bg_ref, cf0_ref, mf0_ref, cb0_ref, mb0_ref,
                  y_ref, cf1_ref, mf1_ref, cb1_ref, mb1_ref,
                  hf_ref, hb_ref, pf_ref, pb_ref, uf_ref, ub_ref, rf_ref, rb_ref):
    T = q_ref.shape[1]
    L = MLSTM_CHUNK
    dh = MLSTM_DH
    aug = MLSTM_AUG
    nc = T // L
    head = pl.program_id(1)

    row = lax.broadcasted_iota(jnp.int32, (L, L), 0)
    col = lax.broadcasted_iota(jnp.int32, (L, L), 1)
    masks = (row <= col, row >= col)
    row2 = lax.broadcasted_iota(jnp.int32, (L, 2 * L), 0)
    col2 = jnp.bitwise_and(lax.broadcasted_iota(jnp.int32, (L, 2 * L), 1), L - 1)
    tri2 = jnp.where(col2 <= row2, 1.0, 0.0).astype(BF16)
    diag = row == col
    sub8 = lax.broadcasted_iota(jnp.int32, (SUBLANE, L), 0)
    ones_t = jnp.ones((aug - dh, L), F32)
    sel_row = jnp.bitwise_and(lax.broadcasted_iota(jnp.int32, (2 * LANE, 2 * LANE), 0), LANE - 1)
    sel_dir = jnp.right_shift(lax.broadcasted_iota(jnp.int32, (2 * LANE, 2 * LANE), 1), LANE_SHIFT)
    sel_i = jnp.where(sel_row == sel_dir * (2 * N_MLSTM) + head, 1.0, 0.0).astype(BF16)
    sel_f = jnp.where(sel_row == sel_dir * (2 * N_MLSTM) + N_MLSTM + head, 1.0, 0.0).astype(BF16)
    p_refs, u_refs, r_refs, h_refs = (pf_ref, pb_ref), (uf_ref, ub_ref), (rf_ref, rb_ref), (hf_ref, hb_ref)
    bias = bg_ref[...]

    def prep_group(c0):
        cs = [c0 + j for j in range(min(MLSTM_GROUP, nc))]
        rs = [pl.multiple_of(c * L, L) for c in cs]
        q = [q_ref[0, pl.ds(r, L), :] for r in rs]
        k = [k_ref[0, pl.ds(r, L), :] for r in rs]
        vt = [jnp.concatenate([v_ref[0, pl.ds(r, L), :].astype(F32).T, ones_t], axis=0).astype(BF16) for r in rs]
        g = [(g_ref[0, pl.ds(r, L), :] + bias) * LOG2_E for r in rs]
        i_rep = [jnp.dot(jnp.concatenate(_split_hi_lo(gg), axis=1), sel_i, preferred_element_type=F32)
                 for gg in g]
        lf_rep = [jnp.dot(jnp.concatenate(_split_hi_lo(_log2_sigmoid(gg)), axis=1), sel_f,
                          preferred_element_type=F32) for gg in g]
        st = [lax.dot_general(kk, qq, (((1,), (1,)), ((), ())), preferred_element_type=F32)
              for kk, qq in zip(k, q)]
        cum = [jnp.dot(tri2, jnp.concatenate(_split_hi_lo(f), axis=0), preferred_element_type=F32)
               for f in lf_rep]
        items = [(j, d) for j in range(len(cs)) for d in range(2)]
        for j, d in items:
            if d == 0:
                b = cum[j][:, :L]
                total = b[L - 1:L, :]
            else:
                total = cum[j][L - 1:L, L:]
                b = total - cum[j][:, L:] + lf_rep[j][:, L:]
            r_rep = i_rep[j][:, d * L:(d + 1) * L] - b
            dmt = jnp.where(masks[d], r_rep, NEG_BIG)
            pm = jnp.max(dmt, axis=0, keepdims=True)
            sg = (st[j] * jnp.exp2(dmt - pm)).astype(BF16)
            b_row = jnp.sum(jnp.where(diag, b, 0.0), axis=0, keepdims=True)
            g_end = total + r_rep
            g_loc = jnp.max(g_end, axis=0, keepdims=True)
            kw = (k[j].astype(F32) * jnp.exp2(g_end - g_loc)).astype(BF16)
            base = pl.multiple_of(cs[j] * aug, BF16_ROWS)
            pu = jnp.dot(vt[j], jnp.concatenate([sg, kw], axis=1), preferred_element_type=F32)
            p_refs[d][pl.ds(base, aug), :] = pu[:, :L]
            u_refs[d][pl.ds(base, aug), :] = pu[:, L:]
            rbase = pl.multiple_of(cs[j] * SUBLANE, SUBLANE)
            r_refs[d][pl.ds(rbase, SUBLANE), :] = jnp.where(
                sub8 == 0, b_row, jnp.where(sub8 == 1, b_row + pm, jnp.where(sub8 == 2, total, g_loc)))

    ga = min(MLSTM_GROUP, nc)

    def body_a(i, carry):
        prep_group(i * ga)
        return carry

    lax.fori_loop(0, nc // ga, body_a, 0)

    def step(d, c, ct, m):
        base = pl.multiple_of(c * aug, BF16_ROWS)
        rows = r_refs[d][pl.ds(pl.multiple_of(c * SUBLANE, SUBLANE), SUBLANE), :]
        b_row, a_row, total, g_loc = rows[0:1], rows[1:2], rows[2:3], rows[3:4]
        inter = b_row + m
        m_j = jnp.maximum(inter, a_row)
        qct = lax.dot_general(ct.astype(BF16), q_ref[0, pl.ds(pl.multiple_of(c * L, L), L), :],
                              (((1,), (1,)), ((), ())), preferred_element_type=F32)
        nd = jnp.exp2(inter - m_j) * qct + jnp.exp2(a_row - m_j) * p_refs[d][pl.ds(base, aug), :]
        den = jnp.maximum(jnp.abs(nd[dh:dh + 1]), jnp.exp2(-m_j))
        h_refs[d][pl.ds(pl.multiple_of(c * dh, dh), dh), :] = nd[:dh] / den
        m_new = jnp.maximum(total + m, g_loc)
        ct_new = jnp.exp2(total + m - m_new) * ct + jnp.exp2(g_loc - m_new) * u_refs[d][pl.ds(base, aug), :]
        return ct_new, m_new

    gb = min(nc, MLSTM_GROUP)

    def body_b(i, carry):
        cf, mf, cb, mb = carry
        for j in range(gb):
            cf, mf = step(0, i * gb + j, cf, mf)
            cb, mb = step(1, nc - 1 - (i * gb + j), cb, mb)
        return cf, mf, cb, mb

    cf, mf, cb, mb = lax.fori_loop(
        0, nc // gb, body_b, (cf0_ref[0, 0], mf0_ref[0, 0][0:1, :], cb0_ref[0, 0], mb0_ref[0, 0][0:1, :]))

    cf1_ref[0, 0] = cf
    cb1_ref[0, 0] = cb
    mf1_ref[0, 0] = jnp.broadcast_to(mf, (SUBLANE, LANE))
    mb1_ref[0, 0] = jnp.broadcast_to(mb, (SUBLANE, LANE))

    go = min(nc, MLSTM_GROUP)

    def body_out(i, carry):
        for j in range(go):
            c = i * go + j
            r = pl.multiple_of(c * L, L)
            rh = pl.multiple_of(c * dh, dh)
            h = (hf_ref[pl.ds(rh, dh), :] + hb_ref[pl.ds(rh, dh), :]).T
            y_ref[0, pl.ds(r, L), :] = (_sigmoid(o_ref[0, pl.ds(r, L), :].astype(F32)) * h).astype(y_ref.dtype)
        return carry

    lax.fori_loop(0, nc // go, body_out, 0)


def _mlstm(z, zg, b_gate, cf0, mf0, cb0, mb0):
    B, T, _ = z.shape
    H, dh, L, aug = N_MLSTM, MLSTM_DH, MLSTM_CHUNK, MLSTM_AUG
    nc = T // L
    zcol = lambda off: (lambda b, h: (b, 0, off // dh + h))
    st = lambda b, h: (b, h, 0, 0)
    c_spec = pl.BlockSpec((1, 1, aug, dh), st)
    m_spec = pl.BlockSpec((1, 1, SUBLANE, LANE), st)
    c_shape = jax.ShapeDtypeStruct((B, H, aug, dh), F32)
    m_shape = jax.ShapeDtypeStruct((B, H, SUBLANE, LANE), F32)
    return pl.pallas_call(
        _mlstm_kernel,
        grid=(B, H),
        in_specs=[
            pl.BlockSpec((1, T, dh), zcol(OFF_Q)),
            pl.BlockSpec((1, T, dh), zcol(OFF_K)),
            pl.BlockSpec((1, T, dh), zcol(OFF_V)),
            pl.BlockSpec((1, T, dh), zcol(OFF_O)),
            pl.BlockSpec((1, T, LANE), lambda b, h: (b, 0, 0)),
            pl.BlockSpec((1, LANE), lambda b, h: (0, 0)),
            c_spec, m_spec, c_spec, m_spec,
        ],
        out_specs=[pl.BlockSpec((1, T, dh), lambda b, h: (b, 0, h)), c_spec, m_spec, c_spec, m_spec],
        out_shape=[jax.ShapeDtypeStruct((B, T, MIX_MLSTM), BF16), c_shape, m_shape, c_shape, m_shape],
        scratch_shapes=[
            pltpu.VMEM((nc * dh, L), F32),
            pltpu.VMEM((nc * dh, L), F32),
            pltpu.VMEM((nc * aug, L), F32),
            pltpu.VMEM((nc * aug, L), F32),
            pltpu.VMEM((nc * aug, dh), F32),
            pltpu.VMEM((nc * aug, dh), F32),
            pltpu.VMEM((nc * SUBLANE, LANE), F32),
            pltpu.VMEM((nc * SUBLANE, LANE), F32),
        ],
        compiler_params=_params(2),
        name="mlstm",
    )(z, z, z, z, zg, b_gate, cf0, mf0, cb0, mb0)


def kernel(x, c, ctx, c_ctx, w_mod, b_mod, w_in, conv_qk, b_gates, pool_w, pool_scale, fourier_w,
           w_out, ln1_g, ln1_b, w_up, conv_ffn_w, conv_ffn_b, w_down, ln2_g, ln2_b):
    B, T, D = x.shape
    Tc = ctx.shape[1]
    tm_x = 512
    tm_c = Tc
    xc = ctx
    s_c = jax.nn.silu(c)
    s_ctx = jax.nn.silu(c_ctx)
    filt_x, filt_c = _pool_filters(True), _pool_filters(False)
    cnt_x, cnt_c = _pool_inv_count(T, True), _pool_inv_count(Tc, False)
    dft_x, dft_c = _dft_matrix(T), _dft_matrix(Tc)
    zero_c = jnp.zeros((B, N_MLSTM, MLSTM_AUG, MLSTM_DH), F32)
    zero_m = jnp.zeros((B, N_MLSTM, SUBLANE, LANE), F32)
    for l in range(DEPTH):
        last = l == DEPTH - 1
        mx = [m[:, None, :] for m in jnp.split(s_c @ w_mod[l] + b_mod[l], 6, axis=-1)]
        mc = [jnp.broadcast_to(m[None, None, :], (B, 1, D))
              for m in jnp.split(s_ctx @ w_mod[l] + b_mod[l], 6, axis=-1)]
        w_main = w_in[l][:, :OFF_G].astype(BF16)
        w_gate = jnp.pad(w_in[l][:, OFF_G:], ((0, 0), (0, LANE - N_GATES))).astype(BF16)
        b_gate = jnp.pad(b_gates[l], (0, LANE - N_GATES))[None]
        w_pool_bd = _block_diag(pool_w[l]).astype(BF16)
        s_pool = pool_scale[l][None]
        w_out_b = w_out[l].astype(BF16)
        w_up_b = w_up[l].astype(BF16)
        w_down_b = w_down[l].astype(BF16)
        g1, b1 = ln1_g[l][None], ln1_b[l][None]
        g2, b2 = ln2_g[l][None], ln2_b[l][None]
        bcv = conv_ffn_b[l][None]

        zc, zgc = _proj_in(xc, mc[0], mc[1], w_main, w_gate, conv_qk[l], tm_c)
        ymc, cf, mf, cb, mb = _mlstm(zc, zgc, b_gate, zero_c, zero_m, zero_c, zero_m)
        zx, zgx = _proj_in(x, mx[0], mx[1], w_main, w_gate, conv_qk[l], tm_x)
        ymx = _mlstm(zx, zgx, b_gate, cf, mf, cb, mb)[0]
        ypx = _pool(zx, filt_x, cnt_x, w_pool_bd, s_pool, True)
        yfx = _fourier(zx, _fourier_mix_weights(fourier_w[l], T), dft_x)
        x = _proj_out(ypx, yfx, ymx, x, mx[2], w_out_b, g1, b1, 2 * tm_x)
        x = _ffn(x, mx[3], mx[4], mx[5], w_up_b, conv_ffn_w[l], bcv, w_down_b, g2, b2, tm_x)
        if not last:
            ypc = _pool(zc, filt_c, cnt_c, w_pool_bd, s_pool, False)
            yfc = _fourier(zc, _fourier_mix_weights(fourier_w[l], Tc), dft_c)
            xc = _proj_out(ypc, yfc, ymc, xc, mc[2], w_out_b, g1, b1, tm_c)
            xc = _ffn(xc, mc[3], mc[4], mc[5], w_up_b, conv_ffn_w[l], bcv, w_down_b, g2, b2, tm_c)
    return x
```

```python
import functools
import math

import jax
import jax.numpy as jnp
import numpy as np
from jax import lax
from jax.experimental import pallas as pl
from jax.experimental.pallas import tpu as pltpu

F32 = jnp.float32
BF16 = jnp.bfloat16

D_MODEL = 1024
DEPTH = 4
GRID_W = 64
GRID_SHIFT = GRID_W.bit_length() - 1
POOL_WINDOWS = (2, 4, 8, 16)
N_POOL = len(POOL_WINDOWS)
MIX_POOL = D_MODEL // 4
POOL_DIM = MIX_POOL // N_POOL
N_FOUR = 4
MIX_FOUR = D_MODEL // 4
FOUR_DIM = MIX_FOUR // N_FOUR
N_MLSTM = 4
MIX_MLSTM = D_MODEL // 2
MLSTM_DH = MIX_MLSTM // N_MLSTM
MLSTM_CHUNK = 128
N_GATES = 2 * 2 * N_MLSTM
D_FF = int(math.ceil(8 * D_MODEL / 3 / 128)) * 128
OFF_FOUR = MIX_POOL
OFF_O = OFF_FOUR + MIX_FOUR
OFF_Q = OFF_O + MIX_MLSTM
OFF_K = OFF_Q + MIX_MLSTM
OFF_V = OFF_K + MIX_MLSTM
OFF_G = OFF_V + MIX_MLSTM
ALPHA = (2 * DEPTH) ** 0.25
LN_EPS = 1e-6
LOG2_E = math.log2(math.e)

LANE = 128
LANE_SHIFT = LANE.bit_length() - 1
SUBLANE = 8
BF16_ROWS = 16
VMEM_LIMIT = 56 * 1024 * 1024
FF_CHUNK = 256
PROJ_CHUNK = 512
POOL_BLOCK = 256
FOUR_MIRROR_BLOCK = 256
POOL_PAD = (max(POOL_WINDOWS) // 2) * GRID_W
NEG_BIG = -1e30
MLSTM_GROUP = 8
MLSTM_AUG = MLSTM_DH + BF16_ROWS


def _params(n_axes):
    return pltpu.CompilerParams(dimension_semantics=("arbitrary",) * n_axes, vmem_limit_bytes=VMEM_LIMIT)


def _ln_rows(x):
    mu = jnp.mean(x, axis=-1, keepdims=True)
    xc = x - mu
    var = jnp.mean(xc * xc, axis=-1, keepdims=True)
    return xc * lax.rsqrt(var + LN_EPS)


def _sigmoid(x):
    return 0.5 * jnp.tanh(0.5 * x) + 0.5


def _log2_sigmoid(x2):
    return jnp.minimum(x2, 0.0) - jnp.log2(1.0 + jnp.exp2(-jnp.abs(x2)))


def _split_hi_lo(x):
    hi = x.astype(BF16)
    return hi, (x - hi.astype(F32)).astype(BF16)


def _proj_in_kernel(x_ref, xp_ref, xn_ref, sh_ref, sc_ref, w_ref, wg_ref, wc_ref, z_ref, g_ref, h_ref):
    i = pl.program_id(1)
    tm = x_ref.shape[1]
    mod_scale = 1.0 + sc_ref[0]
    mod_shift = sh_ref[0]
    u = _ln_rows(x_ref[0]) * mod_scale + mod_shift
    u_prev = (_ln_rows(xp_ref[0]) * mod_scale + mod_shift) * (i > 0).astype(F32)
    u_next = (_ln_rows(xn_ref[0]) * mod_scale + mod_shift) * (i < pl.num_programs(1) - 1).astype(F32)
    ub = u.astype(BF16)
    u_ext = jnp.concatenate([u_prev, u, u_next], axis=0).astype(BF16)
    n = z_ref.shape[-1]
    for c0 in range(0, n, PROJ_CHUNK):
        c1 = min(c0 + PROJ_CHUNK, n)
        if OFF_Q <= c0 and c1 <= OFF_V:
            h_ref[...] = jnp.dot(u_ext, w_ref[:, c0:c1], preferred_element_type=F32)
            w = wc_ref[:, c0 - OFF_Q:c1 - OFF_Q]
            y = (h_ref[SUBLANE - 1:SUBLANE - 1 + tm, :] * w[0:1] + h_ref[SUBLANE:SUBLANE + tm, :] * w[1:2]
                 + h_ref[SUBLANE + 1:SUBLANE + 1 + tm, :] * w[2:3])
            scale = 1.0 if c0 < OFF_K else MLSTM_DH ** -0.5
            z_ref[0, :, c0:c1] = (y * _sigmoid(y) * scale).astype(z_ref.dtype)
        else:
            z_ref[0, :, c0:c1] = jnp.dot(ub, w_ref[:, c0:c1], preferred_element_type=F32).astype(z_ref.dtype)
    g_ref[0] = jnp.dot(ub, wg_ref[...], preferred_element_type=F32)


def _proj_in(x, shift, scale, w_main, w_gate, w_conv, tm):
    B, T, D = x.shape
    n = w_main.shape[1]
    assert OFF_Q % PROJ_CHUNK == 0 and OFF_K % PROJ_CHUNK == 0 and OFF_V % PROJ_CHUNK == 0
    nb = tm // SUBLANE
    last = T // SUBLANE - 1
    return pl.pallas_call(
        _proj_in_kernel,
        grid=(B, T // tm),
        in_specs=[
            pl.BlockSpec((1, tm, D), lambda b, i: (b, i, 0)),
            pl.BlockSpec((1, SUBLANE, D), lambda b, i: (b, jnp.maximum(i * nb - 1, 0), 0)),
            pl.BlockSpec((1, SUBLANE, D), lambda b, i: (b, jnp.minimum((i + 1) * nb, last), 0)),
            pl.BlockSpec((1, 1, D), lambda b, i: (b, 0, 0)),
            pl.BlockSpec((1, 1, D), lambda b, i: (b, 0, 0)),
            pl.BlockSpec((D, n), lambda b, i: (0, 0)),
            pl.BlockSpec((D, LANE), lambda b, i: (0, 0)),
            pl.BlockSpec((3, 2 * MIX_MLSTM), lambda b, i: (0, 0)),
        ],
        out_specs=[
            pl.BlockSpec((1, tm, n), lambda b, i: (b, i, 0)),
            pl.BlockSpec((1, tm, LANE), lambda b, i: (b, i, 0)),
        ],
        out_shape=[
            jax.ShapeDtypeStruct((B, T, n), BF16),
            jax.ShapeDtypeStruct((B, T, LANE), F32),
        ],
        scratch_shapes=[pltpu.VMEM((tm + 2 * SUBLANE, PROJ_CHUNK), F32)],
        compiler_params=_params(2),
        name="proj_in",
    )(x, x, x, shift, scale, w_main, w_gate, w_conv)


def _proj_out_kernel(yp_ref, yf_ref, ym_ref, x_ref, gate_ref, wp_ref, wf_ref, wm_ref, lg_ref, lb_ref, o_ref):
    f = (jnp.dot(yp_ref[0], wp_ref[...], preferred_element_type=F32)
         + jnp.dot(yf_ref[0], wf_ref[...], preferred_element_type=F32)
         + jnp.dot(ym_ref[0], wm_ref[...], preferred_element_type=F32))
    r = ALPHA * x_ref[0] + gate_ref[0] * f
    o_ref[0] = _ln_rows(r) * lg_ref[...] + lb_ref[...]


def _proj_out(yp, yf, ym, x, gate, w, ln_g, ln_b, tm):
    B, T, D = x.shape
    tok = lambda b, i: (b, i, 0)
    return pl.pallas_call(
        _proj_out_kernel,
        grid=(B, T // tm),
        in_specs=[
            pl.BlockSpec((1, tm, MIX_POOL), tok),
            pl.BlockSpec((1, tm, MIX_FOUR), tok),
            pl.BlockSpec((1, tm, MIX_MLSTM), tok),
            pl.BlockSpec((1, tm, D), tok),
            pl.BlockSpec((1, 1, D), lambda b, i: (b, 0, 0)),
            pl.BlockSpec((MIX_POOL, D), lambda b, i: (0, 0)),
            pl.BlockSpec((MIX_FOUR, D), lambda b, i: (OFF_FOUR // MIX_FOUR, 0)),
            pl.BlockSpec((MIX_MLSTM, D), lambda b, i: (OFF_O // MIX_MLSTM, 0)),
            pl.BlockSpec((1, D), lambda b, i: (0, 0)),
            pl.BlockSpec((1, D), lambda b, i: (0, 0)),
        ],
        out_specs=pl.BlockSpec((1, tm, D), tok),
        out_shape=jax.ShapeDtypeStruct((B, T, D), F32),
        compiler_params=_params(2),
        name="proj_out",
    )(yp, yf, ym, x, gate, w, w, w, ln_g, ln_b)


def _gelu_tanh_times(x, half_a):
    c = math.sqrt(2.0 / math.pi)
    inner = x * (c + (c * 0.044715) * (x * x))
    return (x * (1.0 + jnp.tanh(inner))) * half_a


def _ffn_kernel(x_ref, xp_ref, xn_ref, sh_ref, sc_ref, gate_ref, wup_ref, wc_ref, bc_ref, wdn_ref,
                lg_ref, lb_ref, o_ref, ha_ref, hg_ref, p_ref):
    i = pl.program_id(1)
    tm = x_ref.shape[1]
    x = x_ref[0]
    mod_scale = 1.0 + sc_ref[0]
    mod_shift = sh_ref[0]
    has_prev = (i > 0).astype(F32)
    has_next = (i < pl.num_programs(1) - 1).astype(F32)
    u_prev = (_ln_rows(xp_ref[0]) * mod_scale + mod_shift) * has_prev
    u_next = (_ln_rows(xn_ref[0]) * mod_scale + mod_shift) * has_next
    u_main = _ln_rows(x) * mod_scale + mod_shift
    u_ext = jnp.concatenate([u_prev, u_main, u_next], axis=0).astype(BF16)

    def conv(h_ref, cols):
        w = wc_ref[:, cols]
        return (h_ref[SUBLANE - 1:SUBLANE - 1 + tm, :] * w[0:1]
                + h_ref[SUBLANE:SUBLANE + tm, :] * w[1:2]
                + h_ref[SUBLANE + 1:SUBLANE + 1 + tm, :] * w[2:3]
                + bc_ref[:, cols])

    for c in range(D_FF // FF_CHUNK):
        ca = slice(c * FF_CHUNK, (c + 1) * FF_CHUNK)
        cg = slice(D_FF + c * FF_CHUNK, D_FF + (c + 1) * FF_CHUNK)
        ha_ref[...] = jnp.dot(u_ext, wup_ref[:, ca], preferred_element_type=F32)
        hg_ref[...] = jnp.dot(u_ext, wup_ref[:, cg], preferred_element_type=F32)
        p_ref[:, ca] = _gelu_tanh_times(conv(hg_ref, cg), conv(ha_ref, ca)).astype(BF16)

    f = jnp.dot(p_ref[...], wdn_ref[...], preferred_element_type=F32)
    r = ALPHA * x + gate_ref[0] * f
    o_ref[0] = _ln_rows(r) * lg_ref[...] + lb_ref[...]


def _ffn(x, shift, scale, gate, w_up, w_conv, b_conv, w_down, ln_g, ln_b, tm):
    B, T, D = x.shape
    nb = tm // SUBLANE
    last = T // SUBLANE - 1
    const = dict(pipeline_mode=pl.Buffered(1))
    half_value = jnp.concatenate([jnp.full((D_FF,), 0.5, F32), jnp.ones((D_FF,), F32)])
    w_conv = w_conv * half_value
    b_conv = b_conv * half_value
    return pl.pallas_call(
        _ffn_kernel,
        grid=(B, T // tm),
        in_specs=[
            pl.BlockSpec((1, tm, D), lambda b, i: (b, i, 0)),
            pl.BlockSpec((1, SUBLANE, D), lambda b, i: (b, jnp.maximum(i * nb - 1, 0), 0)),
            pl.BlockSpec((1, SUBLANE, D), lambda b, i: (b, jnp.minimum((i + 1) * nb, last), 0)),
            pl.BlockSpec((1, 1, D), lambda b, i: (b, 0, 0)),
            pl.BlockSpec((1, 1, D), lambda b, i: (b, 0, 0)),
            pl.BlockSpec((1, 1, D), lambda b, i: (b, 0, 0)),
            pl.BlockSpec((D, 2 * D_FF), lambda b, i: (0, 0), **const),
            pl.BlockSpec((3, 2 * D_FF), lambda b, i: (0, 0), **const),
            pl.BlockSpec((1, 2 * D_FF), lambda b, i: (0, 0), **const),
            pl.BlockSpec((D_FF, D), lambda b, i: (0, 0), **const),
            pl.BlockSpec((1, D), lambda b, i: (0, 0)),
            pl.BlockSpec((1, D), lambda b, i: (0, 0)),
        ],
        out_specs=pl.BlockSpec((1, tm, D), lambda b, i: (b, i, 0)),
        out_shape=jax.ShapeDtypeStruct((B, T, D), F32),
        scratch_shapes=[
            pltpu.VMEM((tm + 2 * SUBLANE, FF_CHUNK), F32),
            pltpu.VMEM((tm + 2 * SUBLANE, FF_CHUNK), F32),
            pltpu.VMEM((tm, D_FF), BF16),
        ],
        compiler_params=_params(2),
        name="conv_ffn",
    )(x, x, x, shift, scale, gate, w_up, w_conv, b_conv, w_down, ln_g, ln_b)


def _band_matrix(n, w):
    pos = np.arange(n)
    lo = np.maximum(pos - w // 2, 0)
    hi = np.minimum(pos + w // 2 - 1, n - 1)
    return ((pos[None, :] >= lo[:, None]) & (pos[None, :] <= hi[:, None])).astype(np.float32)


def _pool_filters(grid2d):
    mats = []
    for w in POOL_WINDOWS:
        if grid2d:
            mats.append(np.kron(np.eye(POOL_BLOCK // GRID_W, dtype=np.float32), _band_matrix(GRID_W, w)))
        else:
            mats.append(_band_matrix(POOL_BLOCK, w))
    return jnp.asarray(np.stack(mats), BF16)


def _window_count(idx, half, n):
    return jnp.minimum(idx + (half - 1), n - 1) - jnp.maximum(idx - half, 0) + 1


def _pool_inv_count(T, grid2d):
    pos = lax.broadcasted_iota(jnp.int32, (T, MIX_POOL), 0)
    group = jnp.right_shift(lax.broadcasted_iota(jnp.int32, (T, MIX_POOL), 1), GRID_SHIFT)
    half = jnp.left_shift(1, group)
    if grid2d:
        count = (_window_count(jnp.right_shift(pos, GRID_SHIFT), half, T // GRID_W)
                 * _window_count(jnp.bitwise_and(pos, GRID_W - 1), half, GRID_W))
    else:
        count = _window_count(pos, half, T)
    return 1.0 / count.astype(F32)


def _pool_kernel(z_ref, a_ref, n_ref, w_ref, s_ref, y_ref, acc_ref, *, grid2d):
    T = z_ref.shape[1]
    nblk = T // POOL_BLOCK
    lane = lax.broadcasted_iota(jnp.int32, (POOL_BLOCK, MIX_POOL), 1)
    group = jnp.right_shift(lane, GRID_SHIFT)

    def col_filter(blk):
        r0 = pl.multiple_of(blk * POOL_BLOCK, POOL_BLOCK)
        xb = z_ref[0, pl.ds(r0, POOL_BLOCK), :]
        out = jnp.dot(a_ref[0], xb, preferred_element_type=F32)
        for g in range(1, N_POOL):
            out = jnp.where(group == g, jnp.dot(a_ref[g], xb, preferred_element_type=F32), out)
        return r0, xb, out

    def finish(r0, xb, box):
        p = box * n_ref[pl.ds(r0, POOL_BLOCK), :] - xb.astype(F32)
        y = jnp.dot(p.astype(BF16), w_ref[...], preferred_element_type=F32) * s_ref[...]
        y_ref[0, pl.ds(r0, POOL_BLOCK), :] = y.astype(y_ref.dtype)

    if not grid2d:
        assert T == POOL_BLOCK
        finish(*col_filter(0))
        return

    zeros = jnp.zeros((POOL_PAD, MIX_POOL), F32)
    acc_ref[0:POOL_PAD, :] = zeros
    acc_ref[POOL_PAD + T:POOL_PAD + T + POOL_PAD, :] = zeros

    def phase1(blk, carry):
        r0, _, out = col_filter(blk)
        acc_ref[pl.ds(POOL_PAD + r0, POOL_BLOCK), :] = out
        return carry

    lax.fori_loop(0, nblk, phase1, 0)

    lane_t = lax.broadcasted_iota(jnp.int32, (POOL_BLOCK, LANE), 1)

    def phase2(blk, carry):
        r0 = pl.multiple_of(blk * POOL_BLOCK, POOL_BLOCK)
        base = POOL_PAD + r0

        def slab(k, tile):
            return acc_ref[pl.ds(base + k * GRID_W, POOL_BLOCK), tile * LANE:(tile + 1) * LANE]

        s2 = slab(-1, 0) + slab(0, 0)
        s4 = s2 + slab(-2, 0) + slab(1, 0)
        s8 = slab(-4, 1)
        for k in range(-3, 4):
            s8 = s8 + slab(k, 1)
        s16 = s8
        for k in list(range(-8, -4)) + list(range(4, 8)):
            s16 = s16 + slab(k, 1)
        box = jnp.concatenate([jnp.where(lane_t < POOL_DIM, s2, s4), jnp.where(lane_t < POOL_DIM, s8, s16)], axis=1)
        finish(r0, z_ref[0, pl.ds(r0, POOL_BLOCK), :], box)
        return carry

    lax.fori_loop(0, nblk, phase2, 0)


def _pool(z, filters, inv_count, w_bd, s_row, grid2d):
    B, T, _ = z.shape
    return pl.pallas_call(
        functools.partial(_pool_kernel, grid2d=grid2d),
        grid=(B,),
        in_specs=[
            pl.BlockSpec((1, T, MIX_POOL), lambda b: (b, 0, 0)),
            pl.BlockSpec((N_POOL, POOL_BLOCK, POOL_BLOCK), lambda b: (0, 0, 0)),
            pl.BlockSpec((T, MIX_POOL), lambda b: (0, 0)),
            pl.BlockSpec((MIX_POOL, MIX_POOL), lambda b: (0, 0)),
            pl.BlockSpec((1, MIX_POOL), lambda b: (0, 0)),
        ],
        out_specs=pl.BlockSpec((1, T, MIX_POOL), lambda b: (b, 0, 0)),
        out_shape=jax.ShapeDtypeStruct((B, T, MIX_POOL), BF16),
        scratch_shapes=[pltpu.VMEM((T + 2 * POOL_PAD, MIX_POOL), F32)],
        compiler_params=_params(1),
        name="pool2d" if grid2d else "pool1d",
    )(z, filters, inv_count, w_bd, s_row)


def _block_diag(w):
    G, n, m = w.shape
    eye = jnp.eye(G, dtype=w.dtype)
    return (eye[:, None, :, None] * w[:, :, None, :]).reshape(G * n, G * m)


def _four_mix_kernel(z_ref, za_ref, zb_ref, zm_ref, g_ref, y_ref):
    tm = z_ref.shape[1]
    sb = zb_ref.shape[1]
    nsub = tm // sb
    gr, gi = g_ref[:, :MIX_FOUR], g_ref[:, MIX_FOUR:]
    r = lax.broadcasted_iota(jnp.int32, (sb, 2 * sb), 0)
    c = lax.broadcasted_iota(jnp.int32, (sb, 2 * sb), 1)
    perm = jnp.where(((r >= 1) & (c == sb - r)) | ((r == 0) & (c == sb)), 1.0, 0.0).astype(BF16)
    mid = jnp.dot(zm_ref[0], gr, preferred_element_type=F32)[0:1]
    row0 = lax.broadcasted_iota(jnp.int32, (sb, MIX_FOUR), 0) == 0
    for j in range(nsub):
        lo = (nsub - 1 - j) * sb
        if j == 0:
            src = jnp.concatenate([za_ref[0, lo:lo + sb, :], zb_ref[0]], axis=0)
        else:
            src = za_ref[0, lo:lo + 2 * sb, :]
        xr = jnp.dot(perm, src, preferred_element_type=F32)
        x = z_ref[0, j * sb:(j + 1) * sb, :].astype(F32)
        even = jnp.dot((x + xr).astype(BF16), gr, preferred_element_type=F32)
        odd = jnp.dot((x - xr).astype(BF16), gi, preferred_element_type=F32)
        if j == 0:
            odd = jnp.where(row0 & (pl.program_id(1) == 0), mid, odd)
        y_ref[0, 0, j * sb:(j + 1) * sb, :] = even.astype(y_ref.dtype)
        y_ref[0, 1, j * sb:(j + 1) * sb, :] = odd.astype(y_ref.dtype)


def _four_dft_kernel(d_ref, y_ref, o_ref):
    rows = 256
    for r0 in range(0, o_ref.shape[1], rows):
        o_ref[0, r0:r0 + rows, :] = jnp.dot(d_ref[r0:r0 + rows, :], y_ref[0],
                                            preferred_element_type=F32).astype(o_ref.dtype)


def _fourier(z, g_mix, dft):
    B, T, _ = z.shape
    M = T // 2
    tm = min(M, 2048)
    fcol = OFF_FOUR // MIX_FOUR
    sb = min(tm, FOUR_MIRROR_BLOCK)
    nt, nsb, per = T // tm, T // sb, tm // sb
    y = pl.pallas_call(
        _four_mix_kernel,
        grid=(B, M // tm),
        in_specs=[
            pl.BlockSpec((1, tm, MIX_FOUR), lambda b, i: (b, i, fcol)),
            pl.BlockSpec((1, tm, MIX_FOUR), lambda b, i: (b, nt - 1 - i, fcol)),
            pl.BlockSpec((1, sb, MIX_FOUR), lambda b, i: (b, (nsb - i * per) % nsb, fcol)),
            pl.BlockSpec((1, BF16_ROWS, MIX_FOUR), lambda b, i: (b, M // BF16_ROWS, fcol)),
            pl.BlockSpec((MIX_FOUR, 2 * MIX_FOUR), lambda b, i: (0, 0)),
        ],
        out_specs=pl.BlockSpec((1, 2, tm, MIX_FOUR), lambda b, i: (b, 0, i, 0)),
        out_shape=jax.ShapeDtypeStruct((B, 2, M, MIX_FOUR), BF16),
        compiler_params=_params(2),
        name="fourier_mix",
    )(z, z, z, z, g_mix)
    y = y.reshape(B, T, MIX_FOUR)
    tf = min(T, 1024)
    return pl.pallas_call(
        _four_dft_kernel,
        grid=(T // tf, B),
        in_specs=[
            pl.BlockSpec((tf, T), lambda m, b: (m, 0)),
            pl.BlockSpec((1, T, MIX_FOUR), lambda m, b: (b, 0, 0)),
        ],
        out_specs=pl.BlockSpec((1, tf, MIX_FOUR), lambda m, b: (b, m, 0)),
        out_shape=jax.ShapeDtypeStruct((B, T, MIX_FOUR), BF16),
        compiler_params=_params(2),
        name="fourier_dft",
    )(dft, y)


def _dft_matrix(n):
    m = n // 2
    a = m // GRID_W
    f = lax.broadcasted_iota(jnp.int32, (n, 1), 0)
    ang1 = ((f * GRID_W * lax.broadcasted_iota(jnp.int32, (1, a), 1)) % n).astype(F32) * (2.0 * math.pi / n)
    ang2 = ((f * lax.broadcasted_iota(jnp.int32, (1, GRID_W), 1)) % n).astype(F32) * (2.0 * math.pi / n)
    c1, s1 = jnp.cos(ang1)[:, :, None], jnp.sin(ang1)[:, :, None]
    c2, s2 = jnp.cos(ang2)[:, None, :], jnp.sin(ang2)[:, None, :]
    cos = (c1 * c2 - s1 * s2).reshape(n, m)
    sin = (s1 * c2 + c1 * s2).reshape(n, m)
    t0 = lax.broadcasted_iota(jnp.int32, (n, m), 1) == 0
    sign = (1 - 2 * (f % 2)).astype(F32)
    return jnp.concatenate([jnp.where(t0, 0.5, cos), jnp.where(t0, sign, sin)], axis=1).astype(BF16)


def _fourier_mix_weights(w_four, T):
    c = np.arange(FOUR_DIM)
    ang = 2.0 * np.pi * np.outer(c, c) / FOUR_DIM
    norm = 1.0 / math.sqrt(T * FOUR_DIM)
    gr = jnp.einsum('cf,gfe->gce', jnp.asarray(np.cos(ang) * norm, F32), w_four, precision='highest')
    gi = jnp.einsum('cf,gfe->gce', jnp.asarray(-np.sin(ang) * norm, F32), w_four, precision='highest')
    return jnp.concatenate([_block_diag(gr), _block_diag(gi)], axis=1).astype(BF16)


def _mlstm_kernel(q_ref, k_ref, v_ref, o_ref, g_ref, bg_ref, cf0_ref, mf0_ref, cb0_ref, mb0_ref,
                  y_ref, cf1_ref, mf1_ref, cb1_ref, mb1_ref,
                  hf_ref, hb_ref, pf_ref, pb_ref, uf_ref, ub_ref, rf_ref, rb_ref):
    T = q_ref.shape[1]
    L = MLSTM_CHUNK
    dh = MLSTM_DH
    aug = MLSTM_AUG
    nc = T // L
    head = pl.program_id(1)

    row = lax.broadcasted_iota(jnp.int32, (L, L), 0)
    col = lax.broadcasted_iota(jnp.int32, (L, L), 1)
    masks = (row <= col, row >= col)
    row2 = lax.broadcasted_iota(jnp.int32, (L, 2 * L), 0)
    col2 = jnp.bitwise_and(lax.broadcasted_iota(jnp.int32, (L, 2 * L), 1), L - 1)
    tri2 = jnp.where(col2 <= row2, 1.0, 0.0).astype(BF16)
    diag = row == col
    sub8 = lax.broadcasted_iota(jnp.int32, (SUBLANE, L), 0)
    ones_t = jnp.ones((aug - dh, L), F32)
    sel_row = jnp.bitwise_and(lax.broadcasted_iota(jnp.int32, (2 * LANE, 2 * LANE), 0), LANE - 1)
    sel_dir = jnp.right_shift(lax.broadcasted_iota(jnp.int32, (2 * LANE, 2 * LANE), 1), LANE_SHIFT)
    sel_i = jnp.where(sel_row == sel_dir * (2 * N_MLSTM) + head, 1.0, 0.0).astype(BF16)
    sel_f = jnp.where(sel_row == sel_dir * (2 * N_MLSTM) + N_MLSTM + head, 1.0, 0.0).astype(BF16)
    p_refs, u_refs, r_refs, h_refs = (pf_ref, pb_ref), (uf_ref, ub_ref), (rf_ref, rb_ref), (hf_ref, hb_ref)
    bias = bg_ref[...]

    def prep_group(c0):
        cs = [c0 + j for j in range(min(MLSTM_GROUP, nc))]
        rs = [pl.multiple_of(c * L, L) for c in cs]
        q = [q_ref[0, pl.ds(r, L), :] for r in rs]
        k = [k_ref[0, pl.ds(r, L), :] for r in rs]
        vt = [jnp.concatenate([v_ref[0, pl.ds(r, L), :].astype(F32).T, ones_t], axis=0).astype(BF16) for r in rs]
        g = [(g_ref[0, pl.ds(r, L), :] + bias) * LOG2_E for r in rs]
        i_rep = [jnp.dot(jnp.concatenate(_split_hi_lo(gg), axis=1), sel_i, preferred_element_type=F32)
                 for gg in g]
        lf_rep = [jnp.dot(jnp.concatenate(_split_hi_lo(_log2_sigmoid(gg)), axis=1), sel_f,
                          preferred_element_type=F32) for gg in g]
        st = [lax.dot_general(kk, qq, (((1,), (1,)), ((), ())), preferred_element_type=F32)
              for kk, qq in zip(k, q)]
        cum = [jnp.dot(tri2, jnp.concatenate(_split_hi_lo(f), axis=0), preferred_element_type=F32)
               for f in lf_rep]
        items = [(j, d) for j in range(len(cs)) for d in range(2)]
        for j, d in items:
            if d == 0:
                b = cum[j][:, :L]
                total = b[L - 1:L, :]
            else:
                total = cum[j][L - 1:L, L:]
                b = total - cum[j][:, L:] + lf_rep[j][:, L:]
            r_rep = i_rep[j][:, d * L:(d + 1) * L] - b
            dmt = jnp.where(masks[d], r_rep, NEG_BIG)
            pm = jnp.max(dmt, axis=0, keepdims=True)
            sg = (st[j] * jnp.exp2(dmt - pm)).astype(BF16)
            b_row = jnp.sum(jnp.where(diag, b, 0.0), axis=0, keepdims=True)
            g_end = total + r_rep
            g_loc = jnp.max(g_end, axis=0, keepdims=True)
            kw = (k[j].astype(F32) * jnp.exp2(g_end - g_loc)).astype(BF16)
            base = pl.multiple_of(cs[j] * aug, BF16_ROWS)
            pu = jnp.dot(vt[j], jnp.concatenate([sg, kw], axis=1), preferred_element_type=F32)
            p_refs[d][pl.ds(base, aug), :] = pu[:, :L]
            u_refs[d][pl.ds(base, aug), :] = pu[:, L:]
            rbase = pl.multiple_of(cs[j] * SUBLANE, SUBLANE)
            r_refs[d][pl.ds(rbase, SUBLANE), :] = jnp.where(
                sub8 == 0, b_row, jnp.where(sub8 == 1, b_row + pm, jnp.where(sub8 == 2, total, g_loc)))

    ga = min(MLSTM_GROUP, nc)

    def body_a(i, carry):
        prep_group(i * ga)
        return carry

    lax.fori_loop(0, nc // ga, body_a, 0)

    def step(d, c, ct, m):
        base = pl.multiple_of(c * aug, BF16_ROWS)
        rows = r_refs[d][pl.ds(pl.multiple_of(c * SUBLANE, SUBLANE), SUBLANE), :]
        b_row, a_row, total, g_loc = rows[0:1], rows[1:2], rows[2:3], rows[3:4]
        inter = b_row + m
        m_j = jnp.maximum(inter, a_row)
        qct = lax.dot_general(ct.astype(BF16), q_ref[0, pl.ds(pl.multiple_of(c * L, L), L), :],
                              (((1,), (1,)), ((), ())), preferred_element_type=F32)
        nd = jnp.exp2(inter - m_j) * qct + jnp.exp2(a_row - m_j) * p_refs[d][pl.ds(base, aug), :]
        den = jnp.maximum(jnp.abs(nd[dh:dh + 1]), jnp.exp2(-m_j))
        h_refs[d][pl.ds(pl.multiple_of(c * dh, dh), dh), :] = nd[:dh] / den
        m_new = jnp.maximum(total + m, g_loc)
        ct_new = jnp.exp2(total + m - m_new) * ct + jnp.exp2(g_loc - m_new) * u_refs[d][pl.ds(base, aug), :]
        return ct_new, m_new

    gb = min(nc, MLSTM_GROUP)

    def body_b(i, carry):
        cf, mf, cb, mb = carry
        for j in range(gb):
            cf, mf = step(0, i * gb + j, cf, mf)
            cb, mb = step(1, nc - 1 - (i * gb + j), cb, mb)
        return cf, mf, cb, mb

    cf, mf, cb, mb = lax.fori_loop(
        0, nc // gb, body_b, (cf0_ref[0, 0], mf0_ref[0, 0][0:1, :], cb0_ref[0, 0], mb0_ref[0, 0][0:1, :]))

    cf1_ref[0, 0] = cf
    cb1_ref[0, 0] = cb
    mf1_ref[0, 0] = jnp.broadcast_to(mf, (SUBLANE, LANE))
    mb1_ref[0, 0] = jnp.broadcast_to(mb, (SUBLANE, LANE))

    go = min(nc, MLSTM_GROUP)

    def body_out(i, carry):
        for j in range(go):
            c = i * go + j
            r = pl.multiple_of(c * L, L)
            rh = pl.multiple_of(c * dh, dh)
            h = (hf_ref[pl.ds(rh, dh), :] + hb_ref[pl.ds(rh, dh), :]).T
            y_ref[0, pl.ds(r, L), :] = (_sigmoid(o_ref[0, pl.ds(r, L), :].astype(F32)) * h).astype(y_ref.dtype)
        return carry

    lax.fori_loop(0, nc // go, body_out, 0)


def _mlstm(z, zg, b_gate, cf0, mf0, cb0, mb0):
    B, T, _ = z.shape
    H, dh, L, aug = N_MLSTM, MLSTM_DH, MLSTM_CHUNK, MLSTM_AUG
    nc = T // L
    zcol = lambda off: (lambda b, h: (b, 0, off // dh + h))
    st = lambda b, h: (b, h, 0, 0)
    c_spec = pl.BlockSpec((1, 1, aug, dh), st)
    m_spec = pl.BlockSpec((1, 1, SUBLANE, LANE), st)
    c_shape = jax.ShapeDtypeStruct((B, H, aug, dh), F32)
    m_shape = jax.ShapeDtypeStruct((B, H, SUBLANE, LANE), F32)
    return pl.pallas_call(
        _mlstm_kernel,
        grid=(B, H),
        in_specs=[
            pl.BlockSpec((1, T, dh), zcol(OFF_Q)),
            pl.BlockSpec((1, T, dh), zcol(OFF_K)),
            pl.BlockSpec((1, T, dh), zcol(OFF_V)),
            pl.BlockSpec((1, T, dh), zcol(OFF_O)),
            pl.BlockSpec((1, T, LANE), lambda b, h: (b, 0, 0)),
            pl.BlockSpec((1, LANE), lambda b, h: (0, 0)),
            c_spec, m_spec, c_spec, m_spec,
        ],
        out_specs=[pl.BlockSpec((1, T, dh), lambda b, h: (b, 0, h)), c_spec, m_spec, c_spec, m_spec],
        out_shape=[jax.ShapeDtypeStruct((B, T, MIX_MLSTM), BF16), c_shape, m_shape, c_shape, m_shape],
        scratch_shapes=[
            pltpu.VMEM((nc * dh, L), F32),
            pltpu.VMEM((nc * dh, L), F32),
            pltpu.VMEM((nc * aug, L), F32),
            pltpu.VMEM((nc * aug, L), F32),
            pltpu.VMEM((nc * aug, dh), F32),
            pltpu.VMEM((nc * aug, dh), F32),
            pltpu.VMEM((nc * SUBLANE, LANE), F32),
            pltpu.VMEM((nc * SUBLANE, LANE), F32),
        ],
        compiler_params=_params(2),
        name="mlstm",
    )(z, z, z, z, zg, b_gate, cf0, mf0, cb0, mb0)


def kernel(x, c, ctx, c_ctx, w_mod, b_mod, w_in, conv_qk, b_gates, pool_w, pool_scale, fourier_w,
           w_out, ln1_g, ln1_b, w_up, conv_ffn_w, conv_ffn_b, w_down, ln2_g, ln2_b):
    B, T, D = x.shape
    Tc = ctx.shape[1]
    tm_x = 512
    tm_c = Tc
    xc = ctx
    s_c = jax.nn.silu(c)
    s_ctx = jax.nn.silu(c_ctx)
    filt_x, filt_c = _pool_filters(True), _pool_filters(False)
    cnt_x, cnt_c = _pool_inv_count(T, True), _pool_inv_count(Tc, False)
    dft_x, dft_c = _dft_matrix(T), _dft_matrix(Tc)
    zero_c = jnp.zeros((B, N_MLSTM, MLSTM_AUG, MLSTM_DH), F32)
    zero_m = jnp.zeros((B, N_MLSTM, SUBLANE, LANE), F32)
    for l in range(DEPTH):
        last = l == DEPTH - 1
        mx = [m[:, None, :] for m in jnp.split(s_c @ w_mod[l] + b_mod[l], 6, axis=-1)]
        mc = [jnp.broadcast_to(m[None, None, :], (B, 1, D))
              for m in jnp.split(s_ctx @ w_mod[l] + b_mod[l], 6, axis=-1)]
        w_main = w_in[l][:, :OFF_G].astype(BF16)
        w_gate = jnp.pad(w_in[l][:, OFF_G:], ((0, 0), (0, LANE - N_GATES))).astype(BF16)
        b_gate = jnp.pad(b_gates[l], (0, LANE - N_GATES))[None]
        w_pool_bd = _block_diag(pool_w[l]).astype(BF16)
        s_pool = pool_scale[l][None]
        w_out_b = w_out[l].astype(BF16)
        w_up_b = w_up[l].astype(BF16)
        w_down_b = w_down[l].astype(BF16)
        g1, b1 = ln1_g[l][None], ln1_b[l][None]
        g2, b2 = ln2_g[l][None], ln2_b[l][None]
        bcv = conv_ffn_b[l][None]

        zc, zgc = _proj_in(xc, mc[0], mc[1], w_main, w_gate, conv_qk[l], tm_c)
        ymc, cf, mf, cb, mb = _mlstm(zc, zgc, b_gate, zero_c, zero_m, zero_c, zero_m)
        zx, zgx = _proj_in(x, mx[0], mx[1], w_main, w_gate, conv_qk[l], tm_x)
        ymx = _mlstm(zx, zgx, b_gate, cf, mf, cb, mb)[0]
        ypx = _pool(zx, filt_x, cnt_x, w_pool_bd, s_pool, True)
        yfx = _fourier(zx, _fourier_mix_weights(fourier_w[l], T), dft_x)
        x = _proj_out(ypx, yfx, ymx, x, mx[2], w_out_b, g1, b1, 2 * tm_x)
        x = _ffn(x, mx[3], mx[4], mx[5], w_up_b, conv_ffn_w[l], bcv, w_down_b, g2, b2, tm_x)
        if not last:
            ypc = _pool(zc, filt_c, cnt_c, w_pool_bd, s_pool, False)
            yfc = _fourier(zc, _fourier_mix_weights(fourier_w[l], Tc), dft_c)
            xc = _proj_out(ypc, yfc, ymc, xc, mc[2], w_out_b, g1, b1, tm_c)
            xc = _ffn(xc, mc[3], mc[4], mc[5], w_up_b, conv_ffn_w[l], bcv, w_down_b, g2, b2, tm_c)
    return x
```

```python
import functools
import math

import jax
import jax.numpy as jnp
import numpy as np
from jax import lax
from jax.experimental import pallas as pl
from jax.experimental.pallas import tpu as pltpu

F32 = jnp.float32
BF16 = jnp.bfloat16

D_MODEL = 1024
DEPTH = 4
GRID_W = 64
GRID_SHIFT = GRID_W.bit_length() - 1
POOL_WINDOWS = (2, 4, 8, 16)
N_POOL = len(POOL_WINDOWS)
MIX_POOL = D_MODEL // 4
POOL_DIM = MIX_POOL // N_POOL
N_FOUR = 4
MIX_FOUR = D_MODEL // 4
FOUR_DIM = MIX_FOUR // N_FOUR
N_MLSTM = 4
MIX_MLSTM = D_MODEL // 2
MLSTM_DH = MIX_MLSTM // N_MLSTM
MLSTM_CHUNK = 128
N_GATES = 2 * 2 * N_MLSTM
D_FF = int(math.ceil(8 * D_MODEL / 3 / 128)) * 128
OFF_FOUR = MIX_POOL
OFF_O = OFF_FOUR + MIX_FOUR
OFF_Q = OFF_O + MIX_MLSTM
OFF_K = OFF_Q + MIX_MLSTM
OFF_V = OFF_K + MIX_MLSTM
OFF_G = OFF_V + MIX_MLSTM
ALPHA = (2 * DEPTH) ** 0.25
LN_EPS = 1e-6
LOG2_E = math.log2(math.e)

LANE = 128
LANE_SHIFT = LANE.bit_length() - 1
SUBLANE = 8
BF16_ROWS = 16
VMEM_LIMIT = 56 * 1024 * 1024
FF_CHUNK = 256
PROJ_CHUNK = 512
POOL_BLOCK = 256
FOUR_MIRROR_BLOCK = 256
POOL_PAD = (max(POOL_WINDOWS) // 2) * GRID_W
NEG_BIG = -1e30
MLSTM_GROUP = 8
MLSTM_AUG = MLSTM_DH + BF16_ROWS


def _params(n_axes):
    return pltpu.CompilerParams(dimension_semantics=("arbitrary",) * n_axes, vmem_limit_bytes=VMEM_LIMIT)


def _ln_rows(x):
    mu = jnp.mean(x, axis=-1, keepdims=True)
    xc = x - mu
    var = jnp.mean(xc * xc, axis=-1, keepdims=True)
    return xc * lax.rsqrt(var + LN_EPS)


def _sigmoid(x):
    return 0.5 * jnp.tanh(0.5 * x) + 0.5


def _log2_sigmoid(x2):
    return jnp.minimum(x2, 0.0) - jnp.log2(1.0 + jnp.exp2(-jnp.abs(x2)))


def _split_hi_lo(x):
    hi = x.astype(BF16)
    return hi, (x - hi.astype(F32)).astype(BF16)


def _proj_in_kernel(x_ref, xp_ref, xn_ref, sh_ref, sc_ref, w_ref, wg_ref, wc_ref, z_ref, g_ref, h_ref):
    i = pl.program_id(1)
    tm = x_ref.shape[1]
    mod_scale = 1.0 + sc_ref[0]
    mod_shift = sh_ref[0]
    u = _ln_rows(x_ref[0]) * mod_scale + mod_shift
    u_prev = (_ln_rows(xp_ref[0]) * mod_scale + mod_shift) * (i > 0).astype(F32)
    u_next = (_ln_rows(xn_ref[0]) * mod_scale + mod_shift) * (i < pl.num_programs(1) - 1).astype(F32)
    ub = u.astype(BF16)
    u_ext = jnp.concatenate([u_prev, u, u_next], axis=0).astype(BF16)
    n = z_ref.shape[-1]
    for c0 in range(0, n, PROJ_CHUNK):
        c1 = min(c0 + PROJ_CHUNK, n)
        if OFF_Q <= c0 and c1 <= OFF_V:
            h_ref[...] = jnp.dot(u_ext, w_ref[:, c0:c1], preferred_element_type=F32)
            w = wc_ref[:, c0 - OFF_Q:c1 - OFF_Q]
            y = (h_ref[SUBLANE - 1:SUBLANE - 1 + tm, :] * w[0:1] + h_ref[SUBLANE:SUBLANE + tm, :] * w[1:2]
                 + h_ref[SUBLANE + 1:SUBLANE + 1 + tm, :] * w[2:3])
            scale = 1.0 if c0 < OFF_K else MLSTM_DH ** -0.5
            z_ref[0, :, c0:c1] = (y * _sigmoid(y) * scale).astype(z_ref.dtype)
        else:
            z_ref[0, :, c0:c1] = jnp.dot(ub, w_ref[:, c0:c1], preferred_element_type=F32).astype(z_ref.dtype)
    g_ref[0] = jnp.dot(ub, wg_ref[...], preferred_element_type=F32)


def _proj_in(x, shift, scale, w_main, w_gate, w_conv, tm):
    B, T, D = x.shape
    n = w_main.shape[1]
    assert OFF_Q % PROJ_CHUNK == 0 and OFF_K % PROJ_CHUNK == 0 and OFF_V % PROJ_CHUNK == 0
    nb = tm // SUBLANE
    last = T // SUBLANE - 1
    return pl.pallas_call(
        _proj_in_kernel,
        grid=(B, T // tm),
        in_specs=[
            pl.BlockSpec((1, tm, D), lambda b, i: (b, i, 0)),
            pl.BlockSpec((1, SUBLANE, D), lambda b, i: (b, jnp.maximum(i * nb - 1, 0), 0)),
            pl.BlockSpec((1, SUBLANE, D), lambda b, i: (b, jnp.minimum((i + 1) * nb, last), 0)),
            pl.BlockSpec((1, 1, D), lambda b, i: (b, 0, 0)),
            pl.BlockSpec((1, 1, D), lambda b, i: (b, 0, 0)),
            pl.BlockSpec((D, n), lambda b, i: (0, 0)),
            pl.BlockSpec((D, LANE), lambda b, i: (0, 0)),
            pl.BlockSpec((3, 2 * MIX_MLSTM), lambda b, i: (0, 0)),
        ],
        out_specs=[
            pl.BlockSpec((1, tm, n), lambda b, i: (b, i, 0)),
            pl.BlockSpec((1, tm, LANE), lambda b, i: (b, i, 0)),
        ],
        out_shape=[
            jax.ShapeDtypeStruct((B, T, n), BF16),
            jax.ShapeDtypeStruct((B, T, LANE), F32),
        ],
        scratch_shapes=[pltpu.VMEM((tm + 2 * SUBLANE, PROJ_CHUNK), F32)],
        compiler_params=_params(2),
        name="proj_in",
    )(x, x, x, shift, scale, w_main, w_gate, w_conv)


def _proj_out_kernel(yp_ref, yf_ref, ym_ref, x_ref, gate_ref, wp_ref, wf_ref, wm_ref, lg_ref, lb_ref, o_ref):
    f = (jnp.dot(yp_ref[0], wp_ref[...], preferred_element_type=F32)
         + jnp.dot(yf_ref[0], wf_ref[...], preferred_element_type=F32)
         + jnp.dot(ym_ref[0], wm_ref[...], preferred_element_type=F32))
    r = ALPHA * x_ref[0] + gate_ref[0] * f
    o_ref[0] = _ln_rows(r) * lg_ref[...] + lb_ref[...]


def _proj_out(yp, yf, ym, x, gate, w, ln_g, ln_b, tm):
    B, T, D = x.shape
    tok = lambda b, i: (b, i, 0)
    return pl.pallas_call(
        _proj_out_kernel,
        grid=(B, T // tm),
        in_specs=[
            pl.BlockSpec((1, tm, MIX_POOL), tok),
            pl.BlockSpec((1, tm, MIX_FOUR), tok),
            pl.BlockSpec((1, tm, MIX_MLSTM), tok),
            pl.BlockSpec((1, tm, D), tok),
            pl.BlockSpec((1, 1, D), lambda b, i: (b, 0, 0)),
            pl.BlockSpec((MIX_POOL, D), lambda b, i: (0, 0)),
            pl.BlockSpec((MIX_FOUR, D), lambda b, i: (OFF_FOUR // MIX_FOUR, 0)),
            pl.BlockSpec((MIX_MLSTM, D), lambda b, i: (OFF_O // MIX_MLSTM, 0)),
            pl.BlockSpec((1, D), lambda b, i: (0, 0)),
            pl.BlockSpec((1, D), lambda b, i: (0, 0)),
        ],
        out_specs=pl.BlockSpec((1, tm, D), tok),
        out_shape=jax.ShapeDtypeStruct((B, T, D), F32),
        compiler_params=_params(2),
        name="proj_out",
    )(yp, yf, ym, x, gate, w, w, w, ln_g, ln_b)


def _gelu_tanh_times(x, half_a):
    c = math.sqrt(2.0 / math.pi)
    inner = x * (c + (c * 0.044715) * (x * x))
    return (x * (1.0 + jnp.tanh(inner))) * half_a


def _ffn_kernel(x_ref, xp_ref, xn_ref, sh_ref, sc_ref, gate_ref, wup_ref, wc_ref, bc_ref, wdn_ref,
                lg_ref, lb_ref, o_ref, ha_ref, hg_ref, p_ref):
    i = pl.program_id(1)
    tm = x_ref.shape[1]
    x = x_ref[0]
    mod_scale = 1.0 + sc_ref[0]
    mod_shift = sh_ref[0]
    has_prev = (i > 0).astype(F32)
    has_next = (i < pl.num_programs(1) - 1).astype(F32)
    u_prev = (_ln_rows(xp_ref[0]) * mod_scale + mod_shift) * has_prev
    u_next = (_ln_rows(xn_ref[0]) * mod_scale + mod_shift) * has_next
    u_main = _ln_rows(x) * mod_scale + mod_shift
    u_ext = jnp.concatenate([u_prev, u_main, u_next], axis=0).astype(BF16)

    def conv(h_ref, cols):
        w = wc_ref[:, cols]
        return (h_ref[SUBLANE - 1:SUBLANE - 1 + tm, :] * w[0:1]
                + h_ref[SUBLANE:SUBLANE + tm, :] * w[1:2]
                + h_ref[SUBLANE + 1:SUBLANE + 1 + tm, :] * w[2:3]
                + bc_ref[:, cols])

    for c in range(D_FF // FF_CHUNK):
        ca = slice(c * FF_CHUNK, (c + 1) * FF_CHUNK)
        cg = slice(D_FF + c * FF_CHUNK, D_FF + (c + 1) * FF_CHUNK)
        ha_ref[...] = jnp.dot(u_ext, wup_ref[:, ca], preferred_element_type=F32)
        hg_ref[...] = jnp.dot(u_ext, wup_ref[:, cg], preferred_element_type=F32)
        p_ref[:, ca] = _gelu_tanh_times(conv(hg_ref, cg), conv(ha_ref, ca)).astype(BF16)

    f = jnp.dot(p_ref[...], wdn_ref[...], preferred_element_type=F32)
    r = ALPHA * x + gate_ref[0] * f
    o_ref[0] = _ln_rows(r) * lg_ref[...] + lb_ref[...]


def _ffn(x, shift, scale, gate, w_up, w_conv, b_conv, w_down, ln_g, ln_b, tm):
    B, T, D = x.shape
    nb = tm // SUBLANE
    last = T // SUBLANE - 1
    const = dict(pipeline_mode=pl.Buffered(1))
    half_value = jnp.concatenate([jnp.full((D_FF,), 0.5, F32), jnp.ones((D_FF,), F32)])
    w_conv = w_conv * half_value
    b_conv = b_conv * half_value
    return pl.pallas_call(
        _ffn_kernel,
        grid=(B, T // tm),
        in_specs=[
            pl.BlockSpec((1, tm, D), lambda b, i: (b, i, 0)),
            pl.BlockSpec((1, SUBLANE, D), lambda b, i: (b, jnp.maximum(i * nb - 1, 0), 0)),
            pl.BlockSpec((1, SUBLANE, D), lambda b, i: (b, jnp.minimum((i + 1) * nb, last), 0)),
            pl.BlockSpec((1, 1, D), lambda b, i: (b, 0, 0)),
            pl.BlockSpec((1, 1, D), lambda b, i: (b, 0, 0)),
            pl.BlockSpec((1, 1, D), lambda b, i: (b, 0, 0)),
            pl.BlockSpec((D, 2 * D_FF), lambda b, i: (0, 0), **const),
            pl.BlockSpec((3, 2 * D_FF), lambda b, i: (0, 0), **const),
            pl.BlockSpec((1, 2 * D_FF), lambda b, i: (0, 0), **const),
            pl.BlockSpec((D_FF, D), lambda b, i: (0, 0), **const),
            pl.BlockSpec((1, D), lambda b, i: (0, 0)),
            pl.BlockSpec((1, D), lambda b, i: (0, 0)),
        ],
        out_specs=pl.BlockSpec((1, tm, D), lambda b, i: (b, i, 0)),
        out_shape=jax.ShapeDtypeStruct((B, T, D), F32),
        scratch_shapes=[
            pltpu.VMEM((tm + 2 * SUBLANE, FF_CHUNK), F32),
            pltpu.VMEM((tm + 2 * SUBLANE, FF_CHUNK), F32),
            pltpu.VMEM((tm, D_FF), BF16),
        ],
        compiler_params=_params(2),
        name="conv_ffn",
    )(x, x, x, shift, scale, gate, w_up, w_conv, b_conv, w_down, ln_g, ln_b)


def _band_matrix(n, w):
    pos = np.arange(n)
    lo = np.maximum(pos - w // 2, 0)
    hi = np.minimum(pos + w // 2 - 1, n - 1)
    return ((pos[None, :] >= lo[:, None]) & (pos[None, :] <= hi[:, None])).astype(np.float32)


def _pool_filters(grid2d):
    mats = []
    for w in POOL_WINDOWS:
        if grid2d:
            mats.append(np.kron(np.eye(POOL_BLOCK // GRID_W, dtype=np.float32), _band_matrix(GRID_W, w)))
        else:
            mats.append(_band_matrix(POOL_BLOCK, w))
    return jnp.asarray(np.stack(mats), BF16)


def _window_count(idx, half, n):
    return jnp.minimum(idx + (half - 1), n - 1) - jnp.maximum(idx - half, 0) + 1


def _pool_inv_count(T, grid2d):
    pos = lax.broadcasted_iota(jnp.int32, (T, MIX_POOL), 0)
    group = jnp.right_shift(lax.broadcasted_iota(jnp.int32, (T, MIX_POOL), 1), GRID_SHIFT)
    half = jnp.left_shift(1, group)
    if grid2d:
        count = (_window_count(jnp.right_shift(pos, GRID_SHIFT), half, T // GRID_W)
                 * _window_count(jnp.bitwise_and(pos, GRID_W - 1), half, GRID_W))
    else:
        count = _window_count(pos, half, T)
    return 1.0 / count.astype(F32)


def _pool_kernel(z_ref, a_ref, n_ref, w_ref, s_ref, y_ref, acc_ref, *, grid2d):
    T = z_ref.shape[1]
    nblk = T // POOL_BLOCK
    lane = lax.broadcasted_iota(jnp.int32, (POOL_BLOCK, MIX_POOL), 1)
    group = jnp.right_shift(lane, GRID_SHIFT)

    def col_filter(blk):
        r0 = pl.multiple_of(blk * POOL_BLOCK, POOL_BLOCK)
        xb = z_ref[0, pl.ds(r0, POOL_BLOCK), :]
        out = jnp.dot(a_ref[0], xb, preferred_element_type=F32)
        for g in range(1, N_POOL):
            out = jnp.where(group == g, jnp.dot(a_ref[g], xb, preferred_element_type=F32), out)
        return r0, xb, out

    def finish(r0, xb, box):
        p = box * n_ref[pl.ds(r0, POOL_BLOCK), :] - xb.astype(F32)
        y = jnp.dot(p.astype(BF16), w_ref[...], preferred_element_type=F32) * s_ref[...]
        y_ref[0, pl.ds(r0, POOL_BLOCK), :] = y.astype(y_ref.dtype)

    if not grid2d:
        assert T == POOL_BLOCK
        finish(*col_filter(0))
        return

    zeros = jnp.zeros((POOL_PAD, MIX_POOL), F32)
    acc_ref[0:POOL_PAD, :] = zeros
    acc_ref[POOL_PAD + T:POOL_PAD + T + POOL_PAD, :] = zeros

    def phase1(blk, carry):
        r0, _, out = col_filter(blk)
        acc_ref[pl.ds(POOL_PAD + r0, POOL_BLOCK), :] = out
        return carry

    lax.fori_loop(0, nblk, phase1, 0)

    lane_t = lax.broadcasted_iota(jnp.int32, (POOL_BLOCK, LANE), 1)

    def phase2(blk, carry):
        r0 = pl.multiple_of(blk * POOL_BLOCK, POOL_BLOCK)
        base = POOL_PAD + r0

        def slab(k, tile):
            return acc_ref[pl.ds(base + k * GRID_W, POOL_BLOCK), tile * LANE:(tile + 1) * LANE]

        s2 = slab(-1, 0) + slab(0, 0)
        s4 = s2 + slab(-2, 0) + slab(1, 0)
        s8 = slab(-4, 1)
        for k in range(-3, 4):
            s8 = s8 + slab(k, 1)
        s16 = s8
        for k in list(range(-8, -4)) + list(range(4, 8)):
            s16 = s16 + slab(k, 1)
        box = jnp.concatenate([jnp.where(lane_t < POOL_DIM, s2, s4), jnp.where(lane_t < POOL_DIM, s8, s16)], axis=1)
        finish(r0, z_ref[0, pl.ds(r0, POOL_BLOCK), :], box)
        return carry

    lax.fori_loop(0, nblk, phase2, 0)


def _pool(z, filters, inv_count, w_bd, s_row, grid2d):
    B, T, _ = z.shape
    return pl.pallas_call(
        functools.partial(_pool_kernel, grid2d=grid2d),
        grid=(B,),
        in_specs=[
            pl.BlockSpec((1, T, MIX_POOL), lambda b: (b, 0, 0)),
            pl.BlockSpec((N_POOL, POOL_BLOCK, POOL_BLOCK), lambda b: (0, 0, 0)),
            pl.BlockSpec((T, MIX_POOL), lambda b: (0, 0)),
            pl.BlockSpec((MIX_POOL, MIX_POOL), lambda b: (0, 0)),
            pl.BlockSpec((1, MIX_POOL), lambda b: (0, 0)),
        ],
        out_specs=pl.BlockSpec((1, T, MIX_POOL), lambda b: (b, 0, 0)),
        out_shape=jax.ShapeDtypeStruct((B, T, MIX_POOL), BF16),
        scratch_shapes=[pltpu.VMEM((T + 2 * POOL_PAD, MIX_POOL), F32)],
        compiler_params=_params(1),
        name="pool2d" if grid2d else "pool1d",
    )(z, filters, inv_count, w_bd, s_row)


def _block_diag(w):
    G, n, m = w.shape
    eye = jnp.eye(G, dtype=w.dtype)
    return (eye[:, None, :, None] * w[:, :, None, :]).reshape(G * n, G * m)


def _four_mix_kernel(z_ref, za_ref, zb_ref, zm_ref, g_ref, y_ref):
    tm = z_ref.shape[1]
    sb = zb_ref.shape[1]
    nsub = tm // sb
    gr, gi = g_ref[:, :MIX_FOUR], g_ref[:, MIX_FOUR:]
    r = lax.broadcasted_iota(jnp.int32, (sb, 2 * sb), 0)
    c = lax.broadcasted_iota(jnp.int32, (sb, 2 * sb), 1)
    perm = jnp.where(((r >= 1) & (c == sb - r)) | ((r == 0) & (c == sb)), 1.0, 0.0).astype(BF16)
    mid = jnp.dot(zm_ref[0], gr, preferred_element_type=F32)[0:1]
    row0 = lax.broadcasted_iota(jnp.int32, (sb, MIX_FOUR), 0) == 0
    for j in range(nsub):
        lo = (nsub - 1 - j) * sb
        if j == 0:
            src = jnp.concatenate([za_ref[0, lo:lo + sb, :], zb_ref[0]], axis=0)
        else:
            src = za_ref[0, lo:lo + 2 * sb, :]
        xr = jnp.dot(perm, src, preferred_element_type=F32)
        x = z_ref[0, j * sb:(j + 1) * sb, :].astype(F32)
        even = jnp.dot((x + xr).astype(BF16), gr, preferred_element_type=F32)
        odd = jnp.dot((x - xr).astype(BF16), gi, preferred_element_type=F32)
        if j == 0:
            odd = jnp.where(row0 & (pl.program_id(1) == 0), mid, odd)
        y_ref[0, 0, j * sb:(j + 1) * sb, :] = even.astype(y_ref.dtype)
        y_ref[0, 1, j * sb:(j + 1) * sb, :] = odd.astype(y_ref.dtype)


def _four_dft_kernel(d_ref, y_ref, o_ref):
    rows = 256
    for r0 in range(0, o_ref.shape[1], rows):
        o_ref[0, r0:r0 + rows, :] = jnp.dot(d_ref[r0:r0 + rows, :], y_ref[0],
                                            preferred_element_type=F32).astype(o_ref.dtype)


def _fourier(z, g_mix, dft):
    B, T, _ = z.shape
    M = T // 2
    tm = min(M, 2048)
    fcol = OFF_FOUR // MIX_FOUR
    sb = min(tm, FOUR_MIRROR_BLOCK)
    nt, nsb, per = T // tm, T // sb, tm // sb
    y = pl.pallas_call(
        _four_mix_kernel,
        grid=(B, M // tm),
        in_specs=[
            pl.BlockSpec((1, tm, MIX_FOUR), lambda b, i: (b, i, fcol)),
            pl.BlockSpec((1, tm, MIX_FOUR), lambda b, i: (b, nt - 1 - i, fcol)),
            pl.BlockSpec((1, sb, MIX_FOUR), lambda b, i: (b, (nsb - i * per) % nsb, fcol)),
            pl.BlockSpec((1, BF16_ROWS, MIX_FOUR), lambda b, i: (b, M // BF16_ROWS, fcol)),
            pl.BlockSpec((MIX_FOUR, 2 * MIX_FOUR), lambda b, i: (0, 0)),
        ],
        out_specs=pl.BlockSpec((1, 2, tm, MIX_FOUR), lambda b, i: (b, 0, i, 0)),
        out_shape=jax.ShapeDtypeStruct((B, 2, M, MIX_FOUR), BF16),
        compiler_params=_params(2),
        name="fourier_mix",
    )(z, z, z, z, g_mix)
    y = y.reshape(B, T, MIX_FOUR)
    tf = min(T, 1024)
    return pl.pallas_call(
        _four_dft_kernel,
        grid=(T // tf, B),
        in_specs=[
            pl.BlockSpec((tf, T), lambda m, b: (m, 0)),
            pl.BlockSpec((1, T, MIX_FOUR), lambda m, b: (b, 0, 0)),
        ],
        out_specs=pl.BlockSpec((1, tf, MIX_FOUR), lambda m, b: (b, m, 0)),
        out_shape=jax.ShapeDtypeStruct((B, T, MIX_FOUR), BF16),
        compiler_params=_params(2),
        name="fourier_dft",
    )(dft, y)


def _dft_matrix(n):
    m = n // 2
    a = m // GRID_W
    f = lax.broadcasted_iota(jnp.int32, (n, 1), 0)
    ang1 = ((f * GRID_W * lax.broadcasted_iota(jnp.int32, (1, a), 1)) % n).astype(F32) * (2.0 * math.pi / n)
    ang2 = ((f * lax.broadcasted_iota(jnp.int32, (1, GRID_W), 1)) % n).astype(F32) * (2.0 * math.pi / n)
    c1, s1 = jnp.cos(ang1)[:, :, None], jnp.sin(ang1)[:, :, None]
    c2, s2 = jnp.cos(ang2)[:, None, :], jnp.sin(ang2)[:, None, :]
    cos = (c1 * c2 - s1 * s2).reshape(n, m)
    sin = (s1 * c2 + c1 * s2).reshape(n, m)
    t0 = lax.broadcasted_iota(jnp.int32, (n, m), 1) == 0
    sign = (1 - 2 * (f % 2)).astype(F32)
    return jnp.concatenate([jnp.where(t0, 0.5, cos), jnp.where(t0, sign, sin)], axis=1).astype(BF16)


def _fourier_mix_weights(w_four, T):
    c = np.arange(FOUR_DIM)
    ang = 2.0 * np.pi * np.outer(c, c) / FOUR_DIM
    norm = 1.0 / math.sqrt(T * FOUR_DIM)
    gr = jnp.einsum('cf,gfe->gce', jnp.asarray(np.cos(ang) * norm, F32), w_four, precision='highest')
    gi = jnp.einsum('cf,gfe->gce', jnp.asarray(-np.sin(ang) * norm, F32), w_four, precision='highest')
    return jnp.concatenate([_block_diag(gr), _block_diag(gi)], axis=1).astype(BF16)


def _mlstm_kernel(q_ref, k_ref, v_ref, o_ref, g_ref, bg_ref, cf0_ref, mf0_ref, cb0_ref, mb0_ref,
                  y_ref, cf1_ref, mf1_ref, cb1_ref, mb1_ref,
                  hf_ref, hb_ref, pf_ref, pb_ref, uf_ref, ub_ref, rf_ref, rb_ref):
    T = q_ref.shape[1]
    L = MLSTM_CHUNK
    dh = MLSTM_DH
    aug = MLSTM_AUG
    nc = T // L
    head = pl.program_id(1)

    row = lax.broadcasted_iota(jnp.int32, (L, L), 0)
    col = lax.broadcasted_iota(jnp.int32, (L, L), 1)
    masks = (row <= col, row >= col)
    row2 = lax.broadcasted_iota(jnp.int32, (L, 2 * L), 0)
    col2 = jnp.bitwise_and(lax.broadcasted_iota(jnp.int32, (L, 2 * L), 1), L - 1)
    tri2 = jnp.where(col2 <= row2, 1.0, 0.0).astype(BF16)
    diag = row == col
    sub8 = lax.broadcasted_iota(jnp.int32, (SUBLANE, L), 0)
    ones_t = jnp.ones((aug - dh, L), F32)
    sel_row = jnp.bitwise_and(lax.broadcasted_iota(jnp.int32, (2 * LANE, 2 * LANE), 0), LANE - 1)
    sel_dir = jnp.right_shift(lax.broadcasted_iota(jnp.int32, (2 * LANE, 2 * LANE), 1), LANE_SHIFT)
    sel_i = jnp.where(sel_row == sel_dir * (2 * N_MLSTM) + head, 1.0, 0.0).astype(BF16)
    sel_f = jnp.where(sel_row == sel_dir * (2 * N_MLSTM) + N_MLSTM + head, 1.0, 0.0).astype(BF16)
    p_refs, u_refs, r_refs, h_refs = (pf_ref, pb_ref), (uf_ref, ub_ref), (rf_ref, rb_ref), (hf_ref, hb_ref)
    bias = bg_ref[...]

    def prep_group(c0):
        cs = [c0 + j for j in range(min(MLSTM_GROUP, nc))]
        rs = [pl.multiple_of(c * L, L) for c in cs]
        q = [q_ref[0, pl.ds(r, L), :] for r in rs]
        k = [k_ref[0, pl.ds(r, L), :] for r in rs]
        vt = [jnp.concatenate([v_ref[0, pl.ds(r, L), :].astype(F32).T, ones_t], axis=0).astype(BF16) for r in rs]
        g = [(g_ref[0, pl.ds(r, L), :] + bias) * LOG2_E for r in rs]
        i_rep = [jnp.dot(jnp.concatenate(_split_hi_lo(gg), axis=1), sel_i, preferred_element_type=F32)
                 for gg in g]
        lf_rep = [jnp.dot(jnp.concatenate(_split_hi_lo(_log2_sigmoid(gg)), axis=1), sel_f,
                          preferred_element_type=F32) for gg in g]
        st = [lax.dot_general(kk, qq, (((1,), (1,)), ((), ())), preferred_element_type=F32)
              for kk, qq in zip(k, q)]
        cum = [jnp.dot(tri2, jnp.concatenate(_split_hi_lo(f), axis=0), preferred_element_type=F32)
               for f in lf_rep]
        items = [(j, d) for j in range(len(cs)) for d in range(2)]
        for j, d in items:
            if d == 0:
                b = cum[j][:, :L]
                total = b[L - 1:L, :]
            else:
                total = cum[j][L - 1:L, L:]
                b = total - cum[j][:, L:] + lf_rep[j][:, L:]
            r_rep = i_rep[j][:, d * L:(d + 1) * L] - b
            dmt = jnp.where(masks[d], r_rep, NEG_BIG)
            pm = jnp.max(dmt, axis=0, keepdims=True)
            sg = (st[j] * jnp.exp2(dmt - pm)).astype(BF16)
            b_row = jnp.sum(jnp.where(diag, b, 0.0), axis=0, keepdims=True)
            g_end = total + r_rep
            g_loc = jnp.max(g_end, axis=0, keepdims=True)
            kw = (k[j].astype(F32) * jnp.exp2(g_end - g_loc)).astype(BF16)
            base = pl.multiple_of(cs[j] * aug, BF16_ROWS)
            pu = jnp.dot(vt[j], jnp.concatenate([sg, kw], axis=1), preferred_element_type=F32)
            p_refs[d][pl.ds(base, aug), :] = pu[:, :L]
            u_refs[d][pl.ds(base, aug), :] = pu[:, L:]
            rbase = pl.multiple_of(cs[j] * SUBLANE, SUBLANE)
            r_refs[d][pl.ds(rbase, SUBLANE), :] = jnp.where(
                sub8 == 0, b_row, jnp.where(sub8 == 1, b_row + pm, jnp.where(sub8 == 2, total, g_loc)))

    ga = min(MLSTM_GROUP, nc)

    def body_a(i, carry):
        prep_group(i * ga)
        return carry

    lax.fori_loop(0, nc // ga, body_a, 0)

    def step(d, c, ct, m):
        base = pl.multiple_of(c * aug, BF16_ROWS)
        rows = r_refs[d][pl.ds(pl.multiple_of(c * SUBLANE, SUBLANE), SUBLANE), :]
        b_row, a_row, total, g_loc = rows[0:1], rows[1:2], rows[2:3], rows[3:4]
        inter = b_row + m
        m_j = jnp.maximum(inter, a_row)
        qct = lax.dot_general(ct.astype(BF16), q_ref[0, pl.ds(pl.multiple_of(c * L, L), L), :],
                              (((1,), (1,)), ((), ())), preferred_element_type=F32)
        nd = jnp.exp2(inter - m_j) * qct + jnp.exp2(a_row - m_j) * p_refs[d][pl.ds(base, aug), :]
        den = jnp.maximum(jnp.abs(nd[dh:dh + 1]), jnp.exp2(-m_j))
        h_refs[d][pl.ds(pl.multiple_of(c * dh, dh), dh), :] = nd[:dh] / den
        m_new = jnp.maximum(total + m, g_loc)
        ct_new = jnp.exp2(total + m - m_new) * ct + jnp.exp2(g_loc - m_new) * u_refs[d][pl.ds(base, aug), :]
        return ct_new, m_new

    gb = min(nc, MLSTM_GROUP)

    def body_b(i, carry):
        cf, mf, cb, mb = carry
        for j in range(gb):
            cf, mf = step(0, i * gb + j, cf, mf)
            cb, mb = step(1, nc - 1 - (i * gb + j), cb, mb)
        return cf, mf, cb, mb

    cf, mf, cb, mb = lax.fori_loop(
        0, nc // gb, body_b, (cf0_ref[0, 0], mf0_ref[0, 0][0:1, :], cb0_ref[0, 0], mb0_ref[0, 0][0:1, :]))

    cf1_ref[0, 0] = cf
    cb1_ref[0, 0] = cb
    mf1_ref[0, 0] = jnp.broadcast_to(mf, (SUBLANE, LANE))
    mb1_ref[0, 0] = jnp.broadcast_to(mb, (SUBLANE, LANE))

    go = min(nc, MLSTM_GROUP)

    def body_out(i, carry):
        for j in range(go):
            c = i * go + j
            r = pl.multiple_of(c * L, L)
            rh = pl.multiple_of(c * dh, dh)
            h = (hf_ref[pl.ds(rh, dh), :] + hb_ref[pl.ds(rh, dh), :]).T
            y_ref[0, pl.ds(r, L), :] = (_sigmoid(o_ref[0, pl.ds(r, L), :].astype(F32)) * h).astype(y_ref.dtype)
        return carry

    lax.fori_loop(0, nc // go, body_out, 0)


def _mlstm(z, zg, b_gate, cf0, mf0, cb0, mb0):
    B, T, _ = z.shape
    H, dh, L, aug = N_MLSTM, MLSTM_DH, MLSTM_CHUNK, MLSTM_AUG
    nc = T // L
    zcol = lambda off: (lambda b, h: (b, 0, off // dh + h))
    st = lambda b, h: (b, h, 0, 0)
    c_spec = pl.BlockSpec((1, 1, aug, dh), st)
    m_spec = pl.BlockSpec((1, 1, SUBLANE, LANE), st)
    c_shape = jax.ShapeDtypeStruct((B, H, aug, dh), F32)
    m_shape = jax.ShapeDtypeStruct((B, H, SUBLANE, LANE), F32)
    return pl.pallas_call(
        _mlstm_kernel,
        grid=(B, H),
        in_specs=[
            pl.BlockSpec((1, T, dh), zcol(OFF_Q)),
            pl.BlockSpec((1, T, dh), zcol(OFF_K)),
            pl.BlockSpec((1, T, dh), zcol(OFF_V)),
            pl.BlockSpec((1, T, dh), zcol(OFF_O)),
            pl.BlockSpec((1, T, LANE), lambda b, h: (b, 0, 0)),
            pl.BlockSpec((1, LANE), lambda b, h: (0, 0)),
            c_spec, m_spec, c_spec, m_spec,
        ],
        out_specs=[pl.BlockSpec((1, T, dh), lambda b, h: (b, 0, h)), c_spec, m_spec, c_spec, m_spec],
        out_shape=[jax.ShapeDtypeStruct((B, T, MIX_MLSTM), BF16), c_shape, m_shape, c_shape, m_shape],
        scratch_shapes=[
            pltpu.VMEM((nc * dh, L), F32),
            pltpu.VMEM((nc * dh, L), F32),
            pltpu.VMEM((nc * aug, L), F32),
            pltpu.VMEM((nc * aug, L), F32),
            pltpu.VMEM((nc * aug, dh), F32),
            pltpu.VMEM((nc * aug, dh), F32),
            pltpu.VMEM((nc * SUBLANE, LANE), F32),
            pltpu.VMEM((nc * SUBLANE, LANE), F32),
        ],
        compiler_params=_params(2),
        name="mlstm",
    )(z, z, z, z, zg, b_gate, cf0, mf0, cb0, mb0)


def kernel(x, c, ctx, c_ctx, w_mod, b_mod, w_in, conv_qk, b_gates, pool_w, pool_scale, fourier_w,
           w_out, ln1_g, ln1_b, w_up, conv_ffn_w, conv_ffn_b, w_down, ln2_g, ln2_b):
    B, T, D = x.shape
    Tc = ctx.shape[1]
    tm_x = 512
    tm_c = Tc
    xc = ctx
    s_c = jax.nn.silu(c)
    s_ctx = jax.nn.silu(c_ctx)
    filt_x, filt_c = _pool_filters(True), _pool_filters(False)
    cnt_x, cnt_c = _pool_inv_count(T, True), _pool_inv_count(Tc, False)
    dft_x, dft_c = _dft_matrix(T), _dft_matrix(Tc)
    zero_c = jnp.zeros((B, N_MLSTM, MLSTM_AUG, MLSTM_DH), F32)
    zero_m = jnp.zeros((B, N_MLSTM, SUBLANE, LANE), F32)
    for l in range(DEPTH):
        last = l == DEPTH - 1
        mx = [m[:, None, :] for m in jnp.split(s_c @ w_mod[l] + b_mod[l], 6, axis=-1)]
        mc = [jnp.broadcast_to(m[None, None, :], (B, 1, D))
              for m in jnp.split(s_ctx @ w_mod[l] + b_mod[l], 6, axis=-1)]
        w_main = w_in[l][:, :OFF_G].astype(BF16)
        w_gate = jnp.pad(w_in[l][:, OFF_G:], ((0, 0), (0, LANE - N_GATES))).astype(BF16)
        b_gate = jnp.pad(b_gates[l], (0, LANE - N_GATES))[None]
        w_pool_bd = _block_diag(pool_w[l]).astype(BF16)
        s_pool = pool_scale[l][None]
        w_out_b = w_out[l].astype(BF16)
        w_up_b = w_up[l].astype(BF16)
        w_down_b = w_down[l].astype(BF16)
        g1, b1 = ln1_g[l][None], ln1_b[l][None]
        g2, b2 = ln2_g[l][None], ln2_b[l][None]
        bcv = conv_ffn_b[l][None]

        zc, zgc = _proj_in(xc, mc[0], mc[1], w_main, w_gate, conv_qk[l], tm_c)
        ymc, cf, mf, cb, mb = _mlstm(zc, zgc, b_gate, zero_c, zero_m, zero_c, zero_m)
        zx, zgx = _proj_in(x, mx[0], mx[1], w_main, w_gate, conv_qk[l], tm_x)
        ymx = _mlstm(zx, zgx, b_gate, cf, mf, cb, mb)[0]
        ypx = _pool(zx, filt_x, cnt_x, w_pool_bd, s_pool, True)
        yfx = _fourier(zx, _fourier_mix_weights(fourier_w[l], T), dft_x)
        x = _proj_out(ypx, yfx, ymx, x, mx[2], w_out_b, g1, b1, 2 * tm_x)
        x = _ffn(x, mx[3], mx[4], mx[5], w_up_b, conv_ffn_w[l], bcv, w_down_b, g2, b2, 2 * tm_x)
        if not last:
            ypc = _pool(zc, filt_c, cnt_c, w_pool_bd, s_pool, False)
            yfc = _fourier(zc, _fourier_mix_weights(fourier_w[l], Tc), dft_c)
            xc = _proj_out(ypc, yfc, ymc, xc, mc[2], w_out_b, g1, b1, tm_c)
            xc = _ffn(xc, mc[3], mc[4], mc[5], w_up_b, conv_ffn_w[l], bcv, w_down_b, g2, b2, tm_c)
    return x
```

```python
import functools
import math

import jax
import jax.numpy as jnp
import numpy as np
from jax import lax
from jax.experimental import pallas as pl
from jax.experimental.pallas import tpu as pltpu

F32 = jnp.float32
BF16 = jnp.bfloat16

D_MODEL = 1024
DEPTH = 4
GRID_W = 64
GRID_SHIFT = GRID_W.bit_length() - 1
POOL_WINDOWS = (2, 4, 8, 16)
N_POOL = len(POOL_WINDOWS)
MIX_POOL = D_MODEL // 4
POOL_DIM = MIX_POOL // N_POOL
N_FOUR = 4
MIX_FOUR = D_MODEL // 4
FOUR_DIM = MIX_FOUR // N_FOUR
N_MLSTM = 4
MIX_MLSTM = D_MODEL // 2
MLSTM_DH = MIX_MLSTM // N_MLSTM
MLSTM_CHUNK = 128
N_GATES = 2 * 2 * N_MLSTM
D_FF = int(math.ceil(8 * D_MODEL / 3 / 128)) * 128
OFF_FOUR = MIX_POOL
OFF_O = OFF_FOUR + MIX_FOUR
OFF_Q = OFF_O + MIX_MLSTM
OFF_K = OFF_Q + MIX_MLSTM
OFF_V = OFF_K + MIX_MLSTM
OFF_G = OFF_V + MIX_MLSTM
ALPHA = (2 * DEPTH) ** 0.25
LN_EPS = 1e-6
LOG2_E = math.log2(math.e)

LANE = 128
LANE_SHIFT = LANE.bit_length() - 1
SUBLANE = 8
BF16_ROWS = 16
VMEM_LIMIT = 56 * 1024 * 1024
FF_CHUNK = 256
PROJ_CHUNK = 512
POOL_BLOCK = 256
FOUR_MIRROR_BLOCK = 256
POOL_PAD = (max(POOL_WINDOWS) // 2) * GRID_W
NEG_BIG = -1e30
MLSTM_GROUP = 8
MLSTM_AUG = MLSTM_DH + BF16_ROWS


def _params(n_axes):
    return pltpu.CompilerParams(dimension_semantics=("arbitrary",) * n_axes, vmem_limit_bytes=VMEM_LIMIT)


def _ln_rows(x):
    mu = jnp.mean(x, axis=-1, keepdims=True)
    xc = x - mu
    var = jnp.mean(xc * xc, axis=-1, keepdims=True)
    return xc * lax.rsqrt(var + LN_EPS)


def _sigmoid(x):
    return 0.5 * jnp.tanh(0.5 * x) + 0.5


def _log2_sigmoid(x2):
    return jnp.minimum(x2, 0.0) - jnp.log2(1.0 + jnp.exp2(-jnp.abs(x2)))


def _split_hi_lo(x):
    hi = x.astype(BF16)
    return hi, (x - hi.astype(F32)).astype(BF16)


def _proj_in_kernel(x_ref, xp_ref, xn_ref, sh_ref, sc_ref, w_ref, wg_ref, wc_ref, z_ref, g_ref, h_ref):
    i = pl.program_id(1)
    tm = x_ref.shape[1]
    mod_scale = 1.0 + sc_ref[0]
    mod_shift = sh_ref[0]
    u = _ln_rows(x_ref[0]) * mod_scale + mod_shift
    u_prev = (_ln_rows(xp_ref[0]) * mod_scale + mod_shift) * (i > 0).astype(F32)
    u_next = (_ln_rows(xn_ref[0]) * mod_scale + mod_shift) * (i < pl.num_programs(1) - 1).astype(F32)
    ub = u.astype(BF16)
    u_ext = jnp.concatenate([u_prev, u, u_next], axis=0).astype(BF16)
    n = z_ref.shape[-1]
    for c0 in range(0, n, PROJ_CHUNK):
        c1 = min(c0 + PROJ_CHUNK, n)
        if OFF_Q <= c0 and c1 <= OFF_V:
            h_ref[...] = jnp.dot(u_ext, w_ref[:, c0:c1], preferred_element_type=F32)
            w = wc_ref[:, c0 - OFF_Q:c1 - OFF_Q]
            y = (h_ref[SUBLANE - 1:SUBLANE - 1 + tm, :] * w[0:1] + h_ref[SUBLANE:SUBLANE + tm, :] * w[1:2]
                 + h_ref[SUBLANE + 1:SUBLANE + 1 + tm, :] * w[2:3])
            scale = 1.0 if c0 < OFF_K else MLSTM_DH ** -0.5
            z_ref[0, :, c0:c1] = (y * _sigmoid(y) * scale).astype(z_ref.dtype)
        else:
            z_ref[0, :, c0:c1] = jnp.dot(ub, w_ref[:, c0:c1], preferred_element_type=F32).astype(z_ref.dtype)
    g_ref[0] = jnp.dot(ub, wg_ref[...], preferred_element_type=F32)


def _proj_in(x, shift, scale, w_main, w_gate, w_conv, tm):
    B, T, D = x.shape
    n = w_main.shape[1]
    assert OFF_Q % PROJ_CHUNK == 0 and OFF_K % PROJ_CHUNK == 0 and OFF_V % PROJ_CHUNK == 0
    nb = tm // SUBLANE
    last = T // SUBLANE - 1
    return pl.pallas_call(
        _proj_in_kernel,
        grid=(B, T // tm),
        in_specs=[
            pl.BlockSpec((1, tm, D), lambda b, i: (b, i, 0)),
            pl.BlockSpec((1, SUBLANE, D), lambda b, i: (b, jnp.maximum(i * nb - 1, 0), 0)),
            pl.BlockSpec((1, SUBLANE, D), lambda b, i: (b, jnp.minimum((i + 1) * nb, last), 0)),
            pl.BlockSpec((1, 1, D), lambda b, i: (b, 0, 0)),
            pl.BlockSpec((1, 1, D), lambda b, i: (b, 0, 0)),
            pl.BlockSpec((D, n), lambda b, i: (0, 0)),
            pl.BlockSpec((D, LANE), lambda b, i: (0, 0)),
            pl.BlockSpec((3, 2 * MIX_MLSTM), lambda b, i: (0, 0)),
        ],
        out_specs=[
            pl.BlockSpec((1, tm, n), lambda b, i: (b, i, 0)),
            pl.BlockSpec((1, tm, LANE), lambda b, i: (b, i, 0)),
        ],
        out_shape=[
            jax.ShapeDtypeStruct((B, T, n), BF16),
            jax.ShapeDtypeStruct((B, T, LANE), F32),
        ],
        scratch_shapes=[pltpu.VMEM((tm + 2 * SUBLANE, PROJ_CHUNK), F32)],
        compiler_params=_params(2),
        name="proj_in",
    )(x, x, x, shift, scale, w_main, w_gate, w_conv)


def _proj_out_kernel(yp_ref, yf_ref, ym_ref, x_ref, gate_ref, wp_ref, wf_ref, wm_ref, lg_ref, lb_ref, o_ref):
    f = (jnp.dot(yp_ref[0], wp_ref[...], preferred_element_type=F32)
         + jnp.dot(yf_ref[0], wf_ref[...], preferred_element_type=F32)
         + jnp.dot(ym_ref[0], wm_ref[...], preferred_element_type=F32))
    r = ALPHA * x_ref[0] + gate_ref[0] * f
    o_ref[0] = _ln_rows(r) * lg_ref[...] + lb_ref[...]


def _proj_out(yp, yf, ym, x, gate, w, ln_g, ln_b, tm):
    B, T, D = x.shape
    tok = lambda b, i: (b, i, 0)
    return pl.pallas_call(
        _proj_out_kernel,
        grid=(B, T // tm),
        in_specs=[
            pl.BlockSpec((1, tm, MIX_POOL), tok),
            pl.BlockSpec((1, tm, MIX_FOUR), tok),
            pl.BlockSpec((1, tm, MIX_MLSTM), tok),
            pl.BlockSpec((1, tm, D), tok),
            pl.BlockSpec((1, 1, D), lambda b, i: (b, 0, 0)),
            pl.BlockSpec((MIX_POOL, D), lambda b, i: (0, 0)),
            pl.BlockSpec((MIX_FOUR, D), lambda b, i: (OFF_FOUR // MIX_FOUR, 0)),
            pl.BlockSpec((MIX_MLSTM, D), lambda b, i: (OFF_O // MIX_MLSTM, 0)),
            pl.BlockSpec((1, D), lambda b, i: (0, 0)),
            pl.BlockSpec((1, D), lambda b, i: (0, 0)),
        ],
        out_specs=pl.BlockSpec((1, tm, D), tok),
        out_shape=jax.ShapeDtypeStruct((B, T, D), F32),
        compiler_params=_params(2),
        name="proj_out",
    )(yp, yf, ym, x, gate, w, w, w, ln_g, ln_b)


def _gelu_tanh_times(x, half_a):
    c = math.sqrt(2.0 / math.pi)
    inner = x * (c + (c * 0.044715) * (x * x))
    return (x * (1.0 + jnp.tanh(inner))) * half_a


def _ffn_kernel(x_ref, xp_ref, xn_ref, sh_ref, sc_ref, gate_ref, wup_ref, wc_ref, bc_ref, wdn_ref,
                lg_ref, lb_ref, o_ref, ha_ref, hg_ref, p_ref):
    i = pl.program_id(1)
    tm = x_ref.shape[1]
    x = x_ref[0]
    mod_scale = 1.0 + sc_ref[0]
    mod_shift = sh_ref[0]
    has_prev = (i > 0).astype(F32)
    has_next = (i < pl.num_programs(1) - 1).astype(F32)
    u_prev = (_ln_rows(xp_ref[0]) * mod_scale + mod_shift) * has_prev
    u_next = (_ln_rows(xn_ref[0]) * mod_scale + mod_shift) * has_next
    u_main = _ln_rows(x) * mod_scale + mod_shift
    u_ext = jnp.concatenate([u_prev, u_main, u_next], axis=0).astype(BF16)

    def conv(h_ref, cols):
        w = wc_ref[:, cols]
        return (h_ref[SUBLANE - 1:SUBLANE - 1 + tm, :] * w[0:1]
                + h_ref[SUBLANE:SUBLANE + tm, :] * w[1:2]
                + h_ref[SUBLANE + 1:SUBLANE + 1 + tm, :] * w[2:3]
                + bc_ref[:, cols])

    for c in range(D_FF // FF_CHUNK):
        ca = slice(c * FF_CHUNK, (c + 1) * FF_CHUNK)
        cg = slice(D_FF + c * FF_CHUNK, D_FF + (c + 1) * FF_CHUNK)
        ha_ref[...] = jnp.dot(u_ext, wup_ref[:, ca], preferred_element_type=F32)
        hg_ref[...] = jnp.dot(u_ext, wup_ref[:, cg], preferred_element_type=F32)
        p_ref[:, ca] = _gelu_tanh_times(conv(hg_ref, cg), conv(ha_ref, ca)).astype(BF16)

    f = jnp.dot(p_ref[...], wdn_ref[...], preferred_element_type=F32)
    r = ALPHA * x + gate_ref[0] * f
    o_ref[0] = _ln_rows(r) * lg_ref[...] + lb_ref[...]


def _ffn(x, shift, scale, gate, w_up, w_conv, b_conv, w_down, ln_g, ln_b, tm):
    B, T, D = x.shape
    nb = tm // SUBLANE
    last = T // SUBLANE - 1
    const = dict(pipeline_mode=pl.Buffered(1))
    half_value = jnp.concatenate([jnp.full((D_FF,), 0.5, F32), jnp.ones((D_FF,), F32)])
    w_conv = w_conv * half_value
    b_conv = b_conv * half_value
    return pl.pallas_call(
        _ffn_kernel,
        grid=(B, T // tm),
        in_specs=[
            pl.BlockSpec((1, tm, D), lambda b, i: (b, i, 0)),
            pl.BlockSpec((1, SUBLANE, D), lambda b, i: (b, jnp.maximum(i * nb - 1, 0), 0)),
            pl.BlockSpec((1, SUBLANE, D), lambda b, i: (b, jnp.minimum((i + 1) * nb, last), 0)),
            pl.BlockSpec((1, 1, D), lambda b, i: (b, 0, 0)),
            pl.BlockSpec((1, 1, D), lambda b, i: (b, 0, 0)),
            pl.BlockSpec((1, 1, D), lambda b, i: (b, 0, 0)),
            pl.BlockSpec((D, 2 * D_FF), lambda b, i: (0, 0), **const),
            pl.BlockSpec((3, 2 * D_FF), lambda b, i: (0, 0), **const),
            pl.BlockSpec((1, 2 * D_FF), lambda b, i: (0, 0), **const),
            pl.BlockSpec((D_FF, D), lambda b, i: (0, 0), **const),
            pl.BlockSpec((1, D), lambda b, i: (0, 0)),
            pl.BlockSpec((1, D), lambda b, i: (0, 0)),
        ],
        out_specs=pl.BlockSpec((1, tm, D), lambda b, i: (b, i, 0)),
        out_shape=jax.ShapeDtypeStruct((B, T, D), F32),
        scratch_shapes=[
            pltpu.VMEM((tm + 2 * SUBLANE, FF_CHUNK), F32),
            pltpu.VMEM((tm + 2 * SUBLANE, FF_CHUNK), F32),
            pltpu.VMEM((tm, D_FF), BF16),
        ],
        compiler_params=_params(2),
        name="conv_ffn",
    )(x, x, x, shift, scale, gate, w_up, w_conv, b_conv, w_down, ln_g, ln_b)


def _band_matrix(n, w):
    pos = np.arange(n)
    lo = np.maximum(pos - w // 2, 0)
    hi = np.minimum(pos + w // 2 - 1, n - 1)
    return ((pos[None, :] >= lo[:, None]) & (pos[None, :] <= hi[:, None])).astype(np.float32)


def _pool_filters(grid2d):
    mats = []
    for w in POOL_WINDOWS:
        if grid2d:
            mats.append(np.kron(np.eye(POOL_BLOCK // GRID_W, dtype=np.float32), _band_matrix(GRID_W, w)))
        else:
            mats.append(_band_matrix(POOL_BLOCK, w))
    return jnp.asarray(np.stack(mats), BF16)


def _window_count(idx, half, n):
    return jnp.minimum(idx + (half - 1), n - 1) - jnp.maximum(idx - half, 0) + 1


def _pool_inv_count(T, grid2d):
    pos = lax.broadcasted_iota(jnp.int32, (T, MIX_POOL), 0)
    group = jnp.right_shift(lax.broadcasted_iota(jnp.int32, (T, MIX_POOL), 1), GRID_SHIFT)
    half = jnp.left_shift(1, group)
    if grid2d:
        count = (_window_count(jnp.right_shift(pos, GRID_SHIFT), half, T // GRID_W)
                 * _window_count(jnp.bitwise_and(pos, GRID_W - 1), half, GRID_W))
    else:
        count = _window_count(pos, half, T)
    return 1.0 / count.astype(F32)


def _pool_kernel(z_ref, a_ref, n_ref, w_ref, s_ref, y_ref, acc_ref, *, grid2d):
    T = z_ref.shape[1]
    nblk = T // POOL_BLOCK
    lane = lax.broadcasted_iota(jnp.int32, (POOL_BLOCK, MIX_POOL), 1)
    group = jnp.right_shift(lane, GRID_SHIFT)

    def col_filter(blk):
        r0 = pl.multiple_of(blk * POOL_BLOCK, POOL_BLOCK)
        xb = z_ref[0, pl.ds(r0, POOL_BLOCK), :]
        out = jnp.dot(a_ref[0], xb, preferred_element_type=F32)
        for g in range(1, N_POOL):
            out = jnp.where(group == g, jnp.dot(a_ref[g], xb, preferred_element_type=F32), out)
        return r0, xb, out

    def finish(r0, xb, box):
        p = box * n_ref[pl.ds(r0, POOL_BLOCK), :] - xb.astype(F32)
        y = jnp.dot(p.astype(BF16), w_ref[...], preferred_element_type=F32) * s_ref[...]
        y_ref[0, pl.ds(r0, POOL_BLOCK), :] = y.astype(y_ref.dtype)

    if not grid2d:
        assert T == POOL_BLOCK
        finish(*col_filter(0))
        return

    zeros = jnp.zeros((POOL_PAD, MIX_POOL), F32)
    acc_ref[0:POOL_PAD, :] = zeros
    acc_ref[POOL_PAD + T:POOL_PAD + T + POOL_PAD, :] = zeros

    def phase1(blk, carry):
        r0, _, out = col_filter(blk)
        acc_ref[pl.ds(POOL_PAD + r0, POOL_BLOCK), :] = out
        return carry

    lax.fori_loop(0, nblk, phase1, 0)

    lane_t = lax.broadcasted_iota(jnp.int32, (POOL_BLOCK, LANE), 1)

    def phase2(blk, carry):
        r0 = pl.multiple_of(blk * POOL_BLOCK, POOL_BLOCK)
        base = POOL_PAD + r0

        def slab(k, tile):
            return acc_ref[pl.ds(base + k * GRID_W, POOL_BLOCK), tile * LANE:(tile + 1) * LANE]

        s2 = slab(-1, 0) + slab(0, 0)
        s4 = s2 + slab(-2, 0) + slab(1, 0)
        s8 = slab(-4, 1)
        for k in range(-3, 4):
            s8 = s8 + slab(k, 1)
        s16 = s8
        for k in list(range(-8, -4)) + list(range(4, 8)):
            s16 = s16 + slab(k, 1)
        box = jnp.concatenate([jnp.where(lane_t < POOL_DIM, s2, s4), jnp.where(lane_t < POOL_DIM, s8, s16)], axis=1)
        finish(r0, z_ref[0, pl.ds(r0, POOL_BLOCK), :], box)
        return carry

    lax.fori_loop(0, nblk, phase2, 0)


def _pool(z, filters, inv_count, w_bd, s_row, grid2d):
    B, T, _ = z.shape
    return pl.pallas_call(
        functools.partial(_pool_kernel, grid2d=grid2d),
        grid=(B,),
        in_specs=[
            pl.BlockSpec((1, T, MIX_POOL), lambda b: (b, 0, 0)),
            pl.BlockSpec((N_POOL, POOL_BLOCK, POOL_BLOCK), lambda b: (0, 0, 0)),
            pl.BlockSpec((T, MIX_POOL), lambda b: (0, 0)),
            pl.BlockSpec((MIX_POOL, MIX_POOL), lambda b: (0, 0)),
            pl.BlockSpec((1, MIX_POOL), lambda b: (0, 0)),
        ],
        out_specs=pl.BlockSpec((1, T, MIX_POOL), lambda b: (b, 0, 0)),
        out_shape=jax.ShapeDtypeStruct((B, T, MIX_POOL), BF16),
        scratch_shapes=[pltpu.VMEM((T + 2 * POOL_PAD, MIX_POOL), F32)],
        compiler_params=_params(1),
        name="pool2d" if grid2d else "pool1d",
    )(z, filters, inv_count, w_bd, s_row)


def _block_diag(w):
    G, n, m = w.shape
    eye = jnp.eye(G, dtype=w.dtype)
    return (eye[:, None, :, None] * w[:, :, None, :]).reshape(G * n, G * m)


def _four_mix_kernel(z_ref, za_ref, zb_ref, zm_ref, g_ref, y_ref):
    tm = z_ref.shape[1]
    sb = zb_ref.shape[1]
    nsub = tm // sb
    gr, gi = g_ref[:, :MIX_FOUR], g_ref[:, MIX_FOUR:]
    r = lax.broadcasted_iota(jnp.int32, (sb, 2 * sb), 0)
    c = lax.broadcasted_iota(jnp.int32, (sb, 2 * sb), 1)
    perm = jnp.where(((r >= 1) & (c == sb - r)) | ((r == 0) & (c == sb)), 1.0, 0.0).astype(BF16)
    mid = jnp.dot(zm_ref[0], gr, preferred_element_type=F32)[0:1]
    row0 = lax.broadcasted_iota(jnp.int32, (sb, MIX_FOUR), 0) == 0
    for j in range(nsub):
        lo = (nsub - 1 - j) * sb
        if j == 0:
            src = jnp.concatenate([za_ref[0, lo:lo + sb, :], zb_ref[0]], axis=0)
        else:
            src = za_ref[0, lo:lo + 2 * sb, :]
        xr = jnp.dot(perm, src, preferred_element_type=F32)
        x = z_ref[0, j * sb:(j + 1) * sb, :].astype(F32)
        even = jnp.dot((x + xr).astype(BF16), gr, preferred_element_type=F32)
        odd = jnp.dot((x - xr).astype(BF16), gi, preferred_element_type=F32)
        if j == 0:
            odd = jnp.where(row0 & (pl.program_id(1) == 0), mid, odd)
        y_ref[0, 0, j * sb:(j + 1) * sb, :] = even.astype(y_ref.dtype)
        y_ref[0, 1, j * sb:(j + 1) * sb, :] = odd.astype(y_ref.dtype)


def _four_dft_kernel(d_ref, y_ref, o_ref):
    rows = 256
    for r0 in range(0, o_ref.shape[1], rows):
        o_ref[0, r0:r0 + rows, :] = jnp.dot(d_ref[r0:r0 + rows, :], y_ref[0],
                                            preferred_element_type=F32).astype(o_ref.dtype)


def _fourier(z, g_mix, dft):
    B, T, _ = z.shape
    M = T // 2
    tm = min(M, 2048)
    fcol = OFF_FOUR // MIX_FOUR
    sb = min(tm, FOUR_MIRROR_BLOCK)
    nt, nsb, per = T // tm, T // sb, tm // sb
    y = pl.pallas_call(
        _four_mix_kernel,
        grid=(B, M // tm),
        in_specs=[
            pl.BlockSpec((1, tm, MIX_FOUR), lambda b, i: (b, i, fcol)),
            pl.BlockSpec((1, tm, MIX_FOUR), lambda b, i: (b, nt - 1 - i, fcol)),
            pl.BlockSpec((1, sb, MIX_FOUR), lambda b, i: (b, (nsb - i * per) % nsb, fcol)),
            pl.BlockSpec((1, BF16_ROWS, MIX_FOUR), lambda b, i: (b, M // BF16_ROWS, fcol)),
            pl.BlockSpec((MIX_FOUR, 2 * MIX_FOUR), lambda b, i: (0, 0)),
        ],
        out_specs=pl.BlockSpec((1, 2, tm, MIX_FOUR), lambda b, i: (b, 0, i, 0)),
        out_shape=jax.ShapeDtypeStruct((B, 2, M, MIX_FOUR), BF16),
        compiler_params=_params(2),
        name="fourier_mix",
    )(z, z, z, z, g_mix)
    y = y.reshape(B, T, MIX_FOUR)
    tf = min(T, 2048)
    return pl.pallas_call(
        _four_dft_kernel,
        grid=(T // tf, B),
        in_specs=[
            pl.BlockSpec((tf, T), lambda m, b: (m, 0)),
            pl.BlockSpec((1, T, MIX_FOUR), lambda m, b: (b, 0, 0)),
        ],
        out_specs=pl.BlockSpec((1, tf, MIX_FOUR), lambda m, b: (b, m, 0)),
        out_shape=jax.ShapeDtypeStruct((B, T, MIX_FOUR), BF16),
        compiler_params=_params(2),
        name="fourier_dft",
    )(dft, y)


def _dft_matrix(n):
    m = n // 2
    a = m // GRID_W
    f = lax.broadcasted_iota(jnp.int32, (n, 1), 0)
    ang1 = ((f * GRID_W * lax.broadcasted_iota(jnp.int32, (1, a), 1)) % n).astype(F32) * (2.0 * math.pi / n)
    ang2 = ((f * lax.broadcasted_iota(jnp.int32, (1, GRID_W), 1)) % n).astype(F32) * (2.0 * math.pi / n)
    c1, s1 = jnp.cos(ang1)[:, :, None], jnp.sin(ang1)[:, :, None]
    c2, s2 = jnp.cos(ang2)[:, None, :], jnp.sin(ang2)[:, None, :]
    cos = (c1 * c2 - s1 * s2).reshape(n, m)
    sin = (s1 * c2 + c1 * s2).reshape(n, m)
    t0 = lax.broadcasted_iota(jnp.int32, (n, m), 1) == 0
    sign = (1 - 2 * (f % 2)).astype(F32)
    return jnp.concatenate([jnp.where(t0, 0.5, cos), jnp.where(t0, sign, sin)], axis=1).astype(BF16)


def _fourier_mix_weights(w_four, T):
    c = np.arange(FOUR_DIM)
    ang = 2.0 * np.pi * np.outer(c, c) / FOUR_DIM
    norm = 1.0 / math.sqrt(T * FOUR_DIM)
    gr = jnp.einsum('cf,gfe->gce', jnp.asarray(np.cos(ang) * norm, F32), w_four, precision='highest')
    gi = jnp.einsum('cf,gfe->gce', jnp.asarray(-np.sin(ang) * norm, F32), w_four, precision='highest')
    return jnp.concatenate([_block_diag(gr), _block_diag(gi)], axis=1).astype(BF16)


def _mlstm_kernel(q_ref, k_ref, v_ref, o_ref, g_ref, bg_ref, cf0_ref, mf0_ref, cb0_ref, mb0_ref,
                  y_ref, cf1_ref, mf1_ref, cb1_ref, mb1_ref,
                  hf_ref, hb_ref, pf_ref, pb_ref, uf_ref, ub_ref, rf_ref, rb_ref):
    T = q_ref.shape[1]
    L = MLSTM_CHUNK
    dh = MLSTM_DH
    aug = MLSTM_AUG
    nc = T // L
    head = pl.program_id(1)

    row = lax.broadcasted_iota(jnp.int32, (L, L), 0)
    col = lax.broadcasted_iota(jnp.int32, (L, L), 1)
    masks = (row <= col, row >= col)
    row2 = lax.broadcasted_iota(jnp.int32, (L, 2 * L), 0)
    col2 = jnp.bitwise_and(lax.broadcasted_iota(jnp.int32, (L, 2 * L), 1), L - 1)
    tri2 = jnp.where(col2 <= row2, 1.0, 0.0).astype(BF16)
    diag = row == col
    sub8 = lax.broadcasted_iota(jnp.int32, (SUBLANE, L), 0)
    ones_t = jnp.ones((aug - dh, L), F32)
    sel_row = jnp.bitwise_and(lax.broadcasted_iota(jnp.int32, (2 * LANE, 2 * LANE), 0), LANE - 1)
    sel_dir = jnp.right_shift(lax.broadcasted_iota(jnp.int32, (2 * LANE, 2 * LANE), 1), LANE_SHIFT)
    sel_i = jnp.where(sel_row == sel_dir * (2 * N_MLSTM) + head, 1.0, 0.0).astype(BF16)
    sel_f = jnp.where(sel_row == sel_dir * (2 * N_MLSTM) + N_MLSTM + head, 1.0, 0.0).astype(BF16)
    p_refs, u_refs, r_refs, h_refs = (pf_ref, pb_ref), (uf_ref, ub_ref), (rf_ref, rb_ref), (hf_ref, hb_ref)
    bias = bg_ref[...]

    def prep_group(c0):
        cs = [c0 + j for j in range(min(MLSTM_GROUP, nc))]
        rs = [pl.multiple_of(c * L, L) for c in cs]
        q = [q_ref[0, pl.ds(r, L), :] for r in rs]
        k = [k_ref[0, pl.ds(r, L), :] for r in rs]
        vt = [jnp.concatenate([v_ref[0, pl.ds(r, L), :].astype(F32).T, ones_t], axis=0).astype(BF16) for r in rs]
        g = [(g_ref[0, pl.ds(r, L), :] + bias) * LOG2_E for r in rs]
        i_rep = [jnp.dot(jnp.concatenate(_split_hi_lo(gg), axis=1), sel_i, preferred_element_type=F32)
                 for gg in g]
        lf_rep = [jnp.dot(jnp.concatenate(_split_hi_lo(_log2_sigmoid(gg)), axis=1), sel_f,
                          preferred_element_type=F32) for gg in g]
        st = [lax.dot_general(kk, qq, (((1,), (1,)), ((), ())), preferred_element_type=F32)
              for kk, qq in zip(k, q)]
        cum = [jnp.dot(tri2, jnp.concatenate(_split_hi_lo(f), axis=0), preferred_element_type=F32)
               for f in lf_rep]
        items = [(j, d) for j in range(len(cs)) for d in range(2)]
        for j, d in items:
            if d == 0:
                b = cum[j][:, :L]
                total = b[L - 1:L, :]
            else:
                total = cum[j][L - 1:L, L:]
                b = total - cum[j][:, L:] + lf_rep[j][:, L:]
            r_rep = i_rep[j][:, d * L:(d + 1) * L] - b
            dmt = jnp.where(masks[d], r_rep, NEG_BIG)
            pm = jnp.max(dmt, axis=0, keepdims=True)
            sg = (st[j] * jnp.exp2(dmt - pm)).astype(BF16)
            b_row = jnp.sum(jnp.where(diag, b, 0.0), axis=0, keepdims=True)
            g_end = total + r_rep
            g_loc = jnp.max(g_end, axis=0, keepdims=True)
            kw = (k[j].astype(F32) * jnp.exp2(g_end - g_loc)).astype(BF16)
            base = pl.multiple_of(cs[j] * aug, BF16_ROWS)
            pu = jnp.dot(vt[j], jnp.concatenate([sg, kw], axis=1), preferred_element_type=F32)
            p_refs[d][pl.ds(base, aug), :] = pu[:, :L]
            u_refs[d][pl.ds(base, aug), :] = pu[:, L:]
            rbase = pl.multiple_of(cs[j] * SUBLANE, SUBLANE)
            r_refs[d][pl.ds(rbase, SUBLANE), :] = jnp.where(
                sub8 == 0, b_row, jnp.where(sub8 == 1, b_row + pm, jnp.where(sub8 == 2, total, g_loc)))

    ga = min(MLSTM_GROUP, nc)

    def body_a(i, carry):
        prep_group(i * ga)
        return carry

    lax.fori_loop(0, nc // ga, body_a, 0)

    def step(d, c, ct, m):
        base = pl.multiple_of(c * aug, BF16_ROWS)
        rows = r_refs[d][pl.ds(pl.multiple_of(c * SUBLANE, SUBLANE), SUBLANE), :]
        b_row, a_row, total, g_loc = rows[0:1], rows[1:2], rows[2:3], rows[3:4]
        inter = b_row + m
        m_j = jnp.maximum(inter, a_row)
        qct = lax.dot_general(ct.astype(BF16), q_ref[0, pl.ds(pl.multiple_of(c * L, L), L), :],
                              (((1,), (1,)), ((), ())), preferred_element_type=F32)
        nd = jnp.exp2(inter - m_j) * qct + jnp.exp2(a_row - m_j) * p_refs[d][pl.ds(base, aug), :]
        den = jnp.maximum(jnp.abs(nd[dh:dh + 1]), jnp.exp2(-m_j))
        h_refs[d][pl.ds(pl.multiple_of(c * dh, dh), dh), :] = nd[:dh] / den
        m_new = jnp.maximum(total + m, g_loc)
        ct_new = jnp.exp2(total + m - m_new) * ct + jnp.exp2(g_loc - m_new) * u_refs[d][pl.ds(base, aug), :]
        return ct_new, m_new

    gb = min(nc, MLSTM_GROUP)

    def body_b(i, carry):
        cf, mf, cb, mb = carry
        for j in range(gb):
            cf, mf = step(0, i * gb + j, cf, mf)
            cb, mb = step(1, nc - 1 - (i * gb + j), cb, mb)
        return cf, mf, cb, mb

    cf, mf, cb, mb = lax.fori_loop(
        0, nc // gb, body_b, (cf0_ref[0, 0], mf0_ref[0, 0][0:1, :], cb0_ref[0, 0], mb0_ref[0, 0][0:1, :]))

    cf1_ref[0, 0] = cf
    cb1_ref[0, 0] = cb
    mf1_ref[0, 0] = jnp.broadcast_to(mf, (SUBLANE, LANE))
    mb1_ref[0, 0] = jnp.broadcast_to(mb, (SUBLANE, LANE))

    go = min(nc, MLSTM_GROUP)

    def body_out(i, carry):
        for j in range(go):
            c = i * go + j
            r = pl.multiple_of(c * L, L)
            rh = pl.multiple_of(c * dh, dh)
            h = (hf_ref[pl.ds(rh, dh), :] + hb_ref[pl.ds(rh, dh), :]).T
            y_ref[0, pl.ds(r, L), :] = (_sigmoid(o_ref[0, pl.ds(r, L), :].astype(F32)) * h).astype(y_ref.dtype)
        return carry

    lax.fori_loop(0, nc // go, body_out, 0)


def _mlstm(z, zg, b_gate, cf0, mf0, cb0, mb0):
    B, T, _ = z.shape
    H, dh, L, aug = N_MLSTM, MLSTM_DH, MLSTM_CHUNK, MLSTM_AUG
    nc = T // L
    zcol = lambda off: (lambda b, h: (b, 0, off // dh + h))
    st = lambda b, h: (b, h, 0, 0)
    c_spec = pl.BlockSpec((1, 1, aug, dh), st)
    m_spec = pl.BlockSpec((1, 1, SUBLANE, LANE), st)
    c_shape = jax.ShapeDtypeStruct((B, H, aug, dh), F32)
    m_shape = jax.ShapeDtypeStruct((B, H, SUBLANE, LANE), F32)
    return pl.pallas_call(
        _mlstm_kernel,
        grid=(B, H),
        in_specs=[
            pl.BlockSpec((1, T, dh), zcol(OFF_Q)),
            pl.BlockSpec((1, T, dh), zcol(OFF_K)),
            pl.BlockSpec((1, T, dh), zcol(OFF_V)),
            pl.BlockSpec((1, T, dh), zcol(OFF_O)),
            pl.BlockSpec((1, T, LANE), lambda b, h: (b, 0, 0)),
            pl.BlockSpec((1, LANE), lambda b, h: (0, 0)),
            c_spec, m_spec, c_spec, m_spec,
        ],
        out_specs=[pl.BlockSpec((1, T, dh), lambda b, h: (b, 0, h)), c_spec, m_spec, c_spec, m_spec],
        out_shape=[jax.ShapeDtypeStruct((B, T, MIX_MLSTM), BF16), c_shape, m_shape, c_shape, m_shape],
        scratch_shapes=[
            pltpu.VMEM((nc * dh, L), F32),
            pltpu.VMEM((nc * dh, L), F32),
            pltpu.VMEM((nc * aug, L), F32),
            pltpu.VMEM((nc * aug, L), F32),
            pltpu.VMEM((nc * aug, dh), F32),
            pltpu.VMEM((nc * aug, dh), F32),
            pltpu.VMEM((nc * SUBLANE, LANE), F32),
            pltpu.VMEM((nc * SUBLANE, LANE), F32),
        ],
        compiler_params=_params(2),
        name="mlstm",
    )(z, z, z, z, zg, b_gate, cf0, mf0, cb0, mb0)


def kernel(x, c, ctx, c_ctx, w_mod, b_mod, w_in, conv_qk, b_gates, pool_w, pool_scale, fourier_w,
           w_out, ln1_g, ln1_b, w_up, conv_ffn_w, conv_ffn_b, w_down, ln2_g, ln2_b):
    B, T, D = x.shape
    Tc = ctx.shape[1]
    tm_x = min(T, 1024)
    tm_c = Tc
    xc = ctx
    s_c = jax.nn.silu(c)
    s_ctx = jax.nn.silu(c_ctx)
    filt_x, filt_c = _pool_filters(True), _pool_filters(False)
    cnt_x, cnt_c = _pool_inv_count(T, True), _pool_inv_count(Tc, False)
    dft_x, dft_c = _dft_matrix(T), _dft_matrix(Tc)
    zero_c = jnp.zeros((B, N_MLSTM, MLSTM_AUG, MLSTM_DH), F32)
    zero_m = jnp.zeros((B, N_MLSTM, SUBLANE, LANE), F32)
    for l in range(DEPTH):
        last = l == DEPTH - 1
        mx = [m[:, None, :] for m in jnp.split(s_c @ w_mod[l] + b_mod[l], 6, axis=-1)]
        mc = [jnp.broadcast_to(m[None, None, :], (B, 1, D))
              for m in jnp.split(s_ctx @ w_mod[l] + b_mod[l], 6, axis=-1)]
        w_main = w_in[l][:, :OFF_G].astype(BF16)
        w_gate = jnp.pad(w_in[l][:, OFF_G:], ((0, 0), (0, LANE - N_GATES))).astype(BF16)
        b_gate = jnp.pad(b_gates[l], (0, LANE - N_GATES))[None]
        w_pool_bd = _block_diag(pool_w[l]).astype(BF16)
        s_pool = pool_scale[l][None]
        w_out_b = w_out[l].astype(BF16)
        w_up_b = w_up[l].astype(BF16)
        w_down_b = w_down[l].astype(BF16)
        g1, b1 = ln1_g[l][None], ln1_b[l][None]
        g2, b2 = ln2_g[l][None], ln2_b[l][None]
        bcv = conv_ffn_b[l][None]

        zc, zgc = _proj_in(xc, mc[0], mc[1], w_main, w_gate, conv_qk[l], tm_c)
        ymc, cf, mf, cb, mb = _mlstm(zc, zgc, b_gate, zero_c, zero_m, zero_c, zero_m)
        zx, zgx = _proj_in(x, mx[0], mx[1], w_main, w_gate, conv_qk[l], tm_x)
        ymx = _mlstm(zx, zgx, b_gate, cf, mf, cb, mb)[0]
        ypx = _pool(zx, filt_x, cnt_x, w_pool_bd, s_pool, True)
        yfx = _fourier(zx, _fourier_mix_weights(fourier_w[l], T), dft_x)
        x = _proj_out(ypx, yfx, ymx, x, mx[2], w_out_b, g1, b1, min(T, 2 * tm_x))
        x = _ffn(x, mx[3], mx[4], mx[5], w_up_b, conv_ffn_w[l], bcv, w_down_b, g2, b2, tm_x)
        if not last:
            ypc = _pool(zc, filt_c, cnt_c, w_pool_bd, s_pool, False)
            yfc = _fourier(zc, _fourier_mix_weights(fourier_w[l], Tc), dft_c)
            xc = _proj_out(ypc, yfc, ymc, xc, mc[2], w_out_b, g1, b1, tm_c)
            xc = _ffn(xc, mc[3], mc[4], mc[5], w_up_b, conv_ffn_w[l], bcv, w_down_b, g2, b2, tm_c)
    return x
```

```python
import functools
import math

import jax
import jax.numpy as jnp
import numpy as np
from jax import lax
from jax.experimental import pallas as pl
from jax.experimental.pallas import tpu as pltpu

F32 = jnp.float32
BF16 = jnp.bfloat16

D_MODEL = 1024
DEPTH = 4
GRID_W = 64
GRID_SHIFT = GRID_W.bit_length() - 1
POOL_WINDOWS = (2, 4, 8, 16)
N_POOL = len(POOL_WINDOWS)
MIX_POOL = D_MODEL // 4
POOL_DIM = MIX_POOL // N_POOL
N_FOUR = 4
MIX_FOUR = D_MODEL // 4
FOUR_DIM = MIX_FOUR // N_FOUR
N_MLSTM = 4
MIX_MLSTM = D_MODEL // 2
MLSTM_DH = MIX_MLSTM // N_MLSTM
MLSTM_CHUNK = 128
N_GATES = 2 * 2 * N_MLSTM
D_FF = int(math.ceil(8 * D_MODEL / 3 / 128)) * 128
OFF_FOUR = MIX_POOL
OFF_O = OFF_FOUR + MIX_FOUR
OFF_Q = OFF_O + MIX_MLSTM
OFF_K = OFF_Q + MIX_MLSTM
OFF_V = OFF_K + MIX_MLSTM
OFF_G = OFF_V + MIX_MLSTM
ALPHA = (2 * DEPTH) ** 0.25
LN_EPS = 1e-6
LOG2_E = math.log2(math.e)

LANE = 128
LANE_SHIFT = LANE.bit_length() - 1
SUBLANE = 8
BF16_ROWS = 16
VMEM_LIMIT = 56 * 1024 * 1024
FF_CHUNK = 256
PROJ_CHUNK = 512
POOL_BLOCK = 256
FOUR_MIRROR_BLOCK = 256
POOL_PAD = (max(POOL_WINDOWS) // 2) * GRID_W
NEG_BIG = -1e30
MLSTM_GROUP = 8
MLSTM_AUG = MLSTM_DH + BF16_ROWS


def _params(n_axes):
    return pltpu.CompilerParams(dimension_semantics=("arbitrary",) * n_axes, vmem_limit_bytes=VMEM_LIMIT)


def _ln_rows(x):
    mu = jnp.mean(x, axis=-1, keepdims=True)
    xc = x - mu
    var = jnp.mean(xc * xc, axis=-1, keepdims=True)
    return xc * lax.rsqrt(var + LN_EPS)


def _sigmoid(x):
    return 0.5 * jnp.tanh(0.5 * x) + 0.5


def _log2_sigmoid(x2):
    return jnp.minimum(x2, 0.0) - jnp.log2(1.0 + jnp.exp2(-jnp.abs(x2)))


def _split_hi_lo(x):
    hi = x.astype(BF16)
    return hi, (x - hi.astype(F32)).astype(BF16)


def _mod_kernel(c_ref, w_ref, b_ref, o_ref):
    cond = c_ref[...]
    s = cond * _sigmoid(cond)
    o_ref[0] = jnp.dot(s, w_ref[0], preferred_element_type=F32, precision=lax.Precision.HIGHEST) + b_ref[0]


def _modulation(cond, w_mod, b_mod):
    R, D = cond.shape
    L, _, N = w_mod.shape
    tn = N // 4
    return pl.pallas_call(
        _mod_kernel,
        grid=(L, N // tn),
        in_specs=[
            pl.BlockSpec((R, D), lambda l, j: (0, 0)),
            pl.BlockSpec((1, D, tn), lambda l, j: (l, 0, j)),
            pl.BlockSpec((1, 1, tn), lambda l, j: (l, 0, j)),
        ],
        out_specs=pl.BlockSpec((1, R, tn), lambda l, j: (l, 0, j)),
        out_shape=jax.ShapeDtypeStruct((L, R, N), F32),
        compiler_params=_params(2),
        name="modulation",
    )(cond, w_mod, b_mod[:, None, :])


def _proj_in_kernel(x_ref, xp_ref, xn_ref, sh_ref, sc_ref, w_ref, wg_ref, wc_ref, z_ref, g_ref, h_ref):
    i = pl.program_id(1)
    tm = x_ref.shape[1]
    mod_scale = 1.0 + sc_ref[0]
    mod_shift = sh_ref[0]
    u = _ln_rows(x_ref[0]) * mod_scale + mod_shift
    u_prev = (_ln_rows(xp_ref[0]) * mod_scale + mod_shift) * (i > 0).astype(F32)
    u_next = (_ln_rows(xn_ref[0]) * mod_scale + mod_shift) * (i < pl.num_programs(1) - 1).astype(F32)
    ub = u.astype(BF16)
    u_ext = jnp.concatenate([u_prev, u, u_next], axis=0).astype(BF16)
    n = z_ref.shape[-1]
    for c0 in range(0, n, PROJ_CHUNK):
        c1 = min(c0 + PROJ_CHUNK, n)
        if OFF_Q <= c0 and c1 <= OFF_V:
            h_ref[...] = jnp.dot(u_ext, w_ref[:, c0:c1], preferred_element_type=F32)
            w = wc_ref[:, c0 - OFF_Q:c1 - OFF_Q]
            y = (h_ref[SUBLANE - 1:SUBLANE - 1 + tm, :] * w[0:1] + h_ref[SUBLANE:SUBLANE + tm, :] * w[1:2]
                 + h_ref[SUBLANE + 1:SUBLANE + 1 + tm, :] * w[2:3])
            scale = 1.0 if c0 < OFF_K else MLSTM_DH ** -0.5
            z_ref[0, :, c0:c1] = (y * _sigmoid(y) * scale).astype(z_ref.dtype)
        else:
            z_ref[0, :, c0:c1] = jnp.dot(ub, w_ref[:, c0:c1], preferred_element_type=F32).astype(z_ref.dtype)
    g_ref[0] = jnp.dot(ub, wg_ref[...], preferred_element_type=F32)


def _proj_in(x, shift, scale, w_main, w_gate, w_conv, tm):
    B, T, D = x.shape
    n = w_main.shape[1]
    assert OFF_Q % PROJ_CHUNK == 0 and OFF_K % PROJ_CHUNK == 0 and OFF_V % PROJ_CHUNK == 0
    nb = tm // SUBLANE
    last = T // SUBLANE - 1
    return pl.pallas_call(
        _proj_in_kernel,
        grid=(B, T // tm),
        in_specs=[
            pl.BlockSpec((1, tm, D), lambda b, i: (b, i, 0)),
            pl.BlockSpec((1, SUBLANE, D), lambda b, i: (b, jnp.maximum(i * nb - 1, 0), 0)),
            pl.BlockSpec((1, SUBLANE, D), lambda b, i: (b, jnp.minimum((i + 1) * nb, last), 0)),
            pl.BlockSpec((1, 1, D), lambda b, i: (b, 0, 0)),
            pl.BlockSpec((1, 1, D), lambda b, i: (b, 0, 0)),
            pl.BlockSpec((D, n), lambda b, i: (0, 0)),
            pl.BlockSpec((D, LANE), lambda b, i: (0, 0)),
            pl.BlockSpec((3, 2 * MIX_MLSTM), lambda b, i: (0, 0)),
        ],
        out_specs=[
            pl.BlockSpec((1, tm, n), lambda b, i: (b, i, 0)),
            pl.BlockSpec((1, tm, LANE), lambda b, i: (b, i, 0)),
        ],
        out_shape=[
            jax.ShapeDtypeStruct((B, T, n), BF16),
            jax.ShapeDtypeStruct((B, T, LANE), F32),
        ],
        scratch_shapes=[pltpu.VMEM((tm + 2 * SUBLANE, PROJ_CHUNK), F32)],
        compiler_params=_params(2),
        name="proj_in",
    )(x, x, x, shift, scale, w_main, w_gate, w_conv)


def _proj_out_kernel(yp_ref, yf_ref, ym_ref, x_ref, gate_ref, wp_ref, wf_ref, wm_ref, lg_ref, lb_ref, o_ref):
    f = (jnp.dot(yp_ref[0], wp_ref[...], preferred_element_type=F32)
         + jnp.dot(yf_ref[0], wf_ref[...], preferred_element_type=F32)
         + jnp.dot(ym_ref[0], wm_ref[...], preferred_element_type=F32))
    r = ALPHA * x_ref[0] + gate_ref[0] * f
    o_ref[0] = _ln_rows(r) * lg_ref[...] + lb_ref[...]


def _proj_out(yp, yf, ym, x, gate, w, ln_g, ln_b, tm):
    B, T, D = x.shape
    tok = lambda b, i: (b, i, 0)
    return pl.pallas_call(
        _proj_out_kernel,
        grid=(B, T // tm),
        in_specs=[
            pl.BlockSpec((1, tm, MIX_POOL), tok),
            pl.BlockSpec((1, tm, MIX_FOUR), tok),
            pl.BlockSpec((1, tm, MIX_MLSTM), tok),
            pl.BlockSpec((1, tm, D), tok),
            pl.BlockSpec((1, 1, D), lambda b, i: (b, 0, 0)),
            pl.BlockSpec((MIX_POOL, D), lambda b, i: (0, 0)),
            pl.BlockSpec((MIX_FOUR, D), lambda b, i: (OFF_FOUR // MIX_FOUR, 0)),
            pl.BlockSpec((MIX_MLSTM, D), lambda b, i: (OFF_O // MIX_MLSTM, 0)),
            pl.BlockSpec((1, D), lambda b, i: (0, 0)),
            pl.BlockSpec((1, D), lambda b, i: (0, 0)),
        ],
        out_specs=pl.BlockSpec((1, tm, D), tok),
        out_shape=jax.ShapeDtypeStruct((B, T, D), F32),
        compiler_params=_params(2),
        name="proj_out",
    )(yp, yf, ym, x, gate, w, w, w, ln_g, ln_b)


def _gelu_tanh_times(x, half_a):
    c = math.sqrt(2.0 / math.pi)
    inner = x * (c + (c * 0.044715) * (x * x))
    return (x * (1.0 + jnp.tanh(inner))) * half_a


def _ffn_kernel(x_ref, xp_ref, xn_ref, sh_ref, sc_ref, gate_ref, wup_ref, wc_ref, bc_ref, wdn_ref,
                lg_ref, lb_ref, o_ref, ha_ref, hg_ref, p_ref):
    i = pl.program_id(1)
    tm = x_ref.shape[1]
    x = x_ref[0]
    mod_scale = 1.0 + sc_ref[0]
    mod_shift = sh_ref[0]
    has_prev = (i > 0).astype(F32)
    has_next = (i < pl.num_programs(1) - 1).astype(F32)
    u_prev = (_ln_rows(xp_ref[0]) * mod_scale + mod_shift) * has_prev
    u_next = (_ln_rows(xn_ref[0]) * mod_scale + mod_shift) * has_next
    u_main = _ln_rows(x) * mod_scale + mod_shift
    u_ext = jnp.concatenate([u_prev, u_main, u_next], axis=0).astype(BF16)

    def conv(h_ref, cols):
        w = wc_ref[:, cols]
        return (h_ref[SUBLANE - 1:SUBLANE - 1 + tm, :] * w[0:1]
                + h_ref[SUBLANE:SUBLANE + tm, :] * w[1:2]
                + h_ref[SUBLANE + 1:SUBLANE + 1 + tm, :] * w[2:3]
                + bc_ref[:, cols])

    for c in range(D_FF // FF_CHUNK):
        ca = slice(c * FF_CHUNK, (c + 1) * FF_CHUNK)
        cg = slice(D_FF + c * FF_CHUNK, D_FF + (c + 1) * FF_CHUNK)
        ha_ref[...] = jnp.dot(u_ext, wup_ref[:, ca], preferred_element_type=F32)
        hg_ref[...] = jnp.dot(u_ext, wup_ref[:, cg], preferred_element_type=F32)
        p_ref[:, ca] = _gelu_tanh_times(conv(hg_ref, cg), conv(ha_ref, ca)).astype(BF16)

    f = jnp.dot(p_ref[...], wdn_ref[...], preferred_element_type=F32)
    r = ALPHA * x + gate_ref[0] * f
    o_ref[0] = _ln_rows(r) * lg_ref[...] + lb_ref[...]


def _ffn(x, shift, scale, gate, w_up, w_conv, b_conv, w_down, ln_g, ln_b, tm):
    B, T, D = x.shape
    nb = tm // SUBLANE
    last = T // SUBLANE - 1
    const = dict(pipeline_mode=pl.Buffered(1))
    half_value = jnp.concatenate([jnp.full((D_FF,), 0.5, F32), jnp.ones((D_FF,), F32)])
    w_conv = w_conv * half_value
    b_conv = b_conv * half_value
    return pl.pallas_call(
        _ffn_kernel,
        grid=(B, T // tm),
        in_specs=[
            pl.BlockSpec((1, tm, D), lambda b, i: (b, i, 0)),
            pl.BlockSpec((1, SUBLANE, D), lambda b, i: (b, jnp.maximum(i * nb - 1, 0), 0)),
            pl.BlockSpec((1, SUBLANE, D), lambda b, i: (b, jnp.minimum((i + 1) * nb, last), 0)),
            pl.BlockSpec((1, 1, D), lambda b, i: (b, 0, 0)),
            pl.BlockSpec((1, 1, D), lambda b, i: (b, 0, 0)),
            pl.BlockSpec((1, 1, D), lambda b, i: (b, 0, 0)),
            pl.BlockSpec((D, 2 * D_FF), lambda b, i: (0, 0), **const),
            pl.BlockSpec((3, 2 * D_FF), lambda b, i: (0, 0), **const),
            pl.BlockSpec((1, 2 * D_FF), lambda b, i: (0, 0), **const),
            pl.BlockSpec((D_FF, D), lambda b, i: (0, 0), **const),
            pl.BlockSpec((1, D), lambda b, i: (0, 0)),
            pl.BlockSpec((1, D), lambda b, i: (0, 0)),
        ],
        out_specs=pl.BlockSpec((1, tm, D), lambda b, i: (b, i, 0)),
        out_shape=jax.ShapeDtypeStruct((B, T, D), F32),
        scratch_shapes=[
            pltpu.VMEM((tm + 2 * SUBLANE, FF_CHUNK), F32),
            pltpu.VMEM((tm + 2 * SUBLANE, FF_CHUNK), F32),
            pltpu.VMEM((tm, D_FF), BF16),
        ],
        compiler_params=_params(2),
        name="conv_ffn",
    )(x, x, x, shift, scale, gate, w_up, w_conv, b_conv, w_down, ln_g, ln_b)


def _band_matrix(n, w):
    pos = np.arange(n)
    lo = np.maximum(pos - w // 2, 0)
    hi = np.minimum(pos + w // 2 - 1, n - 1)
    return ((pos[None, :] >= lo[:, None]) & (pos[None, :] <= hi[:, None])).astype(np.float32)


def _pool_filters(grid2d):
    mats = []
    for w in POOL_WINDOWS:
        if grid2d:
            mats.append(np.kron(np.eye(POOL_BLOCK // GRID_W, dtype=np.float32), _band_matrix(GRID_W, w)))
        else:
            mats.append(_band_matrix(POOL_BLOCK, w))
    return jnp.asarray(np.stack(mats), BF16)


def _window_count(idx, half, n):
    return jnp.minimum(idx + (half - 1), n - 1) - jnp.maximum(idx - half, 0) + 1


def _pool_inv_count(T, grid2d):
    pos = lax.broadcasted_iota(jnp.int32, (T, MIX_POOL), 0)
    group = jnp.right_shift(lax.broadcasted_iota(jnp.int32, (T, MIX_POOL), 1), GRID_SHIFT)
    half = jnp.left_shift(1, group)
    if grid2d:
        count = (_window_count(jnp.right_shift(pos, GRID_SHIFT), half, T // GRID_W)
                 * _window_count(jnp.bitwise_and(pos, GRID_W - 1), half, GRID_W))
    else:
        count = _window_count(pos, half, T)
    return 1.0 / count.astype(F32)


def _pool_kernel(z_ref, a_ref, n_ref, w_ref, s_ref, y_ref, acc_ref, *, grid2d):
    T = z_ref.shape[1]
    nblk = T // POOL_BLOCK
    lane = lax.broadcasted_iota(jnp.int32, (POOL_BLOCK, MIX_POOL), 1)
    group = jnp.right_shift(lane, GRID_SHIFT)

    def col_filter(blk):
        r0 = pl.multiple_of(blk * POOL_BLOCK, POOL_BLOCK)
        xb = z_ref[0, pl.ds(r0, POOL_BLOCK), :]
        out = jnp.dot(a_ref[0], xb, preferred_element_type=F32)
        for g in range(1, N_POOL):
            out = jnp.where(group == g, jnp.dot(a_ref[g], xb, preferred_element_type=F32), out)
        return r0, xb, out

    def finish(r0, xb, box):
        p = box * n_ref[pl.ds(r0, POOL_BLOCK), :] - xb.astype(F32)
        y = jnp.dot(p.astype(BF16), w_ref[...], preferred_element_type=F32) * s_ref[...]
        y_ref[0, pl.ds(r0, POOL_BLOCK), :] = y.astype(y_ref.dtype)

    if not grid2d:
        assert T == POOL_BLOCK
        finish(*col_filter(0))
        return

    zeros = jnp.zeros((POOL_PAD, MIX_POOL), F32)
    acc_ref[0:POOL_PAD, :] = zeros
    acc_ref[POOL_PAD + T:POOL_PAD + T + POOL_PAD, :] = zeros

    def phase1(blk, carry):
        r0, _, out = col_filter(blk)
        acc_ref[pl.ds(POOL_PAD + r0, POOL_BLOCK), :] = out
        return carry

    lax.fori_loop(0, nblk, phase1, 0)

    lane_t = lax.broadcasted_iota(jnp.int32, (POOL_BLOCK, LANE), 1)

    def phase2(blk, carry):
        r0 = pl.multiple_of(blk * POOL_BLOCK, POOL_BLOCK)
        base = POOL_PAD + r0

        def slab(k, tile):
            return acc_ref[pl.ds(base + k * GRID_W, POOL_BLOCK), tile * LANE:(tile + 1) * LANE]

        s2 = slab(-1, 0) + slab(0, 0)
        s4 = s2 + slab(-2, 0) + slab(1, 0)
        s8 = slab(-4, 1)
        for k in range(-3, 4):
            s8 = s8 + slab(k, 1)
        s16 = s8
        for k in list(range(-8, -4)) + list(range(4, 8)):
            s16 = s16 + slab(k, 1)
        box = jnp.concatenate([jnp.where(lane_t < POOL_DIM, s2, s4), jnp.where(lane_t < POOL_DIM, s8, s16)], axis=1)
        finish(r0, z_ref[0, pl.ds(r0, POOL_BLOCK), :], box)
        return carry

    lax.fori_loop(0, nblk, phase2, 0)


def _pool(z, filters, inv_count, w_bd, s_row, grid2d):
    B, T, _ = z.shape
    return pl.pallas_call(
        functools.partial(_pool_kernel, grid2d=grid2d),
        grid=(B,),
        in_specs=[
            pl.BlockSpec((1, T, MIX_POOL), lambda b: (b, 0, 0)),
            pl.BlockSpec((N_POOL, POOL_BLOCK, POOL_BLOCK), lambda b: (0, 0, 0)),
            pl.BlockSpec((T, MIX_POOL), lambda b: (0, 0)),
            pl.BlockSpec((MIX_POOL, MIX_POOL), lambda b: (0, 0)),
            pl.BlockSpec((1, MIX_POOL), lambda b: (0, 0)),
        ],
        out_specs=pl.BlockSpec((1, T, MIX_POOL), lambda b: (b, 0, 0)),
        out_shape=jax.ShapeDtypeStruct((B, T, MIX_POOL), BF16),
        scratch_shapes=[pltpu.VMEM((T + 2 * POOL_PAD, MIX_POOL), F32)],
        compiler_params=_params(1),
        name="pool2d" if grid2d else "pool1d",
    )(z, filters, inv_count, w_bd, s_row)


def _block_diag(w):
    G, n, m = w.shape
    eye = jnp.eye(G, dtype=w.dtype)
    return (eye[:, None, :, None] * w[:, :, None, :]).reshape(G * n, G * m)


def _four_mix_kernel(z_ref, za_ref, zb_ref, zm_ref, g_ref, y_ref):
    tm = z_ref.shape[1]
    sb = zb_ref.shape[1]
    nsub = tm // sb
    gr, gi = g_ref[:, :MIX_FOUR], g_ref[:, MIX_FOUR:]
    r = lax.broadcasted_iota(jnp.int32, (sb, 2 * sb), 0)
    c = lax.broadcasted_iota(jnp.int32, (sb, 2 * sb), 1)
    perm = jnp.where(((r >= 1) & (c == sb - r)) | ((r == 0) & (c == sb)), 1.0, 0.0).astype(BF16)
    mid = jnp.dot(zm_ref[0], gr, preferred_element_type=F32)[0:1]
    row0 = lax.broadcasted_iota(jnp.int32, (sb, MIX_FOUR), 0) == 0
    for j in range(nsub):
        lo = (nsub - 1 - j) * sb
        if j == 0:
            src = jnp.concatenate([za_ref[0, lo:lo + sb, :], zb_ref[0]], axis=0)
        else:
            src = za_ref[0, lo:lo + 2 * sb, :]
        xr = jnp.dot(perm, src, preferred_element_type=F32)
        x = z_ref[0, j * sb:(j + 1) * sb, :].astype(F32)
        even = jnp.dot((x + xr).astype(BF16), gr, preferred_element_type=F32)
        odd = jnp.dot((x - xr).astype(BF16), gi, preferred_element_type=F32)
        if j == 0:
            odd = jnp.where(row0 & (pl.program_id(1) == 0), mid, odd)
        y_ref[0, 0, j * sb:(j + 1) * sb, :] = even.astype(y_ref.dtype)
        y_ref[0, 1, j * sb:(j + 1) * sb, :] = odd.astype(y_ref.dtype)


def _four_dft_kernel(d_ref, y_ref, o_ref):
    rows = 256
    for r0 in range(0, o_ref.shape[1], rows):
        o_ref[0, r0:r0 + rows, :] = jnp.dot(d_ref[r0:r0 + rows, :], y_ref[0],
                                            preferred_element_type=F32).astype(o_ref.dtype)


def _fourier(z, g_mix, dft):
    B, T, _ = z.shape
    M = T // 2
    tm = min(M, 2048)
    fcol = OFF_FOUR // MIX_FOUR
    sb = min(tm, FOUR_MIRROR_BLOCK)
    nt, nsb, per = T // tm, T // sb, tm // sb
    y = pl.pallas_call(
        _four_mix_kernel,
        grid=(B, M // tm),
        in_specs=[
            pl.BlockSpec((1, tm, MIX_FOUR), lambda b, i: (b, i, fcol)),
            pl.BlockSpec((1, tm, MIX_FOUR), lambda b, i: (b, nt - 1 - i, fcol)),
            pl.BlockSpec((1, sb, MIX_FOUR), lambda b, i: (b, (nsb - i * per) % nsb, fcol)),
            pl.BlockSpec((1, BF16_ROWS, MIX_FOUR), lambda b, i: (b, M // BF16_ROWS, fcol)),
            pl.BlockSpec((MIX_FOUR, 2 * MIX_FOUR), lambda b, i: (0, 0)),
        ],
        out_specs=pl.BlockSpec((1, 2, tm, MIX_FOUR), lambda b, i: (b, 0, i, 0)),
        out_shape=jax.ShapeDtypeStruct((B, 2, M, MIX_FOUR), BF16),
        compiler_params=_params(2),
        name="fourier_mix",
    )(z, z, z, z, g_mix)
    y = y.reshape(B, T, MIX_FOUR)
    tf = min(T, 2048)
    return pl.pallas_call(
        _four_dft_kernel,
        grid=(T // tf, B),
        in_specs=[
            pl.BlockSpec((tf, T), lambda m, b: (m, 0)),
            pl.BlockSpec((1, T, MIX_FOUR), lambda m, b: (b, 0, 0)),
        ],
        out_specs=pl.BlockSpec((1, tf, MIX_FOUR), lambda m, b: (b, m, 0)),
        out_shape=jax.ShapeDtypeStruct((B, T, MIX_FOUR), BF16),
        compiler_params=_params(2),
        name="fourier_dft",
    )(dft, y)


def _dft_matrix(n):
    m = n // 2
    a = m // GRID_W
    f = lax.broadcasted_iota(jnp.int32, (n, 1), 0)
    ang1 = ((f * GRID_W * lax.broadcasted_iota(jnp.int32, (1, a), 1)) % n).astype(F32) * (2.0 * math.pi / n)
    ang2 = ((f * lax.broadcasted_iota(jnp.int32, (1, GRID_W), 1)) % n).astype(F32) * (2.0 * math.pi / n)
    c1, s1 = jnp.cos(ang1)[:, :, None], jnp.sin(ang1)[:, :, None]
    c2, s2 = jnp.cos(ang2)[:, None, :], jnp.sin(ang2)[:, None, :]
    cos = (c1 * c2 - s1 * s2).reshape(n, m)
    sin = (s1 * c2 + c1 * s2).reshape(n, m)
    t0 = lax.broadcasted_iota(jnp.int32, (n, m), 1) == 0
    sign = (1 - 2 * (f % 2)).astype(F32)
    return jnp.concatenate([jnp.where(t0, 0.5, cos), jnp.where(t0, sign, sin)], axis=1).astype(BF16)


def _fourier_mix_weights(w_four, T):
    c = np.arange(FOUR_DIM)
    ang = 2.0 * np.pi * np.outer(c, c) / FOUR_DIM
    norm = 1.0 / math.sqrt(T * FOUR_DIM)
    gr = jnp.einsum('cf,gfe->gce', jnp.asarray(np.cos(ang) * norm, F32), w_four, precision='highest')
    gi = jnp.einsum('cf,gfe->gce', jnp.asarray(-np.sin(ang) * norm, F32), w_four, precision='highest')
    return jnp.concatenate([_block_diag(gr), _block_diag(gi)], axis=1).astype(BF16)


def _mlstm_kernel(q_ref, k_ref, v_ref, o_ref, g_ref, bg_ref, cf0_ref, mf0_ref, cb0_ref, mb0_ref,
                  y_ref, cf1_ref, mf1_ref, cb1_ref, mb1_ref,
                  hf_ref, hb_ref, pf_ref, pb_ref, uf_ref, ub_ref, rf_ref, rb_ref):
    T = q_ref.shape[1]
    L = MLSTM_CHUNK
    dh = MLSTM_DH
    aug = MLSTM_AUG
    nc = T // L
    head = pl.program_id(1)

    row = lax.broadcasted_iota(jnp.int32, (L, L), 0)
    col = lax.broadcasted_iota(jnp.int32, (L, L), 1)
    masks = (row <= col, row >= col)
    row2 = lax.broadcasted_iota(jnp.int32, (L, 2 * L), 0)
    col2 = jnp.bitwise_and(lax.broadcasted_iota(jnp.int32, (L, 2 * L), 1), L - 1)
    tri2 = jnp.where(col2 <= row2, 1.0, 0.0).astype(BF16)
    diag = row == col
    sub8 = lax.broadcasted_iota(jnp.int32, (SUBLANE, L), 0)
    ones_t = jnp.ones((aug - dh, L), F32)
    sel_row = jnp.bitwise_and(lax.broadcasted_iota(jnp.int32, (2 * LANE, 2 * LANE), 0), LANE - 1)
    sel_dir = jnp.right_shift(lax.broadcasted_iota(jnp.int32, (2 * LANE, 2 * LANE), 1), LANE_SHIFT)
    sel_i = jnp.where(sel_row == sel_dir * (2 * N_MLSTM) + head, 1.0, 0.0).astype(BF16)
    sel_f = jnp.where(sel_row == sel_dir * (2 * N_MLSTM) + N_MLSTM + head, 1.0, 0.0).astype(BF16)
    p_refs, u_refs, r_refs, h_refs = (pf_ref, pb_ref), (uf_ref, ub_ref), (rf_ref, rb_ref), (hf_ref, hb_ref)
    bias = bg_ref[...]

    def prep_group(c0):
        cs = [c0 + j for j in range(min(MLSTM_GROUP, nc))]
        rs = [pl.multiple_of(c * L, L) for c in cs]
        q = [q_ref[0, pl.ds(r, L), :] for r in rs]
        k = [k_ref[0, pl.ds(r, L), :] for r in rs]
        vt = [jnp.concatenate([v_ref[0, pl.ds(r, L), :].astype(F32).T, ones_t], axis=0).astype(BF16) for r in rs]
        g = [(g_ref[0, pl.ds(r, L), :] + bias) * LOG2_E for r in rs]
        i_rep = [jnp.dot(jnp.concatenate(_split_hi_lo(gg), axis=1), sel_i, preferred_element_type=F32)
                 for gg in g]
        lf_rep = [jnp.dot(jnp.concatenate(_split_hi_lo(_log2_sigmoid(gg)), axis=1), sel_f,
                          preferred_element_type=F32) for gg in g]
        st = [lax.dot_general(kk, qq, (((1,), (1,)), ((), ())), preferred_element_type=F32)
              for kk, qq in zip(k, q)]
        cum = [jnp.dot(tri2, jnp.concatenate(_split_hi_lo(f), axis=0), preferred_element_type=F32)
               for f in lf_rep]
        items = [(j, d) for j in range(len(cs)) for d in range(2)]
        for j, d in items:
            if d == 0:
                b = cum[j][:, :L]
                total = b[L - 1:L, :]
            else:
                total = cum[j][L - 1:L, L:]
                b = total - cum[j][:, L:] + lf_rep[j][:, L:]
            r_rep = i_rep[j][:, d * L:(d + 1) * L] - b
            dmt = jnp.where(masks[d], r_rep, NEG_BIG)
            pm = jnp.max(dmt, axis=0, keepdims=True)
            sg = (st[j] * jnp.exp2(dmt - pm)).astype(BF16)
            b_row = jnp.sum(jnp.where(diag, b, 0.0), axis=0, keepdims=True)
            g_end = total + r_rep
            g_loc = jnp.max(g_end, axis=0, keepdims=True)
            kw = (k[j].astype(F32) * jnp.exp2(g_end - g_loc)).astype(BF16)
            base = pl.multiple_of(cs[j] * aug, BF16_ROWS)
            pu = jnp.dot(vt[j], jnp.concatenate([sg, kw], axis=1), preferred_element_type=F32)
            p_refs[d][pl.ds(base, aug), :] = pu[:, :L]
            u_refs[d][pl.ds(base, aug), :] = pu[:, L:]
            rbase = pl.multiple_of(cs[j] * SUBLANE, SUBLANE)
            r_refs[d][pl.ds(rbase, SUBLANE), :] = jnp.where(
                sub8 == 0, b_row, jnp.where(sub8 == 1, b_row + pm, jnp.where(sub8 == 2, total, g_loc)))

    ga = min(MLSTM_GROUP, nc)

    def body_a(i, carry):
        prep_group(i * ga)
        return carry

    lax.fori_loop(0, nc // ga, body_a, 0)

    def step(d, c, ct, m):
        base = pl.multiple_of(c * aug, BF16_ROWS)
        rows = r_refs[d][pl.ds(pl.multiple_of(c * SUBLANE, SUBLANE), SUBLANE), :]
        b_row, a_row, total, g_loc = rows[0:1], rows[1:2], rows[2:3], rows[3:4]
        inter = b_row + m
        m_j = jnp.maximum(inter, a_row)
        qct = lax.dot_general(ct.astype(BF16), q_ref[0, pl.ds(pl.multiple_of(c * L, L), L), :],
                              (((1,), (1,)), ((), ())), preferred_element_type=F32)
        nd = jnp.exp2(inter - m_j) * qct + jnp.exp2(a_row - m_j) * p_refs[d][pl.ds(base, aug), :]
        den = jnp.maximum(jnp.abs(nd[dh:dh + 1]), jnp.exp2(-m_j))
        h_refs[d][pl.ds(pl.multiple_of(c * dh, dh), dh), :] = nd[:dh] / den
        m_new = jnp.maximum(total + m, g_loc)
        ct_new = jnp.exp2(total + m - m_new) * ct + jnp.exp2(g_loc - m_new) * u_refs[d][pl.ds(base, aug), :]
        return ct_new, m_new

    gb = min(nc, MLSTM_GROUP)

    def body_b(i, carry):
        cf, mf, cb, mb = carry
        for j in range(gb):
            cf, mf = step(0, i * gb + j, cf, mf)
            cb, mb = step(1, nc - 1 - (i * gb + j), cb, mb)
        return cf, mf, cb, mb

    cf, mf, cb, mb = lax.fori_loop(
        0, nc // gb, body_b, (cf0_ref[0, 0], mf0_ref[0, 0][0:1, :], cb0_ref[0, 0], mb0_ref[0, 0][0:1, :]))

    cf1_ref[0, 0] = cf
    cb1_ref[0, 0] = cb
    mf1_ref[0, 0] = jnp.broadcast_to(mf, (SUBLANE, LANE))
    mb1_ref[0, 0] = jnp.broadcast_to(mb, (SUBLANE, LANE))

    go = min(nc, MLSTM_GROUP)

    def body_out(i, carry):
        for j in range(go):
            c = i * go + j
            r = pl.multiple_of(c * L, L)
            rh = pl.multiple_of(c * dh, dh)
            h = (hf_ref[pl.ds(rh, dh), :] + hb_ref[pl.ds(rh, dh), :]).T
            y_ref[0, pl.ds(r, L), :] = (_sigmoid(o_ref[0, pl.ds(r, L), :].astype(F32)) * h).astype(y_ref.dtype)
        return carry

    lax.fori_loop(0, nc // go, body_out, 0)


def _mlstm(z, zg, b_gate, cf0, mf0, cb0, mb0):
    B, T, _ = z.shape
    H, dh, L, aug = N_MLSTM, MLSTM_DH, MLSTM_CHUNK, MLSTM_AUG
    nc = T // L
    zcol = lambda off: (lambda b, h: (b, 0, off // dh + h))
    st = lambda b, h: (b, h, 0, 0)
    c_spec = pl.BlockSpec((1, 1, aug, dh), st)
    m_spec = pl.BlockSpec((1, 1, SUBLANE, LANE), st)
    c_shape = jax.ShapeDtypeStruct((B, H, aug, dh), F32)
    m_shape = jax.ShapeDtypeStruct((B, H, SUBLANE, LANE), F32)
    return pl.pallas_call(
        _mlstm_kernel,
        grid=(B, H),
        in_specs=[
            pl.BlockSpec((1, T, dh), zcol(OFF_Q)),
            pl.BlockSpec((1, T, dh), zcol(OFF_K)),
            pl.BlockSpec((1, T, dh), zcol(OFF_V)),
            pl.BlockSpec((1, T, dh), zcol(OFF_O)),
            pl.BlockSpec((1, T, LANE), lambda b, h: (b, 0, 0)),
            pl.BlockSpec((1, LANE), lambda b, h: (0, 0)),
            c_spec, m_spec, c_spec, m_spec,
        ],
        out_specs=[pl.BlockSpec((1, T, dh), lambda b, h: (b, 0, h)), c_spec, m_spec, c_spec, m_spec],
        out_shape=[jax.ShapeDtypeStruct((B, T, MIX_MLSTM), BF16), c_shape, m_shape, c_shape, m_shape],
        scratch_shapes=[
            pltpu.VMEM((nc * dh, L), F32),
            pltpu.VMEM((nc * dh, L), F32),
            pltpu.VMEM((nc * aug, L), F32),
            pltpu.VMEM((nc * aug, L), F32),
            pltpu.VMEM((nc * aug, dh), F32),
            pltpu.VMEM((nc * aug, dh), F32),
            pltpu.VMEM((nc * SUBLANE, LANE), F32),
            pltpu.VMEM((nc * SUBLANE, LANE), F32),
        ],
        compiler_params=_params(2),
        name="mlstm",
    )(z, z, z, z, zg, b_gate, cf0, mf0, cb0, mb0)


def kernel(x, c, ctx, c_ctx, w_mod, b_mod, w_in, conv_qk, b_gates, pool_w, pool_scale, fourier_w,
           w_out, ln1_g, ln1_b, w_up, conv_ffn_w, conv_ffn_b, w_down, ln2_g, ln2_b):
    B, T, D = x.shape
    Tc = ctx.shape[1]
    tm_x = min(T, 1024)
    tm_c = Tc
    xc = ctx
    cond = jnp.concatenate([c, c_ctx[None], jnp.zeros((-(B + 1) % SUBLANE, D), F32)], axis=0)
    mods = _modulation(cond, w_mod, b_mod)
    filt_x, filt_c = _pool_filters(True), _pool_filters(False)
    cnt_x, cnt_c = _pool_inv_count(T, True), _pool_inv_count(Tc, False)
    dft_x, dft_c = _dft_matrix(T), _dft_matrix(Tc)
    zero_c = jnp.zeros((B, N_MLSTM, MLSTM_AUG, MLSTM_DH), F32)
    zero_m = jnp.zeros((B, N_MLSTM, SUBLANE, LANE), F32)
    for l in range(DEPTH):
        last = l == DEPTH - 1
        mx = [m[:, None, :] for m in jnp.split(mods[l, :B], 6, axis=-1)]
        mc = [jnp.broadcast_to(m[None, None, :], (B, 1, D)) for m in jnp.split(mods[l, B], 6, axis=-1)]
        w_main = w_in[l][:, :OFF_G].astype(BF16)
        w_gate = jnp.pad(w_in[l][:, OFF_G:], ((0, 0), (0, LANE - N_GATES))).astype(BF16)
        b_gate = jnp.pad(b_gates[l], (0, LANE - N_GATES))[None]
        w_pool_bd = _block_diag(pool_w[l]).astype(BF16)
        s_pool = pool_scale[l][None]
        w_out_b = w_out[l].astype(BF16)
        w_up_b = w_up[l].astype(BF16)
        w_down_b = w_down[l].astype(BF16)
        g1, b1 = ln1_g[l][None], ln1_b[l][None]
        g2, b2 = ln2_g[l][None], ln2_b[l][None]
        bcv = conv_ffn_b[l][None]

        zc, zgc = _proj_in(xc, mc[0], mc[1], w_main, w_gate, conv_qk[l], tm_c)
        ymc, cf, mf, cb, mb = _mlstm(zc, zgc, b_gate, zero_c, zero_m, zero_c, zero_m)
        zx, zgx = _proj_in(x, mx[0], mx[1], w_main, w_gate, conv_qk[l], tm_x)
        ymx = _mlstm(zx, zgx, b_gate, cf, mf, cb, mb)[0]
        ypx = _pool(zx, filt_x, cnt_x, w_pool_bd, s_pool, True)
        yfx = _fourier(zx, _fourier_mix_weights(fourier_w[l], T), dft_x)
        x = _proj_out(ypx, yfx, ymx, x, mx[2], w_out_b, g1, b1, min(T, 2 * tm_x))
        x = _ffn(x, mx[3], mx[4], mx[5], w_up_b, conv_ffn_w[l], bcv, w_down_b, g2, b2, tm_x)
        if not last:
            ypc = _pool(zc, filt_c, cnt_c, w_pool_bd, s_pool, False)
            yfc = _fourier(zc, _fourier_mix_weights(fourier_w[l], Tc), dft_c)
            xc = _proj_out(ypc, yfc, ymc, xc, mc[2], w_out_b, g1, b1, tm_c)
            xc = _ffn(xc, mc[3], mc[4], mc[5], w_up_b, conv_ffn_w[l], bcv, w_down_b, g2, b2, tm_c)
    return x
```

```python
import functools
import math

import jax
import jax.numpy as jnp
import numpy as np
from jax import lax
from jax.experimental import pallas as pl
from jax.experimental.pallas import tpu as pltpu

F32 = jnp.float32
BF16 = jnp.bfloat16

D_MODEL = 1024
DEPTH = 4
GRID_W = 64
GRID_SHIFT = GRID_W.bit_length() - 1
POOL_WINDOWS = (2, 4, 8, 16)
N_POOL = len(POOL_WINDOWS)
MIX_POOL = D_MODEL // 4
POOL_DIM = MIX_POOL // N_POOL
N_FOUR = 4
MIX_FOUR = D_MODEL // 4
FOUR_DIM = MIX_FOUR // N_FOUR
N_MLSTM = 4
MIX_MLSTM = D_MODEL // 2
MLSTM_DH = MIX_MLSTM // N_MLSTM
MLSTM_CHUNK = 128
N_GATES = 2 * 2 * N_MLSTM
D_FF = int(math.ceil(8 * D_MODEL / 3 / 128)) * 128
OFF_FOUR = MIX_POOL
OFF_O = OFF_FOUR + MIX_FOUR
OFF_Q = OFF_O + MIX_MLSTM
OFF_K = OFF_Q + MIX_MLSTM
OFF_V = OFF_K + MIX_MLSTM
OFF_G = OFF_V + MIX_MLSTM
ALPHA = (2 * DEPTH) ** 0.25
LN_EPS = 1e-6
LOG2_E = math.log2(math.e)

LANE = 128
LANE_SHIFT = LANE.bit_length() - 1
SUBLANE = 8
BF16_ROWS = 16
VMEM_LIMIT = 56 * 1024 * 1024
FF_CHUNK = 256
PROJ_CHUNK = 512
POOL_BLOCK = 256
POOL_UNROLL = 4
FOUR_MIRROR_BLOCK = 256
POOL_PAD = (max(POOL_WINDOWS) // 2) * GRID_W
NEG_BIG = -1e30
MLSTM_GROUP = 8
MLSTM_AUG = MLSTM_DH + BF16_ROWS


def _params(n_axes):
    return pltpu.CompilerParams(dimension_semantics=("arbitrary",) * n_axes, vmem_limit_bytes=VMEM_LIMIT)


def _ln_rows(x):
    mu = jnp.mean(x, axis=-1, keepdims=True)
    xc = x - mu
    var = jnp.mean(xc * xc, axis=-1, keepdims=True)
    return xc * lax.rsqrt(var + LN_EPS)


def _sigmoid(x):
    return 0.5 * jnp.tanh(0.5 * x) + 0.5


def _log2_sigmoid(x2):
    return jnp.minimum(x2, 0.0) - jnp.log2(1.0 + jnp.exp2(-jnp.abs(x2)))


def _split_hi_lo(x):
    hi = x.astype(BF16)
    return hi, (x - hi.astype(F32)).astype(BF16)


def _mod_kernel(c_ref, w_ref, b_ref, o_ref):
    cond = c_ref[...]
    s = cond * _sigmoid(cond)
    o_ref[0] = jnp.dot(s, w_ref[0], preferred_element_type=F32, precision=lax.Precision.HIGHEST) + b_ref[0]


def _modulation(cond, w_mod, b_mod):
    R, D = cond.shape
    L, _, N = w_mod.shape
    tn = N // 4
    return pl.pallas_call(
        _mod_kernel,
        grid=(L, N // tn),
        in_specs=[
            pl.BlockSpec((R, D), lambda l, j: (0, 0)),
            pl.BlockSpec((1, D, tn), lambda l, j: (l, 0, j)),
            pl.BlockSpec((1, 1, tn), lambda l, j: (l, 0, j)),
        ],
        out_specs=pl.BlockSpec((1, R, tn), lambda l, j: (l, 0, j)),
        out_shape=jax.ShapeDtypeStruct((L, R, N), F32),
        compiler_params=_params(2),
        name="modulation",
    )(cond, w_mod, b_mod[:, None, :])


def _proj_in_kernel(x_ref, xp_ref, xn_ref, sh_ref, sc_ref, w_ref, wg_ref, wc_ref, z_ref, g_ref, h_ref):
    i = pl.program_id(1)
    tm = x_ref.shape[1]
    mod_scale = 1.0 + sc_ref[0]
    mod_shift = sh_ref[0]
    u = _ln_rows(x_ref[0]) * mod_scale + mod_shift
    u_prev = (_ln_rows(xp_ref[0]) * mod_scale + mod_shift) * (i > 0).astype(F32)
    u_next = (_ln_rows(xn_ref[0]) * mod_scale + mod_shift) * (i < pl.num_programs(1) - 1).astype(F32)
    ub = u.astype(BF16)
    u_ext = jnp.concatenate([u_prev, u, u_next], axis=0).astype(BF16)
    n = z_ref.shape[-1]
    for c0 in range(0, n, PROJ_CHUNK):
        c1 = min(c0 + PROJ_CHUNK, n)
        if OFF_Q <= c0 and c1 <= OFF_V:
            h_ref[...] = jnp.dot(u_ext, w_ref[:, c0:c1], preferred_element_type=F32)
            w = wc_ref[:, c0 - OFF_Q:c1 - OFF_Q]
            y = (h_ref[SUBLANE - 1:SUBLANE - 1 + tm, :] * w[0:1] + h_ref[SUBLANE:SUBLANE + tm, :] * w[1:2]
                 + h_ref[SUBLANE + 1:SUBLANE + 1 + tm, :] * w[2:3])
            scale = 1.0 if c0 < OFF_K else MLSTM_DH ** -0.5
            z_ref[0, :, c0:c1] = (y * _sigmoid(y) * scale).astype(z_ref.dtype)
        else:
            z_ref[0, :, c0:c1] = jnp.dot(ub, w_ref[:, c0:c1], preferred_element_type=F32).astype(z_ref.dtype)
    g_ref[0] = jnp.dot(ub, wg_ref[...], preferred_element_type=F32)


def _proj_in(x, shift, scale, w_main, w_gate, w_conv, tm):
    B, T, D = x.shape
    n = w_main.shape[1]
    assert OFF_Q % PROJ_CHUNK == 0 and OFF_K % PROJ_CHUNK == 0 and OFF_V % PROJ_CHUNK == 0
    nb = tm // SUBLANE
    last = T // SUBLANE - 1
    return pl.pallas_call(
        _proj_in_kernel,
        grid=(B, T // tm),
        in_specs=[
            pl.BlockSpec((1, tm, D), lambda b, i: (b, i, 0)),
            pl.BlockSpec((1, SUBLANE, D), lambda b, i: (b, jnp.maximum(i * nb - 1, 0), 0)),
            pl.BlockSpec((1, SUBLANE, D), lambda b, i: (b, jnp.minimum((i + 1) * nb, last), 0)),
            pl.BlockSpec((1, 1, D), lambda b, i: (b, 0, 0)),
            pl.BlockSpec((1, 1, D), lambda b, i: (b, 0, 0)),
            pl.BlockSpec((D, n), lambda b, i: (0, 0)),
            pl.BlockSpec((D, LANE), lambda b, i: (0, 0)),
            pl.BlockSpec((3, 2 * MIX_MLSTM), lambda b, i: (0, 0)),
        ],
        out_specs=[
            pl.BlockSpec((1, tm, n), lambda b, i: (b, i, 0)),
            pl.BlockSpec((1, tm, LANE), lambda b, i: (b, i, 0)),
        ],
        out_shape=[
            jax.ShapeDtypeStruct((B, T, n), BF16),
            jax.ShapeDtypeStruct((B, T, LANE), F32),
        ],
        scratch_shapes=[pltpu.VMEM((tm + 2 * SUBLANE, PROJ_CHUNK), F32)],
        compiler_params=_params(2),
        name="proj_in",
    )(x, x, x, shift, scale, w_main, w_gate, w_conv)


def _proj_out_kernel(yp_ref, yf_ref, ym_ref, x_ref, gate_ref, wp_ref, wf_ref, wm_ref, lg_ref, lb_ref, o_ref):
    f = (jnp.dot(yp_ref[0], wp_ref[...], preferred_element_type=F32)
         + jnp.dot(yf_ref[0], wf_ref[...], preferred_element_type=F32)
         + jnp.dot(ym_ref[0], wm_ref[...], preferred_element_type=F32))
    r = ALPHA * x_ref[0] + gate_ref[0] * f
    o_ref[0] = _ln_rows(r) * lg_ref[...] + lb_ref[...]


def _proj_out(yp, yf, ym, x, gate, w, ln_g, ln_b, tm):
    B, T, D = x.shape
    tok = lambda b, i: (b, i, 0)
    return pl.pallas_call(
        _proj_out_kernel,
        grid=(B, T // tm),
        in_specs=[
            pl.BlockSpec((1, tm, MIX_POOL), tok),
            pl.BlockSpec((1, tm, MIX_FOUR), tok),
            pl.BlockSpec((1, tm, MIX_MLSTM), tok),
            pl.BlockSpec((1, tm, D), tok),
            pl.BlockSpec((1, 1, D), lambda b, i: (b, 0, 0)),
            pl.BlockSpec((MIX_POOL, D), lambda b, i: (0, 0)),
            pl.BlockSpec((MIX_FOUR, D), lambda b, i: (OFF_FOUR // MIX_FOUR, 0)),
            pl.BlockSpec((MIX_MLSTM, D), lambda b, i: (OFF_O // MIX_MLSTM, 0)),
            pl.BlockSpec((1, D), lambda b, i: (0, 0)),
            pl.BlockSpec((1, D), lambda b, i: (0, 0)),
        ],
        out_specs=pl.BlockSpec((1, tm, D), tok),
        out_shape=jax.ShapeDtypeStruct((B, T, D), F32),
        compiler_params=_params(2),
        name="proj_out",
    )(yp, yf, ym, x, gate, w, w, w, ln_g, ln_b)


def _gelu_tanh_times(x, half_a):
    c = math.sqrt(2.0 / math.pi)
    inner = x * (c + (c * 0.044715) * (x * x))
    return (x * (1.0 + jnp.tanh(inner))) * half_a


def _ffn_kernel(x_ref, xp_ref, xn_ref, sh_ref, sc_ref, gate_ref, wup_ref, wc_ref, bc_ref, wdn_ref,
                lg_ref, lb_ref, o_ref, ha_ref, hg_ref, p_ref):
    i = pl.program_id(1)
    tm = x_ref.shape[1]
    x = x_ref[0]
    mod_scale = 1.0 + sc_ref[0]
    mod_shift = sh_ref[0]
    has_prev = (i > 0).astype(F32)
    has_next = (i < pl.num_programs(1) - 1).astype(F32)
    u_prev = (_ln_rows(xp_ref[0]) * mod_scale + mod_shift) * has_prev
    u_next = (_ln_rows(xn_ref[0]) * mod_scale + mod_shift) * has_next
    u_main = _ln_rows(x) * mod_scale + mod_shift
    u_ext = jnp.concatenate([u_prev, u_main, u_next], axis=0).astype(BF16)

    def conv(h_ref, cols):
        w = wc_ref[:, cols]
        return (h_ref[SUBLANE - 1:SUBLANE - 1 + tm, :] * w[0:1]
                + h_ref[SUBLANE:SUBLANE + tm, :] * w[1:2]
                + h_ref[SUBLANE + 1:SUBLANE + 1 + tm, :] * w[2:3]
                + bc_ref[:, cols])

    for c in range(D_FF // FF_CHUNK):
        ca = slice(c * FF_CHUNK, (c + 1) * FF_CHUNK)
        cg = slice(D_FF + c * FF_CHUNK, D_FF + (c + 1) * FF_CHUNK)
        ha_ref[...] = jnp.dot(u_ext, wup_ref[:, ca], preferred_element_type=F32)
        hg_ref[...] = jnp.dot(u_ext, wup_ref[:, cg], preferred_element_type=F32)
        p_ref[:, ca] = _gelu_tanh_times(conv(hg_ref, cg), conv(ha_ref, ca)).astype(BF16)

    f = jnp.dot(p_ref[...], wdn_ref[...], preferred_element_type=F32)
    r = ALPHA * x + gate_ref[0] * f
    o_ref[0] = _ln_rows(r) * lg_ref[...] + lb_ref[...]


def _ffn(x, shift, scale, gate, w_up, w_conv, b_conv, w_down, ln_g, ln_b, tm):
    B, T, D = x.shape
    nb = tm // SUBLANE
    last = T // SUBLANE - 1
    const = dict(pipeline_mode=pl.Buffered(1))
    half_value = jnp.concatenate([jnp.full((D_FF,), 0.5, F32), jnp.ones((D_FF,), F32)])
    w_conv = w_conv * half_value
    b_conv = b_conv * half_value
    return pl.pallas_call(
        _ffn_kernel,
        grid=(B, T // tm),
        in_specs=[
            pl.BlockSpec((1, tm, D), lambda b, i: (b, i, 0)),
            pl.BlockSpec((1, SUBLANE, D), lambda b, i: (b, jnp.maximum(i * nb - 1, 0), 0)),
            pl.BlockSpec((1, SUBLANE, D), lambda b, i: (b, jnp.minimum((i + 1) * nb, last), 0)),
            pl.BlockSpec((1, 1, D), lambda b, i: (b, 0, 0)),
            pl.BlockSpec((1, 1, D), lambda b, i: (b, 0, 0)),
            pl.BlockSpec((1, 1, D), lambda b, i: (b, 0, 0)),
            pl.BlockSpec((D, 2 * D_FF), lambda b, i: (0, 0), **const),
            pl.BlockSpec((3, 2 * D_FF), lambda b, i: (0, 0), **const),
            pl.BlockSpec((1, 2 * D_FF), lambda b, i: (0, 0), **const),
            pl.BlockSpec((D_FF, D), lambda b, i: (0, 0), **const),
            pl.BlockSpec((1, D), lambda b, i: (0, 0)),
            pl.BlockSpec((1, D), lambda b, i: (0, 0)),
        ],
        out_specs=pl.BlockSpec((1, tm, D), lambda b, i: (b, i, 0)),
        out_shape=jax.ShapeDtypeStruct((B, T, D), F32),
        scratch_shapes=[
            pltpu.VMEM((tm + 2 * SUBLANE, FF_CHUNK), F32),
            pltpu.VMEM((tm + 2 * SUBLANE, FF_CHUNK), F32),
            pltpu.VMEM((tm, D_FF), BF16),
        ],
        compiler_params=_params(2),
        name="conv_ffn",
    )(x, x, x, shift, scale, gate, w_up, w_conv, b_conv, w_down, ln_g, ln_b)


def _band_matrix(n, w):
    pos = np.arange(n)
    lo = np.maximum(pos - w // 2, 0)
    hi = np.minimum(pos + w // 2 - 1, n - 1)
    return ((pos[None, :] >= lo[:, None]) & (pos[None, :] <= hi[:, None])).astype(np.float32)


def _pool_filters(grid2d):
    mats = []
    for w in POOL_WINDOWS:
        if grid2d:
            mats.append(np.kron(np.eye(POOL_BLOCK // GRID_W, dtype=np.float32), _band_matrix(GRID_W, w)))
        else:
            mats.append(_band_matrix(POOL_BLOCK, w))
    return jnp.asarray(np.stack(mats), BF16)


def _window_count(idx, half, n):
    return jnp.minimum(idx + (half - 1), n - 1) - jnp.maximum(idx - half, 0) + 1


def _pool_inv_count(T, grid2d):
    pos = lax.broadcasted_iota(jnp.int32, (T, MIX_POOL), 0)
    group = jnp.right_shift(lax.broadcasted_iota(jnp.int32, (T, MIX_POOL), 1), GRID_SHIFT)
    half = jnp.left_shift(1, group)
    if grid2d:
        count = (_window_count(jnp.right_shift(pos, GRID_SHIFT), half, T // GRID_W)
                 * _window_count(jnp.bitwise_and(pos, GRID_W - 1), half, GRID_W))
    else:
        count = _window_count(pos, half, T)
    return 1.0 / count.astype(F32)


def _pool_kernel(z_ref, a_ref, n_ref, w_ref, s_ref, y_ref, acc_ref, *, grid2d):
    T = z_ref.shape[1]
    nblk = T // POOL_BLOCK
    lane = lax.broadcasted_iota(jnp.int32, (POOL_BLOCK, MIX_POOL), 1)
    group = jnp.right_shift(lane, GRID_SHIFT)

    def col_filter(blk):
        r0 = pl.multiple_of(blk * POOL_BLOCK, POOL_BLOCK)
        xb = z_ref[0, pl.ds(r0, POOL_BLOCK), :]
        out = jnp.dot(a_ref[0], xb, preferred_element_type=F32)
        for g in range(1, N_POOL):
            out = jnp.where(group == g, jnp.dot(a_ref[g], xb, preferred_element_type=F32), out)
        return r0, xb, out

    def finish(r0, xb, box):
        p = box * n_ref[pl.ds(r0, POOL_BLOCK), :] - xb.astype(F32)
        y = jnp.dot(p.astype(BF16), w_ref[...], preferred_element_type=F32) * s_ref[...]
        y_ref[0, pl.ds(r0, POOL_BLOCK), :] = y.astype(y_ref.dtype)

    if not grid2d:
        assert T == POOL_BLOCK
        finish(*col_filter(0))
        return

    zeros = jnp.zeros((POOL_PAD, MIX_POOL), F32)
    acc_ref[0:POOL_PAD, :] = zeros
    acc_ref[POOL_PAD + T:POOL_PAD + T + POOL_PAD, :] = zeros

    def phase1(blk, carry):
        r0, _, out = col_filter(blk)
        acc_ref[pl.ds(POOL_PAD + r0, POOL_BLOCK), :] = out
        return carry

    lax.fori_loop(0, nblk, phase1, 0, unroll=POOL_UNROLL)

    lane_t = lax.broadcasted_iota(jnp.int32, (POOL_BLOCK, LANE), 1)

    def phase2(blk, carry):
        r0 = pl.multiple_of(blk * POOL_BLOCK, POOL_BLOCK)
        base = POOL_PAD + r0

        def slab(k, tile):
            return acc_ref[pl.ds(base + k * GRID_W, POOL_BLOCK), tile * LANE:(tile + 1) * LANE]

        s2 = slab(-1, 0) + slab(0, 0)
        s4 = s2 + slab(-2, 0) + slab(1, 0)
        s8 = slab(-4, 1)
        for k in range(-3, 4):
            s8 = s8 + slab(k, 1)
        s16 = s8
        for k in list(range(-8, -4)) + list(range(4, 8)):
            s16 = s16 + slab(k, 1)
        box = jnp.concatenate([jnp.where(lane_t < POOL_DIM, s2, s4), jnp.where(lane_t < POOL_DIM, s8, s16)], axis=1)
        finish(r0, z_ref[0, pl.ds(r0, POOL_BLOCK), :], box)
        return carry

    lax.fori_loop(0, nblk, phase2, 0, unroll=POOL_UNROLL)


def _pool(z, filters, inv_count, w_bd, s_row, grid2d):
    B, T, _ = z.shape
    return pl.pallas_call(
        functools.partial(_pool_kernel, grid2d=grid2d),
        grid=(B,),
        in_specs=[
            pl.BlockSpec((1, T, MIX_POOL), lambda b: (b, 0, 0)),
            pl.BlockSpec((N_POOL, POOL_BLOCK, POOL_BLOCK), lambda b: (0, 0, 0)),
            pl.BlockSpec((T, MIX_POOL), lambda b: (0, 0)),
            pl.BlockSpec((MIX_POOL, MIX_POOL), lambda b: (0, 0)),
            pl.BlockSpec((1, MIX_POOL), lambda b: (0, 0)),
        ],
        out_specs=pl.BlockSpec((1, T, MIX_POOL), lambda b: (b, 0, 0)),
        out_shape=jax.ShapeDtypeStruct((B, T, MIX_POOL), BF16),
        scratch_shapes=[pltpu.VMEM((T + 2 * POOL_PAD, MIX_POOL), F32)],
        compiler_params=_params(1),
        name="pool2d" if grid2d else "pool1d",
    )(z, filters, inv_count, w_bd, s_row)


def _block_diag(w):
    G, n, m = w.shape
    eye = jnp.eye(G, dtype=w.dtype)
    return (eye[:, None, :, None] * w[:, :, None, :]).reshape(G * n, G * m)


def _four_mix_kernel(z_ref, za_ref, zb_ref, zm_ref, g_ref, y_ref):
    tm = z_ref.shape[1]
    sb = zb_ref.shape[1]
    nsub = tm // sb
    gr, gi = g_ref[:, :MIX_FOUR], g_ref[:, MIX_FOUR:]
    r = lax.broadcasted_iota(jnp.int32, (sb, 2 * sb), 0)
    c = lax.broadcasted_iota(jnp.int32, (sb, 2 * sb), 1)
    perm = jnp.where(((r >= 1) & (c == sb - r)) | ((r == 0) & (c == sb)), 1.0, 0.0).astype(BF16)
    mid = jnp.dot(zm_ref[0], gr, preferred_element_type=F32)[0:1]
    row0 = lax.broadcasted_iota(jnp.int32, (sb, MIX_FOUR), 0) == 0
    for j in range(nsub):
        lo = (nsub - 1 - j) * sb
        if j == 0:
            src = jnp.concatenate([za_ref[0, lo:lo + sb, :], zb_ref[0]], axis=0)
        else:
            src = za_ref[0, lo:lo + 2 * sb, :]
        xr = jnp.dot(perm, src, preferred_element_type=F32)
        x = z_ref[0, j * sb:(j + 1) * sb, :].astype(F32)
        even = jnp.dot((x + xr).astype(BF16), gr, preferred_element_type=F32)
        odd = jnp.dot((x - xr).astype(BF16), gi, preferred_element_type=F32)
        if j == 0:
            odd = jnp.where(row0 & (pl.program_id(1) == 0), mid, odd)
        y_ref[0, 0, j * sb:(j + 1) * sb, :] = even.astype(y_ref.dtype)
        y_ref[0, 1, j * sb:(j + 1) * sb, :] = odd.astype(y_ref.dtype)


def _four_dft_kernel(d_ref, y_ref, o_ref):
    rows = 256
    for r0 in range(0, o_ref.shape[1], rows):
        o_ref[0, r0:r0 + rows, :] = jnp.dot(d_ref[r0:r0 + rows, :], y_ref[0],
                                            preferred_element_type=F32).astype(o_ref.dtype)


def _fourier(z, g_mix, dft):
    B, T, _ = z.shape
    M = T // 2
    tm = min(M, 2048)
    fcol = OFF_FOUR // MIX_FOUR
    sb = min(tm, FOUR_MIRROR_BLOCK)
    nt, nsb, per = T // tm, T // sb, tm // sb
    y = pl.pallas_call(
        _four_mix_kernel,
        grid=(B, M // tm),
        in_specs=[
            pl.BlockSpec((1, tm, MIX_FOUR), lambda b, i: (b, i, fcol)),
            pl.BlockSpec((1, tm, MIX_FOUR), lambda b, i: (b, nt - 1 - i, fcol)),
            pl.BlockSpec((1, sb, MIX_FOUR), lambda b, i: (b, (nsb - i * per) % nsb, fcol)),
            pl.BlockSpec((1, BF16_ROWS, MIX_FOUR), lambda b, i: (b, M // BF16_ROWS, fcol)),
            pl.BlockSpec((MIX_FOUR, 2 * MIX_FOUR), lambda b, i: (0, 0)),
        ],
        out_specs=pl.BlockSpec((1, 2, tm, MIX_FOUR), lambda b, i: (b, 0, i, 0)),
        out_shape=jax.ShapeDtypeStruct((B, 2, M, MIX_FOUR), BF16),
        compiler_params=_params(2),
        name="fourier_mix",
    )(z, z, z, z, g_mix)
    y = y.reshape(B, T, MIX_FOUR)
    tf = min(T, 2048)
    return pl.pallas_call(
        _four_dft_kernel,
        grid=(T // tf, B),
        in_specs=[
            pl.BlockSpec((tf, T), lambda m, b: (m, 0)),
            pl.BlockSpec((1, T, MIX_FOUR), lambda m, b: (b, 0, 0)),
        ],
        out_specs=pl.BlockSpec((1, tf, MIX_FOUR), lambda m, b: (b, m, 0)),
        out_shape=jax.ShapeDtypeStruct((B, T, MIX_FOUR), BF16),
        compiler_params=_params(2),
        name="fourier_dft",
    )(dft, y)


def _dft_matrix(n):
    m = n // 2
    a = m // GRID_W
    f = lax.broadcasted_iota(jnp.int32, (n, 1), 0)
    ang1 = ((f * GRID_W * lax.broadcasted_iota(jnp.int32, (1, a), 1)) % n).astype(F32) * (2.0 * math.pi / n)
    ang2 = ((f * lax.broadcasted_iota(jnp.int32, (1, GRID_W), 1)) % n).astype(F32) * (2.0 * math.pi / n)
    c1, s1 = jnp.cos(ang1)[:, :, None], jnp.sin(ang1)[:, :, None]
    c2, s2 = jnp.cos(ang2)[:, None, :], jnp.sin(ang2)[:, None, :]
    cos = (c1 * c2 - s1 * s2).reshape(n, m)
    sin = (s1 * c2 + c1 * s2).reshape(n, m)
    t0 = lax.broadcasted_iota(jnp.int32, (n, m), 1) == 0
    sign = (1 - 2 * (f % 2)).astype(F32)
    return jnp.concatenate([jnp.where(t0, 0.5, cos), jnp.where(t0, sign, sin)], axis=1).astype(BF16)


def _fourier_mix_weights(w_four, T):
    c = np.arange(FOUR_DIM)
    ang = 2.0 * np.pi * np.outer(c, c) / FOUR_DIM
    norm = 1.0 / math.sqrt(T * FOUR_DIM)
    gr = jnp.einsum('cf,gfe->gce', jnp.asarray(np.cos(ang) * norm, F32), w_four, precision='highest')
    gi = jnp.einsum('cf,gfe->gce', jnp.asarray(-np.sin(ang) * norm, F32), w_four, precision='highest')
    return jnp.concatenate([_block_diag(gr), _block_diag(gi)], axis=1).astype(BF16)


def _mlstm_kernel(q_ref, k_ref, v_ref, o_ref, g_ref, bg_ref, cf0_ref, mf0_ref, cb0_ref, mb0_ref,
                  y_ref, cf1_ref, mf1_ref, cb1_ref, mb1_ref,
                  hf_ref, hb_ref, pf_ref, pb_ref, uf_ref, ub_ref, rf_ref, rb_ref):
    T = q_ref.shape[1]
    L = MLSTM_CHUNK
    dh = MLSTM_DH
    aug = MLSTM_AUG
    nc = T // L
    head = pl.program_id(1)

    row = lax.broadcasted_iota(jnp.int32, (L, L), 0)
    col = lax.broadcasted_iota(jnp.int32, (L, L), 1)
    masks = (row <= col, row >= col)
    row2 = lax.broadcasted_iota(jnp.int32, (L, 2 * L), 0)
    col2 = jnp.bitwise_and(lax.broadcasted_iota(jnp.int32, (L, 2 * L), 1), L - 1)
    tri2 = jnp.where(col2 <= row2, 1.0, 0.0).astype(BF16)
    diag = row == col
    sub8 = lax.broadcasted_iota(jnp.int32, (SUBLANE, L), 0)
    ones_t = jnp.ones((aug - dh, L), F32)
    sel_row = jnp.bitwise_and(lax.broadcasted_iota(jnp.int32, (2 * LANE, 2 * LANE), 0), LANE - 1)
    sel_dir = jnp.right_shift(lax.broadcasted_iota(jnp.int32, (2 * LANE, 2 * LANE), 1), LANE_SHIFT)
    sel_i = jnp.where(sel_row == sel_dir * (2 * N_MLSTM) + head, 1.0, 0.0).astype(BF16)
    sel_f = jnp.where(sel_row == sel_dir * (2 * N_MLSTM) + N_MLSTM + head, 1.0, 0.0).astype(BF16)
    p_refs, u_refs, r_refs, h_refs = (pf_ref, pb_ref), (uf_ref, ub_ref), (rf_ref, rb_ref), (hf_ref, hb_ref)
    bias = bg_ref[...]

    def prep_group(c0):
        cs = [c0 + j for j in range(min(MLSTM_GROUP, nc))]
        rs = [pl.multiple_of(c * L, L) for c in cs]
        q = [q_ref[0, pl.ds(r, L), :] for r in rs]
        k = [k_ref[0, pl.ds(r, L), :] for r in rs]
        vt = [jnp.concatenate([v_ref[0, pl.ds(r, L), :].astype(F32).T, ones_t], axis=0).astype(BF16) for r in rs]
        g = [(g_ref[0, pl.ds(r, L), :] + bias) * LOG2_E for r in rs]
        i_rep = [jnp.dot(jnp.concatenate(_split_hi_lo(gg), axis=1), sel_i, preferred_element_type=F32)
                 for gg in g]
        lf_rep = [jnp.dot(jnp.concatenate(_split_hi_lo(_log2_sigmoid(gg)), axis=1), sel_f,
                          preferred_element_type=F32) for gg in g]
        st = [lax.dot_general(kk, qq, (((1,), (1,)), ((), ())), preferred_element_type=F32)
              for kk, qq in zip(k, q)]
        cum = [jnp.dot(tri2, jnp.concatenate(_split_hi_lo(f), axis=0), preferred_element_type=F32)
               for f in lf_rep]
        items = [(j, d) for j in range(len(cs)) for d in range(2)]
        for j, d in items:
            if d == 0:
                b = cum[j][:, :L]
                total = b[L - 1:L, :]
            else:
                total = cum[j][L - 1:L, L:]
                b = total - cum[j][:, L:] + lf_rep[j][:, L:]
            r_rep = i_rep[j][:, d * L:(d + 1) * L] - b
            dmt = jnp.where(masks[d], r_rep, NEG_BIG)
            pm = jnp.max(dmt, axis=0, keepdims=True)
            sg = (st[j] * jnp.exp2(dmt - pm)).astype(BF16)
            b_row = jnp.sum(jnp.where(diag, b, 0.0), axis=0, keepdims=True)
            g_end = total + r_rep
            g_loc = jnp.max(g_end, axis=0, keepdims=True)
            kw = (k[j].astype(F32) * jnp.exp2(g_end - g_loc)).astype(BF16)
            base = pl.multiple_of(cs[j] * aug, BF16_ROWS)
            pu = jnp.dot(vt[j], jnp.concatenate([sg, kw], axis=1), preferred_element_type=F32)
            p_refs[d][pl.ds(base, aug), :] = pu[:, :L]
            u_refs[d][pl.ds(base, aug), :] = pu[:, L:]
            rbase = pl.multiple_of(cs[j] * SUBLANE, SUBLANE)
            r_refs[d][pl.ds(rbase, SUBLANE), :] = jnp.where(
                sub8 == 0, b_row, jnp.where(sub8 == 1, b_row + pm, jnp.where(sub8 == 2, total, g_loc)))

    ga = min(MLSTM_GROUP, nc)

    def body_a(i, carry):
        prep_group(i * ga)
        return carry

    lax.fori_loop(0, nc // ga, body_a, 0)

    def step(d, c, ct, m):
        base = pl.multiple_of(c * aug, BF16_ROWS)
        rows = r_refs[d][pl.ds(pl.multiple_of(c * SUBLANE, SUBLANE), SUBLANE), :]
        b_row, a_row, total, g_loc = rows[0:1], rows[1:2], rows[2:3], rows[3:4]
        inter = b_row + m
        m_j = jnp.maximum(inter, a_row)
        qct = lax.dot_general(ct.astype(BF16), q_ref[0, pl.ds(pl.multiple_of(c * L, L), L), :],
                              (((1,), (1,)), ((), ())), preferred_element_type=F32)
        nd = jnp.exp2(inter - m_j) * qct + jnp.exp2(a_row - m_j) * p_refs[d][pl.ds(base, aug), :]
        den = jnp.maximum(jnp.abs(nd[dh:dh + 1]), jnp.exp2(-m_j))
        h_refs[d][pl.ds(pl.multiple_of(c * dh, dh), dh), :] = nd[:dh] / den
        m_new = jnp.maximum(total + m, g_loc)
        ct_new = jnp.exp2(total + m - m_new) * ct + jnp.exp2(g_loc - m_new) * u_refs[d][pl.ds(base, aug), :]
        return ct_new, m_new

    gb = min(nc, MLSTM_GROUP)

    def body_b(i, carry):
        cf, mf, cb, mb = carry
        for j in range(gb):
            cf, mf = step(0, i * gb + j, cf, mf)
            cb, mb = step(1, nc - 1 - (i * gb + j), cb, mb)
        return cf, mf, cb, mb

    cf, mf, cb, mb = lax.fori_loop(
        0, nc // gb, body_b, (cf0_ref[0, 0], mf0_ref[0, 0][0:1, :], cb0_ref[0, 0], mb0_ref[0, 0][0:1, :]))

    cf1_ref[0, 0] = cf
    cb1_ref[0, 0] = cb
    mf1_ref[0, 0] = jnp.broadcast_to(mf, (SUBLANE, LANE))
    mb1_ref[0, 0] = jnp.broadcast_to(mb, (SUBLANE, LANE))

    go = min(nc, MLSTM_GROUP)

    def body_out(i, carry):
        for j in range(go):
            c = i * go + j
            r = pl.multiple_of(c * L, L)
            rh = pl.multiple_of(c * dh, dh)
            h = (hf_ref[pl.ds(rh, dh), :] + hb_ref[pl.ds(rh, dh), :]).T
            y_ref[0, pl.ds(r, L), :] = (_sigmoid(o_ref[0, pl.ds(r, L), :].astype(F32)) * h).astype(y_ref.dtype)
        return carry

    lax.fori_loop(0, nc // go, body_out, 0)


def _mlstm(z, zg, b_gate, cf0, mf0, cb0, mb0):
    B, T, _ = z.shape
    H, dh, L, aug = N_MLSTM, MLSTM_DH, MLSTM_CHUNK, MLSTM_AUG
    nc = T // L
    zcol = lambda off: (lambda b, h: (b, 0, off // dh + h))
    st = lambda b, h: (b, h, 0, 0)
    c_spec = pl.BlockSpec((1, 1, aug, dh), st)
    m_spec = pl.BlockSpec((1, 1, SUBLANE, LANE), st)
    c_shape = jax.ShapeDtypeStruct((B, H, aug, dh), F32)
    m_shape = jax.ShapeDtypeStruct((B, H, SUBLANE, LANE), F32)
    return pl.pallas_call(
        _mlstm_kernel,
        grid=(B, H),
        in_specs=[
            pl.BlockSpec((1, T, dh), zcol(OFF_Q)),
            pl.BlockSpec((1, T, dh), zcol(OFF_K)),
            pl.BlockSpec((1, T, dh), zcol(OFF_V)),
            pl.BlockSpec((1, T, dh), zcol(OFF_O)),
            pl.BlockSpec((1, T, LANE), lambda b, h: (b, 0, 0)),
            pl.BlockSpec((1, LANE), lambda b, h: (0, 0)),
            c_spec, m_spec, c_spec, m_spec,
        ],
        out_specs=[pl.BlockSpec((1, T, dh), lambda b, h: (b, 0, h)), c_spec, m_spec, c_spec, m_spec],
        out_shape=[jax.ShapeDtypeStruct((B, T, MIX_MLSTM), BF16), c_shape, m_shape, c_shape, m_shape],
        scratch_shapes=[
            pltpu.VMEM((nc * dh, L), F32),
            pltpu.VMEM((nc * dh, L), F32),
            pltpu.VMEM((nc * aug, L), F32),
            pltpu.VMEM((nc * aug, L), F32),
            pltpu.VMEM((nc * aug, dh), F32),
            pltpu.VMEM((nc * aug, dh), F32),
            pltpu.VMEM((nc * SUBLANE, LANE), F32),
            pltpu.VMEM((nc * SUBLANE, LANE), F32),
        ],
        compiler_params=_params(2),
        name="mlstm",
    )(z, z, z, z, zg, b_gate, cf0, mf0, cb0, mb0)


def kernel(x, c, ctx, c_ctx, w_mod, b_mod, w_in, conv_qk, b_gates, pool_w, pool_scale, fourier_w,
           w_out, ln1_g, ln1_b, w_up, conv_ffn_w, conv_ffn_b, w_down, ln2_g, ln2_b):
    B, T, D = x.shape
    Tc = ctx.shape[1]
    tm_x = min(T, 1024)
    tm_c = Tc
    xc = ctx
    cond = jnp.concatenate([c, c_ctx[None], jnp.zeros((-(B + 1) % SUBLANE, D), F32)], axis=0)
    mods = _modulation(cond, w_mod, b_mod)
    filt_x, filt_c = _pool_filters(True), _pool_filters(False)
    cnt_x, cnt_c = _pool_inv_count(T, True), _pool_inv_count(Tc, False)
    dft_x, dft_c = _dft_matrix(T), _dft_matrix(Tc)
    zero_c = jnp.zeros((B, N_MLSTM, MLSTM_AUG, MLSTM_DH), F32)
    zero_m = jnp.zeros((B, N_MLSTM, SUBLANE, LANE), F32)
    for l in range(DEPTH):
        last = l == DEPTH - 1
        mx = [m[:, None, :] for m in jnp.split(mods[l, :B], 6, axis=-1)]
        mc = [jnp.broadcast_to(m[None, None, :], (B, 1, D)) for m in jnp.split(mods[l, B], 6, axis=-1)]
        w_main = w_in[l][:, :OFF_G].astype(BF16)
        w_gate = jnp.pad(w_in[l][:, OFF_G:], ((0, 0), (0, LANE - N_GATES))).astype(BF16)
        b_gate = jnp.pad(b_gates[l], (0, LANE - N_GATES))[None]
        w_pool_bd = _block_diag(pool_w[l]).astype(BF16)
        s_pool = pool_scale[l][None]
        w_out_b = w_out[l].astype(BF16)
        w_up_b = w_up[l].astype(BF16)
        w_down_b = w_down[l].astype(BF16)
        g1, b1 = ln1_g[l][None], ln1_b[l][None]
        g2, b2 = ln2_g[l][None], ln2_b[l][None]
        bcv = conv_ffn_b[l][None]

        zc, zgc = _proj_in(xc, mc[0], mc[1], w_main, w_gate, conv_qk[l], tm_c)
        ymc, cf, mf, cb, mb = _mlstm(zc, zgc, b_gate, zero_c, zero_m, zero_c, zero_m)
        zx, zgx = _proj_in(x, mx[0], mx[1], w_main, w_gate, conv_qk[l], tm_x)
        ymx = _mlstm(zx, zgx, b_gate, cf, mf, cb, mb)[0]
        ypx = _pool(zx, filt_x, cnt_x, w_pool_bd, s_pool, True)
        yfx = _fourier(zx, _fourier_mix_weights(fourier_w[l], T), dft_x)
        x = _proj_out(ypx, yfx, ymx, x, mx[2], w_out_b, g1, b1, min(T, 2 * tm_x))
        x = _ffn(x, mx[3], mx[4], mx[5], w_up_b, conv_ffn_w[l], bcv, w_down_b, g2, b2, tm_x)
        if not last:
            ypc = _pool(zc, filt_c, cnt_c, w_pool_bd, s_pool, False)
            yfc = _fourier(zc, _fourier_mix_weights(fourier_w[l], Tc), dft_c)
            xc = _proj_out(ypc, yfc, ymc, xc, mc[2], w_out_b, g1, b1, tm_c)
            xc = _ffn(xc, mc[3], mc[4], mc[5], w_up_b, conv_ffn_w[l], bcv, w_down_b, g2, b2, tm_c)
    return x
```

```python
import functools
import math

import jax
import jax.numpy as jnp
import numpy as np
from jax import lax
from jax.experimental import pallas as pl
from jax.experimental.pallas import tpu as pltpu

F32 = jnp.float32
BF16 = jnp.bfloat16

D_MODEL = 1024
DEPTH = 4
GRID_W = 64
GRID_SHIFT = GRID_W.bit_length() - 1
POOL_WINDOWS = (2, 4, 8, 16)
N_POOL = len(POOL_WINDOWS)
MIX_POOL = D_MODEL // 4
POOL_DIM = MIX_POOL // N_POOL
N_FOUR = 4
MIX_FOUR = D_MODEL // 4
FOUR_DIM = MIX_FOUR // N_FOUR
N_MLSTM = 4
MIX_MLSTM = D_MODEL // 2
MLSTM_DH = MIX_MLSTM // N_MLSTM
MLSTM_CHUNK = 128
N_GATES = 2 * 2 * N_MLSTM
D_FF = int(math.ceil(8 * D_MODEL / 3 / 128)) * 128
OFF_FOUR = MIX_POOL
OFF_O = OFF_FOUR + MIX_FOUR
OFF_Q = OFF_O + MIX_MLSTM
OFF_K = OFF_Q + MIX_MLSTM
OFF_V = OFF_K + MIX_MLSTM
OFF_G = OFF_V + MIX_MLSTM
ALPHA = (2 * DEPTH) ** 0.25
LN_EPS = 1e-6
LOG2_E = math.log2(math.e)

LANE = 128
LANE_SHIFT = LANE.bit_length() - 1
SUBLANE = 8
BF16_ROWS = 16
VMEM_LIMIT = 56 * 1024 * 1024
FF_CHUNK = 256
PROJ_CHUNK = 512
POOL_BLOCK = 256
POOL_UNROLL = 4
FOUR_MIRROR_BLOCK = 256
POOL_PAD = (max(POOL_WINDOWS) // 2) * GRID_W
NEG_BIG = -1e30
MLSTM_GROUP = 8
MLSTM_AUG = MLSTM_DH + BF16_ROWS


def _params(n_axes):
    return pltpu.CompilerParams(dimension_semantics=("arbitrary",) * n_axes, vmem_limit_bytes=VMEM_LIMIT)


def _ln_rows(x):
    mu = jnp.mean(x, axis=-1, keepdims=True)
    xc = x - mu
    var = jnp.mean(xc * xc, axis=-1, keepdims=True)
    return xc * lax.rsqrt(var + LN_EPS)


def _sigmoid(x):
    return 0.5 * jnp.tanh(0.5 * x) + 0.5


def _log2_sigmoid(x2):
    return jnp.minimum(x2, 0.0) - jnp.log2(1.0 + jnp.exp2(-jnp.abs(x2)))


def _split_hi_lo(x):
    hi = x.astype(BF16)
    return hi, (x - hi.astype(F32)).astype(BF16)


def _mod_kernel(c_ref, w_ref, b_ref, o_ref):
    cond = c_ref[...]
    s = cond * _sigmoid(cond)
    o_ref[0] = jnp.dot(s, w_ref[0], preferred_element_type=F32, precision=lax.Precision.HIGHEST) + b_ref[0]


def _modulation(cond, w_mod, b_mod):
    R, D = cond.shape
    L, _, N = w_mod.shape
    tn = N // 4
    return pl.pallas_call(
        _mod_kernel,
        grid=(L, N // tn),
        in_specs=[
            pl.BlockSpec((R, D), lambda l, j: (0, 0)),
            pl.BlockSpec((1, D, tn), lambda l, j: (l, 0, j)),
            pl.BlockSpec((1, 1, tn), lambda l, j: (l, 0, j)),
        ],
        out_specs=pl.BlockSpec((1, R, tn), lambda l, j: (l, 0, j)),
        out_shape=jax.ShapeDtypeStruct((L, R, N), F32),
        compiler_params=_params(2),
        name="modulation",
    )(cond, w_mod, b_mod[:, None, :])


def _proj_in_kernel(x_ref, xp_ref, xn_ref, sh_ref, sc_ref, w_ref, wg_ref, wc_ref, z_ref, g_ref, h_ref):
    i = pl.program_id(1)
    tm = x_ref.shape[1]
    mod_scale = 1.0 + sc_ref[0]
    mod_shift = sh_ref[0]
    u = _ln_rows(x_ref[0]) * mod_scale + mod_shift
    u_prev = (_ln_rows(xp_ref[0]) * mod_scale + mod_shift) * (i > 0).astype(F32)
    u_next = (_ln_rows(xn_ref[0]) * mod_scale + mod_shift) * (i < pl.num_programs(1) - 1).astype(F32)
    ub = u.astype(BF16)
    u_ext = jnp.concatenate([u_prev, u, u_next], axis=0).astype(BF16)
    n = z_ref.shape[-1]
    for c0 in range(0, n, PROJ_CHUNK):
        c1 = min(c0 + PROJ_CHUNK, n)
        if OFF_Q <= c0 and c1 <= OFF_V:
            h_ref[...] = jnp.dot(u_ext, w_ref[:, c0:c1], preferred_element_type=F32)
            w = wc_ref[:, c0 - OFF_Q:c1 - OFF_Q]
            y = (h_ref[SUBLANE - 1:SUBLANE - 1 + tm, :] * w[0:1] + h_ref[SUBLANE:SUBLANE + tm, :] * w[1:2]
                 + h_ref[SUBLANE + 1:SUBLANE + 1 + tm, :] * w[2:3])
            scale = 1.0 if c0 < OFF_K else MLSTM_DH ** -0.5
            z_ref[0, :, c0:c1] = (y * _sigmoid(y) * scale).astype(z_ref.dtype)
        else:
            z_ref[0, :, c0:c1] = jnp.dot(ub, w_ref[:, c0:c1], preferred_element_type=F32).astype(z_ref.dtype)
    g_ref[0] = jnp.dot(ub, wg_ref[...], preferred_element_type=F32)


def _proj_in(x, shift, scale, w_main, w_gate, w_conv, tm):
    B, T, D = x.shape
    n = w_main.shape[1]
    assert OFF_Q % PROJ_CHUNK == 0 and OFF_K % PROJ_CHUNK == 0 and OFF_V % PROJ_CHUNK == 0
    nb = tm // SUBLANE
    last = T // SUBLANE - 1
    return pl.pallas_call(
        _proj_in_kernel,
        grid=(B, T // tm),
        in_specs=[
            pl.BlockSpec((1, tm, D), lambda b, i: (b, i, 0)),
            pl.BlockSpec((1, SUBLANE, D), lambda b, i: (b, jnp.maximum(i * nb - 1, 0), 0)),
            pl.BlockSpec((1, SUBLANE, D), lambda b, i: (b, jnp.minimum((i + 1) * nb, last), 0)),
            pl.BlockSpec((1, 1, D), lambda b, i: (b, 0, 0)),
            pl.BlockSpec((1, 1, D), lambda b, i: (b, 0, 0)),
            pl.BlockSpec((D, n), lambda b, i: (0, 0)),
            pl.BlockSpec((D, LANE), lambda b, i: (0, 0)),
            pl.BlockSpec((3, 2 * MIX_MLSTM), lambda b, i: (0, 0)),
        ],
        out_specs=[
            pl.BlockSpec((1, tm, n), lambda b, i: (b, i, 0)),
            pl.BlockSpec((1, tm, LANE), lambda b, i: (b, i, 0)),
        ],
        out_shape=[
            jax.ShapeDtypeStruct((B, T, n), BF16),
            jax.ShapeDtypeStruct((B, T, LANE), F32),
        ],
        scratch_shapes=[pltpu.VMEM((tm + 2 * SUBLANE, PROJ_CHUNK), F32)],
        compiler_params=_params(2),
        name="proj_in",
    )(x, x, x, shift, scale, w_main, w_gate, w_conv)


def _proj_out_kernel(yp_ref, yf_ref, ym_ref, x_ref, gate_ref, wp_ref, wf_ref, wm_ref, lg_ref, lb_ref, o_ref):
    f = (jnp.dot(yp_ref[0], wp_ref[...], preferred_element_type=F32)
         + jnp.dot(yf_ref[0], wf_ref[...], preferred_element_type=F32)
         + jnp.dot(ym_ref[0], wm_ref[...], preferred_element_type=F32))
    r = ALPHA * x_ref[0] + gate_ref[0] * f
    o_ref[0] = _ln_rows(r) * lg_ref[...] + lb_ref[...]


def _proj_out(yp, yf, ym, x, gate, w, ln_g, ln_b, tm):
    B, T, D = x.shape
    tok = lambda b, i: (b, i, 0)
    return pl.pallas_call(
        _proj_out_kernel,
        grid=(B, T // tm),
        in_specs=[
            pl.BlockSpec((1, tm, MIX_POOL), tok),
            pl.BlockSpec((1, tm, MIX_FOUR), tok),
            pl.BlockSpec((1, tm, MIX_MLSTM), tok),
            pl.BlockSpec((1, tm, D), tok),
            pl.BlockSpec((1, 1, D), lambda b, i: (b, 0, 0)),
            pl.BlockSpec((MIX_POOL, D), lambda b, i: (0, 0)),
            pl.BlockSpec((MIX_FOUR, D), lambda b, i: (OFF_FOUR // MIX_FOUR, 0)),
            pl.BlockSpec((MIX_MLSTM, D), lambda b, i: (OFF_O // MIX_MLSTM, 0)),
            pl.BlockSpec((1, D), lambda b, i: (0, 0)),
            pl.BlockSpec((1, D), lambda b, i: (0, 0)),
        ],
        out_specs=pl.BlockSpec((1, tm, D), tok),
        out_shape=jax.ShapeDtypeStruct((B, T, D), F32),
        compiler_params=_params(2),
        name="proj_out",
    )(yp, yf, ym, x, gate, w, w, w, ln_g, ln_b)


def _gelu_tanh_times(x, half_a):
    c = math.sqrt(2.0 / math.pi)
    inner = x * (c + (c * 0.044715) * (x * x))
    return (x * (1.0 + jnp.tanh(inner))) * half_a


def _ffn_kernel(x_ref, xp_ref, xn_ref, sh_ref, sc_ref, gate_ref, wup_ref, wc_ref, bc_ref, wdn_ref,
                lg_ref, lb_ref, o_ref, ha_ref, hg_ref, p_ref):
    i = pl.program_id(1)
    tm = x_ref.shape[1]
    x = x_ref[0]
    mod_scale = 1.0 + sc_ref[0]
    mod_shift = sh_ref[0]
    has_prev = (i > 0).astype(F32)
    has_next = (i < pl.num_programs(1) - 1).astype(F32)
    u_prev = (_ln_rows(xp_ref[0]) * mod_scale + mod_shift) * has_prev
    u_next = (_ln_rows(xn_ref[0]) * mod_scale + mod_shift) * has_next
    u_main = _ln_rows(x) * mod_scale + mod_shift
    u_ext = jnp.concatenate([u_prev, u_main, u_next], axis=0).astype(BF16)

    def conv(h_ref, cols):
        w = wc_ref[:, cols]
        return (h_ref[SUBLANE - 1:SUBLANE - 1 + tm, :] * w[0:1]
                + h_ref[SUBLANE:SUBLANE + tm, :] * w[1:2]
                + h_ref[SUBLANE + 1:SUBLANE + 1 + tm, :] * w[2:3]
                + bc_ref[:, cols])

    for c in range(D_FF // FF_CHUNK):
        ca = slice(c * FF_CHUNK, (c + 1) * FF_CHUNK)
        cg = slice(D_FF + c * FF_CHUNK, D_FF + (c + 1) * FF_CHUNK)
        ha_ref[...] = jnp.dot(u_ext, wup_ref[:, ca], preferred_element_type=F32)
        hg_ref[...] = jnp.dot(u_ext, wup_ref[:, cg], preferred_element_type=F32)
        p_ref[:, ca] = _gelu_tanh_times(conv(hg_ref, cg), conv(ha_ref, ca)).astype(BF16)

    f = jnp.dot(p_ref[...], wdn_ref[...], preferred_element_type=F32)
    r = ALPHA * x + gate_ref[0] * f
    o_ref[0] = _ln_rows(r) * lg_ref[...] + lb_ref[...]


def _ffn(x, shift, scale, gate, w_up, w_conv, b_conv, w_down, ln_g, ln_b, tm):
    B, T, D = x.shape
    nb = tm // SUBLANE
    last = T // SUBLANE - 1
    const = dict(pipeline_mode=pl.Buffered(1))
    half_value = jnp.concatenate([jnp.full((D_FF,), 0.5, F32), jnp.ones((D_FF,), F32)])
    w_conv = w_conv * half_value
    b_conv = b_conv * half_value
    return pl.pallas_call(
        _ffn_kernel,
        grid=(B, T // tm),
        in_specs=[
            pl.BlockSpec((1, tm, D), lambda b, i: (b, i, 0)),
            pl.BlockSpec((1, SUBLANE, D), lambda b, i: (b, jnp.maximum(i * nb - 1, 0), 0)),
            pl.BlockSpec((1, SUBLANE, D), lambda b, i: (b, jnp.minimum((i + 1) * nb, last), 0)),
            pl.BlockSpec((1, 1, D), lambda b, i: (b, 0, 0)),
            pl.BlockSpec((1, 1, D), lambda b, i: (b, 0, 0)),
            pl.BlockSpec((1, 1, D), lambda b, i: (b, 0, 0)),
            pl.BlockSpec((D, 2 * D_FF), lambda b, i: (0, 0), **const),
            pl.BlockSpec((3, 2 * D_FF), lambda b, i: (0, 0), **const),
            pl.BlockSpec((1, 2 * D_FF), lambda b, i: (0, 0), **const),
            pl.BlockSpec((D_FF, D), lambda b, i: (0, 0), **const),
            pl.BlockSpec((1, D), lambda b, i: (0, 0)),
            pl.BlockSpec((1, D), lambda b, i: (0, 0)),
        ],
        out_specs=pl.BlockSpec((1, tm, D), lambda b, i: (b, i, 0)),
        out_shape=jax.ShapeDtypeStruct((B, T, D), F32),
        scratch_shapes=[
            pltpu.VMEM((tm + 2 * SUBLANE, FF_CHUNK), F32),
            pltpu.VMEM((tm + 2 * SUBLANE, FF_CHUNK), F32),
            pltpu.VMEM((tm, D_FF), BF16),
        ],
        compiler_params=_params(2),
        name="conv_ffn",
    )(x, x, x, shift, scale, gate, w_up, w_conv, b_conv, w_down, ln_g, ln_b)


def _band_matrix(n, w):
    pos = np.arange(n)
    lo = np.maximum(pos - w // 2, 0)
    hi = np.minimum(pos + w // 2 - 1, n - 1)
    return ((pos[None, :] >= lo[:, None]) & (pos[None, :] <= hi[:, None])).astype(np.float32)


def _pool_filters(grid2d):
    mats = []
    for w in POOL_WINDOWS:
        if grid2d:
            mats.append(np.kron(np.eye(POOL_BLOCK // GRID_W, dtype=np.float32), _band_matrix(GRID_W, w)))
        else:
            mats.append(_band_matrix(POOL_BLOCK, w))
    return jnp.asarray(np.stack(mats), BF16)


def _window_count(idx, half, n):
    return jnp.minimum(idx + (half - 1), n - 1) - jnp.maximum(idx - half, 0) + 1


def _pool_inv_count(T, grid2d):
    pos = lax.broadcasted_iota(jnp.int32, (T, MIX_POOL), 0)
    group = jnp.right_shift(lax.broadcasted_iota(jnp.int32, (T, MIX_POOL), 1), GRID_SHIFT)
    half = jnp.left_shift(1, group)
    if grid2d:
        count = (_window_count(jnp.right_shift(pos, GRID_SHIFT), half, T // GRID_W)
                 * _window_count(jnp.bitwise_and(pos, GRID_W - 1), half, GRID_W))
    else:
        count = _window_count(pos, half, T)
    return 1.0 / count.astype(F32)


def _pool_kernel(z_ref, a_ref, n_ref, w_ref, s_ref, y_ref, acc_ref, *, grid2d):
    T = z_ref.shape[1]
    nblk = T // POOL_BLOCK
    lane = lax.broadcasted_iota(jnp.int32, (POOL_BLOCK, MIX_POOL), 1)
    group = jnp.right_shift(lane, GRID_SHIFT)

    def col_filter(blk):
        r0 = pl.multiple_of(blk * POOL_BLOCK, POOL_BLOCK)
        xb = z_ref[0, pl.ds(r0, POOL_BLOCK), :]
        out = jnp.dot(a_ref[0], xb, preferred_element_type=F32)
        for g in range(1, N_POOL):
            out = jnp.where(group == g, jnp.dot(a_ref[g], xb, preferred_element_type=F32), out)
        return r0, xb, out

    def finish(r0, xb, box):
        p = box * n_ref[pl.ds(r0, POOL_BLOCK), :] - xb.astype(F32)
        y = jnp.dot(p.astype(BF16), w_ref[...], preferred_element_type=F32) * s_ref[...]
        y_ref[0, pl.ds(r0, POOL_BLOCK), :] = y.astype(y_ref.dtype)

    if not grid2d:
        assert T == POOL_BLOCK
        finish(*col_filter(0))
        return

    zeros = jnp.zeros((POOL_PAD, MIX_POOL), F32)
    acc_ref[0:POOL_PAD, :] = zeros
    acc_ref[POOL_PAD + T:POOL_PAD + T + POOL_PAD, :] = zeros

    def phase1(blk, carry):
        r0, _, out = col_filter(blk)
        acc_ref[pl.ds(POOL_PAD + r0, POOL_BLOCK), :] = out
        return carry

    lax.fori_loop(0, nblk, phase1, 0, unroll=POOL_UNROLL)

    lane_t = lax.broadcasted_iota(jnp.int32, (POOL_BLOCK, LANE), 1)

    def phase2(blk, carry):
        r0 = pl.multiple_of(blk * POOL_BLOCK, POOL_BLOCK)
        base = POOL_PAD + r0

        def slab(k, tile):
            return acc_ref[pl.ds(base + k * GRID_W, POOL_BLOCK), tile * LANE:(tile + 1) * LANE]

        s2 = slab(-1, 0) + slab(0, 0)
        s4 = s2 + slab(-2, 0) + slab(1, 0)
        s8 = slab(-4, 1)
        for k in range(-3, 4):
            s8 = s8 + slab(k, 1)
        s16 = s8
        for k in list(range(-8, -4)) + list(range(4, 8)):
            s16 = s16 + slab(k, 1)
        box = jnp.concatenate([jnp.where(lane_t < POOL_DIM, s2, s4), jnp.where(lane_t < POOL_DIM, s8, s16)], axis=1)
        finish(r0, z_ref[0, pl.ds(r0, POOL_BLOCK), :], box)
        return carry

    lax.fori_loop(0, nblk, phase2, 0, unroll=POOL_UNROLL)


def _pool(z, filters, inv_count, w_bd, s_row, grid2d):
    B, T, _ = z.shape
    return pl.pallas_call(
        functools.partial(_pool_kernel, grid2d=grid2d),
        grid=(B,),
        in_specs=[
            pl.BlockSpec((1, T, MIX_POOL), lambda b: (b, 0, 0)),
            pl.BlockSpec((N_POOL, POOL_BLOCK, POOL_BLOCK), lambda b: (0, 0, 0)),
            pl.BlockSpec((T, MIX_POOL), lambda b: (0, 0)),
            pl.BlockSpec((MIX_POOL, MIX_POOL), lambda b: (0, 0)),
            pl.BlockSpec((1, MIX_POOL), lambda b: (0, 0)),
        ],
        out_specs=pl.BlockSpec((1, T, MIX_POOL), lambda b: (b, 0, 0)),
        out_shape=jax.ShapeDtypeStruct((B, T, MIX_POOL), BF16),
        scratch_shapes=[pltpu.VMEM((T + 2 * POOL_PAD, MIX_POOL), F32)],
        compiler_params=_params(1),
        name="pool2d" if grid2d else "pool1d",
    )(z, filters, inv_count, w_bd, s_row)


def _block_diag(w):
    G, n, m = w.shape
    eye = jnp.eye(G, dtype=w.dtype)
    return (eye[:, None, :, None] * w[:, :, None, :]).reshape(G * n, G * m)


def _four_mix_kernel(z_ref, za_ref, zb_ref, zm_ref, g_ref, y_ref):
    tm = z_ref.shape[1]
    sb = zb_ref.shape[1]
    nsub = tm // sb
    gr, gi = g_ref[:, :MIX_FOUR], g_ref[:, MIX_FOUR:]
    r = lax.broadcasted_iota(jnp.int32, (sb, 2 * sb), 0)
    c = lax.broadcasted_iota(jnp.int32, (sb, 2 * sb), 1)
    perm = jnp.where(((r >= 1) & (c == sb - r)) | ((r == 0) & (c == sb)), 1.0, 0.0).astype(BF16)
    mid = jnp.dot(zm_ref[0], gr, preferred_element_type=F32)[0:1]
    row0 = lax.broadcasted_iota(jnp.int32, (sb, MIX_FOUR), 0) == 0
    for j in range(nsub):
        lo = (nsub - 1 - j) * sb
        if j == 0:
            src = jnp.concatenate([za_ref[0, lo:lo + sb, :], zb_ref[0]], axis=0)
        else:
            src = za_ref[0, lo:lo + 2 * sb, :]
        xr = jnp.dot(perm, src, preferred_element_type=F32)
        x = z_ref[0, j * sb:(j + 1) * sb, :].astype(F32)
        even = jnp.dot((x + xr).astype(BF16), gr, preferred_element_type=F32)
        odd = jnp.dot((x - xr).astype(BF16), gi, preferred_element_type=F32)
        if j == 0:
            odd = jnp.where(row0 & (pl.program_id(1) == 0), mid, odd)
        y_ref[0, 0, j * sb:(j + 1) * sb, :] = even.astype(y_ref.dtype)
        y_ref[0, 1, j * sb:(j + 1) * sb, :] = odd.astype(y_ref.dtype)


def _four_dft_kernel(d_ref, y_ref, o_ref):
    rows = 256
    for r0 in range(0, o_ref.shape[1], rows):
        o_ref[0, r0:r0 + rows, :] = jnp.dot(d_ref[r0:r0 + rows, :], y_ref[0],
                                            preferred_element_type=F32).astype(o_ref.dtype)


def _fourier(z, g_mix, dft):
    B, T, _ = z.shape
    M = T // 2
    tm = min(M, 2048)
    fcol = OFF_FOUR // MIX_FOUR
    sb = min(tm, FOUR_MIRROR_BLOCK)
    nt, nsb, per = T // tm, T // sb, tm // sb
    y = pl.pallas_call(
        _four_mix_kernel,
        grid=(B, M // tm),
        in_specs=[
            pl.BlockSpec((1, tm, MIX_FOUR), lambda b, i: (b, i, fcol)),
            pl.BlockSpec((1, tm, MIX_FOUR), lambda b, i: (b, nt - 1 - i, fcol)),
            pl.BlockSpec((1, sb, MIX_FOUR), lambda b, i: (b, (nsb - i * per) % nsb, fcol)),
            pl.BlockSpec((1, BF16_ROWS, MIX_FOUR), lambda b, i: (b, M // BF16_ROWS, fcol)),
            pl.BlockSpec((MIX_FOUR, 2 * MIX_FOUR), lambda b, i: (0, 0)),
        ],
        out_specs=pl.BlockSpec((1, 2, tm, MIX_FOUR), lambda b, i: (b, 0, i, 0)),
        out_shape=jax.ShapeDtypeStruct((B, 2, M, MIX_FOUR), BF16),
        compiler_params=_params(2),
        name="fourier_mix",
    )(z, z, z, z, g_mix)
    y = y.reshape(B, T, MIX_FOUR)
    tf = min(T, 2048)
    return pl.pallas_call(
        _four_dft_kernel,
        grid=(T // tf, B),
        in_specs=[
            pl.BlockSpec((tf, T), lambda m, b: (m, 0)),
            pl.BlockSpec((1, T, MIX_FOUR), lambda m, b: (b, 0, 0)),
        ],
        out_specs=pl.BlockSpec((1, tf, MIX_FOUR), lambda m, b: (b, m, 0)),
        out_shape=jax.ShapeDtypeStruct((B, T, MIX_FOUR), BF16),
        compiler_params=_params(2),
        name="fourier_dft",
    )(dft, y)


def _dft_matrix(n):
    m = n // 2
    a = m // GRID_W
    f = lax.broadcasted_iota(jnp.int32, (n, 1), 0)
    ang1 = ((f * GRID_W * lax.broadcasted_iota(jnp.int32, (1, a), 1)) % n).astype(F32) * (2.0 * math.pi / n)
    ang2 = ((f * lax.broadcasted_iota(jnp.int32, (1, GRID_W), 1)) % n).astype(F32) * (2.0 * math.pi / n)
    c1, s1 = jnp.cos(ang1)[:, :, None], jnp.sin(ang1)[:, :, None]
    c2, s2 = jnp.cos(ang2)[:, None, :], jnp.sin(ang2)[:, None, :]
    cos = (c1 * c2 - s1 * s2).reshape(n, m)
    sin = (s1 * c2 + c1 * s2).reshape(n, m)
    t0 = lax.broadcasted_iota(jnp.int32, (n, m), 1) == 0
    sign = (1 - 2 * (f % 2)).astype(F32)
    return jnp.concatenate([jnp.where(t0, 0.5, cos), jnp.where(t0, sign, sin)], axis=1).astype(BF16)


def _fourier_mix_weights(w_four, T):
    c = np.arange(FOUR_DIM)
    ang = 2.0 * np.pi * np.outer(c, c) / FOUR_DIM
    norm = 1.0 / math.sqrt(T * FOUR_DIM)
    gr = jnp.einsum('cf,gfe->gce', jnp.asarray(np.cos(ang) * norm, F32), w_four, precision='highest')
    gi = jnp.einsum('cf,gfe->gce', jnp.asarray(-np.sin(ang) * norm, F32), w_four, precision='highest')
    return jnp.concatenate([_block_diag(gr), _block_diag(gi)], axis=1).astype(BF16)


def _mlstm_kernel(q_ref, k_ref, v_ref, o_ref, g_ref, bg_ref, cf0_ref, mf0_ref, cb0_ref, mb0_ref,
                  y_ref, cf1_ref, mf1_ref, cb1_ref, mb1_ref,
                  hf_ref, hb_ref, pf_ref, pb_ref, uf_ref, ub_ref, rf_ref, rb_ref):
    T = q_ref.shape[1]
    L = MLSTM_CHUNK
    dh = MLSTM_DH
    aug = MLSTM_AUG
    nc = T // L
    head = pl.program_id(1)

    row = lax.broadcasted_iota(jnp.int32, (L, L), 0)
    col = lax.broadcasted_iota(jnp.int32, (L, L), 1)
    masks = (row <= col, row >= col)
    row2 = lax.broadcasted_iota(jnp.int32, (L, 2 * L), 0)
    col2 = jnp.bitwise_and(lax.broadcasted_iota(jnp.int32, (L, 2 * L), 1), L - 1)
    tri2 = jnp.where(col2 <= row2, 1.0, 0.0).astype(BF16)
    diag = row == col
    sub8 = lax.broadcasted_iota(jnp.int32, (SUBLANE, L), 0)
    ones_t = jnp.ones((aug - dh, L), F32)
    sel_row = jnp.bitwise_and(lax.broadcasted_iota(jnp.int32, (2 * LANE, 2 * LANE), 0), LANE - 1)
    sel_dir = jnp.right_shift(lax.broadcasted_iota(jnp.int32, (2 * LANE, 2 * LANE), 1), LANE_SHIFT)
    sel_i = jnp.where(sel_row == sel_dir * (2 * N_MLSTM) + head, 1.0, 0.0).astype(BF16)
    sel_f = jnp.where(sel_row == sel_dir * (2 * N_MLSTM) + N_MLSTM + head, 1.0, 0.0).astype(BF16)
    p_refs, u_refs, r_refs, h_refs = (pf_ref, pb_ref), (uf_ref, ub_ref), (rf_ref, rb_ref), (hf_ref, hb_ref)
    bias = bg_ref[...]

    def prep_group(c0):
        cs = [c0 + j for j in range(min(MLSTM_GROUP, nc))]
        rs = [pl.multiple_of(c * L, L) for c in cs]
        q = [q_ref[0, pl.ds(r, L), :] for r in rs]
        k = [k_ref[0, pl.ds(r, L), :] for r in rs]
        vt = [jnp.concatenate([v_ref[0, pl.ds(r, L), :].astype(F32).T, ones_t], axis=0).astype(BF16) for r in rs]
        g = [(g_ref[0, pl.ds(r, L), :] + bias) * LOG2_E for r in rs]
        i_rep = [jnp.dot(jnp.concatenate(_split_hi_lo(gg), axis=1), sel_i, preferred_element_type=F32)
                 for gg in g]
        lf_rep = [jnp.dot(jnp.concatenate(_split_hi_lo(_log2_sigmoid(gg)), axis=1), sel_f,
                          preferred_element_type=F32) for gg in g]
        st = [lax.dot_general(kk, qq, (((1,), (1,)), ((), ())), preferred_element_type=F32)
              for kk, qq in zip(k, q)]
        cum = [jnp.dot(tri2, jnp.concatenate(_split_hi_lo(f), axis=0), preferred_element_type=F32)
               for f in lf_rep]
        items = [(j, d) for j in range(len(cs)) for d in range(2)]
        for j, d in items:
            if d == 0:
                b = cum[j][:, :L]
                total = b[L - 1:L, :]
            else:
                total = cum[j][L - 1:L, L:]
                b = total - cum[j][:, L:] + lf_rep[j][:, L:]
            r_rep = i_rep[j][:, d * L:(d + 1) * L] - b
            dmt = jnp.where(masks[d], r_rep, NEG_BIG)
            pm = jnp.max(dmt, axis=0, keepdims=True)
            sg = (st[j] * jnp.exp2(dmt - pm)).astype(BF16)
            b_row = jnp.sum(jnp.where(diag, b, 0.0), axis=0, keepdims=True)
            g_end = total + r_rep
            g_loc = jnp.max(g_end, axis=0, keepdims=True)
            kw = (k[j].astype(F32) * jnp.exp2(g_end - g_loc)).astype(BF16)
            base = pl.multiple_of(cs[j] * aug, BF16_ROWS)
            pu = jnp.dot(vt[j], jnp.concatenate([sg, kw], axis=1), preferred_element_type=F32)
            p_refs[d][pl.ds(base, aug), :] = pu[:, :L]
            u_refs[d][pl.ds(base, aug), :] = pu[:, L:]
            rbase = pl.multiple_of(cs[j] * SUBLANE, SUBLANE)
            r_refs[d][pl.ds(rbase, SUBLANE), :] = jnp.where(
                sub8 == 0, b_row, jnp.where(sub8 == 1, b_row + pm, jnp.where(sub8 == 2, total, g_loc)))

    ga = min(MLSTM_GROUP, nc)

    def body_a(i, carry):
        prep_group(i * ga)
        return carry

    lax.fori_loop(0, nc // ga, body_a, 0)

    def step(d, c, ct, m):
        base = pl.multiple_of(c * aug, BF16_ROWS)
        rows = r_refs[d][pl.ds(pl.multiple_of(c * SUBLANE, SUBLANE), SUBLANE), :]
        b_row, a_row, total, g_loc = rows[0:1], rows[1:2], rows[2:3], rows[3:4]
        inter = b_row + m
        m_j = jnp.maximum(inter, a_row)
        qct = lax.dot_general(ct.astype(BF16), q_ref[0, pl.ds(pl.multiple_of(c * L, L), L), :],
                              (((1,), (1,)), ((), ())), preferred_element_type=F32)
        nd = jnp.exp2(inter - m_j) * qct + jnp.exp2(a_row - m_j) * p_refs[d][pl.ds(base, aug), :]
        den = jnp.maximum(jnp.abs(nd[dh:dh + 1]), jnp.exp2(-m_j))
        h_refs[d][pl.ds(pl.multiple_of(c * dh, dh), dh), :] = nd[:dh] / den
        m_new = jnp.maximum(total + m, g_loc)
        ct_new = jnp.exp2(total + m - m_new) * ct + jnp.exp2(g_loc - m_new) * u_refs[d][pl.ds(base, aug), :]
        return ct_new, m_new

    gb = min(nc, MLSTM_GROUP)

    def body_b(i, carry):
        cf, mf, cb, mb = carry
        for j in range(gb):
            cf, mf = step(0, i * gb + j, cf, mf)
            cb, mb = step(1, nc - 1 - (i * gb + j), cb, mb)
        return cf, mf, cb, mb

    cf, mf, cb, mb = lax.fori_loop(
        0, nc // gb, body_b, (cf0_ref[0, 0], mf0_ref[0, 0][0:1, :], cb0_ref[0, 0], mb0_ref[0, 0][0:1, :]))

    cf1_ref[0, 0] = cf
    cb1_ref[0, 0] = cb
    mf1_ref[0, 0] = jnp.broadcast_to(mf, (SUBLANE, LANE))
    mb1_ref[0, 0] = jnp.broadcast_to(mb, (SUBLANE, LANE))

    go = min(nc, MLSTM_GROUP)

    def body_out(i, carry):
        for j in range(go):
            c = i * go + j
            r = pl.multiple_of(c * L, L)
            rh = pl.multiple_of(c * dh, dh)
            h = (hf_ref[pl.ds(rh, dh), :] + hb_ref[pl.ds(rh, dh), :]).T
            y_ref[0, pl.ds(r, L), :] = (_sigmoid(o_ref[0, pl.ds(r, L), :].astype(F32)) * h).astype(y_ref.dtype)
        return carry

    lax.fori_loop(0, nc // go, body_out, 0)


def _mlstm(z, zg, b_gate, cf0, mf0, cb0, mb0):
    B, T, _ = z.shape
    H, dh, L, aug = N_MLSTM, MLSTM_DH, MLSTM_CHUNK, MLSTM_AUG
    nc = T // L
    zcol = lambda off: (lambda b, h: (b, 0, off // dh + h))
    st = lambda b, h: (b, h, 0, 0)
    c_spec = pl.BlockSpec((1, 1, aug, dh), st)
    m_spec = pl.BlockSpec((1, 1, SUBLANE, LANE), st)
    c_shape = jax.ShapeDtypeStruct((B, H, aug, dh), F32)
    m_shape = jax.ShapeDtypeStruct((B, H, SUBLANE, LANE), F32)
    return pl.pallas_call(
        _mlstm_kernel,
        grid=(B, H),
        in_specs=[
            pl.BlockSpec((1, T, dh), zcol(OFF_Q)),
            pl.BlockSpec((1, T, dh), zcol(OFF_K)),
            pl.BlockSpec((1, T, dh), zcol(OFF_V)),
            pl.BlockSpec((1, T, dh), zcol(OFF_O)),
            pl.BlockSpec((1, T, LANE), lambda b, h: (b, 0, 0)),
            pl.BlockSpec((1, LANE), lambda b, h: (0, 0)),
            c_spec, m_spec, c_spec, m_spec,
        ],
        out_specs=[pl.BlockSpec((1, T, dh), lambda b, h: (b, 0, h)), c_spec, m_spec, c_spec, m_spec],
        out_shape=[jax.ShapeDtypeStruct((B, T, MIX_MLSTM), BF16), c_shape, m_shape, c_shape, m_shape],
        scratch_shapes=[
            pltpu.VMEM((nc * dh, L), F32),
            pltpu.VMEM((nc * dh, L), F32),
            pltpu.VMEM((nc * aug, L), F32),
            pltpu.VMEM((nc * aug, L), F32),
            pltpu.VMEM((nc * aug, dh), F32),
            pltpu.VMEM((nc * aug, dh), F32),
            pltpu.VMEM((nc * SUBLANE, LANE), F32),
            pltpu.VMEM((nc * SUBLANE, LANE), F32),
        ],
        compiler_params=_params(2),
        name="mlstm",
    )(z, z, z, z, zg, b_gate, cf0, mf0, cb0, mb0)


def kernel(x, c, ctx, c_ctx, w_mod, b_mod, w_in, conv_qk, b_gates, pool_w, pool_scale, fourier_w,
           w_out, ln1_g, ln1_b, w_up, conv_ffn_w, conv_ffn_b, w_down, ln2_g, ln2_b):
    B, T, D = x.shape
    Tc = ctx.shape[1]
    assert D == D_MODEL and ctx.shape == (B, Tc, D) and w_mod.shape[0] == DEPTH
    assert T % (2 * POOL_BLOCK) == 0 and T % MLSTM_CHUNK == 0 and (T // GRID_W) * GRID_W == T
    assert Tc == POOL_BLOCK and Tc % MLSTM_CHUNK == 0
    tm_x = min(T, 1024)
    assert T % (2 * tm_x) == 0 or T == tm_x
    tm_c = Tc
    xc = ctx
    cond = jnp.concatenate([c, c_ctx[None], jnp.zeros((-(B + 1) % SUBLANE, D), F32)], axis=0)
    mods = _modulation(cond, w_mod, b_mod)
    filt_x, filt_c = _pool_filters(True), _pool_filters(False)
    cnt_x, cnt_c = _pool_inv_count(T, True), _pool_inv_count(Tc, False)
    dft_x, dft_c = _dft_matrix(T), _dft_matrix(Tc)
    zero_c = jnp.zeros((B, N_MLSTM, MLSTM_AUG, MLSTM_DH), F32)
    zero_m = jnp.zeros((B, N_MLSTM, SUBLANE, LANE), F32)
    for l in range(DEPTH):
        last = l == DEPTH - 1
        mx = [m[:, None, :] for m in jnp.split(mods[l, :B], 6, axis=-1)]
        mc = [jnp.broadcast_to(m[None, None, :], (B, 1, D)) for m in jnp.split(mods[l, B], 6, axis=-1)]
        w_main = w_in[l][:, :OFF_G].astype(BF16)
        w_gate = jnp.pad(w_in[l][:, OFF_G:], ((0, 0), (0, LANE - N_GATES))).astype(BF16)
        b_gate = jnp.pad(b_gates[l], (0, LANE - N_GATES))[None]
        w_pool_bd = _block_diag(pool_w[l]).astype(BF16)
        s_pool = pool_scale[l][None]
        w_out_b = w_out[l].astype(BF16)
        w_up_b = w_up[l].astype(BF16)
        w_down_b = w_down[l].astype(BF16)
        g1, b1 = ln1_g[l][None], ln1_b[l][None]
        g2, b2 = ln2_g[l][None], ln2_b[l][None]
        bcv = conv_ffn_b[l][None]

        zc, zgc = _proj_in(xc, mc[0], mc[1], w_main, w_gate, conv_qk[l], tm_c)
        ymc, cf, mf, cb, mb = _mlstm(zc, zgc, b_gate, zero_c, zero_m, zero_c, zero_m)
        zx, zgx = _proj_in(x, mx[0], mx[1], w_main, w_gate, conv_qk[l], tm_x)
        ymx = _mlstm(zx, zgx, b_gate, cf, mf, cb, mb)[0]
        ypx = _pool(zx, filt_x, cnt_x, w_pool_bd, s_pool, True)
        yfx = _fourier(zx, _fourier_mix_weights(fourier_w[l], T), dft_x)
        x = _proj_out(ypx, yfx, ymx, x, mx[2], w_out_b, g1, b1, min(T, 2 * tm_x))
        x = _ffn(x, mx[3], mx[4], mx[5], w_up_b, conv_ffn_w[l], bcv, w_down_b, g2, b2, tm_x)
        if not last:
            ypc = _pool(zc, filt_c, cnt_c, w_pool_bd, s_pool, False)
            yfc = _fourier(zc, _fourier_mix_weights(fourier_w[l], Tc), dft_c)
            xc = _proj_out(ypc, yfc, ymc, xc, mc[2], w_out_b, g1, b1, tm_c)
            xc = _ffn(xc, mc[3], mc[4], mc[5], w_up_b, conv_ffn_w[l], bcv, w_down_b, g2, b2, tm_c)
    return x
```

```python
import functools
import math

import jax
import jax.numpy as jnp
import numpy as np
from jax import lax
from jax.experimental import pallas as pl
from jax.experimental.pallas import tpu as pltpu

F32 = jnp.float32
BF16 = jnp.bfloat16

D_MODEL = 1024
DEPTH = 4
GRID_W = 64
GRID_SHIFT = GRID_W.bit_length() - 1
POOL_WINDOWS = (2, 4, 8, 16)
N_POOL = len(POOL_WINDOWS)
MIX_POOL = D_MODEL // 4
POOL_DIM = MIX_POOL // N_POOL
N_FOUR = 4
MIX_FOUR = D_MODEL // 4
FOUR_DIM = MIX_FOUR // N_FOUR
N_MLSTM = 4
MIX_MLSTM = D_MODEL // 2
MLSTM_DH = MIX_MLSTM // N_MLSTM
MLSTM_CHUNK = 128
N_GATES = 2 * 2 * N_MLSTM
D_FF = int(math.ceil(8 * D_MODEL / 3 / 128)) * 128
OFF_FOUR = MIX_POOL
OFF_O = OFF_FOUR + MIX_FOUR
OFF_Q = OFF_O + MIX_MLSTM
OFF_K = OFF_Q + MIX_MLSTM
OFF_V = OFF_K + MIX_MLSTM
OFF_G = OFF_V + MIX_MLSTM
ALPHA = (2 * DEPTH) ** 0.25
LN_EPS = 1e-6
LOG2_E = math.log2(math.e)

LANE = 128
LANE_SHIFT = LANE.bit_length() - 1
SUBLANE = 8
BF16_ROWS = 16
VMEM_LIMIT = 56 * 1024 * 1024
FF_CHUNK = 256
PROJ_CHUNK = 512
POOL_BLOCK = 256
POOL_UNROLL = 8
FOUR_MIRROR_BLOCK = 256
POOL_PAD = (max(POOL_WINDOWS) // 2) * GRID_W
NEG_BIG = -1e30
MLSTM_GROUP = 8
MLSTM_PREP_GROUP = 16
MLSTM_AUG = MLSTM_DH + BF16_ROWS


def _params(n_axes):
    return pltpu.CompilerParams(dimension_semantics=("arbitrary",) * n_axes, vmem_limit_bytes=VMEM_LIMIT)


def _ln_rows(x):
    mu = jnp.mean(x, axis=-1, keepdims=True)
    xc = x - mu
    var = jnp.mean(xc * xc, axis=-1, keepdims=True)
    return xc * lax.rsqrt(var + LN_EPS)


def _sigmoid(x):
    return 0.5 * jnp.tanh(0.5 * x) + 0.5


def _log2_sigmoid(x2):
    return jnp.minimum(x2, 0.0) - jnp.log2(1.0 + jnp.exp2(-jnp.abs(x2)))


def _split_hi_lo(x):
    hi = x.astype(BF16)
    return hi, (x - hi.astype(F32)).astype(BF16)


def _mod_kernel(c_ref, w_ref, b_ref, o_ref):
    cond = c_ref[...]
    s = cond * _sigmoid(cond)
    o_ref[0] = jnp.dot(s, w_ref[0], preferred_element_type=F32, precision=lax.Precision.HIGHEST) + b_ref[0]


def _modulation(cond, w_mod, b_mod):
    R, D = cond.shape
    L, _, N = w_mod.shape
    tn = N // 4
    return pl.pallas_call(
        _mod_kernel,
        grid=(L, N // tn),
        in_specs=[
            pl.BlockSpec((R, D), lambda l, j: (0, 0)),
            pl.BlockSpec((1, D, tn), lambda l, j: (l, 0, j)),
            pl.BlockSpec((1, 1, tn), lambda l, j: (l, 0, j)),
        ],
        out_specs=pl.BlockSpec((1, R, tn), lambda l, j: (l, 0, j)),
        out_shape=jax.ShapeDtypeStruct((L, R, N), F32),
        compiler_params=_params(2),
        name="modulation",
    )(cond, w_mod, b_mod[:, None, :])


def _proj_in_kernel(x_ref, xp_ref, xn_ref, sh_ref, sc_ref, w_ref, wg_ref, wc_ref, z_ref, g_ref, h_ref):
    i = pl.program_id(1)
    tm = x_ref.shape[1]
    mod_scale = 1.0 + sc_ref[0]
    mod_shift = sh_ref[0]
    u = _ln_rows(x_ref[0]) * mod_scale + mod_shift
    u_prev = (_ln_rows(xp_ref[0]) * mod_scale + mod_shift) * (i > 0).astype(F32)
    u_next = (_ln_rows(xn_ref[0]) * mod_scale + mod_shift) * (i < pl.num_programs(1) - 1).astype(F32)
    ub = u.astype(BF16)
    u_ext = jnp.concatenate([u_prev, u, u_next], axis=0).astype(BF16)
    n = z_ref.shape[-1]
    for c0 in range(0, n, PROJ_CHUNK):
        c1 = min(c0 + PROJ_CHUNK, n)
        if OFF_Q <= c0 and c1 <= OFF_V:
            h_ref[...] = jnp.dot(u_ext, w_ref[:, c0:c1], preferred_element_type=F32)
            w = wc_ref[:, c0 - OFF_Q:c1 - OFF_Q]
            y = (h_ref[SUBLANE - 1:SUBLANE - 1 + tm, :] * w[0:1] + h_ref[SUBLANE:SUBLANE + tm, :] * w[1:2]
                 + h_ref[SUBLANE + 1:SUBLANE + 1 + tm, :] * w[2:3])
            scale = 1.0 if c0 < OFF_K else MLSTM_DH ** -0.5
            z_ref[0, :, c0:c1] = (y * _sigmoid(y) * scale).astype(z_ref.dtype)
        else:
            z_ref[0, :, c0:c1] = jnp.dot(ub, w_ref[:, c0:c1], preferred_element_type=F32).astype(z_ref.dtype)
    g_ref[0] = jnp.dot(ub, wg_ref[...], preferred_element_type=F32)


def _proj_in(x, shift, scale, w_main, w_gate, w_conv, tm):
    B, T, D = x.shape
    n = w_main.shape[1]
    assert OFF_Q % PROJ_CHUNK == 0 and OFF_K % PROJ_CHUNK == 0 and OFF_V % PROJ_CHUNK == 0
    nb = tm // SUBLANE
    last = T // SUBLANE - 1
    return pl.pallas_call(
        _proj_in_kernel,
        grid=(B, T // tm),
        in_specs=[
            pl.BlockSpec((1, tm, D), lambda b, i: (b, i, 0)),
            pl.BlockSpec((1, SUBLANE, D), lambda b, i: (b, jnp.maximum(i * nb - 1, 0), 0)),
            pl.BlockSpec((1, SUBLANE, D), lambda b, i: (b, jnp.minimum((i + 1) * nb, last), 0)),
            pl.BlockSpec((1, 1, D), lambda b, i: (b, 0, 0)),
            pl.BlockSpec((1, 1, D), lambda b, i: (b, 0, 0)),
            pl.BlockSpec((D, n), lambda b, i: (0, 0)),
            pl.BlockSpec((D, LANE), lambda b, i: (0, 0)),
            pl.BlockSpec((3, 2 * MIX_MLSTM), lambda b, i: (0, 0)),
        ],
        out_specs=[
            pl.BlockSpec((1, tm, n), lambda b, i: (b, i, 0)),
            pl.BlockSpec((1, tm, LANE), lambda b, i: (b, i, 0)),
        ],
        out_shape=[
            jax.ShapeDtypeStruct((B, T, n), BF16),
            jax.ShapeDtypeStruct((B, T, LANE), F32),
        ],
        scratch_shapes=[pltpu.VMEM((tm + 2 * SUBLANE, PROJ_CHUNK), F32)],
        compiler_params=_params(2),
        name="proj_in",
    )(x, x, x, shift, scale, w_main, w_gate, w_conv)


def _proj_out_kernel(yp_ref, yf_ref, ym_ref, x_ref, gate_ref, wp_ref, wf_ref, wm_ref, lg_ref, lb_ref, o_ref):
    f = (jnp.dot(yp_ref[0], wp_ref[...], preferred_element_type=F32)
         + jnp.dot(yf_ref[0], wf_ref[...], preferred_element_type=F32)
         + jnp.dot(ym_ref[0], wm_ref[...], preferred_element_type=F32))
    r = ALPHA * x_ref[0] + gate_ref[0] * f
    o_ref[0] = _ln_rows(r) * lg_ref[...] + lb_ref[...]


def _proj_out(yp, yf, ym, x, gate, w, ln_g, ln_b, tm):
    B, T, D = x.shape
    tok = lambda b, i: (b, i, 0)
    return pl.pallas_call(
        _proj_out_kernel,
        grid=(B, T // tm),
        in_specs=[
            pl.BlockSpec((1, tm, MIX_POOL), tok),
            pl.BlockSpec((1, tm, MIX_FOUR), tok),
            pl.BlockSpec((1, tm, MIX_MLSTM), tok),
            pl.BlockSpec((1, tm, D), tok),
            pl.BlockSpec((1, 1, D), lambda b, i: (b, 0, 0)),
            pl.BlockSpec((MIX_POOL, D), lambda b, i: (0, 0)),
            pl.BlockSpec((MIX_FOUR, D), lambda b, i: (OFF_FOUR // MIX_FOUR, 0)),
            pl.BlockSpec((MIX_MLSTM, D), lambda b, i: (OFF_O // MIX_MLSTM, 0)),
            pl.BlockSpec((1, D), lambda b, i: (0, 0)),
            pl.BlockSpec((1, D), lambda b, i: (0, 0)),
        ],
        out_specs=pl.BlockSpec((1, tm, D), tok),
        out_shape=jax.ShapeDtypeStruct((B, T, D), F32),
        compiler_params=_params(2),
        name="proj_out",
    )(yp, yf, ym, x, gate, w, w, w, ln_g, ln_b)


def _gelu_tanh_times(x, half_a):
    c = math.sqrt(2.0 / math.pi)
    inner = x * (c + (c * 0.044715) * (x * x))
    return (x * (1.0 + jnp.tanh(inner))) * half_a


def _ffn_kernel(x_ref, xp_ref, xn_ref, sh_ref, sc_ref, gate_ref, wup_ref, wc_ref, bc_ref, wdn_ref,
                lg_ref, lb_ref, o_ref, ha_ref, hg_ref, p_ref):
    i = pl.program_id(1)
    tm = x_ref.shape[1]
    x = x_ref[0]
    mod_scale = 1.0 + sc_ref[0]
    mod_shift = sh_ref[0]
    has_prev = (i > 0).astype(F32)
    has_next = (i < pl.num_programs(1) - 1).astype(F32)
    u_prev = (_ln_rows(xp_ref[0]) * mod_scale + mod_shift) * has_prev
    u_next = (_ln_rows(xn_ref[0]) * mod_scale + mod_shift) * has_next
    u_main = _ln_rows(x) * mod_scale + mod_shift
    u_ext = jnp.concatenate([u_prev, u_main, u_next], axis=0).astype(BF16)

    def conv(h_ref, cols):
        w = wc_ref[:, cols]
        return (h_ref[SUBLANE - 1:SUBLANE - 1 + tm, :] * w[0:1]
                + h_ref[SUBLANE:SUBLANE + tm, :] * w[1:2]
                + h_ref[SUBLANE + 1:SUBLANE + 1 + tm, :] * w[2:3]
                + bc_ref[:, cols])

    for c in range(D_FF // FF_CHUNK):
        ca = slice(c * FF_CHUNK, (c + 1) * FF_CHUNK)
        cg = slice(D_FF + c * FF_CHUNK, D_FF + (c + 1) * FF_CHUNK)
        ha_ref[...] = jnp.dot(u_ext, wup_ref[:, ca], preferred_element_type=F32)
        hg_ref[...] = jnp.dot(u_ext, wup_ref[:, cg], preferred_element_type=F32)
        p_ref[:, ca] = _gelu_tanh_times(conv(hg_ref, cg), conv(ha_ref, ca)).astype(BF16)

    f = jnp.dot(p_ref[...], wdn_ref[...], preferred_element_type=F32)
    r = ALPHA * x + gate_ref[0] * f
    o_ref[0] = _ln_rows(r) * lg_ref[...] + lb_ref[...]


def _ffn(x, shift, scale, gate, w_up, w_conv, b_conv, w_down, ln_g, ln_b, tm):
    B, T, D = x.shape
    nb = tm // SUBLANE
    last = T // SUBLANE - 1
    const = dict(pipeline_mode=pl.Buffered(1))
    half_value = jnp.concatenate([jnp.full((D_FF,), 0.5, F32), jnp.ones((D_FF,), F32)])
    w_conv = w_conv * half_value
    b_conv = b_conv * half_value
    return pl.pallas_call(
        _ffn_kernel,
        grid=(B, T // tm),
        in_specs=[
            pl.BlockSpec((1, tm, D), lambda b, i: (b, i, 0)),
            pl.BlockSpec((1, SUBLANE, D), lambda b, i: (b, jnp.maximum(i * nb - 1, 0), 0)),
            pl.BlockSpec((1, SUBLANE, D), lambda b, i: (b, jnp.minimum((i + 1) * nb, last), 0)),
            pl.BlockSpec((1, 1, D), lambda b, i: (b, 0, 0)),
            pl.BlockSpec((1, 1, D), lambda b, i: (b, 0, 0)),
            pl.BlockSpec((1, 1, D), lambda b, i: (b, 0, 0)),
            pl.BlockSpec((D, 2 * D_FF), lambda b, i: (0, 0), **const),
            pl.BlockSpec((3, 2 * D_FF), lambda b, i: (0, 0), **const),
            pl.BlockSpec((1, 2 * D_FF), lambda b, i: (0, 0), **const),
            pl.BlockSpec((D_FF, D), lambda b, i: (0, 0), **const),
            pl.BlockSpec((1, D), lambda b, i: (0, 0)),
            pl.BlockSpec((1, D), lambda b, i: (0, 0)),
        ],
        out_specs=pl.BlockSpec((1, tm, D), lambda b, i: (b, i, 0)),
        out_shape=jax.ShapeDtypeStruct((B, T, D), F32),
        scratch_shapes=[
            pltpu.VMEM((tm + 2 * SUBLANE, FF_CHUNK), F32),
            pltpu.VMEM((tm + 2 * SUBLANE, FF_CHUNK), F32),
            pltpu.VMEM((tm, D_FF), BF16),
        ],
        compiler_params=_params(2),
        name="conv_ffn",
    )(x, x, x, shift, scale, gate, w_up, w_conv, b_conv, w_down, ln_g, ln_b)


def _band_matrix(n, w):
    pos = np.arange(n)
    lo = np.maximum(pos - w // 2, 0)
    hi = np.minimum(pos + w // 2 - 1, n - 1)
    return ((pos[None, :] >= lo[:, None]) & (pos[None, :] <= hi[:, None])).astype(np.float32)


def _pool_filters(grid2d):
    mats = []
    for w in POOL_WINDOWS:
        if grid2d:
            mats.append(np.kron(np.eye(POOL_BLOCK // GRID_W, dtype=np.float32), _band_matrix(GRID_W, w)))
        else:
            mats.append(_band_matrix(POOL_BLOCK, w))
    return jnp.asarray(np.stack(mats), BF16)


def _window_count(idx, half, n):
    return jnp.minimum(idx + (half - 1), n - 1) - jnp.maximum(idx - half, 0) + 1


def _pool_inv_count(T, grid2d):
    pos = lax.broadcasted_iota(jnp.int32, (T, MIX_POOL), 0)
    group = jnp.right_shift(lax.broadcasted_iota(jnp.int32, (T, MIX_POOL), 1), GRID_SHIFT)
    half = jnp.left_shift(1, group)
    if grid2d:
        count = (_window_count(jnp.right_shift(pos, GRID_SHIFT), half, T // GRID_W)
                 * _window_count(jnp.bitwise_and(pos, GRID_W - 1), half, GRID_W))
    else:
        count = _window_count(pos, half, T)
    return 1.0 / count.astype(F32)


def _pool_kernel(z_ref, a_ref, n_ref, w_ref, s_ref, y_ref, acc_ref, *, grid2d):
    T = z_ref.shape[1]
    nblk = T // POOL_BLOCK
    lane = lax.broadcasted_iota(jnp.int32, (POOL_BLOCK, MIX_POOL), 1)
    group = jnp.right_shift(lane, GRID_SHIFT)

    def col_filter(blk):
        r0 = pl.multiple_of(blk * POOL_BLOCK, POOL_BLOCK)
        xb = z_ref[0, pl.ds(r0, POOL_BLOCK), :]
        out = jnp.dot(a_ref[0], xb, preferred_element_type=F32)
        for g in range(1, N_POOL):
            out = jnp.where(group == g, jnp.dot(a_ref[g], xb, preferred_element_type=F32), out)
        return r0, xb, out

    def finish(r0, xb, box):
        p = box * n_ref[pl.ds(r0, POOL_BLOCK), :] - xb.astype(F32)
        y = jnp.dot(p.astype(BF16), w_ref[...], preferred_element_type=F32) * s_ref[...]
        y_ref[0, pl.ds(r0, POOL_BLOCK), :] = y.astype(y_ref.dtype)

    if not grid2d:
        assert T == POOL_BLOCK
        finish(*col_filter(0))
        return

    zeros = jnp.zeros((POOL_PAD, MIX_POOL), F32)
    acc_ref[0:POOL_PAD, :] = zeros
    acc_ref[POOL_PAD + T:POOL_PAD + T + POOL_PAD, :] = zeros

    def phase1(blk, carry):
        r0, _, out = col_filter(blk)
        acc_ref[pl.ds(POOL_PAD + r0, POOL_BLOCK), :] = out
        return carry

    lax.fori_loop(0, nblk, phase1, 0, unroll=POOL_UNROLL)

    lane_t = lax.broadcasted_iota(jnp.int32, (POOL_BLOCK, LANE), 1)

    def phase2(blk, carry):
        r0 = pl.multiple_of(blk * POOL_BLOCK, POOL_BLOCK)
        base = POOL_PAD + r0

        def slab(k, tile):
            return acc_ref[pl.ds(base + k * GRID_W, POOL_BLOCK), tile * LANE:(tile + 1) * LANE]

        s2 = slab(-1, 0) + slab(0, 0)
        s4 = s2 + slab(-2, 0) + slab(1, 0)
        s8 = slab(-4, 1)
        for k in range(-3, 4):
            s8 = s8 + slab(k, 1)
        s16 = s8
        for k in list(range(-8, -4)) + list(range(4, 8)):
            s16 = s16 + slab(k, 1)
        box = jnp.concatenate([jnp.where(lane_t < POOL_DIM, s2, s4), jnp.where(lane_t < POOL_DIM, s8, s16)], axis=1)
        finish(r0, z_ref[0, pl.ds(r0, POOL_BLOCK), :], box)
        return carry

    lax.fori_loop(0, nblk, phase2, 0, unroll=POOL_UNROLL)


def _pool(z, filters, inv_count, w_bd, s_row, grid2d):
    B, T, _ = z.shape
    return pl.pallas_call(
        functools.partial(_pool_kernel, grid2d=grid2d),
        grid=(B,),
        in_specs=[
            pl.BlockSpec((1, T, MIX_POOL), lambda b: (b, 0, 0)),
            pl.BlockSpec((N_POOL, POOL_BLOCK, POOL_BLOCK), lambda b: (0, 0, 0)),
            pl.BlockSpec((T, MIX_POOL), lambda b: (0, 0)),
            pl.BlockSpec((MIX_POOL, MIX_POOL), lambda b: (0, 0)),
            pl.BlockSpec((1, MIX_POOL), lambda b: (0, 0)),
        ],
        out_specs=pl.BlockSpec((1, T, MIX_POOL), lambda b: (b, 0, 0)),
        out_shape=jax.ShapeDtypeStruct((B, T, MIX_POOL), BF16),
        scratch_shapes=[pltpu.VMEM((T + 2 * POOL_PAD, MIX_POOL), F32)],
        compiler_params=_params(1),
        name="pool2d" if grid2d else "pool1d",
    )(z, filters, inv_count, w_bd, s_row)


def _block_diag(w):
    G, n, m = w.shape
    eye = jnp.eye(G, dtype=w.dtype)
    return (eye[:, None, :, None] * w[:, :, None, :]).reshape(G * n, G * m)


def _four_mix_kernel(z_ref, za_ref, zb_ref, zm_ref, g_ref, y_ref):
    tm = z_ref.shape[1]
    sb = zb_ref.shape[1]
    nsub = tm // sb
    gr, gi = g_ref[:, :MIX_FOUR], g_ref[:, MIX_FOUR:]
    r = lax.broadcasted_iota(jnp.int32, (sb, 2 * sb), 0)
    c = lax.broadcasted_iota(jnp.int32, (sb, 2 * sb), 1)
    perm = jnp.where(((r >= 1) & (c == sb - r)) | ((r == 0) & (c == sb)), 1.0, 0.0).astype(BF16)
    mid = jnp.dot(zm_ref[0], gr, preferred_element_type=F32)[0:1]
    row0 = lax.broadcasted_iota(jnp.int32, (sb, MIX_FOUR), 0) == 0
    for j in range(nsub):
        lo = (nsub - 1 - j) * sb
        if j == 0:
            src = jnp.concatenate([za_ref[0, lo:lo + sb, :], zb_ref[0]], axis=0)
        else:
            src = za_ref[0, lo:lo + 2 * sb, :]
        xr = jnp.dot(perm, src, preferred_element_type=F32)
        x = z_ref[0, j * sb:(j + 1) * sb, :].astype(F32)
        even = jnp.dot((x + xr).astype(BF16), gr, preferred_element_type=F32)
        odd = jnp.dot((x - xr).astype(BF16), gi, preferred_element_type=F32)
        if j == 0:
            odd = jnp.where(row0 & (pl.program_id(1) == 0), mid, odd)
        y_ref[0, 0, j * sb:(j + 1) * sb, :] = even.astype(y_ref.dtype)
        y_ref[0, 1, j * sb:(j + 1) * sb, :] = odd.astype(y_ref.dtype)


def _four_dft_kernel(d_ref, y_ref, o_ref):
    rows = 256
    for r0 in range(0, o_ref.shape[1], rows):
        o_ref[0, r0:r0 + rows, :] = jnp.dot(d_ref[r0:r0 + rows, :], y_ref[0],
                                            preferred_element_type=F32).astype(o_ref.dtype)


def _fourier(z, g_mix, dft):
    B, T, _ = z.shape
    M = T // 2
    tm = min(M, 2048)
    fcol = OFF_FOUR // MIX_FOUR
    sb = min(tm, FOUR_MIRROR_BLOCK)
    nt, nsb, per = T // tm, T // sb, tm // sb
    y = pl.pallas_call(
        _four_mix_kernel,
        grid=(B, M // tm),
        in_specs=[
            pl.BlockSpec((1, tm, MIX_FOUR), lambda b, i: (b, i, fcol)),
            pl.BlockSpec((1, tm, MIX_FOUR), lambda b, i: (b, nt - 1 - i, fcol)),
            pl.BlockSpec((1, sb, MIX_FOUR), lambda b, i: (b, (nsb - i * per) % nsb, fcol)),
            pl.BlockSpec((1, BF16_ROWS, MIX_FOUR), lambda b, i: (b, M // BF16_ROWS, fcol)),
            pl.BlockSpec((MIX_FOUR, 2 * MIX_FOUR), lambda b, i: (0, 0)),
        ],
        out_specs=pl.BlockSpec((1, 2, tm, MIX_FOUR), lambda b, i: (b, 0, i, 0)),
        out_shape=jax.ShapeDtypeStruct((B, 2, M, MIX_FOUR), BF16),
        compiler_params=_params(2),
        name="fourier_mix",
    )(z, z, z, z, g_mix)
    y = y.reshape(B, T, MIX_FOUR)
    tf = min(T, 2048)
    return pl.pallas_call(
        _four_dft_kernel,
        grid=(T // tf, B),
        in_specs=[
            pl.BlockSpec((tf, T), lambda m, b: (m, 0)),
            pl.BlockSpec((1, T, MIX_FOUR), lambda m, b: (b, 0, 0)),
        ],
        out_specs=pl.BlockSpec((1, tf, MIX_FOUR), lambda m, b: (b, m, 0)),
        out_shape=jax.ShapeDtypeStruct((B, T, MIX_FOUR), BF16),
        compiler_params=_params(2),
        name="fourier_dft",
    )(dft, y)


def _dft_matrix(n):
    m = n // 2
    a = m // GRID_W
    f = lax.broadcasted_iota(jnp.int32, (n, 1), 0)
    ang1 = ((f * GRID_W * lax.broadcasted_iota(jnp.int32, (1, a), 1)) % n).astype(F32) * (2.0 * math.pi / n)
    ang2 = ((f * lax.broadcasted_iota(jnp.int32, (1, GRID_W), 1)) % n).astype(F32) * (2.0 * math.pi / n)
    c1, s1 = jnp.cos(ang1)[:, :, None], jnp.sin(ang1)[:, :, None]
    c2, s2 = jnp.cos(ang2)[:, None, :], jnp.sin(ang2)[:, None, :]
    cos = (c1 * c2 - s1 * s2).reshape(n, m)
    sin = (s1 * c2 + c1 * s2).reshape(n, m)
    t0 = lax.broadcasted_iota(jnp.int32, (n, m), 1) == 0
    sign = (1 - 2 * (f % 2)).astype(F32)
    return jnp.concatenate([jnp.where(t0, 0.5, cos), jnp.where(t0, sign, sin)], axis=1).astype(BF16)


def _fourier_mix_weights(w_four, T):
    c = np.arange(FOUR_DIM)
    ang = 2.0 * np.pi * np.outer(c, c) / FOUR_DIM
    norm = 1.0 / math.sqrt(T * FOUR_DIM)
    gr = jnp.einsum('cf,gfe->gce', jnp.asarray(np.cos(ang) * norm, F32), w_four, precision='highest')
    gi = jnp.einsum('cf,gfe->gce', jnp.asarray(-np.sin(ang) * norm, F32), w_four, precision='highest')
    return jnp.concatenate([_block_diag(gr), _block_diag(gi)], axis=1).astype(BF16)


def _mlstm_kernel(q_ref, k_ref, v_ref, o_ref, g_ref, bg_ref, cf0_ref, mf0_ref, cb0_ref, mb0_ref,
                  y_ref, cf1_ref, mf1_ref, cb1_ref, mb1_ref,
                  hf_ref, hb_ref, pf_ref, pb_ref, uf_ref, ub_ref, rf_ref, rb_ref):
    T = q_ref.shape[1]
    L = MLSTM_CHUNK
    dh = MLSTM_DH
    aug = MLSTM_AUG
    nc = T // L
    head = pl.program_id(1)

    row = lax.broadcasted_iota(jnp.int32, (L, L), 0)
    col = lax.broadcasted_iota(jnp.int32, (L, L), 1)
    masks = (row <= col, row >= col)
    row2 = lax.broadcasted_iota(jnp.int32, (L, 2 * L), 0)
    col2 = jnp.bitwise_and(lax.broadcasted_iota(jnp.int32, (L, 2 * L), 1), L - 1)
    tri2 = jnp.where(col2 <= row2, 1.0, 0.0).astype(BF16)
    diag = row == col
    sub8 = lax.broadcasted_iota(jnp.int32, (SUBLANE, L), 0)
    ones_t = jnp.ones((aug - dh, L), F32)
    sel_row = jnp.bitwise_and(lax.broadcasted_iota(jnp.int32, (2 * LANE, 2 * LANE), 0), LANE - 1)
    sel_dir = jnp.right_shift(lax.broadcasted_iota(jnp.int32, (2 * LANE, 2 * LANE), 1), LANE_SHIFT)
    sel_i = jnp.where(sel_row == sel_dir * (2 * N_MLSTM) + head, 1.0, 0.0).astype(BF16)
    sel_f = jnp.where(sel_row == sel_dir * (2 * N_MLSTM) + N_MLSTM + head, 1.0, 0.0).astype(BF16)
    p_refs, u_refs, r_refs, h_refs = (pf_ref, pb_ref), (uf_ref, ub_ref), (rf_ref, rb_ref), (hf_ref, hb_ref)
    bias = bg_ref[...]

    def prep_group(c0):
        cs = [c0 + j for j in range(min(MLSTM_PREP_GROUP, nc))]
        rs = [pl.multiple_of(c * L, L) for c in cs]
        q = [q_ref[0, pl.ds(r, L), :] for r in rs]
        k = [k_ref[0, pl.ds(r, L), :] for r in rs]
        vt = [jnp.concatenate([v_ref[0, pl.ds(r, L), :].astype(F32).T, ones_t], axis=0).astype(BF16) for r in rs]
        g = [(g_ref[0, pl.ds(r, L), :] + bias) * LOG2_E for r in rs]
        i_rep = [jnp.dot(jnp.concatenate(_split_hi_lo(gg), axis=1), sel_i, preferred_element_type=F32)
                 for gg in g]
        lf_rep = [jnp.dot(jnp.concatenate(_split_hi_lo(_log2_sigmoid(gg)), axis=1), sel_f,
                          preferred_element_type=F32) for gg in g]
        st = [lax.dot_general(kk, qq, (((1,), (1,)), ((), ())), preferred_element_type=F32)
              for kk, qq in zip(k, q)]
        cum = [jnp.dot(tri2, jnp.concatenate(_split_hi_lo(f), axis=0), preferred_element_type=F32)
               for f in lf_rep]
        items = [(j, d) for j in range(len(cs)) for d in range(2)]
        for j, d in items:
            if d == 0:
                b = cum[j][:, :L]
                total = b[L - 1:L, :]
            else:
                total = cum[j][L - 1:L, L:]
                b = total - cum[j][:, L:] + lf_rep[j][:, L:]
            r_rep = i_rep[j][:, d * L:(d + 1) * L] - b
            dmt = jnp.where(masks[d], r_rep, NEG_BIG)
            pm = jnp.max(dmt, axis=0, keepdims=True)
            sg = (st[j] * jnp.exp2(dmt - pm)).astype(BF16)
            b_row = jnp.sum(jnp.where(diag, b, 0.0), axis=0, keepdims=True)
            g_end = total + r_rep
            g_loc = jnp.max(g_end, axis=0, keepdims=True)
            kw = (k[j].astype(F32) * jnp.exp2(g_end - g_loc)).astype(BF16)
            base = pl.multiple_of(cs[j] * aug, BF16_ROWS)
            pu = jnp.dot(vt[j], jnp.concatenate([sg, kw], axis=1), preferred_element_type=F32)
            p_refs[d][pl.ds(base, aug), :] = pu[:, :L]
            u_refs[d][pl.ds(base, aug), :] = pu[:, L:]
            rbase = pl.multiple_of(cs[j] * SUBLANE, SUBLANE)
            r_refs[d][pl.ds(rbase, SUBLANE), :] = jnp.where(
                sub8 == 0, b_row, jnp.where(sub8 == 1, b_row + pm, jnp.where(sub8 == 2, total, g_loc)))

    ga = min(MLSTM_PREP_GROUP, nc)

    def body_a(i, carry):
        prep_group(i * ga)
        return carry

    lax.fori_loop(0, nc // ga, body_a, 0)

    def step(d, c, ct, m):
        base = pl.multiple_of(c * aug, BF16_ROWS)
        rows = r_refs[d][pl.ds(pl.multiple_of(c * SUBLANE, SUBLANE), SUBLANE), :]
        b_row, a_row, total, g_loc = rows[0:1], rows[1:2], rows[2:3], rows[3:4]
        inter = b_row + m
        m_j = jnp.maximum(inter, a_row)
        qct = lax.dot_general(ct.astype(BF16), q_ref[0, pl.ds(pl.multiple_of(c * L, L), L), :],
                              (((1,), (1,)), ((), ())), preferred_element_type=F32)
        nd = jnp.exp2(inter - m_j) * qct + jnp.exp2(a_row - m_j) * p_refs[d][pl.ds(base, aug), :]
        den = jnp.maximum(jnp.abs(nd[dh:dh + 1]), jnp.exp2(-m_j))
        h_refs[d][pl.ds(pl.multiple_of(c * dh, dh), dh), :] = nd[:dh] / den
        m_new = jnp.maximum(total + m, g_loc)
        ct_new = jnp.exp2(total + m - m_new) * ct + jnp.exp2(g_loc - m_new) * u_refs[d][pl.ds(base, aug), :]
        return ct_new, m_new

    gb = min(nc, MLSTM_GROUP)

    def body_b(i, carry):
        cf, mf, cb, mb = carry
        for j in range(gb):
            cf, mf = step(0, i * gb + j, cf, mf)
            cb, mb = step(1, nc - 1 - (i * gb + j), cb, mb)
        return cf, mf, cb, mb

    cf, mf, cb, mb = lax.fori_loop(
        0, nc // gb, body_b, (cf0_ref[0, 0], mf0_ref[0, 0][0:1, :], cb0_ref[0, 0], mb0_ref[0, 0][0:1, :]))

    cf1_ref[0, 0] = cf
    cb1_ref[0, 0] = cb
    mf1_ref[0, 0] = jnp.broadcast_to(mf, (SUBLANE, LANE))
    mb1_ref[0, 0] = jnp.broadcast_to(mb, (SUBLANE, LANE))

    go = min(nc, MLSTM_GROUP)

    def body_out(i, carry):
        for j in range(go):
            c = i * go + j
            r = pl.multiple_of(c * L, L)
            rh = pl.multiple_of(c * dh, dh)
            h = (hf_ref[pl.ds(rh, dh), :] + hb_ref[pl.ds(rh, dh), :]).T
            y_ref[0, pl.ds(r, L), :] = (_sigmoid(o_ref[0, pl.ds(r, L), :].astype(F32)) * h).astype(y_ref.dtype)
        return carry

    lax.fori_loop(0, nc // go, body_out, 0)


def _mlstm(z, zg, b_gate, cf0, mf0, cb0, mb0):
    B, T, _ = z.shape
    H, dh, L, aug = N_MLSTM, MLSTM_DH, MLSTM_CHUNK, MLSTM_AUG
    nc = T // L
    zcol = lambda off: (lambda b, h: (b, 0, off // dh + h))
    st = lambda b, h: (b, h, 0, 0)
    c_spec = pl.BlockSpec((1, 1, aug, dh), st)
    m_spec = pl.BlockSpec((1, 1, SUBLANE, LANE), st)
    c_shape = jax.ShapeDtypeStruct((B, H, aug, dh), F32)
    m_shape = jax.ShapeDtypeStruct((B, H, SUBLANE, LANE), F32)
    return pl.pallas_call(
        _mlstm_kernel,
        grid=(B, H),
        in_specs=[
            pl.BlockSpec((1, T, dh), zcol(OFF_Q)),
            pl.BlockSpec((1, T, dh), zcol(OFF_K)),
            pl.BlockSpec((1, T, dh), zcol(OFF_V)),
            pl.BlockSpec((1, T, dh), zcol(OFF_O)),
            pl.BlockSpec((1, T, LANE), lambda b, h: (b, 0, 0)),
            pl.BlockSpec((1, LANE), lambda b, h: (0, 0)),
            c_spec, m_spec, c_spec, m_spec,
        ],
        out_specs=[pl.BlockSpec((1, T, dh), lambda b, h: (b, 0, h)), c_spec, m_spec, c_spec, m_spec],
        out_shape=[jax.ShapeDtypeStruct((B, T, MIX_MLSTM), BF16), c_shape, m_shape, c_shape, m_shape],
        scratch_shapes=[
            pltpu.VMEM((nc * dh, L), F32),
            pltpu.VMEM((nc * dh, L), F32),
            pltpu.VMEM((nc * aug, L), F32),
            pltpu.VMEM((nc * aug, L), F32),
            pltpu.VMEM((nc * aug, dh), F32),
            pltpu.VMEM((nc * aug, dh), F32),
            pltpu.VMEM((nc * SUBLANE, LANE), F32),
            pltpu.VMEM((nc * SUBLANE, LANE), F32),
        ],
        compiler_params=_params(2),
        name="mlstm",
    )(z, z, z, z, zg, b_gate, cf0, mf0, cb0, mb0)


def kernel(x, c, ctx, c_ctx, w_mod, b_mod, w_in, conv_qk, b_gates, pool_w, pool_scale, fourier_w,
           w_out, ln1_g, ln1_b, w_up, conv_ffn_w, conv_ffn_b, w_down, ln2_g, ln2_b):
    B, T, D = x.shape
    Tc = ctx.shape[1]
    assert D == D_MODEL and ctx.shape == (B, Tc, D) and w_mod.shape[0] == DEPTH
    assert T % (2 * POOL_BLOCK) == 0 and T % MLSTM_CHUNK == 0 and (T // GRID_W) * GRID_W == T
    assert Tc == POOL_BLOCK and Tc % MLSTM_CHUNK == 0
    tm_x = min(T, 1024)
    assert T % (2 * tm_x) == 0 or T == tm_x
    tm_c = Tc
    xc = ctx
    cond = jnp.concatenate([c, c_ctx[None], jnp.zeros((-(B + 1) % SUBLANE, D), F32)], axis=0)
    mods = _modulation(cond, w_mod, b_mod)
    filt_x, filt_c = _pool_filters(True), _pool_filters(False)
    cnt_x, cnt_c = _pool_inv_count(T, True), _pool_inv_count(Tc, False)
    dft_x, dft_c = _dft_matrix(T), _dft_matrix(Tc)
    zero_c = jnp.zeros((B, N_MLSTM, MLSTM_AUG, MLSTM_DH), F32)
    zero_m = jnp.zeros((B, N_MLSTM, SUBLANE, LANE), F32)
    for l in range(DEPTH):
        last = l == DEPTH - 1
        mx = [m[:, None, :] for m in jnp.split(mods[l, :B], 6, axis=-1)]
        mc = [jnp.broadcast_to(m[None, None, :], (B, 1, D)) for m in jnp.split(mods[l, B], 6, axis=-1)]
        w_main = w_in[l][:, :OFF_G].astype(BF16)
        w_gate = jnp.pad(w_in[l][:, OFF_G:], ((0, 0), (0, LANE - N_GATES))).astype(BF16)
        b_gate = jnp.pad(b_gates[l], (0, LANE - N_GATES))[None]
        w_pool_bd = _block_diag(pool_w[l]).astype(BF16)
        s_pool = pool_scale[l][None]
        w_out_b = w_out[l].astype(BF16)
        w_up_b = w_up[l].astype(BF16)
        w_down_b = w_down[l].astype(BF16)
        g1, b1 = ln1_g[l][None], ln1_b[l][None]
        g2, b2 = ln2_g[l][None], ln2_b[l][None]
        bcv = conv_ffn_b[l][None]

        zc, zgc = _proj_in(xc, mc[0], mc[1], w_main, w_gate, conv_qk[l], tm_c)
        ymc, cf, mf, cb, mb = _mlstm(zc, zgc, b_gate, zero_c, zero_m, zero_c, zero_m)
        zx, zgx = _proj_in(x, mx[0], mx[1], w_main, w_gate, conv_qk[l], tm_x)
        ymx = _mlstm(zx, zgx, b_gate, cf, mf, cb, mb)[0]
        ypx = _pool(zx, filt_x, cnt_x, w_pool_bd, s_pool, True)
        yfx = _fourier(zx, _fourier_mix_weights(fourier_w[l], T), dft_x)
        x = _proj_out(ypx, yfx, ymx, x, mx[2], w_out_b, g1, b1, min(T, 2 * tm_x))
        x = _ffn(x, mx[3], mx[4], mx[5], w_up_b, conv_ffn_w[l], bcv, w_down_b, g2, b2, tm_x)
        if not last:
            ypc = _pool(zc, filt_c, cnt_c, w_pool_bd, s_pool, False)
            yfc = _fourier(zc, _fourier_mix_weights(fourier_w[l], Tc), dft_c)
            xc = _proj_out(ypc, yfc, ymc, xc, mc[2], w_out_b, g1, b1, tm_c)
            xc = _ffn(xc, mc[3], mc[4], mc[5], w_up_b, conv_ffn_w[l], bcv, w_down_b, g2, b2, tm_c)
    return x
```

```python
import functools
import math

import jax
import jax.numpy as jnp
import numpy as np
from jax import lax
from jax.experimental import pallas as pl
from jax.experimental.pallas import tpu as pltpu

F32 = jnp.float32
BF16 = jnp.bfloat16

D_MODEL = 1024
DEPTH = 4
GRID_W = 64
GRID_SHIFT = GRID_W.bit_length() - 1
POOL_WINDOWS = (2, 4, 8, 16)
N_POOL = len(POOL_WINDOWS)
MIX_POOL = D_MODEL // 4
POOL_DIM = MIX_POOL // N_POOL
N_FOUR = 4
MIX_FOUR = D_MODEL // 4
FOUR_DIM = MIX_FOUR // N_FOUR
N_MLSTM = 4
MIX_MLSTM = D_MODEL // 2
MLSTM_DH = MIX_MLSTM // N_MLSTM
MLSTM_CHUNK = 128
N_GATES = 2 * 2 * N_MLSTM
D_FF = int(math.ceil(8 * D_MODEL / 3 / 128)) * 128
OFF_FOUR = MIX_POOL
OFF_O = OFF_FOUR + MIX_FOUR
OFF_Q = OFF_O + MIX_MLSTM
OFF_K = OFF_Q + MIX_MLSTM
OFF_V = OFF_K + MIX_MLSTM
OFF_G = OFF_V + MIX_MLSTM
ALPHA = (2 * DEPTH) ** 0.25
LN_EPS = 1e-6
LOG2_E = math.log2(math.e)

LANE = 128
LANE_SHIFT = LANE.bit_length() - 1
SUBLANE = 8
BF16_ROWS = 16
VMEM_LIMIT = 56 * 1024 * 1024
FF_CHUNK = 256
PROJ_CHUNK = 512
POOL_BLOCK = 256
POOL_UNROLL = 8
FOUR_MIRROR_BLOCK = 256
POOL_PAD = (max(POOL_WINDOWS) // 2) * GRID_W
NEG_BIG = -1e30
MLSTM_GROUP = 16
MLSTM_PREP_GROUP = 32
MLSTM_AUG = MLSTM_DH + BF16_ROWS


def _params(n_axes):
    return pltpu.CompilerParams(dimension_semantics=("arbitrary",) * n_axes, vmem_limit_bytes=VMEM_LIMIT)


def _ln_rows(x):
    mu = jnp.mean(x, axis=-1, keepdims=True)
    xc = x - mu
    var = jnp.mean(xc * xc, axis=-1, keepdims=True)
    return xc * lax.rsqrt(var + LN_EPS)


def _sigmoid(x):
    return 0.5 * jnp.tanh(0.5 * x) + 0.5


def _log2_sigmoid(x2):
    return jnp.minimum(x2, 0.0) - jnp.log2(1.0 + jnp.exp2(-jnp.abs(x2)))


def _split_hi_lo(x):
    hi = x.astype(BF16)
    return hi, (x - hi.astype(F32)).astype(BF16)


def _mod_kernel(c_ref, w_ref, b_ref, o_ref):
    cond = c_ref[...]
    s = cond * _sigmoid(cond)
    o_ref[0] = jnp.dot(s, w_ref[0], preferred_element_type=F32, precision=lax.Precision.HIGHEST) + b_ref[0]


def _modulation(cond, w_mod, b_mod):
    R, D = cond.shape
    L, _, N = w_mod.shape
    tn = N // 4
    return pl.pallas_call(
        _mod_kernel,
        grid=(L, N // tn),
        in_specs=[
            pl.BlockSpec((R, D), lambda l, j: (0, 0)),
            pl.BlockSpec((1, D, tn), lambda l, j: (l, 0, j)),
            pl.BlockSpec((1, 1, tn), lambda l, j: (l, 0, j)),
        ],
        out_specs=pl.BlockSpec((1, R, tn), lambda l, j: (l, 0, j)),
        out_shape=jax.ShapeDtypeStruct((L, R, N), F32),
        compiler_params=_params(2),
        name="modulation",
    )(cond, w_mod, b_mod[:, None, :])


def _proj_in_kernel(x_ref, xp_ref, xn_ref, sh_ref, sc_ref, w_ref, wg_ref, wc_ref, z_ref, g_ref, h_ref):
    i = pl.program_id(1)
    tm = x_ref.shape[1]
    mod_scale = 1.0 + sc_ref[0]
    mod_shift = sh_ref[0]
    u = _ln_rows(x_ref[0]) * mod_scale + mod_shift
    u_prev = (_ln_rows(xp_ref[0]) * mod_scale + mod_shift) * (i > 0).astype(F32)
    u_next = (_ln_rows(xn_ref[0]) * mod_scale + mod_shift) * (i < pl.num_programs(1) - 1).astype(F32)
    ub = u.astype(BF16)
    u_ext = jnp.concatenate([u_prev, u, u_next], axis=0).astype(BF16)
    n = z_ref.shape[-1]
    for c0 in range(0, n, PROJ_CHUNK):
        c1 = min(c0 + PROJ_CHUNK, n)
        if OFF_Q <= c0 and c1 <= OFF_V:
            h_ref[...] = jnp.dot(u_ext, w_ref[:, c0:c1], preferred_element_type=F32)
            w = wc_ref[:, c0 - OFF_Q:c1 - OFF_Q]
            y = (h_ref[SUBLANE - 1:SUBLANE - 1 + tm, :] * w[0:1] + h_ref[SUBLANE:SUBLANE + tm, :] * w[1:2]
                 + h_ref[SUBLANE + 1:SUBLANE + 1 + tm, :] * w[2:3])
            scale = 1.0 if c0 < OFF_K else MLSTM_DH ** -0.5
            z_ref[0, :, c0:c1] = (y * _sigmoid(y) * scale).astype(z_ref.dtype)
        else:
            z_ref[0, :, c0:c1] = jnp.dot(ub, w_ref[:, c0:c1], preferred_element_type=F32).astype(z_ref.dtype)
    g_ref[0] = jnp.dot(ub, wg_ref[...], preferred_element_type=F32)


def _proj_in(x, shift, scale, w_main, w_gate, w_conv, tm):
    B, T, D = x.shape
    n = w_main.shape[1]
    assert OFF_Q % PROJ_CHUNK == 0 and OFF_K % PROJ_CHUNK == 0 and OFF_V % PROJ_CHUNK == 0
    nb = tm // SUBLANE
    last = T // SUBLANE - 1
    return pl.pallas_call(
        _proj_in_kernel,
        grid=(B, T // tm),
        in_specs=[
            pl.BlockSpec((1, tm, D), lambda b, i: (b, i, 0)),
            pl.BlockSpec((1, SUBLANE, D), lambda b, i: (b, jnp.maximum(i * nb - 1, 0), 0)),
            pl.BlockSpec((1, SUBLANE, D), lambda b, i: (b, jnp.minimum((i + 1) * nb, last), 0)),
            pl.BlockSpec((1, 1, D), lambda b, i: (b, 0, 0)),
            pl.BlockSpec((1, 1, D), lambda b, i: (b, 0, 0)),
            pl.BlockSpec((D, n), lambda b, i: (0, 0)),
            pl.BlockSpec((D, LANE), lambda b, i: (0, 0)),
            pl.BlockSpec((3, 2 * MIX_MLSTM), lambda b, i: (0, 0)),
        ],
        out_specs=[
            pl.BlockSpec((1, tm, n), lambda b, i: (b, i, 0)),
            pl.BlockSpec((1, tm, LANE), lambda b, i: (b, i, 0)),
        ],
        out_shape=[
            jax.ShapeDtypeStruct((B, T, n), BF16),
            jax.ShapeDtypeStruct((B, T, LANE), F32),
        ],
        scratch_shapes=[pltpu.VMEM((tm + 2 * SUBLANE, PROJ_CHUNK), F32)],
        compiler_params=_params(2),
        name="proj_in",
    )(x, x, x, shift, scale, w_main, w_gate, w_conv)


def _proj_out_kernel(yp_ref, yf_ref, ym_ref, x_ref, gate_ref, wp_ref, wf_ref, wm_ref, lg_ref, lb_ref, o_ref):
    f = (jnp.dot(yp_ref[0], wp_ref[...], preferred_element_type=F32)
         + jnp.dot(yf_ref[0], wf_ref[...], preferred_element_type=F32)
         + jnp.dot(ym_ref[0], wm_ref[...], preferred_element_type=F32))
    r = ALPHA * x_ref[0] + gate_ref[0] * f
    o_ref[0] = _ln_rows(r) * lg_ref[...] + lb_ref[...]


def _proj_out(yp, yf, ym, x, gate, w, ln_g, ln_b, tm):
    B, T, D = x.shape
    tok = lambda b, i: (b, i, 0)
    return pl.pallas_call(
        _proj_out_kernel,
        grid=(B, T // tm),
        in_specs=[
            pl.BlockSpec((1, tm, MIX_POOL), tok),
            pl.BlockSpec((1, tm, MIX_FOUR), tok),
            pl.BlockSpec((1, tm, MIX_MLSTM), tok),
            pl.BlockSpec((1, tm, D), tok),
            pl.BlockSpec((1, 1, D), lambda b, i: (b, 0, 0)),
            pl.BlockSpec((MIX_POOL, D), lambda b, i: (0, 0)),
            pl.BlockSpec((MIX_FOUR, D), lambda b, i: (OFF_FOUR // MIX_FOUR, 0)),
            pl.BlockSpec((MIX_MLSTM, D), lambda b, i: (OFF_O // MIX_MLSTM, 0)),
            pl.BlockSpec((1, D), lambda b, i: (0, 0)),
            pl.BlockSpec((1, D), lambda b, i: (0, 0)),
        ],
        out_specs=pl.BlockSpec((1, tm, D), tok),
        out_shape=jax.ShapeDtypeStruct((B, T, D), F32),
        compiler_params=_params(2),
        name="proj_out",
    )(yp, yf, ym, x, gate, w, w, w, ln_g, ln_b)


def _gelu_tanh_times(x, half_a):
    c = math.sqrt(2.0 / math.pi)
    inner = x * (c + (c * 0.044715) * (x * x))
    return (x * (1.0 + jnp.tanh(inner))) * half_a


def _ffn_kernel(x_ref, xp_ref, xn_ref, sh_ref, sc_ref, gate_ref, wup_ref, wc_ref, bc_ref, wdn_ref,
                lg_ref, lb_ref, o_ref, ha_ref, hg_ref, p_ref):
    i = pl.program_id(1)
    tm = x_ref.shape[1]
    x = x_ref[0]
    mod_scale = 1.0 + sc_ref[0]
    mod_shift = sh_ref[0]
    has_prev = (i > 0).astype(F32)
    has_next = (i < pl.num_programs(1) - 1).astype(F32)
    u_prev = (_ln_rows(xp_ref[0]) * mod_scale + mod_shift) * has_prev
    u_next = (_ln_rows(xn_ref[0]) * mod_scale + mod_shift) * has_next
    u_main = _ln_rows(x) * mod_scale + mod_shift
    u_ext = jnp.concatenate([u_prev, u_main, u_next], axis=0).astype(BF16)

    def conv(h_ref, cols):
        w = wc_ref[:, cols]
        return (h_ref[SUBLANE - 1:SUBLANE - 1 + tm, :] * w[0:1]
                + h_ref[SUBLANE:SUBLANE + tm, :] * w[1:2]
                + h_ref[SUBLANE + 1:SUBLANE + 1 + tm, :] * w[2:3]
                + bc_ref[:, cols])

    for c in range(D_FF // FF_CHUNK):
        ca = slice(c * FF_CHUNK, (c + 1) * FF_CHUNK)
        cg = slice(D_FF + c * FF_CHUNK, D_FF + (c + 1) * FF_CHUNK)
        ha_ref[...] = jnp.dot(u_ext, wup_ref[:, ca], preferred_element_type=F32)
        hg_ref[...] = jnp.dot(u_ext, wup_ref[:, cg], preferred_element_type=F32)
        p_ref[:, ca] = _gelu_tanh_times(conv(hg_ref, cg), conv(ha_ref, ca)).astype(BF16)

    f = jnp.dot(p_ref[...], wdn_ref[...], preferred_element_type=F32)
    r = ALPHA * x + gate_ref[0] * f
    o_ref[0] = _ln_rows(r) * lg_ref[...] + lb_ref[...]


def _ffn(x, shift, scale, gate, w_up, w_conv, b_conv, w_down, ln_g, ln_b, tm):
    B, T, D = x.shape
    nb = tm // SUBLANE
    last = T // SUBLANE - 1
    const = dict(pipeline_mode=pl.Buffered(1))
    half_value = jnp.concatenate([jnp.full((D_FF,), 0.5, F32), jnp.ones((D_FF,), F32)])
    w_conv = w_conv * half_value
    b_conv = b_conv * half_value
    return pl.pallas_call(
        _ffn_kernel,
        grid=(B, T // tm),
        in_specs=[
            pl.BlockSpec((1, tm, D), lambda b, i: (b, i, 0)),
            pl.BlockSpec((1, SUBLANE, D), lambda b, i: (b, jnp.maximum(i * nb - 1, 0), 0)),
            pl.BlockSpec((1, SUBLANE, D), lambda b, i: (b, jnp.minimum((i + 1) * nb, last), 0)),
            pl.BlockSpec((1, 1, D), lambda b, i: (b, 0, 0)),
            pl.BlockSpec((1, 1, D), lambda b, i: (b, 0, 0)),
            pl.BlockSpec((1, 1, D), lambda b, i: (b, 0, 0)),
            pl.BlockSpec((D, 2 * D_FF), lambda b, i: (0, 0), **const),
            pl.BlockSpec((3, 2 * D_FF), lambda b, i: (0, 0), **const),
            pl.BlockSpec((1, 2 * D_FF), lambda b, i: (0, 0), **const),
            pl.BlockSpec((D_FF, D), lambda b, i: (0, 0), **const),
            pl.BlockSpec((1, D), lambda b, i: (0, 0)),
            pl.BlockSpec((1, D), lambda b, i: (0, 0)),
        ],
        out_specs=pl.BlockSpec((1, tm, D), lambda b, i: (b, i, 0)),
        out_shape=jax.ShapeDtypeStruct((B, T, D), F32),
        scratch_shapes=[
            pltpu.VMEM((tm + 2 * SUBLANE, FF_CHUNK), F32),
            pltpu.VMEM((tm + 2 * SUBLANE, FF_CHUNK), F32),
            pltpu.VMEM((tm, D_FF), BF16),
        ],
        compiler_params=_params(2),
        name="conv_ffn",
    )(x, x, x, shift, scale, gate, w_up, w_conv, b_conv, w_down, ln_g, ln_b)


def _band_matrix(n, w):
    pos = np.arange(n)
    lo = np.maximum(pos - w // 2, 0)
    hi = np.minimum(pos + w // 2 - 1, n - 1)
    return ((pos[None, :] >= lo[:, None]) & (pos[None, :] <= hi[:, None])).astype(np.float32)


def _pool_filters(grid2d):
    mats = []
    for w in POOL_WINDOWS:
        if grid2d:
            mats.append(np.kron(np.eye(POOL_BLOCK // GRID_W, dtype=np.float32), _band_matrix(GRID_W, w)))
        else:
            mats.append(_band_matrix(POOL_BLOCK, w))
    return jnp.asarray(np.stack(mats), BF16)


def _window_count(idx, half, n):
    return jnp.minimum(idx + (half - 1), n - 1) - jnp.maximum(idx - half, 0) + 1


def _pool_inv_count(T, grid2d):
    pos = lax.broadcasted_iota(jnp.int32, (T, MIX_POOL), 0)
    group = jnp.right_shift(lax.broadcasted_iota(jnp.int32, (T, MIX_POOL), 1), GRID_SHIFT)
    half = jnp.left_shift(1, group)
    if grid2d:
        count = (_window_count(jnp.right_shift(pos, GRID_SHIFT), half, T // GRID_W)
                 * _window_count(jnp.bitwise_and(pos, GRID_W - 1), half, GRID_W))
    else:
        count = _window_count(pos, half, T)
    return 1.0 / count.astype(F32)


def _pool_kernel(z_ref, a_ref, n_ref, w_ref, s_ref, y_ref, acc_ref, *, grid2d):
    T = z_ref.shape[1]
    nblk = T // POOL_BLOCK
    lane = lax.broadcasted_iota(jnp.int32, (POOL_BLOCK, MIX_POOL), 1)
    group = jnp.right_shift(lane, GRID_SHIFT)

    def col_filter(blk):
        r0 = pl.multiple_of(blk * POOL_BLOCK, POOL_BLOCK)
        xb = z_ref[0, pl.ds(r0, POOL_BLOCK), :]
        out = jnp.dot(a_ref[0], xb, preferred_element_type=F32)
        for g in range(1, N_POOL):
            out = jnp.where(group == g, jnp.dot(a_ref[g], xb, preferred_element_type=F32), out)
        return r0, xb, out

    def finish(r0, xb, box):
        p = box * n_ref[pl.ds(r0, POOL_BLOCK), :] - xb.astype(F32)
        y = jnp.dot(p.astype(BF16), w_ref[...], preferred_element_type=F32) * s_ref[...]
        y_ref[0, pl.ds(r0, POOL_BLOCK), :] = y.astype(y_ref.dtype)

    if not grid2d:
        assert T == POOL_BLOCK
        finish(*col_filter(0))
        return

    zeros = jnp.zeros((POOL_PAD, MIX_POOL), F32)
    acc_ref[0:POOL_PAD, :] = zeros
    acc_ref[POOL_PAD + T:POOL_PAD + T + POOL_PAD, :] = zeros

    def phase1(blk, carry):
        r0, _, out = col_filter(blk)
        acc_ref[pl.ds(POOL_PAD + r0, POOL_BLOCK), :] = out
        return carry

    lax.fori_loop(0, nblk, phase1, 0, unroll=POOL_UNROLL)

    lane_t = lax.broadcasted_iota(jnp.int32, (POOL_BLOCK, LANE), 1)

    def phase2(blk, carry):
        r0 = pl.multiple_of(blk * POOL_BLOCK, POOL_BLOCK)
        base = POOL_PAD + r0

        def slab(k, tile):
            return acc_ref[pl.ds(base + k * GRID_W, POOL_BLOCK), tile * LANE:(tile + 1) * LANE]

        s2 = slab(-1, 0) + slab(0, 0)
        s4 = s2 + slab(-2, 0) + slab(1, 0)
        s8 = slab(-4, 1)
        for k in range(-3, 4):
            s8 = s8 + slab(k, 1)
        s16 = s8
        for k in list(range(-8, -4)) + list(range(4, 8)):
            s16 = s16 + slab(k, 1)
        box = jnp.concatenate([jnp.where(lane_t < POOL_DIM, s2, s4), jnp.where(lane_t < POOL_DIM, s8, s16)], axis=1)
        finish(r0, z_ref[0, pl.ds(r0, POOL_BLOCK), :], box)
        return carry

    lax.fori_loop(0, nblk, phase2, 0, unroll=POOL_UNROLL)


def _pool(z, filters, inv_count, w_bd, s_row, grid2d):
    B, T, _ = z.shape
    return pl.pallas_call(
        functools.partial(_pool_kernel, grid2d=grid2d),
        grid=(B,),
        in_specs=[
            pl.BlockSpec((1, T, MIX_POOL), lambda b: (b, 0, 0)),
            pl.BlockSpec((N_POOL, POOL_BLOCK, POOL_BLOCK), lambda b: (0, 0, 0)),
            pl.BlockSpec((T, MIX_POOL), lambda b: (0, 0)),
            pl.BlockSpec((MIX_POOL, MIX_POOL), lambda b: (0, 0)),
            pl.BlockSpec((1, MIX_POOL), lambda b: (0, 0)),
        ],
        out_specs=pl.BlockSpec((1, T, MIX_POOL), lambda b: (b, 0, 0)),
        out_shape=jax.ShapeDtypeStruct((B, T, MIX_POOL), BF16),
        scratch_shapes=[pltpu.VMEM((T + 2 * POOL_PAD, MIX_POOL), F32)],
        compiler_params=_params(1),
        name="pool2d" if grid2d else "pool1d",
    )(z, filters, inv_count, w_bd, s_row)


def _block_diag(w):
    G, n, m = w.shape
    eye = jnp.eye(G, dtype=w.dtype)
    return (eye[:, None, :, None] * w[:, :, None, :]).reshape(G * n, G * m)


def _four_mix_kernel(z_ref, za_ref, zb_ref, zm_ref, g_ref, y_ref):
    tm = z_ref.shape[1]
    sb = zb_ref.shape[1]
    nsub = tm // sb
    gr, gi = g_ref[:, :MIX_FOUR], g_ref[:, MIX_FOUR:]
    r = lax.broadcasted_iota(jnp.int32, (sb, 2 * sb), 0)
    c = lax.broadcasted_iota(jnp.int32, (sb, 2 * sb), 1)
    perm = jnp.where(((r >= 1) & (c == sb - r)) | ((r == 0) & (c == sb)), 1.0, 0.0).astype(BF16)
    mid = jnp.dot(zm_ref[0], gr, preferred_element_type=F32)[0:1]
    row0 = lax.broadcasted_iota(jnp.int32, (sb, MIX_FOUR), 0) == 0
    for j in range(nsub):
        lo = (nsub - 1 - j) * sb
        if j == 0:
            src = jnp.concatenate([za_ref[0, lo:lo + sb, :], zb_ref[0]], axis=0)
        else:
            src = za_ref[0, lo:lo + 2 * sb, :]
        xr = jnp.dot(perm, src, preferred_element_type=F32)
        x = z_ref[0, j * sb:(j + 1) * sb, :].astype(F32)
        even = jnp.dot((x + xr).astype(BF16), gr, preferred_element_type=F32)
        odd = jnp.dot((x - xr).astype(BF16), gi, preferred_element_type=F32)
        if j == 0:
            odd = jnp.where(row0 & (pl.program_id(1) == 0), mid, odd)
        y_ref[0, 0, j * sb:(j + 1) * sb, :] = even.astype(y_ref.dtype)
        y_ref[0, 1, j * sb:(j + 1) * sb, :] = odd.astype(y_ref.dtype)


def _four_dft_kernel(d_ref, y_ref, o_ref):
    rows = 256
    for r0 in range(0, o_ref.shape[1], rows):
        o_ref[0, r0:r0 + rows, :] = jnp.dot(d_ref[r0:r0 + rows, :], y_ref[0],
                                            preferred_element_type=F32).astype(o_ref.dtype)


def _fourier(z, g_mix, dft):
    B, T, _ = z.shape
    M = T // 2
    tm = min(M, 2048)
    fcol = OFF_FOUR // MIX_FOUR
    sb = min(tm, FOUR_MIRROR_BLOCK)
    nt, nsb, per = T // tm, T // sb, tm // sb
    y = pl.pallas_call(
        _four_mix_kernel,
        grid=(B, M // tm),
        in_specs=[
            pl.BlockSpec((1, tm, MIX_FOUR), lambda b, i: (b, i, fcol)),
            pl.BlockSpec((1, tm, MIX_FOUR), lambda b, i: (b, nt - 1 - i, fcol)),
            pl.BlockSpec((1, sb, MIX_FOUR), lambda b, i: (b, (nsb - i * per) % nsb, fcol)),
            pl.BlockSpec((1, BF16_ROWS, MIX_FOUR), lambda b, i: (b, M // BF16_ROWS, fcol)),
            pl.BlockSpec((MIX_FOUR, 2 * MIX_FOUR), lambda b, i: (0, 0)),
        ],
        out_specs=pl.BlockSpec((1, 2, tm, MIX_FOUR), lambda b, i: (b, 0, i, 0)),
        out_shape=jax.ShapeDtypeStruct((B, 2, M, MIX_FOUR), BF16),
        compiler_params=_params(2),
        name="fourier_mix",
    )(z, z, z, z, g_mix)
    y = y.reshape(B, T, MIX_FOUR)
    tf = min(T, 2048)
    return pl.pallas_call(
        _four_dft_kernel,
        grid=(T // tf, B),
        in_specs=[
            pl.BlockSpec((tf, T), lambda m, b: (m, 0)),
            pl.BlockSpec((1, T, MIX_FOUR), lambda m, b: (b, 0, 0)),
        ],
        out_specs=pl.BlockSpec((1, tf, MIX_FOUR), lambda m, b: (b, m, 0)),
        out_shape=jax.ShapeDtypeStruct((B, T, MIX_FOUR), BF16),
        compiler_params=_params(2),
        name="fourier_dft",
    )(dft, y)


def _dft_matrix(n):
    m = n // 2
    a = m // GRID_W
    f = lax.broadcasted_iota(jnp.int32, (n, 1), 0)
    ang1 = ((f * GRID_W * lax.broadcasted_iota(jnp.int32, (1, a), 1)) % n).astype(F32) * (2.0 * math.pi / n)
    ang2 = ((f * lax.broadcasted_iota(jnp.int32, (1, GRID_W), 1)) % n).astype(F32) * (2.0 * math.pi / n)
    c1, s1 = jnp.cos(ang1)[:, :, None], jnp.sin(ang1)[:, :, None]
    c2, s2 = jnp.cos(ang2)[:, None, :], jnp.sin(ang2)[:, None, :]
    cos = (c1 * c2 - s1 * s2).reshape(n, m)
    sin = (s1 * c2 + c1 * s2).reshape(n, m)
    t0 = lax.broadcasted_iota(jnp.int32, (n, m), 1) == 0
    sign = (1 - 2 * (f % 2)).astype(F32)
    return jnp.concatenate([jnp.where(t0, 0.5, cos), jnp.where(t0, sign, sin)], axis=1).astype(BF16)


def _fourier_mix_weights(w_four, T):
    c = np.arange(FOUR_DIM)
    ang = 2.0 * np.pi * np.outer(c, c) / FOUR_DIM
    norm = 1.0 / math.sqrt(T * FOUR_DIM)
    gr = jnp.einsum('cf,gfe->gce', jnp.asarray(np.cos(ang) * norm, F32), w_four, precision='highest')
    gi = jnp.einsum('cf,gfe->gce', jnp.asarray(-np.sin(ang) * norm, F32), w_four, precision='highest')
    return jnp.concatenate([_block_diag(gr), _block_diag(gi)], axis=1).astype(BF16)


def _mlstm_kernel(q_ref, k_ref, v_ref, o_ref, g_ref, bg_ref, cf0_ref, mf0_ref, cb0_ref, mb0_ref,
                  y_ref, cf1_ref, mf1_ref, cb1_ref, mb1_ref,
                  hf_ref, hb_ref, pf_ref, pb_ref, uf_ref, ub_ref, rf_ref, rb_ref):
    T = q_ref.shape[1]
    L = MLSTM_CHUNK
    dh = MLSTM_DH
    aug = MLSTM_AUG
    nc = T // L
    head = pl.program_id(1)

    row = lax.broadcasted_iota(jnp.int32, (L, L), 0)
    col = lax.broadcasted_iota(jnp.int32, (L, L), 1)
    masks = (row <= col, row >= col)
    row2 = lax.broadcasted_iota(jnp.int32, (L, 2 * L), 0)
    col2 = jnp.bitwise_and(lax.broadcasted_iota(jnp.int32, (L, 2 * L), 1), L - 1)
    tri2 = jnp.where(col2 <= row2, 1.0, 0.0).astype(BF16)
    diag = row == col
    sub8 = lax.broadcasted_iota(jnp.int32, (SUBLANE, L), 0)
    ones_t = jnp.ones((aug - dh, L), F32)
    sel_row = jnp.bitwise_and(lax.broadcasted_iota(jnp.int32, (2 * LANE, 2 * LANE), 0), LANE - 1)
    sel_dir = jnp.right_shift(lax.broadcasted_iota(jnp.int32, (2 * LANE, 2 * LANE), 1), LANE_SHIFT)
    sel_i = jnp.where(sel_row == sel_dir * (2 * N_MLSTM) + head, 1.0, 0.0).astype(BF16)
    sel_f = jnp.where(sel_row == sel_dir * (2 * N_MLSTM) + N_MLSTM + head, 1.0, 0.0).astype(BF16)
    p_refs, u_refs, r_refs, h_refs = (pf_ref, pb_ref), (uf_ref, ub_ref), (rf_ref, rb_ref), (hf_ref, hb_ref)
    bias = bg_ref[...]

    def prep_group(c0):
        cs = [c0 + j for j in range(min(MLSTM_PREP_GROUP, nc))]
        rs = [pl.multiple_of(c * L, L) for c in cs]
        q = [q_ref[0, pl.ds(r, L), :] for r in rs]
        k = [k_ref[0, pl.ds(r, L), :] for r in rs]
        vt = [jnp.concatenate([v_ref[0, pl.ds(r, L), :].astype(F32).T, ones_t], axis=0).astype(BF16) for r in rs]
        g = [(g_ref[0, pl.ds(r, L), :] + bias) * LOG2_E for r in rs]
        i_rep = [jnp.dot(jnp.concatenate(_split_hi_lo(gg), axis=1), sel_i, preferred_element_type=F32)
                 for gg in g]
        lf_rep = [jnp.dot(jnp.concatenate(_split_hi_lo(_log2_sigmoid(gg)), axis=1), sel_f,
                          preferred_element_type=F32) for gg in g]
        st = [lax.dot_general(kk, qq, (((1,), (1,)), ((), ())), preferred_element_type=F32)
              for kk, qq in zip(k, q)]
        cum = [jnp.dot(tri2, jnp.concatenate(_split_hi_lo(f), axis=0), preferred_element_type=F32)
               for f in lf_rep]
        items = [(j, d) for j in range(len(cs)) for d in range(2)]
        for j, d in items:
            if d == 0:
                b = cum[j][:, :L]
                total = b[L - 1:L, :]
            else:
                total = cum[j][L - 1:L, L:]
                b = total - cum[j][:, L:] + lf_rep[j][:, L:]
            r_rep = i_rep[j][:, d * L:(d + 1) * L] - b
            dmt = jnp.where(masks[d], r_rep, NEG_BIG)
            pm = jnp.max(dmt, axis=0, keepdims=True)
            sg = (st[j] * jnp.exp2(dmt - pm)).astype(BF16)
            b_row = jnp.sum(jnp.where(diag, b, 0.0), axis=0, keepdims=True)
            g_end = total + r_rep
            g_loc = jnp.max(g_end, axis=0, keepdims=True)
            kw = (k[j].astype(F32) * jnp.exp2(g_end - g_loc)).astype(BF16)
            base = pl.multiple_of(cs[j] * aug, BF16_ROWS)
            pu = jnp.dot(vt[j], jnp.concatenate([sg, kw], axis=1), preferred_element_type=F32)
            p_refs[d][pl.ds(base, aug), :] = pu[:, :L]
            u_refs[d][pl.ds(base, aug), :] = pu[:, L:]
            rbase = pl.multiple_of(cs[j] * SUBLANE, SUBLANE)
            r_refs[d][pl.ds(rbase, SUBLANE), :] = jnp.where(
                sub8 == 0, b_row, jnp.where(sub8 == 1, b_row + pm, jnp.where(sub8 == 2, total, g_loc)))

    ga = min(MLSTM_PREP_GROUP, nc)

    def body_a(i, carry):
        prep_group(i * ga)
        return carry

    lax.fori_loop(0, nc // ga, body_a, 0)

    def step(d, c, ct, m):
        base = pl.multiple_of(c * aug, BF16_ROWS)
        rows = r_refs[d][pl.ds(pl.multiple_of(c * SUBLANE, SUBLANE), SUBLANE), :]
        b_row, a_row, total, g_loc = rows[0:1], rows[1:2], rows[2:3], rows[3:4]
        inter = b_row + m
        m_j = jnp.maximum(inter, a_row)
        qct = lax.dot_general(ct.astype(BF16), q_ref[0, pl.ds(pl.multiple_of(c * L, L), L), :],
                              (((1,), (1,)), ((), ())), preferred_element_type=F32)
        nd = jnp.exp2(inter - m_j) * qct + jnp.exp2(a_row - m_j) * p_refs[d][pl.ds(base, aug), :]
        den = jnp.maximum(jnp.abs(nd[dh:dh + 1]), jnp.exp2(-m_j))
        h_refs[d][pl.ds(pl.multiple_of(c * dh, dh), dh), :] = nd[:dh] / den
        m_new = jnp.maximum(total + m, g_loc)
        ct_new = jnp.exp2(total + m - m_new) * ct + jnp.exp2(g_loc - m_new) * u_refs[d][pl.ds(base, aug), :]
        return ct_new, m_new

    gb = min(nc, MLSTM_GROUP)

    def body_b(i, carry):
        cf, mf, cb, mb = carry
        for j in range(gb):
            cf, mf = step(0, i * gb + j, cf, mf)
            cb, mb = step(1, nc - 1 - (i * gb + j), cb, mb)
        return cf, mf, cb, mb

    cf, mf, cb, mb = lax.fori_loop(
        0, nc // gb, body_b, (cf0_ref[0, 0], mf0_ref[0, 0][0:1, :], cb0_ref[0, 0], mb0_ref[0, 0][0:1, :]))

    cf1_ref[0, 0] = cf
    cb1_ref[0, 0] = cb
    mf1_ref[0, 0] = jnp.broadcast_to(mf, (SUBLANE, LANE))
    mb1_ref[0, 0] = jnp.broadcast_to(mb, (SUBLANE, LANE))

    go = min(nc, MLSTM_GROUP)

    def body_out(i, carry):
        for j in range(go):
            c = i * go + j
            r = pl.multiple_of(c * L, L)
            rh = pl.multiple_of(c * dh, dh)
            h = (hf_ref[pl.ds(rh, dh), :] + hb_ref[pl.ds(rh, dh), :]).T
            y_ref[0, pl.ds(r, L), :] = (_sigmoid(o_ref[0, pl.ds(r, L), :].astype(F32)) * h).astype(y_ref.dtype)
        return carry

    lax.fori_loop(0, nc // go, body_out, 0)


def _mlstm(z, zg, b_gate, cf0, mf0, cb0, mb0):
    B, T, _ = z.shape
    H, dh, L, aug = N_MLSTM, MLSTM_DH, MLSTM_CHUNK, MLSTM_AUG
    nc = T // L
    zcol = lambda off: (lambda b, h: (b, 0, off // dh + h))
    st = lambda b, h: (b, h, 0, 0)
    c_spec = pl.BlockSpec((1, 1, aug, dh), st)
    m_spec = pl.BlockSpec((1, 1, SUBLANE, LANE), st)
    c_shape = jax.ShapeDtypeStruct((B, H, aug, dh), F32)
    m_shape = jax.ShapeDtypeStruct((B, H, SUBLANE, LANE), F32)
    return pl.pallas_call(
        _mlstm_kernel,
        grid=(B, H),
        in_specs=[
            pl.BlockSpec((1, T, dh), zcol(OFF_Q)),
            pl.BlockSpec((1, T, dh), zcol(OFF_K)),
            pl.BlockSpec((1, T, dh), zcol(OFF_V)),
            pl.BlockSpec((1, T, dh), zcol(OFF_O)),
            pl.BlockSpec((1, T, LANE), lambda b, h: (b, 0, 0)),
            pl.BlockSpec((1, LANE), lambda b, h: (0, 0)),
            c_spec, m_spec, c_spec, m_spec,
        ],
        out_specs=[pl.BlockSpec((1, T, dh), lambda b, h: (b, 0, h)), c_spec, m_spec, c_spec, m_spec],
        out_shape=[jax.ShapeDtypeStruct((B, T, MIX_MLSTM), BF16), c_shape, m_shape, c_shape, m_shape],
        scratch_shapes=[
            pltpu.VMEM((nc * dh, L), F32),
            pltpu.VMEM((nc * dh, L), F32),
            pltpu.VMEM((nc * aug, L), F32),
            pltpu.VMEM((nc * aug, L), F32),
            pltpu.VMEM((nc * aug, dh), F32),
            pltpu.VMEM((nc * aug, dh), F32),
            pltpu.VMEM((nc * SUBLANE, LANE), F32),
            pltpu.VMEM((nc * SUBLANE, LANE), F32),
        ],
        compiler_params=_params(2),
        name="mlstm",
    )(z, z, z, z, zg, b_gate, cf0, mf0, cb0, mb0)


def kernel(x, c, ctx, c_ctx, w_mod, b_mod, w_in, conv_qk, b_gates, pool_w, pool_scale, fourier_w,
           w_out, ln1_g, ln1_b, w_up, conv_ffn_w, conv_ffn_b, w_down, ln2_g, ln2_b):
    B, T, D = x.shape
    Tc = ctx.shape[1]
    assert D == D_MODEL and ctx.shape == (B, Tc, D) and w_mod.shape[0] == DEPTH
    assert T % (2 * POOL_BLOCK) == 0 and T % MLSTM_CHUNK == 0 and (T // GRID_W) * GRID_W == T
    assert Tc == POOL_BLOCK and Tc % MLSTM_CHUNK == 0
    tm_x = min(T, 1024)
    assert T % (2 * tm_x) == 0 or T == tm_x
    tm_c = Tc
    xc = ctx
    cond = jnp.concatenate([c, c_ctx[None], jnp.zeros((-(B + 1) % SUBLANE, D), F32)], axis=0)
    mods = _modulation(cond, w_mod, b_mod)
    filt_x, filt_c = _pool_filters(True), _pool_filters(False)
    cnt_x, cnt_c = _pool_inv_count(T, True), _pool_inv_count(Tc, False)
    dft_x, dft_c = _dft_matrix(T), _dft_matrix(Tc)
    zero_c = jnp.zeros((B, N_MLSTM, MLSTM_AUG, MLSTM_DH), F32)
    zero_m = jnp.zeros((B, N_MLSTM, SUBLANE, LANE), F32)
    for l in range(DEPTH):
        last = l == DEPTH - 1
        mx = [m[:, None, :] for m in jnp.split(mods[l, :B], 6, axis=-1)]
        mc = [jnp.broadcast_to(m[None, None, :], (B, 1, D)) for m in jnp.split(mods[l, B], 6, axis=-1)]
        w_main = w_in[l][:, :OFF_G].astype(BF16)
        w_gate = jnp.pad(w_in[l][:, OFF_G:], ((0, 0), (0, LANE - N_GATES))).astype(BF16)
        b_gate = jnp.pad(b_gates[l], (0, LANE - N_GATES))[None]
        w_pool_bd = _block_diag(pool_w[l]).astype(BF16)
        s_pool = pool_scale[l][None]
        w_out_b = w_out[l].astype(BF16)
        w_up_b = w_up[l].astype(BF16)
        w_down_b = w_down[l].astype(BF16)
        g1, b1 = ln1_g[l][None], ln1_b[l][None]
        g2, b2 = ln2_g[l][None], ln2_b[l][None]
        bcv = conv_ffn_b[l][None]

        zc, zgc = _proj_in(xc, mc[0], mc[1], w_main, w_gate, conv_qk[l], tm_c)
        ymc, cf, mf, cb, mb = _mlstm(zc, zgc, b_gate, zero_c, zero_m, zero_c, zero_m)
        zx, zgx = _proj_in(x, mx[0], mx[1], w_main, w_gate, conv_qk[l], tm_x)
        ymx = _mlstm(zx, zgx, b_gate, cf, mf, cb, mb)[0]
        ypx = _pool(zx, filt_x, cnt_x, w_pool_bd, s_pool, True)
        yfx = _fourier(zx, _fourier_mix_weights(fourier_w[l], T), dft_x)
        x = _proj_out(ypx, yfx, ymx, x, mx[2], w_out_b, g1, b1, min(T, 2 * tm_x))
        x = _ffn(x, mx[3], mx[4], mx[5], w_up_b, conv_ffn_w[l], bcv, w_down_b, g2, b2, tm_x)
        if not last:
            ypc = _pool(zc, filt_c, cnt_c, w_pool_bd, s_pool, False)
            yfc = _fourier(zc, _fourier_mix_weights(fourier_w[l], Tc), dft_c)
            xc = _proj_out(ypc, yfc, ymc, xc, mc[2], w_out_b, g1, b1, tm_c)
            xc = _ffn(xc, mc[3], mc[4], mc[5], w_up_b, conv_ffn_w[l], bcv, w_down_b, g2, b2, tm_c)
    return x
```

```python
import functools
import math

import jax
import jax.numpy as jnp
import numpy as np
from jax import lax
from jax.experimental import pallas as pl
from jax.experimental.pallas import tpu as pltpu

F32 = jnp.float32
BF16 = jnp.bfloat16

D_MODEL = 1024
DEPTH = 4
GRID_W = 64
GRID_SHIFT = GRID_W.bit_length() - 1
POOL_WINDOWS = (2, 4, 8, 16)
N_POOL = len(POOL_WINDOWS)
MIX_POOL = D_MODEL // 4
POOL_DIM = MIX_POOL // N_POOL
N_FOUR = 4
MIX_FOUR = D_MODEL // 4
FOUR_DIM = MIX_FOUR // N_FOUR
N_MLSTM = 4
MIX_MLSTM = D_MODEL // 2
MLSTM_DH = MIX_MLSTM // N_MLSTM
MLSTM_CHUNK = 128
N_GATES = 2 * 2 * N_MLSTM
D_FF = int(math.ceil(8 * D_MODEL / 3 / 128)) * 128
OFF_FOUR = MIX_POOL
OFF_O = OFF_FOUR + MIX_FOUR
OFF_Q = OFF_O + MIX_MLSTM
OFF_K = OFF_Q + MIX_MLSTM
OFF_V = OFF_K + MIX_MLSTM
OFF_G = OFF_V + MIX_MLSTM
ALPHA = (2 * DEPTH) ** 0.25
LN_EPS = 1e-6
LOG2_E = math.log2(math.e)

LANE = 128
LANE_SHIFT = LANE.bit_length() - 1
SUBLANE = 8
BF16_ROWS = 16
VMEM_LIMIT = 56 * 1024 * 1024
FF_CHUNK = 256
PROJ_CHUNK = 512
POOL_BLOCK = 256
POOL_UNROLL = 8
FOUR_MIRROR_BLOCK = 256
POOL_PAD = (max(POOL_WINDOWS) // 2) * GRID_W
NEG_BIG = -1e30
MLSTM_GROUP = 32
MLSTM_PREP_GROUP = 32
MLSTM_AUG = MLSTM_DH + BF16_ROWS


def _params(n_axes):
    return pltpu.CompilerParams(dimension_semantics=("arbitrary",) * n_axes, vmem_limit_bytes=VMEM_LIMIT)


def _ln_rows(x):
    mu = jnp.mean(x, axis=-1, keepdims=True)
    xc = x - mu
    var = jnp.mean(xc * xc, axis=-1, keepdims=True)
    return xc * lax.rsqrt(var + LN_EPS)


def _sigmoid(x):
    return 0.5 * jnp.tanh(0.5 * x) + 0.5


def _log2_sigmoid(x2):
    return jnp.minimum(x2, 0.0) - jnp.log2(1.0 + jnp.exp2(-jnp.abs(x2)))


def _split_hi_lo(x):
    hi = x.astype(BF16)
    return hi, (x - hi.astype(F32)).astype(BF16)


def _mod_kernel(c_ref, w_ref, b_ref, o_ref):
    cond = c_ref[...]
    s = cond * _sigmoid(cond)
    o_ref[0] = jnp.dot(s, w_ref[0], preferred_element_type=F32, precision=lax.Precision.HIGHEST) + b_ref[0]


def _modulation(cond, w_mod, b_mod):
    R, D = cond.shape
    L, _, N = w_mod.shape
    tn = N // 4
    return pl.pallas_call(
        _mod_kernel,
        grid=(L, N // tn),
        in_specs=[
            pl.BlockSpec((R, D), lambda l, j: (0, 0)),
            pl.BlockSpec((1, D, tn), lambda l, j: (l, 0, j)),
            pl.BlockSpec((1, 1, tn), lambda l, j: (l, 0, j)),
        ],
        out_specs=pl.BlockSpec((1, R, tn), lambda l, j: (l, 0, j)),
        out_shape=jax.ShapeDtypeStruct((L, R, N), F32),
        compiler_params=_params(2),
        name="modulation",
    )(cond, w_mod, b_mod[:, None, :])


def _proj_in_kernel(x_ref, xp_ref, xn_ref, sh_ref, sc_ref, w_ref, wg_ref, wc_ref, z_ref, g_ref, h_ref):
    i = pl.program_id(1)
    tm = x_ref.shape[1]
    mod_scale = 1.0 + sc_ref[0]
    mod_shift = sh_ref[0]
    u = _ln_rows(x_ref[0]) * mod_scale + mod_shift
    u_prev = (_ln_rows(xp_ref[0]) * mod_scale + mod_shift) * (i > 0).astype(F32)
    u_next = (_ln_rows(xn_ref[0]) * mod_scale + mod_shift) * (i < pl.num_programs(1) - 1).astype(F32)
    ub = u.astype(BF16)
    u_ext = jnp.concatenate([u_prev, u, u_next], axis=0).astype(BF16)
    n = z_ref.shape[-1]
    for c0 in range(0, n, PROJ_CHUNK):
        c1 = min(c0 + PROJ_CHUNK, n)
        if OFF_Q <= c0 and c1 <= OFF_V:
            h_ref[...] = jnp.dot(u_ext, w_ref[:, c0:c1], preferred_element_type=F32)
            w = wc_ref[:, c0 - OFF_Q:c1 - OFF_Q]
            y = (h_ref[SUBLANE - 1:SUBLANE - 1 + tm, :] * w[0:1] + h_ref[SUBLANE:SUBLANE + tm, :] * w[1:2]
                 + h_ref[SUBLANE + 1:SUBLANE + 1 + tm, :] * w[2:3])
            scale = 1.0 if c0 < OFF_K else MLSTM_DH ** -0.5
            z_ref[0, :, c0:c1] = (y * _sigmoid(y) * scale).astype(z_ref.dtype)
        else:
            z_ref[0, :, c0:c1] = jnp.dot(ub, w_ref[:, c0:c1], preferred_element_type=F32).astype(z_ref.dtype)
    g_ref[0] = jnp.dot(ub, wg_ref[...], preferred_element_type=F32)


def _proj_in(x, shift, scale, w_main, w_gate, w_conv, tm):
    B, T, D = x.shape
    n = w_main.shape[1]
    assert OFF_Q % PROJ_CHUNK == 0 and OFF_K % PROJ_CHUNK == 0 and OFF_V % PROJ_CHUNK == 0
    nb = tm // SUBLANE
    last = T // SUBLANE - 1
    return pl.pallas_call(
        _proj_in_kernel,
        grid=(B, T // tm),
        in_specs=[
            pl.BlockSpec((1, tm, D), lambda b, i: (b, i, 0)),
            pl.BlockSpec((1, SUBLANE, D), lambda b, i: (b, jnp.maximum(i * nb - 1, 0), 0)),
            pl.BlockSpec((1, SUBLANE, D), lambda b, i: (b, jnp.minimum((i + 1) * nb, last), 0)),
            pl.BlockSpec((1, 1, D), lambda b, i: (b, 0, 0)),
            pl.BlockSpec((1, 1, D), lambda b, i: (b, 0, 0)),
            pl.BlockSpec((D, n), lambda b, i: (0, 0)),
            pl.BlockSpec((D, LANE), lambda b, i: (0, 0)),
            pl.BlockSpec((3, 2 * MIX_MLSTM), lambda b, i: (0, 0)),
        ],
        out_specs=[
            pl.BlockSpec((1, tm, n), lambda b, i: (b, i, 0)),
            pl.BlockSpec((1, tm, LANE), lambda b, i: (b, i, 0)),
        ],
        out_shape=[
            jax.ShapeDtypeStruct((B, T, n), BF16),
            jax.ShapeDtypeStruct((B, T, LANE), F32),
        ],
        scratch_shapes=[pltpu.VMEM((tm + 2 * SUBLANE, PROJ_CHUNK), F32)],
        compiler_params=_params(2),
        name="proj_in",
    )(x, x, x, shift, scale, w_main, w_gate, w_conv)


def _proj_out_kernel(yp_ref, yf_ref, ym_ref, x_ref, gate_ref, wp_ref, wf_ref, wm_ref, lg_ref, lb_ref, o_ref):
    f = (jnp.dot(yp_ref[0], wp_ref[...], preferred_element_type=F32)
         + jnp.dot(yf_ref[0], wf_ref[...], preferred_element_type=F32)
         + jnp.dot(ym_ref[0], wm_ref[...], preferred_element_type=F32))
    r = ALPHA * x_ref[0] + gate_ref[0] * f
    o_ref[0] = _ln_rows(r) * lg_ref[...] + lb_ref[...]


def _proj_out(yp, yf, ym, x, gate, w, ln_g, ln_b, tm):
    B, T, D = x.shape
    tok = lambda b, i: (b, i, 0)
    return pl.pallas_call(
        _proj_out_kernel,
        grid=(B, T // tm),
        in_specs=[
            pl.BlockSpec((1, tm, MIX_POOL), tok),
            pl.BlockSpec((1, tm, MIX_FOUR), tok),
            pl.BlockSpec((1, tm, MIX_MLSTM), tok),
            pl.BlockSpec((1, tm, D), tok),
            pl.BlockSpec((1, 1, D), lambda b, i: (b, 0, 0)),
            pl.BlockSpec((MIX_POOL, D), lambda b, i: (0, 0)),
            pl.BlockSpec((MIX_FOUR, D), lambda b, i: (OFF_FOUR // MIX_FOUR, 0)),
            pl.BlockSpec((MIX_MLSTM, D), lambda b, i: (OFF_O // MIX_MLSTM, 0)),
            pl.BlockSpec((1, D), lambda b, i: (0, 0)),
            pl.BlockSpec((1, D), lambda b, i: (0, 0)),
        ],
        out_specs=pl.BlockSpec((1, tm, D), tok),
        out_shape=jax.ShapeDtypeStruct((B, T, D), F32),
        compiler_params=_params(2),
        name="proj_out",
    )(yp, yf, ym, x, gate, w, w, w, ln_g, ln_b)


def _gelu_tanh_times(x, half_a):
    c = math.sqrt(2.0 / math.pi)
    inner = x * (c + (c * 0.044715) * (x * x))
    return (x * (1.0 + jnp.tanh(inner))) * half_a


def _ffn_kernel(x_ref, xp_ref, xn_ref, sh_ref, sc_ref, gate_ref, wup_ref, wc_ref, bc_ref, wdn_ref,
                lg_ref, lb_ref, o_ref, ha_ref, hg_ref, p_ref):
    i = pl.program_id(1)
    tm = x_ref.shape[1]
    x = x_ref[0]
    mod_scale = 1.0 + sc_ref[0]
    mod_shift = sh_ref[0]
    has_prev = (i > 0).astype(F32)
    has_next = (i < pl.num_programs(1) - 1).astype(F32)
    u_prev = (_ln_rows(xp_ref[0]) * mod_scale + mod_shift) * has_prev
    u_next = (_ln_rows(xn_ref[0]) * mod_scale + mod_shift) * has_next
    u_main = _ln_rows(x) * mod_scale + mod_shift
    u_ext = jnp.concatenate([u_prev, u_main, u_next], axis=0).astype(BF16)

    def conv(h_ref, cols):
        w = wc_ref[:, cols]
        return (h_ref[SUBLANE - 1:SUBLANE - 1 + tm, :] * w[0:1]
                + h_ref[SUBLANE:SUBLANE + tm, :] * w[1:2]
                + h_ref[SUBLANE + 1:SUBLANE + 1 + tm, :] * w[2:3]
                + bc_ref[:, cols])

    for c in range(D_FF // FF_CHUNK):
        ca = slice(c * FF_CHUNK, (c + 1) * FF_CHUNK)
        cg = slice(D_FF + c * FF_CHUNK, D_FF + (c + 1) * FF_CHUNK)
        ha_ref[...] = jnp.dot(u_ext, wup_ref[:, ca], preferred_element_type=F32)
        hg_ref[...] = jnp.dot(u_ext, wup_ref[:, cg], preferred_element_type=F32)
        p_ref[:, ca] = _gelu_tanh_times(conv(hg_ref, cg), conv(ha_ref, ca)).astype(BF16)

    f = jnp.dot(p_ref[...], wdn_ref[...], preferred_element_type=F32)
    r = ALPHA * x + gate_ref[0] * f
    o_ref[0] = _ln_rows(r) * lg_ref[...] + lb_ref[...]


def _ffn(x, shift, scale, gate, w_up, w_conv, b_conv, w_down, ln_g, ln_b, tm):
    B, T, D = x.shape
    nb = tm // SUBLANE
    last = T // SUBLANE - 1
    const = dict(pipeline_mode=pl.Buffered(1))
    half_value = jnp.concatenate([jnp.full((D_FF,), 0.5, F32), jnp.ones((D_FF,), F32)])
    w_conv = w_conv * half_value
    b_conv = b_conv * half_value
    return pl.pallas_call(
        _ffn_kernel,
        grid=(B, T // tm),
        in_specs=[
            pl.BlockSpec((1, tm, D), lambda b, i: (b, i, 0)),
            pl.BlockSpec((1, SUBLANE, D), lambda b, i: (b, jnp.maximum(i * nb - 1, 0), 0)),
            pl.BlockSpec((1, SUBLANE, D), lambda b, i: (b, jnp.minimum((i + 1) * nb, last), 0)),
            pl.BlockSpec((1, 1, D), lambda b, i: (b, 0, 0)),
            pl.BlockSpec((1, 1, D), lambda b, i: (b, 0, 0)),
            pl.BlockSpec((1, 1, D), lambda b, i: (b, 0, 0)),
            pl.BlockSpec((D, 2 * D_FF), lambda b, i: (0, 0), **const),
            pl.BlockSpec((3, 2 * D_FF), lambda b, i: (0, 0), **const),
            pl.BlockSpec((1, 2 * D_FF), lambda b, i: (0, 0), **const),
            pl.BlockSpec((D_FF, D), lambda b, i: (0, 0), **const),
            pl.BlockSpec((1, D), lambda b, i: (0, 0)),
            pl.BlockSpec((1, D), lambda b, i: (0, 0)),
        ],
        out_specs=pl.BlockSpec((1, tm, D), lambda b, i: (b, i, 0)),
        out_shape=jax.ShapeDtypeStruct((B, T, D), F32),
        scratch_shapes=[
            pltpu.VMEM((tm + 2 * SUBLANE, FF_CHUNK), F32),
            pltpu.VMEM((tm + 2 * SUBLANE, FF_CHUNK), F32),
            pltpu.VMEM((tm, D_FF), BF16),
        ],
        compiler_params=_params(2),
        name="conv_ffn",
    )(x, x, x, shift, scale, gate, w_up, w_conv, b_conv, w_down, ln_g, ln_b)


def _band_matrix(n, w):
    pos = np.arange(n)
    lo = np.maximum(pos - w // 2, 0)
    hi = np.minimum(pos + w // 2 - 1, n - 1)
    return ((pos[None, :] >= lo[:, None]) & (pos[None, :] <= hi[:, None])).astype(np.float32)


def _pool_filters(grid2d):
    mats = []
    for w in POOL_WINDOWS:
        if grid2d:
            mats.append(np.kron(np.eye(POOL_BLOCK // GRID_W, dtype=np.float32), _band_matrix(GRID_W, w)))
        else:
            mats.append(_band_matrix(POOL_BLOCK, w))
    return jnp.asarray(np.stack(mats), BF16)


def _window_count(idx, half, n):
    return jnp.minimum(idx + (half - 1), n - 1) - jnp.maximum(idx - half, 0) + 1


def _pool_inv_count(T, grid2d):
    pos = lax.broadcasted_iota(jnp.int32, (T, MIX_POOL), 0)
    group = jnp.right_shift(lax.broadcasted_iota(jnp.int32, (T, MIX_POOL), 1), GRID_SHIFT)
    half = jnp.left_shift(1, group)
    if grid2d:
        count = (_window_count(jnp.right_shift(pos, GRID_SHIFT), half, T // GRID_W)
                 * _window_count(jnp.bitwise_and(pos, GRID_W - 1), half, GRID_W))
    else:
        count = _window_count(pos, half, T)
    return 1.0 / count.astype(F32)


def _pool_kernel(z_ref, a_ref, n_ref, w_ref, s_ref, y_ref, acc_ref, *, grid2d):
    T = z_ref.shape[1]
    nblk = T // POOL_BLOCK
    lane = lax.broadcasted_iota(jnp.int32, (POOL_BLOCK, MIX_POOL), 1)
    group = jnp.right_shift(lane, GRID_SHIFT)

    def col_filter(blk):
        r0 = pl.multiple_of(blk * POOL_BLOCK, POOL_BLOCK)
        xb = z_ref[0, pl.ds(r0, POOL_BLOCK), :]
        out = jnp.dot(a_ref[0], xb, preferred_element_type=F32)
        for g in range(1, N_POOL):
            out = jnp.where(group == g, jnp.dot(a_ref[g], xb, preferred_element_type=F32), out)
        return r0, xb, out

    def finish(r0, xb, box):
        p = box * n_ref[pl.ds(r0, POOL_BLOCK), :] - xb.astype(F32)
        y = jnp.dot(p.astype(BF16), w_ref[...], preferred_element_type=F32) * s_ref[...]
        y_ref[0, pl.ds(r0, POOL_BLOCK), :] = y.astype(y_ref.dtype)

    if not grid2d:
        assert T == POOL_BLOCK
        finish(*col_filter(0))
        return

    zeros = jnp.zeros((POOL_PAD, MIX_POOL), F32)
    acc_ref[0:POOL_PAD, :] = zeros
    acc_ref[POOL_PAD + T:POOL_PAD + T + POOL_PAD, :] = zeros

    def phase1(blk, carry):
        r0, _, out = col_filter(blk)
        acc_ref[pl.ds(POOL_PAD + r0, POOL_BLOCK), :] = out
        return carry

    lax.fori_loop(0, nblk, phase1, 0, unroll=POOL_UNROLL)

    lane_t = lax.broadcasted_iota(jnp.int32, (POOL_BLOCK, LANE), 1)

    def phase2(blk, carry):
        r0 = pl.multiple_of(blk * POOL_BLOCK, POOL_BLOCK)
        base = POOL_PAD + r0

        def slab(k, tile):
            return acc_ref[pl.ds(base + k * GRID_W, POOL_BLOCK), tile * LANE:(tile + 1) * LANE]

        s2 = slab(-1, 0) + slab(0, 0)
        s4 = s2 + slab(-2, 0) + slab(1, 0)
        s8 = slab(-4, 1)
        for k in range(-3, 4):
            s8 = s8 + slab(k, 1)
        s16 = s8
        for k in list(range(-8, -4)) + list(range(4, 8)):
            s16 = s16 + slab(k, 1)
        box = jnp.concatenate([jnp.where(lane_t < POOL_DIM, s2, s4), jnp.where(lane_t < POOL_DIM, s8, s16)], axis=1)
        finish(r0, z_ref[0, pl.ds(r0, POOL_BLOCK), :], box)
        return carry

    lax.fori_loop(0, nblk, phase2, 0, unroll=POOL_UNROLL)


def _pool(z, filters, inv_count, w_bd, s_row, grid2d):
    B, T, _ = z.shape
    return pl.pallas_call(
        functools.partial(_pool_kernel, grid2d=grid2d),
        grid=(B,),
        in_specs=[
            pl.BlockSpec((1, T, MIX_POOL), lambda b: (b, 0, 0)),
            pl.BlockSpec((N_POOL, POOL_BLOCK, POOL_BLOCK), lambda b: (0, 0, 0)),
            pl.BlockSpec((T, MIX_POOL), lambda b: (0, 0)),
            pl.BlockSpec((MIX_POOL, MIX_POOL), lambda b: (0, 0)),
            pl.BlockSpec((1, MIX_POOL), lambda b: (0, 0)),
        ],
        out_specs=pl.BlockSpec((1, T, MIX_POOL), lambda b: (b, 0, 0)),
        out_shape=jax.ShapeDtypeStruct((B, T, MIX_POOL), BF16),
        scratch_shapes=[pltpu.VMEM((T + 2 * POOL_PAD, MIX_POOL), F32)],
        compiler_params=_params(1),
        name="pool2d" if grid2d else "pool1d",
    )(z, filters, inv_count, w_bd, s_row)


def _block_diag(w):
    G, n, m = w.shape
    eye = jnp.eye(G, dtype=w.dtype)
    return (eye[:, None, :, None] * w[:, :, None, :]).reshape(G * n, G * m)


def _four_mix_kernel(z_ref, za_ref, zb_ref, zm_ref, g_ref, y_ref):
    tm = z_ref.shape[1]
    sb = zb_ref.shape[1]
    nsub = tm // sb
    gr, gi = g_ref[:, :MIX_FOUR], g_ref[:, MIX_FOUR:]
    r = lax.broadcasted_iota(jnp.int32, (sb, 2 * sb), 0)
    c = lax.broadcasted_iota(jnp.int32, (sb, 2 * sb), 1)
    perm = jnp.where(((r >= 1) & (c == sb - r)) | ((r == 0) & (c == sb)), 1.0, 0.0).astype(BF16)
    mid = jnp.dot(zm_ref[0], gr, preferred_element_type=F32)[0:1]
    row0 = lax.broadcasted_iota(jnp.int32, (sb, MIX_FOUR), 0) == 0
    for j in range(nsub):
        lo = (nsub - 1 - j) * sb
        if j == 0:
            src = jnp.concatenate([za_ref[0, lo:lo + sb, :], zb_ref[0]], axis=0)
        else:
            src = za_ref[0, lo:lo + 2 * sb, :]
        xr = jnp.dot(perm, src, preferred_element_type=F32)
        x = z_ref[0, j * sb:(j + 1) * sb, :].astype(F32)
        even = jnp.dot((x + xr).astype(BF16), gr, preferred_element_type=F32)
        odd = jnp.dot((x - xr).astype(BF16), gi, preferred_element_type=F32)
        if j == 0:
            odd = jnp.where(row0 & (pl.program_id(1) == 0), mid, odd)
        y_ref[0, 0, j * sb:(j + 1) * sb, :] = even.astype(y_ref.dtype)
        y_ref[0, 1, j * sb:(j + 1) * sb, :] = odd.astype(y_ref.dtype)


def _four_dft_kernel(d_ref, y_ref, o_ref):
    rows = 256
    for r0 in range(0, o_ref.shape[1], rows):
        o_ref[0, r0:r0 + rows, :] = jnp.dot(d_ref[r0:r0 + rows, :], y_ref[0],
                                            preferred_element_type=F32).astype(o_ref.dtype)


def _fourier(z, g_mix, dft):
    B, T, _ = z.shape
    M = T // 2
    tm = min(M, 2048)
    fcol = OFF_FOUR // MIX_FOUR
    sb = min(tm, FOUR_MIRROR_BLOCK)
    nt, nsb, per = T // tm, T // sb, tm // sb
    y = pl.pallas_call(
        _four_mix_kernel,
        grid=(B, M // tm),
        in_specs=[
            pl.BlockSpec((1, tm, MIX_FOUR), lambda b, i: (b, i, fcol)),
            pl.BlockSpec((1, tm, MIX_FOUR), lambda b, i: (b, nt - 1 - i, fcol)),
            pl.BlockSpec((1, sb, MIX_FOUR), lambda b, i: (b, (nsb - i * per) % nsb, fcol)),
            pl.BlockSpec((1, BF16_ROWS, MIX_FOUR), lambda b, i: (b, M // BF16_ROWS, fcol)),
            pl.BlockSpec((MIX_FOUR, 2 * MIX_FOUR), lambda b, i: (0, 0)),
        ],
        out_specs=pl.BlockSpec((1, 2, tm, MIX_FOUR), lambda b, i: (b, 0, i, 0)),
        out_shape=jax.ShapeDtypeStruct((B, 2, M, MIX_FOUR), BF16),
        compiler_params=_params(2),
        name="fourier_mix",
    )(z, z, z, z, g_mix)
    y = y.reshape(B, T, MIX_FOUR)
    tf = min(T, 2048)
    return pl.pallas_call(
        _four_dft_kernel,
        grid=(T // tf, B),
        in_specs=[
            pl.BlockSpec((tf, T), lambda m, b: (m, 0)),
            pl.BlockSpec((1, T, MIX_FOUR), lambda m, b: (b, 0, 0)),
        ],
        out_specs=pl.BlockSpec((1, tf, MIX_FOUR), lambda m, b: (b, m, 0)),
        out_shape=jax.ShapeDtypeStruct((B, T, MIX_FOUR), BF16),
        compiler_params=_params(2),
        name="fourier_dft",
    )(dft, y)


def _dft_matrix(n):
    m = n // 2
    a = m // GRID_W
    f = lax.broadcasted_iota(jnp.int32, (n, 1), 0)
    ang1 = ((f * GRID_W * lax.broadcasted_iota(jnp.int32, (1, a), 1)) % n).astype(F32) * (2.0 * math.pi / n)
    ang2 = ((f * lax.broadcasted_iota(jnp.int32, (1, GRID_W), 1)) % n).astype(F32) * (2.0 * math.pi / n)
    c1, s1 = jnp.cos(ang1)[:, :, None], jnp.sin(ang1)[:, :, None]
    c2, s2 = jnp.cos(ang2)[:, None, :], jnp.sin(ang2)[:, None, :]
    cos = (c1 * c2 - s1 * s2).reshape(n, m)
    sin = (s1 * c2 + c1 * s2).reshape(n, m)
    t0 = lax.broadcasted_iota(jnp.int32, (n, m), 1) == 0
    sign = (1 - 2 * (f % 2)).astype(F32)
    return jnp.concatenate([jnp.where(t0, 0.5, cos), jnp.where(t0, sign, sin)], axis=1).astype(BF16)


def _fourier_mix_weights(w_four, T):
    c = np.arange(FOUR_DIM)
    ang = 2.0 * np.pi * np.outer(c, c) / FOUR_DIM
    norm = 1.0 / math.sqrt(T * FOUR_DIM)
    gr = jnp.einsum('cf,gfe->gce', jnp.asarray(np.cos(ang) * norm, F32), w_four, precision='highest')
    gi = jnp.einsum('cf,gfe->gce', jnp.asarray(-np.sin(ang) * norm, F32), w_four, precision='highest')
    return jnp.concatenate([_block_diag(gr), _block_diag(gi)], axis=1).astype(BF16)


def _mlstm_kernel(q_ref, k_ref, v_ref, o_ref, g_ref, bg_ref, cf0_ref, mf0_ref, cb0_ref, mb0_ref,
                  y_ref, cf1_ref, mf1_ref, cb1_ref, mb1_ref,
                  hf_ref, hb_ref, pf_ref, pb_ref, uf_ref, ub_ref, rf_ref, rb_ref):
    T = q_ref.shape[1]
    L = MLSTM_CHUNK
    dh = MLSTM_DH
    aug = MLSTM_AUG
    nc = T // L
    head = pl.program_id(1)

    row = lax.broadcasted_iota(jnp.int32, (L, L), 0)
    col = lax.broadcasted_iota(jnp.int32, (L, L), 1)
    masks = (row <= col, row >= col)
    row2 = lax.broadcasted_iota(jnp.int32, (L, 2 * L), 0)
    col2 = jnp.bitwise_and(lax.broadcasted_iota(jnp.int32, (L, 2 * L), 1), L - 1)
    tri2 = jnp.where(col2 <= row2, 1.0, 0.0).astype(BF16)
    diag = row == col
    sub8 = lax.broadcasted_iota(jnp.int32, (SUBLANE, L), 0)
    ones_t = jnp.ones((aug - dh, L), F32)
    sel_row = jnp.bitwise_and(lax.broadcasted_iota(jnp.int32, (2 * LANE, 2 * LANE), 0), LANE - 1)
    sel_dir = jnp.right_shift(lax.broadcasted_iota(jnp.int32, (2 * LANE, 2 * LANE), 1), LANE_SHIFT)
    sel_i = jnp.where(sel_row == sel_dir * (2 * N_MLSTM) + head, 1.0, 0.0).astype(BF16)
    sel_f = jnp.where(sel_row == sel_dir * (2 * N_MLSTM) + N_MLSTM + head, 1.0, 0.0).astype(BF16)
    p_refs, u_refs, r_refs, h_refs = (pf_ref, pb_ref), (uf_ref, ub_ref), (rf_ref, rb_ref), (hf_ref, hb_ref)
    bias = bg_ref[...]

    def prep_group(c0):
        cs = [c0 + j for j in range(min(MLSTM_PREP_GROUP, nc))]
        rs = [pl.multiple_of(c * L, L) for c in cs]
        q = [q_ref[0, pl.ds(r, L), :] for r in rs]
        k = [k_ref[0, pl.ds(r, L), :] for r in rs]
        vt = [jnp.concatenate([v_ref[0, pl.ds(r, L), :].astype(F32).T, ones_t], axis=0).astype(BF16) for r in rs]
        g = [(g_ref[0, pl.ds(r, L), :] + bias) * LOG2_E for r in rs]
        i_rep = [jnp.dot(jnp.concatenate(_split_hi_lo(gg), axis=1), sel_i, preferred_element_type=F32)
                 for gg in g]
        lf_rep = [jnp.dot(jnp.concatenate(_split_hi_lo(_log2_sigmoid(gg)), axis=1), sel_f,
                          preferred_element_type=F32) for gg in g]
        st = [lax.dot_general(kk, qq, (((1,), (1,)), ((), ())), preferred_element_type=F32)
              for kk, qq in zip(k, q)]
        cum = [jnp.dot(tri2, jnp.concatenate(_split_hi_lo(f), axis=0), preferred_element_type=F32)
               for f in lf_rep]
        items = [(j, d) for j in range(len(cs)) for d in range(2)]
        for j, d in items:
            if d == 0:
                b = cum[j][:, :L]
                total = b[L - 1:L, :]
            else:
                total = cum[j][L - 1:L, L:]
                b = total - cum[j][:, L:] + lf_rep[j][:, L:]
            r_rep = i_rep[j][:, d * L:(d + 1) * L] - b
            dmt = jnp.where(masks[d], r_rep, NEG_BIG)
            pm = jnp.max(dmt, axis=0, keepdims=True)
            sg = (st[j] * jnp.exp2(dmt - pm)).astype(BF16)
            b_row = jnp.sum(jnp.where(diag, b, 0.0), axis=0, keepdims=True)
            g_end = total + r_rep
            g_loc = jnp.max(g_end, axis=0, keepdims=True)
            kw = (k[j].astype(F32) * jnp.exp2(g_end - g_loc)).astype(BF16)
            base = pl.multiple_of(cs[j] * aug, BF16_ROWS)
            pu = jnp.dot(vt[j], jnp.concatenate([sg, kw], axis=1), preferred_element_type=F32)
            p_refs[d][pl.ds(base, aug), :] = pu[:, :L]
            u_refs[d][pl.ds(base, aug), :] = pu[:, L:]
            rbase = pl.multiple_of(cs[j] * SUBLANE, SUBLANE)
            r_refs[d][pl.ds(rbase, SUBLANE), :] = jnp.where(
                sub8 == 0, b_row, jnp.where(sub8 == 1, b_row + pm, jnp.where(sub8 == 2, total, g_loc)))

    ga = min(MLSTM_PREP_GROUP, nc)

    def body_a(i, carry):
        prep_group(i * ga)
        return carry

    lax.fori_loop(0, nc // ga, body_a, 0)

    def step(d, c, ct, m):
        base = pl.multiple_of(c * aug, BF16_ROWS)
        rows = r_refs[d][pl.ds(pl.multiple_of(c * SUBLANE, SUBLANE), SUBLANE), :]
        b_row, a_row, total, g_loc = rows[0:1], rows[1:2], rows[2:3], rows[3:4]
        inter = b_row + m
        m_j = jnp.maximum(inter, a_row)
        qct = lax.dot_general(ct.astype(BF16), q_ref[0, pl.ds(pl.multiple_of(c * L, L), L), :],
                              (((1,), (1,)), ((), ())), preferred_element_type=F32)
        nd = jnp.exp2(inter - m_j) * qct + jnp.exp2(a_row - m_j) * p_refs[d][pl.ds(base, aug), :]
        den = jnp.maximum(jnp.abs(nd[dh:dh + 1]), jnp.exp2(-m_j))
        h_refs[d][pl.ds(pl.multiple_of(c * dh, dh), dh), :] = nd[:dh] / den
        m_new = jnp.maximum(total + m, g_loc)
        ct_new = jnp.exp2(total + m - m_new) * ct + jnp.exp2(g_loc - m_new) * u_refs[d][pl.ds(base, aug), :]
        return ct_new, m_new

    gb = min(nc, MLSTM_GROUP)

    def body_b(i, carry):
        cf, mf, cb, mb = carry
        for j in range(gb):
            cf, mf = step(0, i * gb + j, cf, mf)
            cb, mb = step(1, nc - 1 - (i * gb + j), cb, mb)
        return cf, mf, cb, mb

    cf, mf, cb, mb = lax.fori_loop(
        0, nc // gb, body_b, (cf0_ref[0, 0], mf0_ref[0, 0][0:1, :], cb0_ref[0, 0], mb0_ref[0, 0][0:1, :]))

    cf1_ref[0, 0] = cf
    cb1_ref[0, 0] = cb
    mf1_ref[0, 0] = jnp.broadcast_to(mf, (SUBLANE, LANE))
    mb1_ref[0, 0] = jnp.broadcast_to(mb, (SUBLANE, LANE))

    go = min(nc, MLSTM_GROUP)

    def body_out(i, carry):
        for j in range(go):
            c = i * go + j
            r = pl.multiple_of(c * L, L)
            rh = pl.multiple_of(c * dh, dh)
            h = (hf_ref[pl.ds(rh, dh), :] + hb_ref[pl.ds(rh, dh), :]).T
            y_ref[0, pl.ds(r, L), :] = (_sigmoid(o_ref[0, pl.ds(r, L), :].astype(F32)) * h).astype(y_ref.dtype)
        return carry

    lax.fori_loop(0, nc // go, body_out, 0)


def _mlstm(z, zg, b_gate, cf0, mf0, cb0, mb0):
    B, T, _ = z.shape
    H, dh, L, aug = N_MLSTM, MLSTM_DH, MLSTM_CHUNK, MLSTM_AUG
    nc = T // L
    zcol = lambda off: (lambda b, h: (b, 0, off // dh + h))
    st = lambda b, h: (b, h, 0, 0)
    c_spec = pl.BlockSpec((1, 1, aug, dh), st)
    m_spec = pl.BlockSpec((1, 1, SUBLANE, LANE), st)
    c_shape = jax.ShapeDtypeStruct((B, H, aug, dh), F32)
    m_shape = jax.ShapeDtypeStruct((B, H, SUBLANE, LANE), F32)
    return pl.pallas_call(
        _mlstm_kernel,
        grid=(B, H),
        in_specs=[
            pl.BlockSpec((1, T, dh), zcol(OFF_Q)),
            pl.BlockSpec((1, T, dh), zcol(OFF_K)),
            pl.BlockSpec((1, T, dh), zcol(OFF_V)),
            pl.BlockSpec((1, T, dh), zcol(OFF_O)),
            pl.BlockSpec((1, T, LANE), lambda b, h: (b, 0, 0)),
            pl.BlockSpec((1, LANE), lambda b, h: (0, 0)),
            c_spec, m_spec, c_spec, m_spec,
        ],
        out_specs=[pl.BlockSpec((1, T, dh), lambda b, h: (b, 0, h)), c_spec, m_spec, c_spec, m_spec],
        out_shape=[jax.ShapeDtypeStruct((B, T, MIX_MLSTM), BF16), c_shape, m_shape, c_shape, m_shape],
        scratch_shapes=[
            pltpu.VMEM((nc * dh, L), F32),
            pltpu.VMEM((nc * dh, L), F32),
            pltpu.VMEM((nc * aug, L), F32),
            pltpu.VMEM((nc * aug, L), F32),
            pltpu.VMEM((nc * aug, dh), F32),
            pltpu.VMEM((nc * aug, dh), F32),
            pltpu.VMEM((nc * SUBLANE, LANE), F32),
            pltpu.VMEM((nc * SUBLANE, LANE), F32),
        ],
        compiler_params=_params(2),
        name="mlstm",
    )(z, z, z, z, zg, b_gate, cf0, mf0, cb0, mb0)


def kernel(x, c, ctx, c_ctx, w_mod, b_mod, w_in, conv_qk, b_gates, pool_w, pool_scale, fourier_w,
           w_out, ln1_g, ln1_b, w_up, conv_ffn_w, conv_ffn_b, w_down, ln2_g, ln2_b):
    B, T, D = x.shape
    Tc = ctx.shape[1]
    assert D == D_MODEL and ctx.shape == (B, Tc, D) and w_mod.shape[0] == DEPTH
    assert T % (2 * POOL_BLOCK) == 0 and T % MLSTM_CHUNK == 0 and (T // GRID_W) * GRID_W == T
    assert Tc == POOL_BLOCK and Tc % MLSTM_CHUNK == 0
    tm_x = min(T, 1024)
    assert T % (2 * tm_x) == 0 or T == tm_x
    tm_c = Tc
    xc = ctx
    cond = jnp.concatenate([c, c_ctx[None], jnp.zeros((-(B + 1) % SUBLANE, D), F32)], axis=0)
    mods = _modulation(cond, w_mod, b_mod)
    filt_x, filt_c = _pool_filters(True), _pool_filters(False)
    cnt_x, cnt_c = _pool_inv_count(T, True), _pool_inv_count(Tc, False)
    dft_x, dft_c = _dft_matrix(T), _dft_matrix(Tc)
    zero_c = jnp.zeros((B, N_MLSTM, MLSTM_AUG, MLSTM_DH), F32)
    zero_m = jnp.zeros((B, N_MLSTM, SUBLANE, LANE), F32)
    for l in range(DEPTH):
        last = l == DEPTH - 1
        mx = [m[:, None, :] for m in jnp.split(mods[l, :B], 6, axis=-1)]
        mc = [jnp.broadcast_to(m[None, None, :], (B, 1, D)) for m in jnp.split(mods[l, B], 6, axis=-1)]
        w_main = w_in[l][:, :OFF_G].astype(BF16)
        w_gate = jnp.pad(w_in[l][:, OFF_G:], ((0, 0), (0, LANE - N_GATES))).astype(BF16)
        b_gate = jnp.pad(b_gates[l], (0, LANE - N_GATES))[None]
        w_pool_bd = _block_diag(pool_w[l]).astype(BF16)
        s_pool = pool_scale[l][None]
        w_out_b = w_out[l].astype(BF16)
        w_up_b = w_up[l].astype(BF16)
        w_down_b = w_down[l].astype(BF16)
        g1, b1 = ln1_g[l][None], ln1_b[l][None]
        g2, b2 = ln2_g[l][None], ln2_b[l][None]
        bcv = conv_ffn_b[l][None]

        zc, zgc = _proj_in(xc, mc[0], mc[1], w_main, w_gate, conv_qk[l], tm_c)
        ymc, cf, mf, cb, mb = _mlstm(zc, zgc, b_gate, zero_c, zero_m, zero_c, zero_m)
        zx, zgx = _proj_in(x, mx[0], mx[1], w_main, w_gate, conv_qk[l], tm_x)
        ymx = _mlstm(zx, zgx, b_gate, cf, mf, cb, mb)[0]
        ypx = _pool(zx, filt_x, cnt_x, w_pool_bd, s_pool, True)
        yfx = _fourier(zx, _fourier_mix_weights(fourier_w[l], T), dft_x)
        x = _proj_out(ypx, yfx, ymx, x, mx[2], w_out_b, g1, b1, min(T, 2 * tm_x))
        x = _ffn(x, mx[3], mx[4], mx[5], w_up_b, conv_ffn_w[l], bcv, w_down_b, g2, b2, tm_x)
        if not last:
            ypc = _pool(zc, filt_c, cnt_c, w_pool_bd, s_pool, False)
            yfc = _fourier(zc, _fourier_mix_weights(fourier_w[l], Tc), dft_c)
            xc = _proj_out(ypc, yfc, ymc, xc, mc[2], w_out_b, g1, b1, tm_c)
            xc = _ffn(xc, mc[3], mc[4], mc[5], w_up_b, conv_ffn_w[l], bcv, w_down_b, g2, b2, tm_c)
    return x
```
